```python
import math
import jax
import jax.numpy as jnp
from jax import lax
import numpy as np

D_MODEL = 2048
BATCH = 4
SEQ = 2048
DEPTH = 2
DEC_BATCH = 128
DEC_SEQ = 8
PAST_LEN = 16384
PAGE_SIZE = 128

N_A_LAYERS = (DEPTH + 1) // 2
N_C_LAYERS = DEPTH // 2
EPS = 1e-6
CHUNK = 64

RET_HEADS = 4
RET_DK = D_MODEL // 8
RET_DV = D_MODEL // 8
ROPE_BASE = 10000.0
GLA_HEADS = 4
GLA_DK = D_MODEL // 16
GLA_DV = D_MODEL // 8
GLA_RANK = 16
GLA_TEMP = 16.0
RET_QK = RET_HEADS * RET_DK
RET_V = RET_HEADS * RET_DV
GLA_QK = GLA_HEADS * GLA_DK
GLA_V = GLA_HEADS * GLA_DV
A_SPLITS = (RET_QK, RET_QK, RET_V, RET_V, GLA_QK, GLA_QK, GLA_V, GLA_V, GLA_RANK)
IN_A = 2 * RET_QK + 2 * RET_V + 2 * GLA_QK + 2 * GLA_V + GLA_RANK
MIX_A = RET_V + GLA_V
SSM_EXPAND = 2
D_INNER = SSM_EXPAND * D_MODEL
SSM_HEADDIM = 64
SSM_HEADS = D_INNER // SSM_HEADDIM
SSM_GROUPS = 8
SSM_STATE = 128
CONV_W = 4
CONV_DIM = D_INNER + 2 * SSM_GROUPS * SSM_STATE
IN_C = D_INNER + CONV_DIM + SSM_HEADS
MOE_GROUPS = 4
MOE_PER_GROUP = 8
MOE_EXPERTS = MOE_GROUPS * MOE_PER_GROUP
MOE_TOPK = 2
MOE_FF = D_MODEL // 4

kernel_name = 'hybrid_retnet_gla_mamba2_hmoe_step'


def _rms(x, g):
    xf = x.astype(jnp.float32)
    y = xf * lax.rsqrt(jnp.mean(xf * xf, axis=-1, keepdims=True) + EPS)
    return (y * g.astype(jnp.float32)).astype(x.dtype)


def _head_norm(o, g, center):
    if center:
        o = o - jnp.mean(o, axis=-1, keepdims=True)
    o = o * lax.rsqrt(jnp.mean(o * o, axis=-1, keepdims=True) + EPS)
    return o.reshape(o.shape[0], o.shape[1], -1) * g.astype(jnp.float32)


def _split(a, sizes):
    out, start = [], 0
    for s in sizes:
        out.append(a[..., start:start + s])
        start += s
    return out


def _chunk_len(t):
    return CHUNK if t % CHUNK == 0 else t


def _to_chunks(a, n, l):
    return a.reshape(a.shape[0], n, l, *a.shape[2:]).swapaxes(0, 1)


def _from_chunks(a):
    a = a.swapaxes(0, 1)
    return a.reshape(a.shape[0], a.shape[1] * a.shape[2], *a.shape[3:])


def _rotary(x, pos):
    half = x.shape[-1] // 2
    inv = ROPE_BASE ** (-jnp.arange(half, dtype=jnp.float32) / half)
    ang = pos[:, None] * inv[None, :]
    cos = jnp.cos(ang)[None, :, None, :]
    sin = jnp.sin(ang)[None, :, None, :]
    x1, x2 = x[..., :half], x[..., half:]
    return jnp.concatenate([x1 * cos - x2 * sin, x1 * sin + x2 * cos], axis=-1)


def _retention_chunked(q, k, v, log_gamma, s0):
    B, T, H, _ = q.shape
    L = _chunk_len(T)
    n = T // L
    idx = jnp.arange(L, dtype=jnp.float32)
    rel = idx[:, None] - idx[None, :]
    causal = rel >= 0
    decay_in = jnp.where(causal[None], jnp.exp(log_gamma[:, None, None] * jnp.where(causal, rel, 0.0)[None]), 0.0)
    decay_q = jnp.exp(log_gamma[None, :] * (idx + 1.0)[:, None])[None, :, :, None]
    decay_k = jnp.exp(log_gamma[None, :] * (L - 1.0 - idx)[:, None])[None, :, :, None]
    decay_chunk = jnp.exp(log_gamma * L)[None, :, None, None]

    def step(s, inp):
        qc, kc, vc = inp
        scores = jnp.einsum('bihd,bjhd->bhij', qc, kc) * decay_in[None]
        o = jnp.einsum('bhij,bjhv->bihv', scores, vc)
        o = o + jnp.einsum('bihd,bhdv->bihv', qc, s) * decay_q
        s = s * decay_chunk + jnp.einsum('bjhd,bjhv->bhdv', kc * decay_k, vc)
        return s, o

    s, o = lax.scan(step, s0, (_to_chunks(q, n, L), _to_chunks(k, n, L), _to_chunks(v, n, L)))
    return _from_chunks(o), s


def _gla_chunked(q, k, v, g, s0):
    B, T, H, _ = q.shape
    L = _chunk_len(T)
    n = T // L
    causal = jnp.tril(jnp.ones((L, L), dtype=bool))[None, :, :, None, None]

    def step(s, inp):
        qc, kc, vc, gc = inp
        b = jnp.cumsum(gc, axis=1)
        diff = b[:, :, None] - b[:, None, :]
        dec = jnp.exp(jnp.where(causal, diff, -jnp.inf))
        scores = jnp.einsum('bihd,bjhd,bijhd->bhij', qc, kc, dec)
        o = jnp.einsum('bhij,bjhv->bihv', scores, vc)
        o = o + jnp.einsum('bihd,bhdv->bihv', qc * jnp.exp(b), s)
        b_last = b[:, -1]
        s = s * jnp.exp(b_last)[..., None] + jnp.einsum('bjhd,bjhv->bhdv', kc * jnp.exp(b_last[:, None] - b), vc)
        return s, o

    s, o = lax.scan(step, s0, (_to_chunks(q, n, L), _to_chunks(k, n, L), _to_chunks(v, n, L), _to_chunks(g, n, L)))
    return _from_chunks(o), s


def _ssd_chunked(x, dt, a_neg, bm, cm, s0):
    B, T, H, P = x.shape
    G, N = bm.shape[2], bm.shape[3]
    R = H // G
    L = _chunk_len(T)
    n = T // L
    x = x.reshape(B, T, G, R, P)
    dt = dt.reshape(B, T, G, R)
    a_gr = a_neg.reshape(G, R)
    causal = jnp.tril(jnp.ones((L, L), dtype=bool))[None, :, :, None, None]

    def step(s, inp):
        xc, dtc, bc, cc = inp
        cum = jnp.cumsum(dtc * a_gr, axis=1)
        seg = cum[:, :, None] - cum[:, None, :]
        decay = jnp.exp(jnp.where(causal, seg, -jnp.inf))
        cb = jnp.einsum('bign,bjgn->bijg', cc, bc)
        w = cb[..., None] * decay * dtc[:, None]
        y = jnp.einsum('bijgr,bjgrp->bigrp', w, xc)
        y = y + jnp.einsum('bign,bgrpn->bigrp', cc, s) * jnp.exp(cum)[..., None]
        tail = jnp.exp(cum[:, -1:] - cum) * dtc
        s = s * jnp.exp(cum[:, -1])[..., None, None] + jnp.einsum('bjgn,bjgrp->bgrpn', bc, xc * tail[..., None])
        return s, y

    s, y = lax.scan(step, s0.reshape(B, G, R, P, N),
                    (_to_chunks(x, n, L), _to_chunks(dt, n, L), _to_chunks(bm, n, L), _to_chunks(cm, n, L)))
    return _from_chunks(y).reshape(B, T, H, P), s.reshape(B, H, P, N)


def _mixer_a(h, pos, s_ret, s_gla, p, i):
    B, T, _ = h.shape
    f32 = jnp.float32
    q_r, k_r, v_r, g_r, q_g, k_g, v_g, g_g, a_low = _split(h @ p['a_w_in'][i], A_SPLITS)
    q_r = _rotary(q_r.reshape(B, T, RET_HEADS, RET_DK).astype(f32), pos)
    k_r = _rotary(k_r.reshape(B, T, RET_HEADS, RET_DK).astype(f32), pos) * (RET_DK ** -0.5)
    v_r = v_r.reshape(B, T, RET_HEADS, RET_DV).astype(f32)
    log_gamma = jnp.log(1.0 - 2.0 ** (-5.0 - jnp.arange(RET_HEADS, dtype=f32)))
    o_r, s_ret_new = _retention_chunked(q_r, k_r, v_r, log_gamma, s_ret.astype(f32))
    o_r = _head_norm(o_r, p['a_ret_gn'][i], True) * jax.nn.silu(g_r.astype(f32))
    q_g = q_g.reshape(B, T, GLA_HEADS, GLA_DK).astype(f32) * (GLA_DK ** -0.5)
    k_g = k_g.reshape(B, T, GLA_HEADS, GLA_DK).astype(f32)
    v_g = v_g.reshape(B, T, GLA_HEADS, GLA_DV).astype(f32)
    log_a = jax.nn.log_sigmoid((a_low @ p['a_gla_w2'][i] + p['a_gla_b2'][i]).astype(f32)) / GLA_TEMP
    log_a = log_a.reshape(B, T, GLA_HEADS, GLA_DK)
    o_g, s_gla_new = _gla_chunked(q_g, k_g, v_g, log_a, s_gla.astype(f32))
    o_g = _head_norm(o_g, p['a_gla_gn'][i], False) * jax.nn.silu(g_g.astype(f32))
    o = jnp.concatenate([o_r, o_g], axis=-1).astype(h.dtype)
    return o @ p['a_w_out'][i], s_ret_new.astype(h.dtype), s_gla_new.astype(h.dtype)


def _mixer_c(h, s_ssm, conv_buf, p, i):
    B, T, _ = h.shape
    f32 = jnp.float32
    z, xbc, dt = _split(h @ p['c_w_in'][i], (D_INNER, CONV_DIM, SSM_HEADS))
    xpad = jnp.concatenate([conv_buf.astype(xbc.dtype), xbc], axis=1)
    cw = p['c_conv_w'][i]
    conv = p['c_conv_b'][i] + xpad[:, 0:T] * cw[0]
    for j in range(1, CONV_W):
        conv = conv + xpad[:, j:j + T] * cw[j]
    xbc_act = jax.nn.silu(conv.astype(f32))
    xs, bm, cm = _split(xbc_act, (D_INNER, SSM_GROUPS * SSM_STATE, SSM_GROUPS * SSM_STATE))
    xs = xs.reshape(B, T, SSM_HEADS, SSM_HEADDIM)
    bm = bm.reshape(B, T, SSM_GROUPS, SSM_STATE)
    cm = cm.reshape(B, T, SSM_GROUPS, SSM_STATE)
    dt = jax.nn.softplus(dt.astype(f32) + p['c_dt_bias'][i].astype(f32))
    a_neg = -jnp.exp(p['c_a_log'][i].astype(f32))
    y, s_new = _ssd_chunked(xs, dt, a_neg, bm, cm, s_ssm.astype(f32))
    y = y + p['c_d'][i].astype(f32)[:, None] * xs
    y = y.reshape(B, T, D_INNER) * jax.nn.silu(z.astype(f32))
    y = _rms(y, p['c_norm'][i]).astype(h.dtype)
    return y @ p['c_w_out'][i], s_new.astype(h.dtype), xpad[:, T:].astype(h.dtype)


def _hier_moe(h, p, l):
    B, T, D = h.shape
    hf = h.reshape(B * T, D)
    lg = (hf @ p['moe_w_group'][l]).astype(jnp.float32)
    p_grp = jax.nn.softmax(lg, axis=-1)
    top_g = lax.top_k(lg, 1)[1][:, 0]
    p_top = jnp.take_along_axis(p_grp, top_g[:, None], axis=1)
    le = jnp.einsum('nd,gde->nge', hf, p['moe_w_expert'][l]).astype(jnp.float32)
    le_sel = jnp.take_along_axis(le, top_g[:, None, None], axis=1)[:, 0]
    v2, i2 = lax.top_k(le_sel, MOE_TOPK)
    w2 = jax.nn.softmax(v2, axis=-1) * p_top
    eid = top_g[:, None] * MOE_PER_GROUP + i2
    gates = jnp.sum(jax.nn.one_hot(eid, MOE_EXPERTS, dtype=jnp.float32) * w2[..., None], axis=1)
    hid = jax.nn.silu(jnp.einsum('nd,edf->nef', hf, p['moe_w_gate'][l])) * jnp.einsum('nd,edf->nef', hf, p['moe_w_up'][l])
    hid = hid * gates[..., None].astype(hid.dtype)
    y = jnp.einsum('nef,efd->nd', hid, p['moe_w_down'][l])
    return y.reshape(B, T, D)


def _modulate(x, g, shift, scale):
    return _rms(x, g) * (1.0 + scale) + shift


def _trunk(x, c, pos, ret_st, gla_st, ssm_st, conv_st, p):
    new_ret, new_gla, new_ssm, new_conv = [], [], [], []
    for l in range(DEPTH):
        m = (jax.nn.silu(c) @ p['w_ada'][l] + p['b_ada'][l])[:, None, :]
        sh1, sc1, g1, sh2, sc2, g2 = jnp.split(m, 6, axis=-1)
        hmix = _modulate(x, p['norm_mix'][l], sh1, sc1)
        if l % 2 == 0:
            i = l // 2
            out, s_r, s_g = _mixer_a(hmix, pos, ret_st[i], gla_st[i], p, i)
            new_ret.append(s_r)
            new_gla.append(s_g)
        else:
            i = l // 2
            out, s_s, buf = _mixer_c(hmix, ssm_st[i], conv_st[i], p, i)
            new_ssm.append(s_s)
            new_conv.append(buf)
        x = x + g1 * out
        hmoe = _modulate(x, p['norm_moe'][l], sh2, sc2)
        x = x + g2 * _hier_moe(hmoe, p, l)
    y = _rms(x, p['final_norm'])
    return y, jnp.stack(new_ret), jnp.stack(new_gla), jnp.stack(new_ssm), jnp.stack(new_conv)


def setup_inputs(seed: int = 0) -> dict:
    key = jax.random.key(seed)
    ks = iter(jax.random.split(key, 48))

    def nrm(shape, scale):
        return scale * jax.random.normal(next(ks), shape, jnp.float32)

    def gain(shape):
        return 1.0 + nrm(shape, 0.02)

    dt0 = jnp.exp(jax.random.uniform(next(ks), (N_C_LAYERS, SSM_HEADS), jnp.float32, math.log(1e-3), math.log(1e-1)))
    return {
        'x_prompt': nrm((BATCH, SEQ, D_MODEL), 1.0),
        'x_sample': nrm((DEC_BATCH, DEC_SEQ, D_MODEL), 1.0),
        'c_prompt': nrm((BATCH, D_MODEL), 1.0),
        'c_sample': nrm((DEC_BATCH, D_MODEL), 1.0),
        'state_ret': nrm((N_A_LAYERS, DEC_BATCH, RET_HEADS, RET_DK, RET_DV), 0.25),
        'state_gla': nrm((N_A_LAYERS, DEC_BATCH, GLA_HEADS, GLA_DK, GLA_DV), 0.1),
        'state_ssm': nrm((N_C_LAYERS, DEC_BATCH, SSM_HEADS, SSM_HEADDIM, SSM_STATE), 0.1),
        'cache_conv': nrm((N_C_LAYERS, DEC_BATCH, CONV_W - 1, CONV_DIM), 1.0),
        'w_ada': nrm((DEPTH, D_MODEL, 6 * D_MODEL), 0.5 * D_MODEL ** -0.5),
        'b_ada': nrm((DEPTH, 6 * D_MODEL), 0.1),
        'norm_mix': gain((DEPTH, D_MODEL)),
        'norm_moe': gain((DEPTH, D_MODEL)),
        'a_w_in': nrm((N_A_LAYERS, D_MODEL, IN_A), D_MODEL ** -0.5),
        'a_ret_gn': gain((N_A_LAYERS, RET_V)),
        'a_gla_w2': nrm((N_A_LAYERS, GLA_RANK, GLA_QK), GLA_RANK ** -0.5),
        'a_gla_b2': nrm((N_A_LAYERS, GLA_QK), 0.1),
        'a_gla_gn': gain((N_A_LAYERS, GLA_V)),
        'a_w_out': nrm((N_A_LAYERS, MIX_A, D_MODEL), MIX_A ** -0.5),
        'c_w_in': nrm((N_C_LAYERS, D_MODEL, IN_C), D_MODEL ** -0.5),
        'c_conv_w': nrm((N_C_LAYERS, CONV_W, CONV_DIM), CONV_W ** -0.5),
        'c_conv_b': nrm((N_C_LAYERS, CONV_DIM), 0.02),
        'c_dt_bias': dt0 + jnp.log(-jnp.expm1(-dt0)),
        'c_a_log': jnp.log(jax.random.uniform(next(ks), (N_C_LAYERS, SSM_HEADS), jnp.float32, 1.0, 16.0)),
        'c_d': gain((N_C_LAYERS, SSM_HEADS)),
        'c_norm': gain((N_C_LAYERS, D_INNER)),
        'c_w_out': nrm((N_C_LAYERS, D_INNER, D_MODEL), D_INNER ** -0.5),
        'moe_w_group': nrm((DEPTH, D_MODEL, MOE_GROUPS), D_MODEL ** -0.5),
        'moe_w_expert': nrm((DEPTH, MOE_GROUPS, D_MODEL, MOE_PER_GROUP), D_MODEL ** -0.5),
        'moe_w_gate': nrm((DEPTH, MOE_EXPERTS, D_MODEL, MOE_FF), D_MODEL ** -0.5),
        'moe_w_up': nrm((DEPTH, MOE_EXPERTS, D_MODEL, MOE_FF), D_MODEL ** -0.5),
        'moe_w_down': nrm((DEPTH, MOE_EXPERTS, MOE_FF, D_MODEL), MOE_FF ** -0.5),
        'final_norm': gain((D_MODEL,)),
    }


def reference(x_prompt, x_sample, c_prompt, c_sample, state_ret, state_gla, state_ssm, cache_conv,
              w_ada, b_ada, norm_mix, norm_moe,
              a_w_in, a_ret_gn, a_gla_w2, a_gla_b2, a_gla_gn, a_w_out,
              c_w_in, c_conv_w, c_conv_b, c_dt_bias, c_a_log, c_d, c_norm, c_w_out,
              moe_w_group, moe_w_expert, moe_w_gate, moe_w_up, moe_w_down, final_norm):
    p = dict(w_ada=w_ada, b_ada=b_ada, norm_mix=norm_mix, norm_moe=norm_moe,
             a_w_in=a_w_in, a_ret_gn=a_ret_gn, a_gla_w2=a_gla_w2, a_gla_b2=a_gla_b2,
             a_gla_gn=a_gla_gn, a_w_out=a_w_out,
             c_w_in=c_w_in, c_conv_w=c_conv_w, c_conv_b=c_conv_b, c_dt_bias=c_dt_bias,
             c_a_log=c_a_log, c_d=c_d, c_norm=c_norm, c_w_out=c_w_out,
             moe_w_group=moe_w_group, moe_w_expert=moe_w_expert, moe_w_gate=moe_w_gate,
             moe_w_up=moe_w_up, moe_w_down=moe_w_down, final_norm=final_norm)
    bp, tp = x_prompt.shape[0], x_prompt.shape[1]
    ts = x_sample.shape[1]
    dt_ = x_prompt.dtype
    pos_p = jnp.arange(tp, dtype=jnp.float32)
    y_prompt, ret_p, gla_p, ssm_p, conv_p = _trunk(
        x_prompt, c_prompt, pos_p,
        jnp.zeros((N_A_LAYERS, bp, RET_HEADS, RET_DK, RET_DV), dt_),
        jnp.zeros((N_A_LAYERS, bp, GLA_HEADS, GLA_DK, GLA_DV), dt_),
        jnp.zeros((N_C_LAYERS, bp, SSM_HEADS, SSM_HEADDIM, SSM_STATE), dt_),
        jnp.zeros((N_C_LAYERS, bp, CONV_W - 1, CONV_DIM), dt_), p)
    pos_s = PAST_LEN + jnp.arange(ts, dtype=jnp.float32)
    y_sample, ret_s, gla_s, ssm_s, conv_s = _trunk(
        x_sample, c_sample, pos_s, state_ret, state_gla, state_ssm, cache_conv, p)
    return (y_prompt, y_sample, ret_p, ret_s, gla_p, gla_s, ssm_p, ssm_s, conv_p, conv_s)
```

```python
import functools
import math

import jax
import jax.numpy as jnp
from jax import lax
from jax.experimental import pallas as pl
from jax.experimental.pallas import tpu as pltpu

F32 = jnp.float32
BF16 = jnp.bfloat16
I32 = jnp.int32

EPS = 1e-6
PAST_LEN = 16384
ROPE_BASE = 10000.0
GLA_TEMP = 16.0
GLA_RANK = 16
RET_HEADS = 4
GLA_HEADS = 4
SSM_GROUPS = 8
SSM_HEADDIM = 64
SSM_STATE = 128
CONV_W = 4
MOE_GROUPS = 4
MOE_PER_GROUP = 8
MOE_EXPERTS = MOE_GROUPS * MOE_PER_GROUP

V7X_LANES = 128
V7X_SUBLANES = 8
V7X_VMEM_BYTES = 64 * 2 ** 20
V7X_VMEM_RESERVE = 6 * 2 ** 20

MIX_CHUNK = 128
GLA_BLOCK = 16
MOE_TILE = 256


def _cparams(n_grid, block_bytes, scratch_bytes=0, temp_bytes=0):
    need = 2 * block_bytes + scratch_bytes + temp_bytes + 8 * 2 ** 20
    limit = int(min(max(need, 24 * 2 ** 20), V7X_VMEM_BYTES - V7X_VMEM_RESERVE))
    return pltpu.CompilerParams(dimension_semantics=("arbitrary",) * n_grid,
                                vmem_limit_bytes=limit)


def _nbytes(shape, dtype=F32):
    return math.prod(shape) * jnp.dtype(dtype).itemsize


def _silu(x):
    return x * (1.0 / (1.0 + jnp.exp(-x)))


def _softplus(x):
    return jnp.maximum(x, 0.0) + jnp.log1p(jnp.exp(-jnp.abs(x)))


def _log_sigmoid(x):
    return -_softplus(-x)


def _dot(a, b):
    return jnp.dot(a, b, preferred_element_type=F32)


def _dot_nt(a, b):
    return lax.dot_general(a, b, (((1,), (1,)), ((), ())), preferred_element_type=F32)


def _split2(x):
    hi = x.astype(BF16)
    return hi, (x - hi.astype(F32)).astype(BF16)


def _dot3p(a, b):
    return _dot(a[0], b[0]) + _dot(a[0], b[1]) + _dot(a[1], b[0])


def _dot3(a, b):
    return _dot3p(_split2(a), _split2(b))


def _dot3_nt(a, b):
    ah, al = _split2(a)
    bh, bl = _split2(b)
    return _dot_nt(ah, bh) + _dot_nt(ah, bl) + _dot_nt(al, bh)


def _split3(x):
    a = x.astype(BF16)
    r = x - a.astype(F32)
    b = r.astype(BF16)
    r = r - b.astype(F32)
    return a, b, r.astype(BF16)


def _dot01_left(m01, x):
    a, b, c = _split3(x)
    return _dot(m01, a) + _dot(m01, b) + _dot(m01, c)


def _dot01_right(x, m01):
    a, b, c = _split3(x)
    return _dot(a, m01) + _dot(b, m01) + _dot(c, m01)


def _iota(shape, dim, dtype=I32):
    return lax.broadcasted_iota(dtype, shape, dim)


def _tril01(n, block=None, strict=False):
    i = _iota((n, n), 0)
    j = _iota((n, n), 1)
    m = (i > j) if strict else (i >= j)
    if block is not None:
        m = m & ((i // block) == (j // block))
    return jnp.where(m, 1.0, 0.0).astype(BF16)


def _rms_rows(x, g):
    return x * lax.rsqrt(jnp.mean(x * x, axis=-1, keepdims=True) + EPS) * g


def _rot(x, cos, sin):
    half = x.shape[-1] // 2
    x1, x2 = x[:, :half], x[:, half:]
    return jnp.concatenate([x1 * cos - x2 * sin, x1 * sin + x2 * cos], axis=-1)


def _head_norm(o, gain, gate, center):
    if center:
        o = o - jnp.mean(o, axis=-1, keepdims=True)
    o = o * lax.rsqrt(jnp.mean(o * o, axis=-1, keepdims=True) + EPS)
    return o * gain * _silu(gate)


def _pad_rows(x, rows):
    return jnp.concatenate([x, jnp.zeros((rows - x.shape[0], x.shape[1]), x.dtype)], axis=0)


def _ada_body(c_ref, w_ref, b_ref, o_ref):
    o_ref[0] = _dot3(_silu(c_ref[...]), w_ref[0]) + b_ref[0]


def _ada(c_all, w_ada, b_ada):
    depth, d, n6 = w_ada.shape
    m = c_all.shape[0]
    tn = 1024
    blocks = _nbytes((m, d)) + _nbytes((d, tn)) + _nbytes((m, tn))
    return pl.pallas_call(
        _ada_body, name="ada",
        grid=(depth, n6 // tn),
        in_specs=[pl.BlockSpec((m, d), lambda l, j: (0, 0)),
                  pl.BlockSpec((1, d, tn), lambda l, j: (l, 0, j)),
                  pl.BlockSpec((1, 1, tn), lambda l, j: (l, 0, j))],
        out_specs=pl.BlockSpec((1, m, tn), lambda l, j: (l, 0, j)),
        out_shape=jax.ShapeDtypeStruct((depth, m, n6), F32),
        compiler_params=_cparams(2, blocks, temp_bytes=3 * _nbytes((d, tn))),
    )(c_all, w_ada, b_ada.reshape(depth, 1, n6))


def _inproj_body(passes, x_ref, sh_ref, sc_ref, g_ref, w_ref, ws_ref, o_ref, os_ref, h_scr):
    @pl.when(pl.program_id(1) == 0)
    def _():
        x = x_ref[...]
        h = _rms_rows(x, g_ref[...]) * (1.0 + sc_ref[...]) + sh_ref[...]
        h = h.reshape(h_scr.shape[1:])
        if passes == 3:
            hs = _split2(h)
            h_scr[0] = hs[0]
            h_scr[1] = hs[1]
            os_ref[...] = _dot3p(hs, _split2(ws_ref[...]))
        else:
            hb = h.astype(BF16)
            h_scr[0] = hb
            os_ref[...] = _dot(hb, ws_ref[...].astype(BF16))

    if passes == 3:
        o_ref[...] = _dot3p((h_scr[0], h_scr[1]), _split2(w_ref[...]))
    else:
        o_ref[...] = _dot(h_scr[0], w_ref[...].astype(BF16))


def _inproj(x3, m3, chunk_shift, chunk_scale, gain, w, n_main, ws, sb, rb, passes, tn=512):
    sq, rq, d = x3.shape
    nb = rq // rb
    tm = sb * rb
    n_tok = sq * rq
    ns = ws.shape[1]
    n_h = 2 if passes == 3 else 1
    blocks = (_nbytes((tm, d)) + 2 * _nbytes((sb, V7X_SUBLANES, d)) + _nbytes((d, tn))
              + _nbytes((d, ns), ws.dtype) + _nbytes((tm, tn)) + _nbytes((tm, ns)))
    return pl.pallas_call(
        functools.partial(_inproj_body, passes), name="inproj",
        grid=(n_tok // tm, n_main // tn),
        in_specs=[pl.BlockSpec((sb, rb, d), lambda i, j: (i // nb, i % nb, 0)),
                  pl.BlockSpec((sb, 1, d), lambda i, j: (i // nb, 0, chunk_shift)),
                  pl.BlockSpec((sb, 1, d), lambda i, j: (i // nb, 0, chunk_scale)),
                  pl.BlockSpec((1, d), lambda i, j: (0, 0)),
                  pl.BlockSpec((d, tn), lambda i, j: (0, j)),
                  pl.BlockSpec((d, ns), lambda i, j: (0, 0))],
        out_specs=[pl.BlockSpec((tm, tn), lambda i, j: (i, j)),
                   pl.BlockSpec((tm, ns), lambda i, j: (i, 0))],
        out_shape=[jax.ShapeDtypeStruct((n_tok, n_main), F32),
                   jax.ShapeDtypeStruct((n_tok, ns), F32)],
        scratch_shapes=[pltpu.VMEM((n_h, tm, d), BF16)],
        compiler_params=_cparams(2, blocks, n_h * _nbytes((tm, d), BF16),
                                 2 * _nbytes((tm, d)) + 2 * _nbytes((d, tn))),
    )(x3, m3, m3, gain.reshape(1, d), w, ws)


def _outproj_body(n_in, norm, passes, *refs):
    o_refs = refs[:n_in]
    w_refs = refs[n_in:2 * n_in]
    pos = 2 * n_in
    if norm:
        gn_ref = refs[pos]
        pos += 1
    x_ref, gate_ref, out_ref = refs[pos:pos + 3]
    acc = None
    for o_ref, w_ref in zip(o_refs, w_refs):
        o = o_ref[...]
        if norm:
            o = _rms_rows(o, gn_ref[...])
        if passes == 3:
            part = _dot3(o, w_ref[...])
        else:
            part = _dot(o.astype(BF16), w_ref[...].astype(BF16))
        acc = part if acc is None else acc + part
    x = x_ref[...]
    out_ref[...] = x + gate_ref[...] * acc.reshape(x.shape)


def _outproj(o_list, w, x3, m3, chunk_gate, sb, rb, passes, norm_gain=None, tn=512):
    sq, rq, d = x3.shape
    nb = rq // rb
    tm = sb * rb
    n_in = len(o_list)
    nj = d // tn
    widths = [o.shape[1] for o in o_list]
    assert all(wd == widths[0] for wd in widths) and sum(widths) == w.shape[0]
    kw = widths[0]
    in_specs = [pl.BlockSpec((tm, kw), lambda j, i: (i, 0)) for _ in o_list]
    in_specs += [pl.BlockSpec((kw, tn), lambda j, i, k=k: (k, j)) for k in range(n_in)]
    args = list(o_list) + [w] * n_in
    if norm_gain is not None:
        in_specs.append(pl.BlockSpec((1, kw), lambda j, i: (0, 0)))
        args.append(norm_gain.reshape(1, kw))
    in_specs += [pl.BlockSpec((sb, rb, tn), lambda j, i: (i // nb, i % nb, j)),
                 pl.BlockSpec((sb, 1, tn), lambda j, i: (i // nb, 0, chunk_gate * nj + j))]
    args += [x3, m3]
    blocks = (n_in * (_nbytes((tm, kw), o_list[0].dtype) + _nbytes((kw, tn)))
              + 2 * _nbytes((tm, tn)) + _nbytes((sb, V7X_SUBLANES, tn)))
    return pl.pallas_call(
        functools.partial(_outproj_body, n_in, norm_gain is not None, passes), name="outproj",
        grid=(nj, (sq * rq) // tm),
        in_specs=in_specs,
        out_specs=pl.BlockSpec((sb, rb, tn), lambda j, i: (i // nb, i % nb, j)),
        out_shape=jax.ShapeDtypeStruct(x3.shape, F32),
        compiler_params=_cparams(2, blocks, 0,
                                 n_in * (2 * _nbytes((kw, tn)) + 2 * _nbytes((tm, kw)))),
    )(*args)


def _ret_tables(lam, c, width):
    lam1 = lam[:, :1]
    rel = (_iota((c, c), 0) - _iota((c, c), 1)).astype(F32)
    din = jnp.where(rel >= 0, jnp.exp(lam1 * jnp.maximum(rel, 0.0)), 0.0)
    r = _iota((c, width), 0).astype(F32)
    dq = jnp.exp(lam * (r + 1.0))
    dk = jnp.exp(lam * (c - 1.0 - r))
    gc = jnp.exp(lam * float(c))
    return din, dq, dk, gc


def _gla_log_alpha(alow, w2, b2):
    xa = _dot3(alow, w2) + b2
    return _log_sigmoid(xa) * (1.0 / GLA_TEMP)


def _mixa_prompt_body(qr, kr, vr, gr, qg, kg, vg, gg, alow, w2, b2, cos, sin, gnr, gng, lam,
                      o_r, o_g, sret_out, sgla_out,
                      s_ret, st_gla, din_s, dq_s, dk_s, b_s, a_s):
    c_id = pl.program_id(2)
    n_c = pl.num_programs(2)
    c = qr.shape[0]
    dk_r = qr.shape[1]
    dk_g = qg.shape[1]
    lam_row = lam[0]

    @pl.when(c_id == 0)
    def _():
        s_ret[...] = jnp.zeros_like(s_ret)
        st_gla[...] = jnp.zeros_like(st_gla)
        din, dq, dk, _ = _ret_tables(lam_row, c, dk_r)
        din_s[...] = din
        dq_s[...] = dq
        dk_s[...] = dk

    cs, sn = cos[...], sin[...]
    q = _rot(qr[...], cs, sn)
    k = _rot(kr[...], cs, sn) * (dk_r ** -0.5)
    qs, vs = _split2(q), _split2(vr[...])
    sc = _dot3_nt(q, k) * din_s[...]
    s_old = s_ret[...]
    o = _dot3p(_split2(sc), vs) + _dot3p(qs, _split2(s_old)) * dq_s[...]
    kd = k * dk_s[...]
    s_ret[...] = s_old * jnp.exp(lam_row * float(c)) + _dot3p(_split2(kd.T), vs)
    o_r[...] = _head_norm(o, gnr[...], gr[...], True)

    la = _gla_log_alpha(alow[...], w2[...], b2[...])
    b_s[...] = _dot01_left(_tril01(c), la)
    bc = b_s[...]
    qg_v = qg[...] * (dk_g ** -0.5)
    kg_v = kg[...]
    rows = _iota((c, dk_g), 0)
    nblk = c // GLA_BLOCK
    row16 = _iota((GLA_BLOCK, dk_g), 0)
    lane16 = _iota((GLA_BLOCK, c), 1)
    for blk in range(nblk):
        r0 = blk * GLA_BLOCK
        q_i = qg_v[r0:r0 + GLA_BLOCK]
        b_i = bc[r0:r0 + GLA_BLOCK]
        if blk == 0:
            a_i = jnp.zeros((GLA_BLOCK, c), F32)
        else:
            ref_b = bc[r0 - 1:r0]
            qt = q_i * jnp.exp(b_i - ref_b)
            kt = jnp.where(rows < r0, kg_v * jnp.exp(jnp.minimum(ref_b - bc, 0.0)), 0.0)
            a_i = _dot3_nt(qt, kt)
        for jj in range(GLA_BLOCK):
            j = r0 + jj
            e = jnp.exp(jnp.minimum(b_i - bc[j:j + 1], 0.0))
            col = jnp.sum(q_i * e * kg_v[j:j + 1], axis=-1, keepdims=True)
            col = jnp.where(row16[:, :1] >= jj, col, 0.0)
            a_i = a_i + jnp.where(lane16 == j, col, 0.0)
        a_s[r0:r0 + GLA_BLOCK, :] = a_i
    st_old = st_gla[...]
    og = _dot3(a_s[...], vg[...]) + _dot3_nt(qg_v * jnp.exp(bc), st_old)
    b_last = bc[c - 1:c]
    kdg = kg_v * jnp.exp(b_last - bc)
    st_gla[...] = st_old * jnp.exp(b_last) + _dot3(vg[...].T, kdg)
    o_g[...] = _head_norm(og, gng[...], gg[...], False)

    @pl.when(c_id == n_c - 1)
    def _():
        sret_out[0, 0] = s_ret[...]
        sgla_out[0, 0] = st_gla[...].T


def _mixa_prompt(pa, pa_low, w2p, b2, cos, sin, gnr, gng, lam, batch, seq):
    c = MIX_CHUNK
    nc = seq // c
    h = RET_HEADS
    dkr = 256
    dkg = 128
    dv = 256

    def rows(b, hh, cc):
        return b * nc + cc

    def col(width, base):
        return pl.BlockSpec((c, width), lambda b, hh, cc: (rows(b, hh, cc), base // width + hh))

    in_specs = [col(dkr, 0), col(dkr, 1024), col(dv, 2048), col(dv, 3072),
                col(dkg, 4096), col(dkg, 4608), col(dv, 5120), col(dv, 6144),
                pl.BlockSpec((c, V7X_LANES), lambda b, hh, cc: (rows(b, hh, cc), 0)),
                pl.BlockSpec((V7X_LANES, dkg), lambda b, hh, cc: (0, hh)),
                pl.BlockSpec((1, dkg), lambda b, hh, cc: (0, hh)),
                pl.BlockSpec((c, dkr // 2), lambda b, hh, cc: (cc, 0)),
                pl.BlockSpec((c, dkr // 2), lambda b, hh, cc: (cc, 0)),
                pl.BlockSpec((1, dv), lambda b, hh, cc: (0, hh)),
                pl.BlockSpec((1, dv), lambda b, hh, cc: (0, hh)),
                pl.BlockSpec((1, 1, dkr), lambda b, hh, cc: (hh, 0, 0))]
    out_specs = [pl.BlockSpec((c, dv), lambda b, hh, cc: (rows(b, hh, cc), hh)),
                 pl.BlockSpec((c, dv), lambda b, hh, cc: (rows(b, hh, cc), hh)),
                 pl.BlockSpec((1, 1, dkr, dv), lambda b, hh, cc: (b, hh, 0, 0)),
                 pl.BlockSpec((1, 1, dkg, dv), lambda b, hh, cc: (b, hh, 0, 0))]
    n_tok = batch * seq
    out_shape = [jax.ShapeDtypeStruct((n_tok, h * dv), F32),
                 jax.ShapeDtypeStruct((n_tok, h * dv), F32),
                 jax.ShapeDtypeStruct((batch, h, dkr, dv), F32),
                 jax.ShapeDtypeStruct((batch, h, dkg, dv), F32)]
    scratch = [pltpu.VMEM((dkr, dv), F32), pltpu.VMEM((dv, dkg), F32),
               pltpu.VMEM((c, c), F32), pltpu.VMEM((c, dkr), F32), pltpu.VMEM((c, dkr), F32),
               pltpu.VMEM((c, dkg), F32), pltpu.VMEM((c, c), F32)]
    blocks = 12 * _nbytes((c, dv)) + 2 * _nbytes((dkr, dv))
    return pl.pallas_call(
        _mixa_prompt_body, name="mixa_prompt",
        grid=(batch, h, nc),
        in_specs=in_specs, out_specs=out_specs, out_shape=out_shape,
        scratch_shapes=scratch,
        compiler_params=_cparams(3, blocks, 8 * _nbytes((dkr, dv)), 16 * 2 ** 20),
    )(pa, pa, pa, pa, pa, pa, pa, pa, pa_low, w2p, b2, cos, sin, gnr, gng, lam)


def _direct_intra(q, k, v, decay_fn):
    t = q.shape[0]
    row = _iota((t, 1), 0)
    o = jnp.zeros((t, v.shape[1]), F32)
    for j in range(t):
        col = jnp.sum(q * decay_fn(j) * k[j:j + 1], axis=-1, keepdims=True)
        col = jnp.where(row >= j, col, 0.0)
        o = o + col * v[j:j + 1]
    return o


def _mixa_sample_body(sb, pa, alow, w2, b2, cos, sin, gnr, gng, lam, sret_in, sgla_in,
                      o_r, o_g, sret_out, sgla_out):
    t = 8
    dkr, dkg, dv = 256, 128, 256
    cs, sn = cos[...], sin[...]
    la = _gla_log_alpha(alow[...], w2[...], b2[...])
    bc_all = _dot01_left(_tril01(sb * t, block=t), la)
    rel = (_iota((t, 1), 0)).astype(F32)
    for hh in range(RET_HEADS):
        lam_row = lam[hh]
        dq = jnp.exp(lam_row * (rel + 1.0))
        dk = jnp.exp(lam_row * (t - 1.0 - rel))
        gc = jnp.exp(lam_row * float(t))
        for s in range(sb):
            r0 = s * t
            q = _rot(pa[r0:r0 + t, hh * dkr:(hh + 1) * dkr], cs, sn)
            k = _rot(pa[r0:r0 + t, 1024 + hh * dkr:1024 + (hh + 1) * dkr], cs, sn) * (dkr ** -0.5)
            v = pa[r0:r0 + t, 2048 + hh * dv:2048 + (hh + 1) * dv]
            g = pa[r0:r0 + t, 3072 + hh * dv:3072 + (hh + 1) * dv]
            s_old = sret_in[s, hh]
            o = _direct_intra(q, k, v, lambda j: jnp.exp(lam_row * jnp.maximum(rel - float(j), 0.0)))
            o = o + _dot3(q, s_old) * dq
            kdt = _pad_rows(k * dk, V7X_LANES).T
            sret_out[s, hh] = s_old * gc + _dot3(kdt, _pad_rows(v, V7X_LANES))
            o_r[r0:r0 + t, hh * dv:(hh + 1) * dv] = _head_norm(
                o, gnr[:, hh * dv:(hh + 1) * dv], g, True)
            qg = pa[r0:r0 + t, 4096 + hh * dkg:4096 + (hh + 1) * dkg] * (dkg ** -0.5)
            kg = pa[r0:r0 + t, 4608 + hh * dkg:4608 + (hh + 1) * dkg]
            vg = pa[r0:r0 + t, 5120 + hh * dv:5120 + (hh + 1) * dv]
            gg = pa[r0:r0 + t, 6144 + hh * dv:6144 + (hh + 1) * dv]
            bc = bc_all[r0:r0 + t, hh * dkg:(hh + 1) * dkg]
            sg_old = sgla_in[s, hh]
            og = _direct_intra(qg, kg, vg, lambda j: jnp.exp(jnp.minimum(bc - bc[j:j + 1], 0.0)))
            og = og + _dot3(qg * jnp.exp(bc), sg_old)
            b_last = bc[t - 1:t]
            m = _pad_rows(jnp.concatenate([kg * jnp.exp(b_last - bc), jnp.exp(b_last)], axis=0),
                          V7X_LANES).T
            sgla_out[s, hh] = sg_old * m[:, t:t + 1] + _dot3(m, _pad_rows(vg, V7X_LANES))
            o_g[r0:r0 + t, hh * dv:(hh + 1) * dv] = _head_norm(
                og, gng[:, hh * dv:(hh + 1) * dv], gg, False)


def _mixa_sample(pa, pa_low, w2p, b2, cos, sin, gnr, gng, lam, sret, sgla, sb=4):
    nseq = sret.shape[0]
    t = 8
    h, dkr, dkg, dv = RET_HEADS, 256, 128, 256
    width = pa.shape[1]
    const2 = lambda i: (0, 0)
    in_specs = [pl.BlockSpec((sb * t, width), lambda i: (i, 0)),
                pl.BlockSpec((sb * t, V7X_LANES), lambda i: (i, 0)),
                pl.BlockSpec(w2p.shape, const2), pl.BlockSpec(b2.shape, const2),
                pl.BlockSpec(cos.shape, const2), pl.BlockSpec(sin.shape, const2),
                pl.BlockSpec(gnr.shape, const2), pl.BlockSpec(gng.shape, const2),
                pl.BlockSpec(lam.shape, lambda i: (0, 0, 0)),
                pl.BlockSpec((sb, h, dkr, dv), lambda i: (i, 0, 0, 0)),
                pl.BlockSpec((sb, h, dkg, dv), lambda i: (i, 0, 0, 0))]
    out_specs = [pl.BlockSpec((sb * t, h * dv), lambda i: (i, 0)),
                 pl.BlockSpec((sb * t, h * dv), lambda i: (i, 0)),
                 pl.BlockSpec((sb, h, dkr, dv), lambda i: (i, 0, 0, 0)),
                 pl.BlockSpec((sb, h, dkg, dv), lambda i: (i, 0, 0, 0))]
    out_shape = [jax.ShapeDtypeStruct((nseq * t, h * dv), F32),
                 jax.ShapeDtypeStruct((nseq * t, h * dv), F32),
                 jax.ShapeDtypeStruct(sret.shape, F32),
                 jax.ShapeDtypeStruct(sgla.shape, F32)]
    blocks = (_nbytes((sb * t, width)) + 2 * _nbytes((sb, h, dkr, dv))
              + 2 * _nbytes((sb, h, dkg, dv)) + 2 * 2 ** 20)
    return pl.pallas_call(
        functools.partial(_mixa_sample_body, sb), name="mixa_sample",
        grid=(nseq // sb,),
        in_specs=in_specs, out_specs=out_specs, out_shape=out_shape,
        compiler_params=_cparams(1, blocks, 0, 4 * 2 ** 20),
    )(pa, pa_low, w2p, b2, cos, sin, gnr, gng, lam, sret, sgla)


def _conv_prompt_body(x_ref, prev_ref, w_ref, b_ref, o_ref, scr):
    t = x_ref.shape[0]
    first = pl.program_id(1) == 0
    scr[0:V7X_SUBLANES, :] = jnp.where(first, 0.0, prev_ref[...])
    scr[V7X_SUBLANES:V7X_SUBLANES + t, :] = x_ref[...]
    acc = b_ref[...] + scr[V7X_SUBLANES - 3:V7X_SUBLANES - 3 + t, :] * w_ref[0:1, :]
    for j in range(1, CONV_W):
        off = V7X_SUBLANES - 3 + j
        acc = acc + scr[off:off + t, :] * w_ref[j:j + 1, :]
    o_ref[...] = _silu(acc)


def _conv_prompt(pc, conv_w, conv_b, batch, seq, col0, width, tc=512, tn=512):
    nc = seq // tc
    cb0 = col0 // tn
    blocks = 2 * _nbytes((tc, tn)) + _nbytes((V7X_SUBLANES, tn)) * 3
    return pl.pallas_call(
        _conv_prompt_body, name="conv_prompt",
        grid=(batch, nc, width // tn),
        in_specs=[pl.BlockSpec((tc, tn), lambda b, c, j: (b * nc + c, cb0 + j)),
                  pl.BlockSpec((V7X_SUBLANES, tn),
                               lambda b, c, j: (jnp.maximum((b * nc + c) * (tc // V7X_SUBLANES) - 1, 0),
                                                cb0 + j)),
                  pl.BlockSpec((CONV_W, tn), lambda b, c, j: (0, j)),
                  pl.BlockSpec((1, tn), lambda b, c, j: (0, j))],
        out_specs=pl.BlockSpec((tc, tn), lambda b, c, j: (b * nc + c, j)),
        out_shape=jax.ShapeDtypeStruct((batch * seq, width), F32),
        scratch_shapes=[pltpu.VMEM((tc + V7X_SUBLANES, tn), F32)],
        compiler_params=_cparams(3, blocks, _nbytes((tc + 8, tn)), 4 * _nbytes((tc, tn))),
    )(pc, pc, conv_w, conv_b.reshape(1, width))


def _conv_sample_body(x_ref, cache_ref, w_ref, b_ref, o_ref, scr):
    t = x_ref.shape[1]
    hist = CONV_W - 1
    scr[:, t - hist:t, :] = cache_ref[...]
    scr[:, t:2 * t, :] = x_ref[...]
    acc = b_ref[...] + scr[:, t - hist:2 * t - hist, :] * w_ref[0:1, :]
    for j in range(1, CONV_W):
        off = t - hist + j
        acc = acc + scr[:, off:off + t, :] * w_ref[j:j + 1, :]
    o_ref[...] = _silu(acc)


def _conv_sample(pc3, cache, conv_w, conv_b, col0, width, sbc=32, tn=512):
    nseq, t, _ = pc3.shape
    cb0 = col0 // tn
    blocks = 2 * _nbytes((sbc, t, tn)) + _nbytes((sbc, 8, tn))
    return pl.pallas_call(
        _conv_sample_body, name="conv_sample",
        grid=(nseq // sbc, width // tn),
        in_specs=[pl.BlockSpec((sbc, t, tn), lambda i, j: (i, 0, cb0 + j)),
                  pl.BlockSpec((sbc, CONV_W - 1, tn), lambda i, j: (i, 0, j)),
                  pl.BlockSpec((CONV_W, tn), lambda i, j: (0, j)),
                  pl.BlockSpec((1, tn), lambda i, j: (0, j))],
        out_specs=pl.BlockSpec((sbc, t, tn), lambda i, j: (i, 0, j)),
        out_shape=jax.ShapeDtypeStruct((nseq, t, width), F32),
        scratch_shapes=[pltpu.VMEM((sbc, 2 * t, tn), F32)],
        compiler_params=_cparams(2, blocks, _nbytes((sbc, 2 * t, tn)), 4 * _nbytes((sbc, t, tn))),
    )(pc3, cache, conv_w, conv_b.reshape(1, width))


def _expand01(heads, width):
    r = _iota((V7X_LANES, heads * width), 0)
    l = _iota((V7X_LANES, heads * width), 1)
    return jnp.where((l // width) == r, 1.0, 0.0).astype(BF16)


def _ssd_prompt_body(xs, bm, cm, z, dtc, dtr, bias_c, alog_c, bias_r, alog_r, dfull,
                     y_out, s_out, st):
    c_id = pl.program_id(2)
    n_c = pl.num_programs(2)
    l = xs.shape[0]
    heads = dtr.shape[1]
    p = xs.shape[1] // heads

    @pl.when(c_id == 0)
    def _():
        st[...] = jnp.zeros_like(st)

    dt_c = _softplus(dtc[...] + bias_c[0])
    cum_c = _dot01_left(_tril01(l), dt_c * (-jnp.exp(alog_c[0])))
    dt_r = _softplus(dtr[0] + bias_r[0])
    tri_u = jnp.where(_iota((l, l), 0) <= _iota((l, l), 1), 1.0, 0.0).astype(BF16)
    cum_r = _dot01_right(dt_r * (-jnp.exp(alog_r[0])), tri_u)

    x = xs[...]
    cb = _dot_nt(cm[...].astype(BF16), bm[...].astype(BF16))
    causal = _iota((l, l), 0) >= _iota((l, l), 1)
    lane = _iota((l, 2 * p), 1)
    y_tiles = []
    for pair in range(heads // 2):
        xt = x[:, pair * 2 * p:(pair + 1) * 2 * p]
        acc = None
        for sub in range(2):
            r = 2 * pair + sub
            seg = cum_c[:, r:r + 1] - cum_r[r:r + 1, :]
            w = jnp.where(causal, cb * jnp.exp(jnp.minimum(seg, 0.0)) * dt_r[r:r + 1, :], 0.0)
            xm = jnp.where((lane // p) == sub, xt, 0.0)
            part = _dot(w.astype(BF16), xm.astype(BF16))
            acc = part if acc is None else acc + part
        y_tiles.append(acc)
    y = jnp.concatenate(y_tiles, axis=-1)

    ex = _expand01(heads, p)
    cum_f = _dot01_right(cum_c, ex)
    dt_f = _dot01_right(dt_c, ex)
    st_old = st[...]
    y = y + _dot(cm[...].astype(BF16), st_old.astype(BF16)) * jnp.exp(cum_f)
    cl_f = cum_f[l - 1:l]
    tail = jnp.exp(cl_f - cum_f) * dt_f
    st[...] = st_old * jnp.exp(cl_f) + _dot(bm[...].T.astype(BF16), (x * tail).astype(BF16))
    y = y + dfull[0] * x
    y_out[...] = y * _silu(z[...])

    @pl.when(c_id == n_c - 1)
    def _():
        s_out[0] = st[...].T.reshape(s_out.shape[1:])


def _ssd_prompt(act, pc, pcs, dt_t, bias_c, alog_c, bias_r, alog_r, dfull, batch, seq):
    l = MIX_CHUNK
    nc = seq // l
    g = SSM_GROUPS
    hp = act.shape[1]
    n = SSM_STATE
    p = SSM_HEADDIM
    d_inner = pc.shape[1] - 6144
    gw = d_inner // g
    hpg = gw // p
    xb0 = 0
    bb0 = d_inner // n
    cb0 = (d_inner + g * n) // n

    def row(b, gg, cc):
        return b * nc + cc

    in_specs = [pl.BlockSpec((l, gw), lambda b, gg, cc: (row(b, gg, cc), gg)),
                pl.BlockSpec((l, n), lambda b, gg, cc: (row(b, gg, cc), bb0 + gg)),
                pl.BlockSpec((l, n), lambda b, gg, cc: (row(b, gg, cc), cb0 + gg)),
                pl.BlockSpec((l, gw), lambda b, gg, cc: (row(b, gg, cc), gg)),
                pl.BlockSpec((l, V7X_LANES), lambda b, gg, cc: (row(b, gg, cc), gg)),
                pl.BlockSpec((1, hpg, l), lambda b, gg, cc: (b, gg, cc)),
                pl.BlockSpec((1, 1, V7X_LANES), lambda b, gg, cc: (gg, 0, 0)),
                pl.BlockSpec((1, 1, V7X_LANES), lambda b, gg, cc: (gg, 0, 0)),
                pl.BlockSpec((1, hpg, l), lambda b, gg, cc: (gg, 0, 0)),
                pl.BlockSpec((1, hpg, l), lambda b, gg, cc: (gg, 0, 0)),
                pl.BlockSpec((1, 1, gw), lambda b, gg, cc: (gg, 0, 0))]
    out_specs = [pl.BlockSpec((l, gw), lambda b, gg, cc: (row(b, gg, cc), gg)),
                 pl.BlockSpec((1, hpg, p, n), lambda b, gg, cc: (b, gg, 0, 0))]
    out_shape = [jax.ShapeDtypeStruct((batch * seq, d_inner), F32),
                 jax.ShapeDtypeStruct((batch, g * hpg, p, n), F32)]
    del hp, xb0
    blocks = 3 * _nbytes((l, gw)) + 3 * _nbytes((l, n)) + _nbytes((hpg, p, n))
    return pl.pallas_call(
        _ssd_prompt_body, name="ssd_prompt",
        grid=(batch, g, nc),
        in_specs=in_specs, out_specs=out_specs, out_shape=out_shape,
        scratch_shapes=[pltpu.VMEM((n, gw), F32)],
        compiler_params=_cparams(3, blocks, _nbytes((n, gw)), 16 * 2 ** 20),
    )(act, act, act, pc, pcs, dt_t, bias_c, alog_c, bias_r, alog_r, dfull)


def _ssd_sample_body(sb, xs, bm, cm, z, dtc, bias_c, alog_c, dfull, s_in, y_out, s_out, y_s):
    t = 8
    rows = sb * t
    gw = xs.shape[1]
    hpg = s_in.shape[1]
    p = gw // hpg
    n = bm.shape[1]
    dt_c = _softplus(dtc[...] + bias_c[0])
    cum_c = _dot01_left(_tril01(rows, block=t), dt_c * (-jnp.exp(alog_c[0])))
    ex = _expand01(hpg, p)
    cum_f = _dot01_right(cum_c, ex)
    dt_f = _dot01_right(dt_c, ex)
    x = xs[...]
    x3 = x.reshape(sb, t, gw)
    c3 = cm[...].reshape(sb, t, n)
    b3 = bm[...].reshape(sb, t, n)
    cum3 = cum_f.reshape(sb, t, gw)
    dt3 = dt_f.reshape(sb, t, gw)
    rowi = _iota((sb, t, 1), 1)
    y3 = jnp.zeros((sb, t, gw), F32)
    for j in range(t):
        cbj = jnp.sum(c3 * b3[:, j:j + 1, :], axis=-1, keepdims=True)
        dec = jnp.exp(jnp.minimum(cum3 - cum3[:, j:j + 1, :], 0.0))
        w = jnp.where(rowi >= j, cbj * dec * dt3[:, j:j + 1, :], 0.0)
        y3 = y3 + w * x3[:, j:j + 1, :]
    y_s[...] = y3.reshape(rows, gw)
    cmv = cm[...]
    bmv = bm[...]
    for s in range(sb):
        r0 = s * t
        sg = s_in[s].reshape(hpg * p, n)
        cf = cum_f[r0:r0 + t]
        yi = _dot_nt(cmv[r0:r0 + t].astype(BF16), sg.astype(BF16)) * jnp.exp(cf)
        cl = cf[t - 1:t]
        xt = x[r0:r0 + t] * (jnp.exp(cl - cf) * dt_f[r0:r0 + t])
        m = _pad_rows(jnp.concatenate([xt, jnp.exp(cl)], axis=0), V7X_LANES).T
        s_new = sg * m[:, t:t + 1] + _dot(m.astype(BF16), _pad_rows(bmv[r0:r0 + t], V7X_LANES).astype(BF16))
        s_out[s] = s_new.reshape(hpg, p, n)
        y_s[r0:r0 + t, :] = y_s[r0:r0 + t, :] + yi
    y_out[...] = (y_s[...] + dfull[0] * x) * _silu(z[...])


def _ssd_sample(act, pc, pcs, bias_c, alog_c, dfull, s_ssm, sb=8):
    t = 8
    nseq = s_ssm.shape[0]
    g = SSM_GROUPS
    n = SSM_STATE
    p = SSM_HEADDIM
    d_inner = pc.shape[1] - 6144
    gw = d_inner // g
    hpg = gw // p
    bb0 = d_inner // n
    cb0 = (d_inner + g * n) // n
    rows = sb * t
    in_specs = [pl.BlockSpec((rows, gw), lambda i, gg: (i, gg)),
                pl.BlockSpec((rows, n), lambda i, gg: (i, bb0 + gg)),
                pl.BlockSpec((rows, n), lambda i, gg: (i, cb0 + gg)),
                pl.BlockSpec((rows, gw), lambda i, gg: (i, gg)),
                pl.BlockSpec((rows, V7X_LANES), lambda i, gg: (i, gg)),
                pl.BlockSpec((1, 1, V7X_LANES), lambda i, gg: (gg, 0, 0)),
                pl.BlockSpec((1, 1, V7X_LANES), lambda i, gg: (gg, 0, 0)),
                pl.BlockSpec((1, 1, gw), lambda i, gg: (gg, 0, 0)),
                pl.BlockSpec((sb, hpg, p, n), lambda i, gg: (i, gg, 0, 0))]
    out_specs = [pl.BlockSpec((rows, gw), lambda i, gg: (i, gg)),
                 pl.BlockSpec((sb, hpg, p, n), lambda i, gg: (i, gg, 0, 0))]
    out_shape = [jax.ShapeDtypeStruct((nseq * t, d_inner), F32),
                 jax.ShapeDtypeStruct(s_ssm.shape, F32)]
    blocks = 3 * _nbytes((rows, gw)) + 3 * _nbytes((rows, n)) + 2 * _nbytes((sb, hpg, p, n))
    return pl.pallas_call(
        functools.partial(_ssd_sample_body, sb), name="ssd_sample",
        grid=(nseq // sb, g),
        in_specs=in_specs, out_specs=out_specs, out_shape=out_shape,
        scratch_shapes=[pltpu.VMEM((rows, gw), F32)],
        compiler_params=_cparams(2, blocks, _nbytes((rows, gw)), 8 * 2 ** 20),
    )(act, act, act, pc, pcs, bias_c, alog_c, dfull, s_ssm)


def _moe_pre_body(n_pt, xp_ref, shp_ref, scp_ref, xs_ref, shs_ref, scs_ref, g_ref, wr_ref,
                  h_out, route_out):
    i = pl.program_id(0)

    @pl.when(i < n_pt)
    def _():
        _moe_route(xp_ref, shp_ref, scp_ref, g_ref, wr_ref, h_out, route_out)

    @pl.when(i >= n_pt)
    def _():
        _moe_route(xs_ref, shs_ref, scs_ref, g_ref, wr_ref, h_out, route_out)


def _moe_route(x_ref, sh_ref, sc_ref, g_ref, wr_ref, h_out, route_out):
    x = x_ref[...]
    h = _rms_rows(x, g_ref[...]) * (1.0 + sc_ref[...]) + sh_ref[...]
    h = h.reshape(h_out.shape)
    h_out[...] = h
    logits = jnp.dot(h, wr_ref[...], precision=lax.Precision.HIGHEST, preferred_element_type=F32)
    tm = logits.shape[0]
    lane = _iota((tm, V7X_LANES), 1).astype(F32)
    neg = -jnp.inf
    lg = jnp.where(lane < MOE_GROUPS, logits, neg)
    mg = jnp.max(lg, axis=-1, keepdims=True)
    top_g = jnp.min(jnp.where(lg == mg, lane, float(V7X_LANES)), axis=-1, keepdims=True)
    p_top = 1.0 / jnp.sum(jnp.where(lane < MOE_GROUPS, jnp.exp(logits - mg), 0.0),
                          axis=-1, keepdims=True)
    lo = MOE_GROUPS + MOE_PER_GROUP * top_g
    le = jnp.where((lane >= lo) & (lane < lo + MOE_PER_GROUP), logits, neg)
    v1 = jnp.max(le, axis=-1, keepdims=True)
    i1 = jnp.min(jnp.where(le == v1, lane, float(V7X_LANES)), axis=-1, keepdims=True)
    le2 = jnp.where(lane == i1, neg, le)
    v2 = jnp.max(le2, axis=-1, keepdims=True)
    i2 = jnp.min(jnp.where(le2 == v2, lane, float(V7X_LANES)), axis=-1, keepdims=True)
    tt = jnp.exp(v2 - v1)
    w1 = p_top / (1.0 + tt)
    w2 = p_top * tt / (1.0 + tt)
    route_out[...] = jnp.where(lane == 0, i1 - MOE_GROUPS,
                     jnp.where(lane == 1, i2 - MOE_GROUPS,
                     jnp.where(lane == 2, w1, jnp.where(lane == 3, w2, 0.0))))


def _moe_pre(xp, mp, xs_, ms, chunk_shift, chunk_scale, gain, wr, tm):
    bp, tp, d = xp.shape
    bs, ts, _ = xs_.shape
    nb = tp // tm
    n_pt = bp * nb
    sbs = tm // ts
    n_st = bs // sbs
    n_all = bp * tp + bs * ts

    def pi(i):
        return jnp.minimum(i, n_pt - 1)

    def si(i):
        return jnp.maximum(i - n_pt, 0)

    in_specs = [pl.BlockSpec((1, tm, d), lambda i: (pi(i) // nb, pi(i) % nb, 0)),
                pl.BlockSpec((1, 1, d), lambda i: (pi(i) // nb, 0, chunk_shift)),
                pl.BlockSpec((1, 1, d), lambda i: (pi(i) // nb, 0, chunk_scale)),
                pl.BlockSpec((sbs, ts, d), lambda i: (si(i), 0, 0)),
                pl.BlockSpec((sbs, 1, d), lambda i: (si(i), 0, chunk_shift)),
                pl.BlockSpec((sbs, 1, d), lambda i: (si(i), 0, chunk_scale)),
                pl.BlockSpec((1, d), lambda i: (0, 0)),
                pl.BlockSpec((d, V7X_LANES), lambda i: (0, 0))]
    blocks = (3 * _nbytes((tm, d)) + 2 * _nbytes((sbs, 8, d)) + _nbytes((d, V7X_LANES))
              + _nbytes((tm, 128)))
    return pl.pallas_call(
        functools.partial(_moe_pre_body, n_pt), name="moe_pre",
        grid=(n_pt + n_st,),
        in_specs=in_specs,
        out_specs=[pl.BlockSpec((tm, d), lambda i: (i, 0)),
                   pl.BlockSpec((tm, V7X_LANES), lambda i: (i, 0))],
        out_shape=[jax.ShapeDtypeStruct((n_all, d), F32),
                   jax.ShapeDtypeStruct((n_all, V7X_LANES), F32)],
        compiler_params=_cparams(1, blocks, 0, 4 * _nbytes((tm, d))),
    )(xp, mp, mp, xs_, ms, ms, gain.reshape(1, d), wr)


def _rank_body(route_ref, rank_ref, cnt_ref, carry):
    @pl.when(pl.program_id(0) == 0)
    def _():
        carry[...] = jnp.zeros_like(carry)

    r = route_ref[...]
    tm = r.shape[0]
    lane = _iota((tm, V7X_LANES), 1).astype(F32)
    o1 = lane == r[:, 0:1]
    o2 = lane == r[:, 1:2]
    oh = jnp.where(o1, 1.0, 0.0) + jnp.where(o2, 1.0, 0.0)
    pex = _dot(_tril01(tm, strict=True), oh.astype(BF16)) + carry[0:1, :]
    r1 = jnp.sum(jnp.where(o1, pex, 0.0), axis=-1, keepdims=True)
    r2 = jnp.sum(jnp.where(o2, pex, 0.0), axis=-1, keepdims=True)
    rank_ref[...] = jnp.where(lane == 0, r1, jnp.where(lane == 1, r2, 0.0))
    carry[...] = carry[...] + jnp.sum(oh, axis=0, keepdims=True)
    cnt_ref[...] = carry[...]


def _rank(route, tm=256):
    n = route.shape[0]
    return pl.pallas_call(
        _rank_body, name="moe_rank",
        grid=(n // tm,),
        in_specs=[pl.BlockSpec((tm, V7X_LANES), lambda i: (i, 0))],
        out_specs=[pl.BlockSpec((tm, V7X_LANES), lambda i: (i, 0)),
                   pl.BlockSpec((V7X_SUBLANES, V7X_LANES), lambda i: (0, 0))],
        out_shape=[jax.ShapeDtypeStruct((n, V7X_LANES), F32),
                   jax.ShapeDtypeStruct((V7X_SUBLANES, V7X_LANES), F32)],
        scratch_shapes=[pltpu.VMEM((V7X_SUBLANES, V7X_LANES), F32)],
        compiler_params=_cparams(1, 2 * _nbytes((tm, 128))),
    )(route)


def _row_copy(src_hbm, dst_vmem, sem, src_row, dst_row):
    return pltpu.make_async_copy(src_hbm.at[pl.ds(src_row, 1), :],
                                 dst_vmem.at[pl.ds(dst_row, 1), :], sem)


def _gather_body(inv_ref, nt_ref, h_hbm, o_ref, sem):
    i = pl.program_id(0)
    tg = o_ref.shape[0]
    active = i < nt_ref[0]

    @pl.when(active)
    def _():
        def issue(r, carry):
            _row_copy(h_hbm, o_ref, sem, inv_ref[i * tg + r], r).start()
            return carry

        lax.fori_loop(0, tg, issue, 0)

        def drain(r, carry):
            _row_copy(h_hbm, o_ref, sem, 0, r).wait()
            return carry

        lax.fori_loop(0, tg, drain, 0)

    @pl.when(jnp.logical_not(active))
    def _():
        o_ref[...] = jnp.zeros_like(o_ref)


def _gather_rows(inv, ntiles, h_all, p_pad):
    d = h_all.shape[1]
    tg = MOE_TILE
    return pl.pallas_call(
        _gather_body, name="moe_gather",
        grid_spec=pltpu.PrefetchScalarGridSpec(
            num_scalar_prefetch=2,
            grid=(p_pad // tg,),
            in_specs=[pl.BlockSpec(memory_space=pl.ANY)],
            out_specs=pl.BlockSpec((tg, d), lambda i, inv_r, nt_r: (i, 0)),
            scratch_shapes=[pltpu.SemaphoreType.DMA(())]),
        out_shape=jax.ShapeDtypeStruct((p_pad, d), F32),
        compiler_params=_cparams(1, _nbytes((tg, d))),
    )(inv, ntiles, h_all)


def _expert_body(te_ref, nt_ref, x_ref, wg_ref, wu_ref, wd_ref, o_ref, wg_b, wu_b, wd_b):
    i = pl.program_id(0)
    active = i < nt_ref[0]
    changed = jnp.logical_or(i == 0, te_ref[i] != te_ref[jnp.maximum(i - 1, 0)])

    @pl.when(jnp.logical_and(active, changed))
    def _():
        wg_b[...] = wg_ref[0].astype(BF16)
        wu_b[...] = wu_ref[0].astype(BF16)
        wd_b[...] = wd_ref[0].astype(BF16)

    @pl.when(active)
    def _():
        xb = x_ref[...].astype(BF16)
        hid = _silu(_dot(xb, wg_b[...])) * _dot(xb, wu_b[...])
        o_ref[...] = _dot(hid.astype(BF16), wd_b[...])

    @pl.when(jnp.logical_not(active))
    def _():
        o_ref[...] = jnp.zeros_like(o_ref)


def _experts(tile_expert, ntiles, xs, wg, wu, wd):
    p_pad, d = xs.shape
    ff = wg.shape[2]
    tm = MOE_TILE
    blocks = 2 * _nbytes((tm, d)) + 3 * _nbytes((d, ff))
    return pl.pallas_call(
        _expert_body, name="moe_experts",
        grid_spec=pltpu.PrefetchScalarGridSpec(
            num_scalar_prefetch=2,
            grid=(p_pad // tm,),
            in_specs=[pl.BlockSpec((tm, d), lambda i, te, nt: (i, 0)),
                      pl.BlockSpec((1, d, ff), lambda i, te, nt: (te[i], 0, 0)),
                      pl.BlockSpec((1, d, ff), lambda i, te, nt: (te[i], 0, 0)),
                      pl.BlockSpec((1, ff, d), lambda i, te, nt: (te[i], 0, 0))],
            out_specs=pl.BlockSpec((tm, d), lambda i, te, nt: (i, 0)),
            scratch_shapes=[pltpu.VMEM((d, ff), BF16), pltpu.VMEM((d, ff), BF16),
                            pltpu.VMEM((ff, d), BF16)]),
        out_shape=jax.ShapeDtypeStruct((p_pad, d), F32),
        compiler_params=_cparams(1, blocks, 3 * _nbytes((d, ff), BF16), 4 * _nbytes((tm, d))),
    )(tile_expert, ntiles, xs, wg, wu, wd)


def _combine_body(tok_off, final, pos_ref, eo_hbm, route_ref, x_ref, gate_ref, fn_ref, out_ref,
                  buf, sem):
    i = pl.program_id(0)
    tc = route_ref.shape[0]
    base = 2 * (tok_off + i * tc)

    def issue(r, carry):
        _row_copy(eo_hbm, buf.at[0], sem, pos_ref[base + 2 * r], r).start()
        _row_copy(eo_hbm, buf.at[1], sem, pos_ref[base + 2 * r + 1], r).start()
        return carry

    lax.fori_loop(0, tc, issue, 0)

    def drain(r, carry):
        _row_copy(eo_hbm, buf.at[0], sem, 0, r).wait()
        _row_copy(eo_hbm, buf.at[1], sem, 0, r).wait()
        return carry

    lax.fori_loop(0, tc, drain, 0)
    rt = route_ref[...]
    y = rt[:, 2:3] * buf[0] + rt[:, 3:4] * buf[1]
    x = x_ref[...]
    xn = x + gate_ref[...] * y.reshape(x.shape)
    if final:
        xn = _rms_rows(xn, fn_ref[...])
    out_ref[...] = xn


def _combine(pos_flat, eo, route, x3, m3, chunk_gate, sb, rb, tok_off, final_gain=None):
    sq, rq, d = x3.shape
    nb = rq // rb
    tc = sb * rb
    blk_off = tok_off // tc
    final = final_gain is not None
    fn = (final_gain if final else jnp.ones((d,), F32)).reshape(1, d)
    blocks = 2 * _nbytes((tc, d)) + _nbytes((tc, 128)) + _nbytes((sb, 8, d))
    return pl.pallas_call(
        functools.partial(_combine_body, tok_off, final), name="moe_combine",
        grid_spec=pltpu.PrefetchScalarGridSpec(
            num_scalar_prefetch=1,
            grid=((sq * rq) // tc,),
            in_specs=[pl.BlockSpec(memory_space=pl.ANY),
                      pl.BlockSpec((tc, V7X_LANES), lambda i, p: (blk_off + i, 0)),
                      pl.BlockSpec((sb, rb, d), lambda i, p: (i // nb, i % nb, 0)),
                      pl.BlockSpec((sb, 1, d), lambda i, p: (i // nb, 0, chunk_gate)),
                      pl.BlockSpec((1, d), lambda i, p: (0, 0))],
            out_specs=pl.BlockSpec((sb, rb, d), lambda i, p: (i // nb, i % nb, 0)),
            scratch_shapes=[pltpu.VMEM((2, tc, d), F32), pltpu.SemaphoreType.DMA(())]),
        out_shape=jax.ShapeDtypeStruct(x3.shape, F32),
        compiler_params=_cparams(1, blocks, 2 * _nbytes((tc, d)), 4 * _nbytes((tc, d))),
    )(pos_flat, eo, route, x3, m3, fn)


def _moe_layer(xp, xs_, mp, ms, l, norm_moe, w_group, w_expert, w_gate, w_up, w_down, final_gain):
    bp, tp, d = xp.shape
    bs, ts, _ = xs_.shape
    n_p, n_s = bp * tp, bs * ts
    n_all = n_p + n_s
    wr = jnp.concatenate([w_group[l], jnp.transpose(w_expert[l], (1, 0, 2)).reshape(d, MOE_EXPERTS)],
                         axis=1)
    wr = jnp.pad(wr, ((0, 0), (0, V7X_LANES - wr.shape[1])))
    tm = MOE_TILE
    h_all, route = _moe_pre(xp, mp, xs_, ms, 3, 4, norm_moe[l], wr, tm)
    rank, cnt = _rank(route)
    counts = cnt[0, :MOE_EXPERTS].astype(I32)
    tiles_per = (counts + tm - 1) // tm
    tile_end = jnp.cumsum(tiles_per)
    row_start = (tile_end - tiles_per) * tm
    ntiles = tile_end[-1]
    e_idx = route[:, 0:2].astype(I32)
    pos = row_start[e_idx] + rank[:, 0:2].astype(I32)
    nt_max = (2 * n_all) // tm + MOE_EXPERTS
    p_pad = nt_max * tm
    tile_ids = jnp.arange(nt_max, dtype=I32)
    te = jnp.minimum(jnp.searchsorted(tile_end, tile_ids, side='right').astype(I32), MOE_EXPERTS - 1)
    te = jnp.where(tile_ids < ntiles, te, te[jnp.maximum(ntiles - 1, 0)])
    pos_flat = pos.reshape(-1)
    inv = jnp.zeros((p_pad,), I32).at[pos_flat].set(jnp.arange(2 * n_all, dtype=I32) // 2)
    nt_arr = ntiles.reshape(1).astype(I32)
    xsorted = _gather_rows(inv, nt_arr, h_all, p_pad)
    eo = _experts(te, nt_arr, xsorted, w_gate[l], w_up[l], w_down[l])
    xp_new = _combine(pos_flat, eo, route, xp, mp, 5, 1, tm, 0, final_gain)
    xs_new = _combine(pos_flat, eo, route, xs_, ms, 5, tm // ts, ts, n_p, final_gain)
    return xp_new, xs_new


def kernel(x_prompt, x_sample, c_prompt, c_sample, state_ret, state_gla, state_ssm, cache_conv,
           w_ada, b_ada, norm_mix, norm_moe,
           a_w_in, a_ret_gn, a_gla_w2, a_gla_b2, a_gla_gn, a_w_out,
           c_w_in, c_conv_w, c_conv_b, c_dt_bias, c_a_log, c_d, c_norm, c_w_out,
           moe_w_group, moe_w_expert, moe_w_gate, moe_w_up, moe_w_down, final_norm):
    bp, tp, d = x_prompt.shape
    bs, ts, _ = x_sample.shape
    n_p, n_s = bp * tp, bs * ts

    pad = (-bp) % V7X_SUBLANES
    c_all = jnp.concatenate([c_prompt, jnp.zeros((pad, d), F32), c_sample], axis=0)
    m_all = _ada(c_all, w_ada, b_ada)
    mp = [m_all[l, :bp].reshape(bp, 1, 6 * d) for l in range(2)]
    ms = [m_all[l, bp + pad:].reshape(bs, 1, 6 * d) for l in range(2)]

    tm_p, tm_s = 512, 512

    wa = a_w_in[0]
    n_main_a = (wa.shape[1] // 512) * 512
    wa_low = jnp.pad(wa[:, n_main_a:], ((0, 0), (0, V7X_LANES - (wa.shape[1] - n_main_a))))
    w2p = jnp.pad(a_gla_w2[0], ((0, V7X_LANES - GLA_RANK), (0, 0)))
    b2 = a_gla_b2[0].reshape(1, -1)
    gnr = a_ret_gn[0].reshape(1, -1)
    gng = a_gla_gn[0].reshape(1, -1)
    half = 128
    inv_freq = ROPE_BASE ** (-jnp.arange(half, dtype=F32) / half)
    ang_p = jnp.arange(tp, dtype=F32)[:, None] * inv_freq[None, :]
    ang_s = (PAST_LEN + jnp.arange(ts, dtype=F32))[:, None] * inv_freq[None, :]
    log_gamma = jnp.log(1.0 - 2.0 ** (-5.0 - jnp.arange(RET_HEADS, dtype=F32)))
    lam = jnp.broadcast_to(log_gamma[:, None, None], (RET_HEADS, 1, 256))

    pa_p, low_p = _inproj(x_prompt, mp[0], 0, 1, norm_mix[0], wa, n_main_a, wa_low, 1, tm_p, 3)
    pa_s, low_s = _inproj(x_sample, ms[0], 0, 1, norm_mix[0], wa, n_main_a, wa_low, tm_s // ts, ts, 3)
    or_p, og_p, ret_p, gla_p = _mixa_prompt(pa_p, low_p, w2p, b2, jnp.cos(ang_p), jnp.sin(ang_p),
                                            gnr, gng, lam, bp, tp)
    or_s, og_s, ret_s, gla_s = _mixa_sample(pa_s, low_s, w2p, b2, jnp.cos(ang_s), jnp.sin(ang_s),
                                            gnr, gng, lam, state_ret[0], state_gla[0])
    xp = _outproj([or_p, og_p], a_w_out[0], x_prompt, mp[0], 2, 1, tm_p, 3)
    xs_ = _outproj([or_s, og_s], a_w_out[0], x_sample, ms[0], 2, tm_s // ts, ts, 3)
    xp, xs_ = _moe_layer(xp, xs_, mp[0], ms[0], 0, norm_moe, moe_w_group, moe_w_expert,
                         moe_w_gate, moe_w_up, moe_w_down, None)

    wc = c_w_in[0]
    n_main_c = (wc.shape[1] // 512) * 512
    heads = wc.shape[1] - n_main_c
    hpg = heads // SSM_GROUPS
    d_inner = heads * SSM_HEADDIM
    conv_dim = n_main_c - d_inner
    wdt = jnp.pad(wc[:, n_main_c:].reshape(d, SSM_GROUPS, hpg),
                  ((0, 0), (0, 0), (0, V7X_LANES - hpg))).reshape(d, SSM_GROUPS * V7X_LANES).astype(BF16)
    pc_p, dt_p = _inproj(x3=xp, m3=mp[1], chunk_shift=0, chunk_scale=1, gain=norm_mix[1], w=wc,
                         n_main=n_main_c, ws=wdt, sb=1, rb=tm_p, passes=1)
    pc_s, dt_s = _inproj(x3=xs_, m3=ms[1], chunk_shift=0, chunk_scale=1, gain=norm_mix[1], w=wc,
                         n_main=n_main_c, ws=wdt, sb=tm_s // ts, rb=ts, passes=1)
    act_p = _conv_prompt(pc_p, c_conv_w[0], c_conv_b[0], bp, tp, d_inner, conv_dim)
    act_s = _conv_sample(pc_s.reshape(bs, ts, n_main_c), cache_conv[0], c_conv_w[0], c_conv_b[0],
                         d_inner, conv_dim).reshape(n_s, conv_dim)
    conv_p = pc_p.reshape(bp, tp, n_main_c)[:, tp - (CONV_W - 1):, d_inner:]
    conv_s = pc_s.reshape(bs, ts, n_main_c)[:, ts - (CONV_W - 1):, d_inner:]

    def lanes_pad(v):
        return jnp.pad(v.reshape(SSM_GROUPS, 1, hpg), ((0, 0), (0, 0), (0, V7X_LANES - hpg)))

    bias_c, alog_c = lanes_pad(c_dt_bias[0]), lanes_pad(c_a_log[0])
    bias_r = jnp.broadcast_to(c_dt_bias[0].reshape(SSM_GROUPS, hpg, 1), (SSM_GROUPS, hpg, MIX_CHUNK))
    alog_r = jnp.broadcast_to(c_a_log[0].reshape(SSM_GROUPS, hpg, 1), (SSM_GROUPS, hpg, MIX_CHUNK))
    dfull = jnp.repeat(c_d[0], SSM_HEADDIM).reshape(SSM_GROUPS, 1, hpg * SSM_HEADDIM)
    dt_t = jnp.transpose(dt_p.reshape(bp, tp, SSM_GROUPS, V7X_LANES)[..., :hpg].reshape(bp, tp, heads),
                         (0, 2, 1))
    yg_p, ssm_p = _ssd_prompt(act_p, pc_p, dt_p, dt_t, bias_c, alog_c, bias_r, alog_r, dfull, bp, tp)
    yg_s, ssm_s = _ssd_sample(act_s, pc_s, dt_s, bias_c, alog_c, dfull, state_ssm[0])
    xp = _outproj([yg_p], c_w_out[0], xp, mp[1], 2, 1, 256, 1, norm_gain=c_norm[0])
    xs_ = _outproj([yg_s], c_w_out[0], xs_, ms[1], 2, 256 // ts, ts, 1, norm_gain=c_norm[0])
    yp, ys = _moe_layer(xp, xs_, mp[1], ms[1], 1, norm_moe, moe_w_group, moe_w_expert,
                        moe_w_gate, moe_w_up, moe_w_down, final_norm)

    return (yp, ys, ret_p[None], ret_s[None], gla_p[None], gla_s[None],
            ssm_p[None], ssm_s[None], conv_p[None], conv_s[None])
```

```python
import functools
import math

import jax
import jax.numpy as jnp
from jax import lax
from jax.experimental import pallas as pl
from jax.experimental.pallas import tpu as pltpu

F32 = jnp.float32
BF16 = jnp.bfloat16
I32 = jnp.int32

EPS = 1e-6
PAST_LEN = 16384
ROPE_BASE = 10000.0
GLA_TEMP = 16.0
GLA_RANK = 16
RET_HEADS = 4
GLA_HEADS = 4
SSM_GROUPS = 8
SSM_HEADDIM = 64
SSM_STATE = 128
CONV_W = 4
MOE_GROUPS = 4
MOE_PER_GROUP = 8
MOE_EXPERTS = MOE_GROUPS * MOE_PER_GROUP

V7X_LANES = 128
V7X_SUBLANES = 8
V7X_VMEM_BYTES = 64 * 2 ** 20
V7X_VMEM_RESERVE = 6 * 2 ** 20

MIX_CHUNK = 128
GLA_BLOCK = 16
MOE_TILE = 256
DMA_UNROLL = 8


def _cparams(n_grid, block_bytes, scratch_bytes=0, temp_bytes=0, **kwargs):
    need = 2 * block_bytes + scratch_bytes + temp_bytes + 8 * 2 ** 20
    limit = int(min(max(need, 24 * 2 ** 20), V7X_VMEM_BYTES - V7X_VMEM_RESERVE))
    return pltpu.CompilerParams(dimension_semantics=("arbitrary",) * n_grid,
                                vmem_limit_bytes=limit, **kwargs)


def _nbytes(shape, dtype=F32):
    return math.prod(shape) * jnp.dtype(dtype).itemsize


def _silu(x):
    return x * (1.0 / (1.0 + jnp.exp(-x)))


def _softplus(x):
    return jnp.maximum(x, 0.0) + jnp.log1p(jnp.exp(-jnp.abs(x)))


def _log_sigmoid(x):
    return -_softplus(-x)


def _dot(a, b):
    return jnp.dot(a, b, preferred_element_type=F32)


def _dot_nt(a, b):
    return lax.dot_general(a, b, (((1,), (1,)), ((), ())), preferred_element_type=F32)


def _split2(x):
    hi = x.astype(BF16)
    return hi, (x - hi.astype(F32)).astype(BF16)


def _dot3p(a, b):
    return _dot(a[0], b[0]) + _dot(a[0], b[1]) + _dot(a[1], b[0])


def _dot3(a, b):
    return _dot3p(_split2(a), _split2(b))


def _dot3_nt(a, b):
    ah, al = _split2(a)
    bh, bl = _split2(b)
    return _dot_nt(ah, bh) + _dot_nt(ah, bl) + _dot_nt(al, bh)


def _split3(x):
    a = x.astype(BF16)
    r = x - a.astype(F32)
    b = r.astype(BF16)
    r = r - b.astype(F32)
    return a, b, r.astype(BF16)


def _dot01_left(m01, x):
    a, b, c = _split3(x)
    return _dot(m01, a) + _dot(m01, b) + _dot(m01, c)


def _dot01_right(x, m01):
    a, b, c = _split3(x)
    return _dot(a, m01) + _dot(b, m01) + _dot(c, m01)


def _iota(shape, dim, dtype=I32):
    return lax.broadcasted_iota(dtype, shape, dim)


def _tril01(n, block=None, strict=False):
    i = _iota((n, n), 0)
    j = _iota((n, n), 1)
    m = (i > j) if strict else (i >= j)
    if block is not None:
        m = m & ((i // block) == (j // block))
    return jnp.where(m, 1.0, 0.0).astype(BF16)


def _rms_rows(x, g):
    return x * lax.rsqrt(jnp.mean(x * x, axis=-1, keepdims=True) + EPS) * g


def _rot(x, cos, sin):
    half = x.shape[-1] // 2
    x1, x2 = x[:, :half], x[:, half:]
    return jnp.concatenate([x1 * cos - x2 * sin, x1 * sin + x2 * cos], axis=-1)


def _head_norm(o, gain, gate, center):
    if center:
        o = o - jnp.mean(o, axis=-1, keepdims=True)
    o = o * lax.rsqrt(jnp.mean(o * o, axis=-1, keepdims=True) + EPS)
    return o * gain * _silu(gate)


def _pad_rows(x, rows):
    return jnp.concatenate([x, jnp.zeros((rows - x.shape[0], x.shape[1]), x.dtype)], axis=0)


def _ada_body(c_ref, w_ref, b_ref, o_ref):
    o_ref[0] = _dot3(_silu(c_ref[...]), w_ref[0]) + b_ref[0]


def _ada(c_all, w_ada, b_ada):
    depth, d, n6 = w_ada.shape
    m = c_all.shape[0]
    tn = 1024
    blocks = _nbytes((m, d)) + _nbytes((d, tn)) + _nbytes((m, tn))
    return pl.pallas_call(
        _ada_body, name="ada",
        grid=(depth, n6 // tn),
        in_specs=[pl.BlockSpec((m, d), lambda l, j: (0, 0)),
                  pl.BlockSpec((1, d, tn), lambda l, j: (l, 0, j)),
                  pl.BlockSpec((1, 1, tn), lambda l, j: (l, 0, j))],
        out_specs=pl.BlockSpec((1, m, tn), lambda l, j: (l, 0, j)),
        out_shape=jax.ShapeDtypeStruct((depth, m, n6), F32),
        compiler_params=_cparams(2, blocks, temp_bytes=3 * _nbytes((d, tn))),
    )(c_all, w_ada, b_ada.reshape(depth, 1, n6))


def _wparts(x):
    return x if isinstance(x, tuple) else (x,)


def _split_bf16(w):
    hi = w.astype(BF16)
    return hi, (w - hi.astype(F32)).astype(BF16)


def _dot_parts(a, b):
    if len(b) == 1:
        return _dot(a[0], b[0])
    return _dot3p(a, b)


INPROJ_SUB = 512


def _inproj_body(n_w, x_ref, sh_ref, sc_ref, g_ref, *refs):
    w_refs, ws_refs = refs[:n_w], refs[n_w:2 * n_w]
    o_ref, os_ref, h_scr = refs[2 * n_w:]
    tm = h_scr.shape[1]
    sb, rb, _ = x_ref.shape

    @pl.when(pl.program_id(1) == 0)
    def _():
        for r0 in range(0, tm, INPROJ_SUB):
            rows = min(INPROJ_SUB, tm)
            if sb == 1:
                x = x_ref[:, r0:r0 + rows, :]
                sc, sh = sc_ref[...], sh_ref[...]
            else:
                s0, s1 = r0 // rb, (r0 + rows) // rb
                x = x_ref[s0:s1]
                sc, sh = sc_ref[s0:s1], sh_ref[s0:s1]
            h = _rms_rows(x, g_ref[...]) * (1.0 + sc) + sh
            h = h.reshape(rows, h.shape[-1])
            hs = _split2(h) if n_w == 2 else (h.astype(BF16),)
            for k in range(n_w):
                h_scr[k, r0:r0 + rows, :] = hs[k]
            os_ref[r0:r0 + rows, :] = _dot_parts(hs, tuple(r[...] for r in ws_refs))

    o_ref[...] = _dot_parts(tuple(h_scr[k] for k in range(n_w)), tuple(r[...] for r in w_refs))


def _inproj(x3, m3, chunk_shift, chunk_scale, gain, w, n_main, ws, sb, rb, tn=512):
    w, ws = _wparts(w), _wparts(ws)
    n_w = len(w)
    sq, rq, d = x3.shape
    nb = rq // rb
    tm = sb * rb
    n_tok = sq * rq
    ns = ws[0].shape[1]
    blocks = (_nbytes((tm, d)) + 2 * _nbytes((sb, V7X_SUBLANES, d))
              + n_w * (_nbytes((d, tn), BF16) + _nbytes((d, ns), BF16))
              + _nbytes((tm, tn)) + _nbytes((tm, ns)))
    w_specs = [pl.BlockSpec((d, tn), lambda i, j: (0, j)) for _ in w]
    ws_specs = [pl.BlockSpec((d, ns), lambda i, j: (0, 0)) for _ in ws]
    return pl.pallas_call(
        functools.partial(_inproj_body, n_w), name="inproj",
        grid=(n_tok // tm, n_main // tn),
        in_specs=[pl.BlockSpec((sb, rb, d), lambda i, j: (i // nb, i % nb, 0)),
                  pl.BlockSpec((sb, 1, d), lambda i, j: (i // nb, 0, chunk_shift)),
                  pl.BlockSpec((sb, 1, d), lambda i, j: (i // nb, 0, chunk_scale)),
                  pl.BlockSpec((1, d), lambda i, j: (0, 0))] + w_specs + ws_specs,
        out_specs=[pl.BlockSpec((tm, tn), lambda i, j: (i, j)),
                   pl.BlockSpec((tm, ns), lambda i, j: (i, 0))],
        out_shape=[jax.ShapeDtypeStruct((n_tok, n_main), F32),
                   jax.ShapeDtypeStruct((n_tok, ns), F32)],
        scratch_shapes=[pltpu.VMEM((n_w, tm, d), BF16)],
        compiler_params=_cparams(2, blocks, n_w * _nbytes((tm, d), BF16),
                                 4 * _nbytes((min(tm, INPROJ_SUB), d))),
    )(x3, m3, m3, gain.reshape(1, d), *w, *ws)


def _outproj_body(n_in, n_w, norm, *refs):
    o_refs = refs[:n_in]
    w_refs = refs[n_in:n_in + n_in * n_w]
    pos = n_in + n_in * n_w
    if norm:
        gn_ref = refs[pos]
        pos += 1
    x_ref, gate_ref, out_ref, o_scr = refs[pos:pos + 4]

    @pl.when(pl.program_id(1) == 0)
    def _():
        for k in range(n_in):
            o = o_refs[k][...]
            if norm:
                o = _rms_rows(o, gn_ref[...])
            parts = _split2(o) if n_w == 2 else (o.astype(BF16),)
            for p in range(n_w):
                o_scr[k, p] = parts[p]

    acc = None
    for k in range(n_in):
        part = _dot_parts(tuple(o_scr[k, p] for p in range(n_w)),
                          tuple(w_refs[k * n_w + p][...] for p in range(n_w)))
        acc = part if acc is None else acc + part
    x = x_ref[...]
    out_ref[...] = x + gate_ref[...] * acc.reshape(x.shape)


def _outproj(o_list, w, x3, m3, chunk_gate, sb, rb, norm_gain=None, tn=512):
    w = _wparts(w)
    n_w = len(w)
    sq, rq, d = x3.shape
    nb = rq // rb
    tm = sb * rb
    n_in = len(o_list)
    nj = d // tn
    widths = [o.shape[1] for o in o_list]
    assert all(wd == widths[0] for wd in widths) and sum(widths) == w[0].shape[0]
    kw = widths[0]
    in_specs = [pl.BlockSpec((tm, kw), lambda i, j: (i, 0)) for _ in o_list]
    args = list(o_list)
    for k in range(n_in):
        for part in w:
            in_specs.append(pl.BlockSpec((kw, tn), lambda i, j, k=k: (k, j)))
            args.append(part)
    if norm_gain is not None:
        in_specs.append(pl.BlockSpec((1, kw), lambda i, j: (0, 0)))
        args.append(norm_gain.reshape(1, kw))
    in_specs += [pl.BlockSpec((sb, rb, tn), lambda i, j: (i // nb, i % nb, j)),
                 pl.BlockSpec((sb, 1, tn), lambda i, j: (i // nb, 0, chunk_gate * nj + j))]
    args += [x3, m3]
    blocks = (n_in * (_nbytes((tm, kw)) + n_w * _nbytes((kw, tn), BF16))
              + 2 * _nbytes((tm, tn)) + _nbytes((sb, V7X_SUBLANES, tn)))
    scr = n_in * n_w * _nbytes((tm, kw), BF16)
    return pl.pallas_call(
        functools.partial(_outproj_body, n_in, n_w, norm_gain is not None), name="outproj",
        grid=((sq * rq) // tm, nj),
        in_specs=in_specs,
        out_specs=pl.BlockSpec((sb, rb, tn), lambda i, j: (i // nb, i % nb, j)),
        out_shape=jax.ShapeDtypeStruct(x3.shape, F32),
        scratch_shapes=[pltpu.VMEM((n_in, n_w, tm, kw), BF16)],
        compiler_params=_cparams(2, blocks, scr, 3 * _nbytes((tm, kw))),
    )(*args)


def _ret_tables(lam, c, width):
    lam1 = lam[:, :1]
    rel = (_iota((c, c), 0) - _iota((c, c), 1)).astype(F32)
    din = jnp.where(rel >= 0, jnp.exp(lam1 * jnp.maximum(rel, 0.0)), 0.0)
    r = _iota((c, width), 0).astype(F32)
    dq = jnp.exp(lam * (r + 1.0))
    dk = jnp.exp(lam * (c - 1.0 - r))
    gc = jnp.exp(lam * float(c))
    return din, dq, dk, gc


def _gla_log_alpha(alow, w2, b2):
    xa = _dot3(alow, w2) + b2
    return _log_sigmoid(xa) * (1.0 / GLA_TEMP)


def _mixa_prompt_body(qr, kr, vr, gr, qg, kg, vg, gg, alow, w2, b2, cos, sin, gnr, gng, lam,
                      o_r, o_g, sret_out, sgla_out,
                      s_ret, st_gla, din_s, dq_s, dk_s, b_s, a_s):
    c_id = pl.program_id(2)
    n_c = pl.num_programs(2)
    c = qr.shape[0]
    dk_r = qr.shape[1]
    dk_g = qg.shape[1]
    lam_row = lam[0]

    @pl.when(c_id == 0)
    def _():
        s_ret[...] = jnp.zeros_like(s_ret)
        st_gla[...] = jnp.zeros_like(st_gla)
        din, dq, dk, _ = _ret_tables(lam_row, c, dk_r)
        din_s[...] = din
        dq_s[...] = dq
        dk_s[...] = dk

    cs, sn = cos[...], sin[...]
    q = _rot(qr[...], cs, sn)
    k = _rot(kr[...], cs, sn) * (dk_r ** -0.5)
    qs, vs = _split2(q), _split2(vr[...])
    sc = _dot3_nt(q, k) * din_s[...]
    s_old = s_ret[...]
    o = _dot3p(_split2(sc), vs) + _dot3p(qs, _split2(s_old)) * dq_s[...]
    kd = k * dk_s[...]
    s_ret[...] = s_old * jnp.exp(lam_row * float(c)) + _dot3p(_split2(kd.T), vs)
    o_r[...] = _head_norm(o, gnr[...], gr[...], True)

    la = _gla_log_alpha(alow[...], w2[...], b2[...])
    b_s[...] = _dot01_left(_tril01(c), la)
    bc = b_s[...]
    qg_v = qg[...] * (dk_g ** -0.5)
    kg_v = kg[...]
    rows = _iota((c, dk_g), 0)
    nblk = c // GLA_BLOCK
    row16 = _iota((GLA_BLOCK, dk_g), 0)
    lane16 = _iota((GLA_BLOCK, c), 1)
    for blk in range(nblk):
        r0 = blk * GLA_BLOCK
        q_i = qg_v[r0:r0 + GLA_BLOCK]
        b_i = bc[r0:r0 + GLA_BLOCK]
        if blk == 0:
            a_i = jnp.zeros((GLA_BLOCK, c), F32)
        else:
            ref_b = bc[r0 - 1:r0]
            qt = q_i * jnp.exp(b_i - ref_b)
            kt = jnp.where(rows < r0, kg_v * jnp.exp(jnp.minimum(ref_b - bc, 0.0)), 0.0)
            a_i = _dot3_nt(qt, kt)
        for jj in range(GLA_BLOCK):
            j = r0 + jj
            e = jnp.exp(jnp.minimum(b_i - bc[j:j + 1], 0.0))
            col = jnp.sum(q_i * e * kg_v[j:j + 1], axis=-1, keepdims=True)
            col = jnp.where(row16[:, :1] >= jj, col, 0.0)
            a_i = a_i + jnp.where(lane16 == j, col, 0.0)
        a_s[r0:r0 + GLA_BLOCK, :] = a_i
    st_old = st_gla[...]
    og = _dot3(a_s[...], vg[...]) + _dot3_nt(qg_v * jnp.exp(bc), st_old)
    b_last = bc[c - 1:c]
    kdg = kg_v * jnp.exp(b_last - bc)
    st_gla[...] = st_old * jnp.exp(b_last) + _dot3(vg[...].T, kdg)
    o_g[...] = _head_norm(og, gng[...], gg[...], False)

    @pl.when(c_id == n_c - 1)
    def _():
        sret_out[0, 0] = s_ret[...]
        sgla_out[0, 0] = st_gla[...].T


def _mixa_prompt(pa, pa_low, w2p, b2, cos, sin, gnr, gng, lam, batch, seq):
    c = MIX_CHUNK
    nc = seq // c
    h = RET_HEADS
    dkr = 256
    dkg = 128
    dv = 256

    def rows(b, hh, cc):
        return b * nc + cc

    def col(width, base):
        return pl.BlockSpec((c, width), lambda b, hh, cc: (rows(b, hh, cc), base // width + hh))

    in_specs = [col(dkr, 0), col(dkr, 1024), col(dv, 2048), col(dv, 3072),
                col(dkg, 4096), col(dkg, 4608), col(dv, 5120), col(dv, 6144),
                pl.BlockSpec((c, V7X_LANES), lambda b, hh, cc: (rows(b, hh, cc), 0)),
                pl.BlockSpec((V7X_LANES, dkg), lambda b, hh, cc: (0, hh)),
                pl.BlockSpec((1, dkg), lambda b, hh, cc: (0, hh)),
                pl.BlockSpec((c, dkr // 2), lambda b, hh, cc: (cc, 0)),
                pl.BlockSpec((c, dkr // 2), lambda b, hh, cc: (cc, 0)),
                pl.BlockSpec((1, dv), lambda b, hh, cc: (0, hh)),
                pl.BlockSpec((1, dv), lambda b, hh, cc: (0, hh)),
                pl.BlockSpec((1, 1, dkr), lambda b, hh, cc: (hh, 0, 0))]
    out_specs = [pl.BlockSpec((c, dv), lambda b, hh, cc: (rows(b, hh, cc), hh)),
                 pl.BlockSpec((c, dv), lambda b, hh, cc: (rows(b, hh, cc), hh)),
                 pl.BlockSpec((1, 1, dkr, dv), lambda b, hh, cc: (b, hh, 0, 0)),
                 pl.BlockSpec((1, 1, dkg, dv), lambda b, hh, cc: (b, hh, 0, 0))]
    n_tok = batch * seq
    out_shape = [jax.ShapeDtypeStruct((n_tok, h * dv), F32),
                 jax.ShapeDtypeStruct((n_tok, h * dv), F32),
                 jax.ShapeDtypeStruct((batch, h, dkr, dv), F32),
                 jax.ShapeDtypeStruct((batch, h, dkg, dv), F32)]
    scratch = [pltpu.VMEM((dkr, dv), F32), pltpu.VMEM((dv, dkg), F32),
               pltpu.VMEM((c, c), F32), pltpu.VMEM((c, dkr), F32), pltpu.VMEM((c, dkr), F32),
               pltpu.VMEM((c, dkg), F32), pltpu.VMEM((c, c), F32)]
    blocks = 12 * _nbytes((c, dv)) + 2 * _nbytes((dkr, dv))
    return pl.pallas_call(
        _mixa_prompt_body, name="mixa_prompt",
        grid=(batch, h, nc),
        in_specs=in_specs, out_specs=out_specs, out_shape=out_shape,
        scratch_shapes=scratch,
        compiler_params=_cparams(3, blocks, 8 * _nbytes((dkr, dv)), 16 * 2 ** 20),
    )(pa, pa, pa, pa, pa, pa, pa, pa, pa_low, w2p, b2, cos, sin, gnr, gng, lam)


def _direct_intra(q, k, v, decay_fn):
    t = q.shape[0]
    row = _iota((t, 1), 0)
    o = jnp.zeros((t, v.shape[1]), F32)
    for j in range(t):
        col = jnp.sum(q * decay_fn(j) * k[j:j + 1], axis=-1, keepdims=True)
        col = jnp.where(row >= j, col, 0.0)
        o = o + col * v[j:j + 1]
    return o


def _mixa_sample_body(sb, pa, alow, w2, b2, cos, sin, gnr, gng, lam, sret_in, sgla_in,
                      o_r, o_g, sret_out, sgla_out):
    t = 8
    dkr, dkg, dv = 256, 128, 256
    cs, sn = cos[...], sin[...]
    la = _gla_log_alpha(alow[...], w2[...], b2[...])
    bc_all = _dot01_left(_tril01(sb * t, block=t), la)
    rel = (_iota((t, 1), 0)).astype(F32)
    for hh in range(RET_HEADS):
        lam_row = lam[hh]
        dq = jnp.exp(lam_row * (rel + 1.0))
        dk = jnp.exp(lam_row * (t - 1.0 - rel))
        gc = jnp.exp(lam_row * float(t))
        for s in range(sb):
            r0 = s * t
            q = _rot(pa[r0:r0 + t, hh * dkr:(hh + 1) * dkr], cs, sn)
            k = _rot(pa[r0:r0 + t, 1024 + hh * dkr:1024 + (hh + 1) * dkr], cs, sn) * (dkr ** -0.5)
            v = pa[r0:r0 + t, 2048 + hh * dv:2048 + (hh + 1) * dv]
            g = pa[r0:r0 + t, 3072 + hh * dv:3072 + (hh + 1) * dv]
            s_old = sret_in[s, hh]
            o = _direct_intra(q, k, v, lambda j: jnp.exp(lam_row * jnp.maximum(rel - float(j), 0.0)))
            o = o + _dot3(q, s_old) * dq
            kdt = _pad_rows(k * dk, V7X_LANES).T
            sret_out[s, hh] = s_old * gc + _dot3(kdt, _pad_rows(v, V7X_LANES))
            o_r[r0:r0 + t, hh * dv:(hh + 1) * dv] = _head_norm(
                o, gnr[:, hh * dv:(hh + 1) * dv], g, True)
            qg = pa[r0:r0 + t, 4096 + hh * dkg:4096 + (hh + 1) * dkg] * (dkg ** -0.5)
            kg = pa[r0:r0 + t, 4608 + hh * dkg:4608 + (hh + 1) * dkg]
            vg = pa[r0:r0 + t, 5120 + hh * dv:5120 + (hh + 1) * dv]
            gg = pa[r0:r0 + t, 6144 + hh * dv:6144 + (hh + 1) * dv]
            bc = bc_all[r0:r0 + t, hh * dkg:(hh + 1) * dkg]
            sg_old = sgla_in[s, hh]
            og = _direct_intra(qg, kg, vg, lambda j: jnp.exp(jnp.minimum(bc - bc[j:j + 1], 0.0)))
            og = og + _dot3(qg * jnp.exp(bc), sg_old)
            b_last = bc[t - 1:t]
            m = _pad_rows(jnp.concatenate([kg * jnp.exp(b_last - bc), jnp.exp(b_last)], axis=0),
                          V7X_LANES).T
            sgla_out[s, hh] = sg_old * m[:, t:t + 1] + _dot3(m, _pad_rows(vg, V7X_LANES))
            o_g[r0:r0 + t, hh * dv:(hh + 1) * dv] = _head_norm(
                og, gng[:, hh * dv:(hh + 1) * dv], gg, False)


def _mixa_sample(pa, pa_low, w2p, b2, cos, sin, gnr, gng, lam, sret, sgla, sb=4):
    nseq = sret.shape[0]
    t = 8
    h, dkr, dkg, dv = RET_HEADS, 256, 128, 256
    width = pa.shape[1]
    const2 = lambda i: (0, 0)
    in_specs = [pl.BlockSpec((sb * t, width), lambda i: (i, 0)),
                pl.BlockSpec((sb * t, V7X_LANES), lambda i: (i, 0)),
                pl.BlockSpec(w2p.shape, const2), pl.BlockSpec(b2.shape, const2),
                pl.BlockSpec(cos.shape, const2), pl.BlockSpec(sin.shape, const2),
                pl.BlockSpec(gnr.shape, const2), pl.BlockSpec(gng.shape, const2),
                pl.BlockSpec(lam.shape, lambda i: (0, 0, 0)),
                pl.BlockSpec((sb, h, dkr, dv), lambda i: (i, 0, 0, 0)),
                pl.BlockSpec((sb, h, dkg, dv), lambda i: (i, 0, 0, 0))]
    out_specs = [pl.BlockSpec((sb * t, h * dv), lambda i: (i, 0)),
                 pl.BlockSpec((sb * t, h * dv), lambda i: (i, 0)),
                 pl.BlockSpec((sb, h, dkr, dv), lambda i: (i, 0, 0, 0)),
                 pl.BlockSpec((sb, h, dkg, dv), lambda i: (i, 0, 0, 0))]
    out_shape = [jax.ShapeDtypeStruct((nseq * t, h * dv), F32),
                 jax.ShapeDtypeStruct((nseq * t, h * dv), F32),
                 jax.ShapeDtypeStruct(sret.shape, F32),
                 jax.ShapeDtypeStruct(sgla.shape, F32)]
    blocks = (_nbytes((sb * t, width)) + 2 * _nbytes((sb, h, dkr, dv))
              + 2 * _nbytes((sb, h, dkg, dv)) + 2 * 2 ** 20)
    return pl.pallas_call(
        functools.partial(_mixa_sample_body, sb), name="mixa_sample",
        grid=(nseq // sb,),
        in_specs=in_specs, out_specs=out_specs, out_shape=out_shape,
        compiler_params=_cparams(1, blocks, 0, 4 * 2 ** 20),
    )(pa, pa_low, w2p, b2, cos, sin, gnr, gng, lam, sret, sgla)


def _conv_prompt_body(x_ref, prev_ref, w_ref, b_ref, o_ref, scr):
    t = x_ref.shape[0]
    first = pl.program_id(1) == 0
    scr[0:V7X_SUBLANES, :] = jnp.where(first, 0.0, prev_ref[...])
    scr[V7X_SUBLANES:V7X_SUBLANES + t, :] = x_ref[...]
    acc = b_ref[...] + scr[V7X_SUBLANES - 3:V7X_SUBLANES - 3 + t, :] * w_ref[0:1, :]
    for j in range(1, CONV_W):
        off = V7X_SUBLANES - 3 + j
        acc = acc + scr[off:off + t, :] * w_ref[j:j + 1, :]
    o_ref[...] = _silu(acc)


def _conv_prompt(pc, conv_w, conv_b, batch, seq, col0, width, tc=512, tn=512):
    nc = seq // tc
    cb0 = col0 // tn
    blocks = 2 * _nbytes((tc, tn)) + _nbytes((V7X_SUBLANES, tn)) * 3
    return pl.pallas_call(
        _conv_prompt_body, name="conv_prompt",
        grid=(batch, nc, width // tn),
        in_specs=[pl.BlockSpec((tc, tn), lambda b, c, j: (b * nc + c, cb0 + j)),
                  pl.BlockSpec((V7X_SUBLANES, tn),
                               lambda b, c, j: (jnp.maximum((b * nc + c) * (tc // V7X_SUBLANES) - 1, 0),
                                                cb0 + j)),
                  pl.BlockSpec((CONV_W, tn), lambda b, c, j: (0, j)),
                  pl.BlockSpec((1, tn), lambda b, c, j: (0, j))],
        out_specs=pl.BlockSpec((tc, tn), lambda b, c, j: (b * nc + c, j)),
        out_shape=jax.ShapeDtypeStruct((batch * seq, width), F32),
        scratch_shapes=[pltpu.VMEM((tc + V7X_SUBLANES, tn), F32)],
        compiler_params=_cparams(3, blocks, _nbytes((tc + 8, tn)), 4 * _nbytes((tc, tn))),
    )(pc, pc, conv_w, conv_b.reshape(1, width))


def _conv_sample_body(x_ref, cache_ref, w_ref, b_ref, o_ref, scr):
    t = x_ref.shape[1]
    hist = CONV_W - 1
    scr[:, t - hist:t, :] = cache_ref[...]
    scr[:, t:2 * t, :] = x_ref[...]
    acc = b_ref[...] + scr[:, t - hist:2 * t - hist, :] * w_ref[0:1, :]
    for j in range(1, CONV_W):
        off = t - hist + j
        acc = acc + scr[:, off:off + t, :] * w_ref[j:j + 1, :]
    o_ref[...] = _silu(acc)


def _conv_sample(pc3, cache, conv_w, conv_b, col0, width, sbc=32, tn=512):
    nseq, t, _ = pc3.shape
    cb0 = col0 // tn
    blocks = 2 * _nbytes((sbc, t, tn)) + _nbytes((sbc, 8, tn))
    return pl.pallas_call(
        _conv_sample_body, name="conv_sample",
        grid=(nseq // sbc, width // tn),
        in_specs=[pl.BlockSpec((sbc, t, tn), lambda i, j: (i, 0, cb0 + j)),
                  pl.BlockSpec((sbc, CONV_W - 1, tn), lambda i, j: (i, 0, j)),
                  pl.BlockSpec((CONV_W, tn), lambda i, j: (0, j)),
                  pl.BlockSpec((1, tn), lambda i, j: (0, j))],
        out_specs=pl.BlockSpec((sbc, t, tn), lambda i, j: (i, 0, j)),
        out_shape=jax.ShapeDtypeStruct((nseq, t, width), F32),
        scratch_shapes=[pltpu.VMEM((sbc, 2 * t, tn), F32)],
        compiler_params=_cparams(2, blocks, _nbytes((sbc, 2 * t, tn)), 4 * _nbytes((sbc, t, tn))),
    )(pc3, cache, conv_w, conv_b.reshape(1, width))


def _expand01(heads, width):
    r = _iota((V7X_LANES, heads * width), 0)
    l = _iota((V7X_LANES, heads * width), 1)
    return jnp.where((l // width) == r, 1.0, 0.0).astype(BF16)


def _ssd_prompt_body(xs, bm, cm, z, dtc, dtr, bias_c, alog_c, bias_r, alog_r, dfull,
                     y_out, s_out, st):
    c_id = pl.program_id(2)
    n_c = pl.num_programs(2)
    l = xs.shape[0]
    heads = dtr.shape[1]
    p = xs.shape[1] // heads

    @pl.when(c_id == 0)
    def _():
        st[...] = jnp.zeros_like(st)

    dt_c = _softplus(dtc[...] + bias_c[0])
    cum_c = _dot01_left(_tril01(l), dt_c * (-jnp.exp(alog_c[0])))
    dt_r = _softplus(dtr[0] + bias_r[0])
    tri_u = jnp.where(_iota((l, l), 0) <= _iota((l, l), 1), 1.0, 0.0).astype(BF16)
    cum_r = _dot01_right(dt_r * (-jnp.exp(alog_r[0])), tri_u)

    x = xs[...]
    cb = _dot_nt(cm[...].astype(BF16), bm[...].astype(BF16))
    causal = _iota((l, l), 0) >= _iota((l, l), 1)
    lane = _iota((l, 2 * p), 1)
    y_tiles = []
    for pair in range(heads // 2):
        xt = x[:, pair * 2 * p:(pair + 1) * 2 * p]
        acc = None
        for sub in range(2):
            r = 2 * pair + sub
            seg = cum_c[:, r:r + 1] - cum_r[r:r + 1, :]
            w = jnp.where(causal, cb * jnp.exp(jnp.minimum(seg, 0.0)) * dt_r[r:r + 1, :], 0.0)
            xm = jnp.where((lane // p) == sub, xt, 0.0)
            part = _dot(w.astype(BF16), xm.astype(BF16))
            acc = part if acc is None else acc + part
        y_tiles.append(acc)
    y = jnp.concatenate(y_tiles, axis=-1)

    ex = _expand01(heads, p)
    cum_f = _dot01_right(cum_c, ex)
    dt_f = _dot01_right(dt_c, ex)
    st_old = st[...]
    y = y + _dot(cm[...].astype(BF16), st_old.astype(BF16)) * jnp.exp(cum_f)
    cl_f = cum_f[l - 1:l]
    tail = jnp.exp(cl_f - cum_f) * dt_f
    st[...] = st_old * jnp.exp(cl_f) + _dot(bm[...].T.astype(BF16), (x * tail).astype(BF16))
    y = y + dfull[0] * x
    y_out[...] = y * _silu(z[...])

    @pl.when(c_id == n_c - 1)
    def _():
        s_out[0] = st[...].T.reshape(s_out.shape[1:])


def _ssd_prompt(act, pc, pcs, dt_t, bias_c, alog_c, bias_r, alog_r, dfull, batch, seq):
    l = MIX_CHUNK
    nc = seq // l
    g = SSM_GROUPS
    hp = act.shape[1]
    n = SSM_STATE
    p = SSM_HEADDIM
    d_inner = pc.shape[1] - 6144
    gw = d_inner // g
    hpg = gw // p
    xb0 = 0
    bb0 = d_inner // n
    cb0 = (d_inner + g * n) // n

    def row(b, gg, cc):
        return b * nc + cc

    in_specs = [pl.BlockSpec((l, gw), lambda b, gg, cc: (row(b, gg, cc), gg)),
                pl.BlockSpec((l, n), lambda b, gg, cc: (row(b, gg, cc), bb0 + gg)),
                pl.BlockSpec((l, n), lambda b, gg, cc: (row(b, gg, cc), cb0 + gg)),
                pl.BlockSpec((l, gw), lambda b, gg, cc: (row(b, gg, cc), gg)),
                pl.BlockSpec((l, V7X_LANES), lambda b, gg, cc: (row(b, gg, cc), gg)),
                pl.BlockSpec((1, hpg, l), lambda b, gg, cc: (b, gg, cc)),
                pl.BlockSpec((1, 1, V7X_LANES), lambda b, gg, cc: (gg, 0, 0)),
                pl.BlockSpec((1, 1, V7X_LANES), lambda b, gg, cc: (gg, 0, 0)),
                pl.BlockSpec((1, hpg, l), lambda b, gg, cc: (gg, 0, 0)),
                pl.BlockSpec((1, hpg, l), lambda b, gg, cc: (gg, 0, 0)),
                pl.BlockSpec((1, 1, gw), lambda b, gg, cc: (gg, 0, 0))]
    out_specs = [pl.BlockSpec((l, gw), lambda b, gg, cc: (row(b, gg, cc), gg)),
                 pl.BlockSpec((1, hpg, p, n), lambda b, gg, cc: (b, gg, 0, 0))]
    out_shape = [jax.ShapeDtypeStruct((batch * seq, d_inner), F32),
                 jax.ShapeDtypeStruct((batch, g * hpg, p, n), F32)]
    del hp, xb0
    blocks = 3 * _nbytes((l, gw)) + 3 * _nbytes((l, n)) + _nbytes((hpg, p, n))
    return pl.pallas_call(
        _ssd_prompt_body, name="ssd_prompt",
        grid=(batch, g, nc),
        in_specs=in_specs, out_specs=out_specs, out_shape=out_shape,
        scratch_shapes=[pltpu.VMEM((n, gw), F32)],
        compiler_params=_cparams(3, blocks, _nbytes((n, gw)), 16 * 2 ** 20),
    )(act, act, act, pc, pcs, dt_t, bias_c, alog_c, bias_r, alog_r, dfull)


def _ssd_sample_body(sb, xs, bm, cm, z, dtc, bias_c, alog_c, dfull, s_in, y_out, s_out, y_s):
    t = 8
    rows = sb * t
    gw = xs.shape[1]
    hpg = s_in.shape[1]
    p = gw // hpg
    n = bm.shape[1]
    dt_c = _softplus(dtc[...] + bias_c[0])
    cum_c = _dot01_left(_tril01(rows, block=t), dt_c * (-jnp.exp(alog_c[0])))
    ex = _expand01(hpg, p)
    cum_f = _dot01_right(cum_c, ex)
    dt_f = _dot01_right(dt_c, ex)
    x = xs[...]
    x3 = x.reshape(sb, t, gw)
    c3 = cm[...].reshape(sb, t, n)
    b3 = bm[...].reshape(sb, t, n)
    cum3 = cum_f.reshape(sb, t, gw)
    dt3 = dt_f.reshape(sb, t, gw)
    rowi = _iota((sb, t, 1), 1)
    y3 = jnp.zeros((sb, t, gw), F32)
    for j in range(t):
        cbj = jnp.sum(c3 * b3[:, j:j + 1, :], axis=-1, keepdims=True)
        dec = jnp.exp(jnp.minimum(cum3 - cum3[:, j:j + 1, :], 0.0))
        w = jnp.where(rowi >= j, cbj * dec * dt3[:, j:j + 1, :], 0.0)
        y3 = y3 + w * x3[:, j:j + 1, :]
    y_s[...] = y3.reshape(rows, gw)
    cmv = cm[...]
    bmv = bm[...]
    for s in range(sb):
        r0 = s * t
        sg = s_in[s].reshape(hpg * p, n)
        cf = cum_f[r0:r0 + t]
        yi = _dot_nt(cmv[r0:r0 + t].astype(BF16), sg.astype(BF16)) * jnp.exp(cf)
        cl = cf[t - 1:t]
        xt = x[r0:r0 + t] * (jnp.exp(cl - cf) * dt_f[r0:r0 + t])
        m = _pad_rows(jnp.concatenate([xt, jnp.exp(cl)], axis=0), V7X_LANES).T
        s_new = sg * m[:, t:t + 1] + _dot(m.astype(BF16), _pad_rows(bmv[r0:r0 + t], V7X_LANES).astype(BF16))
        s_out[s] = s_new.reshape(hpg, p, n)
        y_s[r0:r0 + t, :] = y_s[r0:r0 + t, :] + yi
    y_out[...] = (y_s[...] + dfull[0] * x) * _silu(z[...])


def _ssd_sample(act, pc, pcs, bias_c, alog_c, dfull, s_ssm, sb=8):
    t = 8
    nseq = s_ssm.shape[0]
    g = SSM_GROUPS
    n = SSM_STATE
    p = SSM_HEADDIM
    d_inner = pc.shape[1] - 6144
    gw = d_inner // g
    hpg = gw // p
    bb0 = d_inner // n
    cb0 = (d_inner + g * n) // n
    rows = sb * t
    in_specs = [pl.BlockSpec((rows, gw), lambda i, gg: (i, gg)),
                pl.BlockSpec((rows, n), lambda i, gg: (i, bb0 + gg)),
                pl.BlockSpec((rows, n), lambda i, gg: (i, cb0 + gg)),
                pl.BlockSpec((rows, gw), lambda i, gg: (i, gg)),
                pl.BlockSpec((rows, V7X_LANES), lambda i, gg: (i, gg)),
                pl.BlockSpec((1, 1, V7X_LANES), lambda i, gg: (gg, 0, 0)),
                pl.BlockSpec((1, 1, V7X_LANES), lambda i, gg: (gg, 0, 0)),
                pl.BlockSpec((1, 1, gw), lambda i, gg: (gg, 0, 0)),
                pl.BlockSpec((sb, hpg, p, n), lambda i, gg: (i, gg, 0, 0))]
    out_specs = [pl.BlockSpec((rows, gw), lambda i, gg: (i, gg)),
                 pl.BlockSpec((sb, hpg, p, n), lambda i, gg: (i, gg, 0, 0))]
    out_shape = [jax.ShapeDtypeStruct((nseq * t, d_inner), F32),
                 jax.ShapeDtypeStruct(s_ssm.shape, F32)]
    blocks = 3 * _nbytes((rows, gw)) + 3 * _nbytes((rows, n)) + 2 * _nbytes((sb, hpg, p, n))
    return pl.pallas_call(
        functools.partial(_ssd_sample_body, sb), name="ssd_sample",
        grid=(nseq // sb, g),
        in_specs=in_specs, out_specs=out_specs, out_shape=out_shape,
        scratch_shapes=[pltpu.VMEM((rows, gw), F32)],
        compiler_params=_cparams(2, blocks, _nbytes((rows, gw)), 8 * 2 ** 20),
    )(act, act, act, pc, pcs, bias_c, alog_c, dfull, s_ssm)


def _moe_pre_body(n_pt, xp_ref, shp_ref, scp_ref, xs_ref, shs_ref, scs_ref, g_ref, wr_ref,
                  h_out, route_out):
    i = pl.program_id(0)

    @pl.when(i < n_pt)
    def _():
        _moe_route(xp_ref, shp_ref, scp_ref, g_ref, wr_ref, h_out, route_out)

    @pl.when(i >= n_pt)
    def _():
        _moe_route(xs_ref, shs_ref, scs_ref, g_ref, wr_ref, h_out, route_out)


def _moe_route(x_ref, sh_ref, sc_ref, g_ref, wr_ref, h_out, route_out):
    x = x_ref[...]
    h = _rms_rows(x, g_ref[...]) * (1.0 + sc_ref[...]) + sh_ref[...]
    h = h.reshape(h_out.shape)
    h_out[...] = h
    logits = jnp.dot(h, wr_ref[...], precision=lax.Precision.HIGHEST, preferred_element_type=F32)
    tm = logits.shape[0]
    lane = _iota((tm, V7X_LANES), 1).astype(F32)
    neg = -jnp.inf
    lg = jnp.where(lane < MOE_GROUPS, logits, neg)
    mg = jnp.max(lg, axis=-1, keepdims=True)
    top_g = jnp.min(jnp.where(lg == mg, lane, float(V7X_LANES)), axis=-1, keepdims=True)
    p_top = 1.0 / jnp.sum(jnp.where(lane < MOE_GROUPS, jnp.exp(logits - mg), 0.0),
                          axis=-1, keepdims=True)
    lo = MOE_GROUPS + MOE_PER_GROUP * top_g
    le = jnp.where((lane >= lo) & (lane < lo + MOE_PER_GROUP), logits, neg)
    v1 = jnp.max(le, axis=-1, keepdims=True)
    i1 = jnp.min(jnp.where(le == v1, lane, float(V7X_LANES)), axis=-1, keepdims=True)
    le2 = jnp.where(lane == i1, neg, le)
    v2 = jnp.max(le2, axis=-1, keepdims=True)
    i2 = jnp.min(jnp.where(le2 == v2, lane, float(V7X_LANES)), axis=-1, keepdims=True)
    tt = jnp.exp(v2 - v1)
    w1 = p_top / (1.0 + tt)
    w2 = p_top * tt / (1.0 + tt)
    route_out[...] = jnp.where(lane == 0, i1 - MOE_GROUPS,
                     jnp.where(lane == 1, i2 - MOE_GROUPS,
                     jnp.where(lane == 2, w1, jnp.where(lane == 3, w2, 0.0))))


def _moe_pre(xp, mp, xs_, ms, chunk_shift, chunk_scale, gain, wr, tm):
    bp, tp, d = xp.shape
    bs, ts, _ = xs_.shape
    nb = tp // tm
    n_pt = bp * nb
    sbs = tm // ts
    n_st = bs // sbs
    n_all = bp * tp + bs * ts

    def pi(i):
        return jnp.minimum(i, n_pt - 1)

    def si(i):
        return jnp.maximum(i - n_pt, 0)

    in_specs = [pl.BlockSpec((1, tm, d), lambda i: (pi(i) // nb, pi(i) % nb, 0)),
                pl.BlockSpec((1, 1, d), lambda i: (pi(i) // nb, 0, chunk_shift)),
                pl.BlockSpec((1, 1, d), lambda i: (pi(i) // nb, 0, chunk_scale)),
                pl.BlockSpec((sbs, ts, d), lambda i: (si(i), 0, 0)),
                pl.BlockSpec((sbs, 1, d), lambda i: (si(i), 0, chunk_shift)),
                pl.BlockSpec((sbs, 1, d), lambda i: (si(i), 0, chunk_scale)),
                pl.BlockSpec((1, d), lambda i: (0, 0)),
                pl.BlockSpec((d, V7X_LANES), lambda i: (0, 0))]
    blocks = (3 * _nbytes((tm, d)) + 2 * _nbytes((sbs, 8, d)) + _nbytes((d, V7X_LANES))
              + _nbytes((tm, 128)))
    return pl.pallas_call(
        functools.partial(_moe_pre_body, n_pt), name="moe_pre",
        grid=(n_pt + n_st,),
        in_specs=in_specs,
        out_specs=[pl.BlockSpec((tm, d), lambda i: (i, 0)),
                   pl.BlockSpec((tm, V7X_LANES), lambda i: (i, 0))],
        out_shape=[jax.ShapeDtypeStruct((n_all, d), F32),
                   jax.ShapeDtypeStruct((n_all, V7X_LANES), F32)],
        compiler_params=_cparams(1, blocks, 0, 4 * _nbytes((tm, d))),
    )(xp, mp, mp, xs_, ms, ms, gain.reshape(1, d), wr)


def _rank_body(route_ref, rank_ref, cnt_ref, carry):
    @pl.when(pl.program_id(0) == 0)
    def _():
        carry[...] = jnp.zeros_like(carry)

    r = route_ref[...]
    tm = r.shape[0]
    lane = _iota((tm, V7X_LANES), 1).astype(F32)
    o1 = lane == r[:, 0:1]
    o2 = lane == r[:, 1:2]
    oh = jnp.where(o1, 1.0, 0.0) + jnp.where(o2, 1.0, 0.0)
    pex = _dot(_tril01(tm, strict=True), oh.astype(BF16)) + carry[0:1, :]
    r1 = jnp.sum(jnp.where(o1, pex, 0.0), axis=-1, keepdims=True)
    r2 = jnp.sum(jnp.where(o2, pex, 0.0), axis=-1, keepdims=True)
    rank_ref[...] = jnp.where(lane == 0, r1, jnp.where(lane == 1, r2, 0.0))
    carry[...] = carry[...] + jnp.sum(oh, axis=0, keepdims=True)
    cnt_ref[...] = carry[...]


def _rank(route, tm=256):
    n = route.shape[0]
    return pl.pallas_call(
        _rank_body, name="moe_rank",
        grid=(n // tm,),
        in_specs=[pl.BlockSpec((tm, V7X_LANES), lambda i: (i, 0))],
        out_specs=[pl.BlockSpec((tm, V7X_LANES), lambda i: (i, 0)),
                   pl.BlockSpec((V7X_SUBLANES, V7X_LANES), lambda i: (0, 0))],
        out_shape=[jax.ShapeDtypeStruct((n, V7X_LANES), F32),
                   jax.ShapeDtypeStruct((V7X_SUBLANES, V7X_LANES), F32)],
        scratch_shapes=[pltpu.VMEM((V7X_SUBLANES, V7X_LANES), F32)],
        compiler_params=_cparams(1, 2 * _nbytes((tm, 128))),
    )(route)


def _row_copy(src_hbm, dst_vmem, sem, src_row, dst_row):
    return pltpu.make_async_copy(src_hbm.at[pl.ds(src_row, 1), :],
                                 dst_vmem.at[pl.ds(dst_row, 1), :], sem)


def _gather_body(inv_ref, nt_ref, h_hbm, o_ref, sem):
    i = pl.program_id(0)
    tg = o_ref.shape[0]
    active = i < nt_ref[0]

    @pl.when(active)
    def _():
        def issue(r, carry):
            _row_copy(h_hbm, o_ref, sem, inv_ref[i * tg + r], r).start()
            return carry

        lax.fori_loop(0, tg, issue, 0, unroll=DMA_UNROLL)
        pltpu.make_async_copy(h_hbm.at[pl.ds(0, tg), :], o_ref, sem).wait()

    @pl.when(jnp.logical_not(active))
    def _():
        o_ref[...] = jnp.zeros_like(o_ref)


def _gather_rows(inv, ntiles, h_all, p_pad):
    d = h_all.shape[1]
    tg = MOE_TILE
    return pl.pallas_call(
        _gather_body, name="moe_gather",
        grid_spec=pltpu.PrefetchScalarGridSpec(
            num_scalar_prefetch=2,
            grid=(p_pad // tg,),
            in_specs=[pl.BlockSpec(memory_space=pl.ANY)],
            out_specs=pl.BlockSpec((tg, d), lambda i, inv_r, nt_r: (i, 0)),
            scratch_shapes=[pltpu.SemaphoreType.DMA(())]),
        out_shape=jax.ShapeDtypeStruct((p_pad, d), F32),
        compiler_params=_cparams(1, _nbytes((tg, d)), disable_bounds_checks=True),
    )(inv, ntiles, h_all)


def _expert_body(te_ref, nt_ref, x_ref, wg_ref, wu_ref, wd_ref, o_ref, wg_b, wu_b, wd_b):
    i = pl.program_id(0)
    active = i < nt_ref[0]
    changed = jnp.logical_or(i == 0, te_ref[i] != te_ref[jnp.maximum(i - 1, 0)])

    @pl.when(jnp.logical_and(active, changed))
    def _():
        wg_b[...] = wg_ref[0, 0].astype(BF16)
        wu_b[...] = wu_ref[0, 0].astype(BF16)
        wd_b[...] = wd_ref[0, 0].astype(BF16)

    @pl.when(active)
    def _():
        xb = x_ref[...].astype(BF16)
        hid = _silu(_dot(xb, wg_b[...])) * _dot(xb, wu_b[...])
        o_ref[...] = _dot(hid.astype(BF16), wd_b[...])

    @pl.when(jnp.logical_not(active))
    def _():
        o_ref[...] = jnp.zeros_like(o_ref)


def _experts(tile_expert, ntiles, xs, wg, wu, wd, layer):
    p_pad, d = xs.shape
    ff = wg.shape[3]
    tm = MOE_TILE
    blocks = 2 * _nbytes((tm, d)) + 3 * _nbytes((d, ff))
    return pl.pallas_call(
        _expert_body, name="moe_experts",
        grid_spec=pltpu.PrefetchScalarGridSpec(
            num_scalar_prefetch=2,
            grid=(p_pad // tm,),
            in_specs=[pl.BlockSpec((tm, d), lambda i, te, nt: (i, 0)),
                      pl.BlockSpec((1, 1, d, ff), lambda i, te, nt: (layer, te[i], 0, 0)),
                      pl.BlockSpec((1, 1, d, ff), lambda i, te, nt: (layer, te[i], 0, 0)),
                      pl.BlockSpec((1, 1, ff, d), lambda i, te, nt: (layer, te[i], 0, 0))],
            out_specs=pl.BlockSpec((tm, d), lambda i, te, nt: (i, 0)),
            scratch_shapes=[pltpu.VMEM((d, ff), BF16), pltpu.VMEM((d, ff), BF16),
                            pltpu.VMEM((ff, d), BF16)]),
        out_shape=jax.ShapeDtypeStruct((p_pad, d), F32),
        compiler_params=_cparams(1, blocks, 3 * _nbytes((d, ff), BF16), 4 * _nbytes((tm, d))),
    )(tile_expert, ntiles, xs, wg, wu, wd)


def _combine_body(tok_off, final, pos_ref, eo_hbm, route_ref, x_ref, gate_ref, fn_ref, out_ref,
                  buf, sem):
    i = pl.program_id(0)
    tc = route_ref.shape[0]
    base = 2 * (tok_off + i * tc)

    def issue(r, carry):
        _row_copy(eo_hbm, buf.at[0], sem, pos_ref[base + 2 * r], r).start()
        _row_copy(eo_hbm, buf.at[1], sem, pos_ref[base + 2 * r + 1], r).start()
        return carry

    lax.fori_loop(0, tc, issue, 0, unroll=DMA_UNROLL // 2)
    for slot in range(2):
        pltpu.make_async_copy(eo_hbm.at[pl.ds(0, tc), :], buf.at[slot], sem).wait()
    rt = route_ref[...]
    y = rt[:, 2:3] * buf[0] + rt[:, 3:4] * buf[1]
    x = x_ref[...]
    xn = x + gate_ref[...] * y.reshape(x.shape)
    if final:
        xn = _rms_rows(xn, fn_ref[...])
    out_ref[...] = xn


def _combine(pos_flat, eo, route, x3, m3, chunk_gate, sb, rb, tok_off, final_gain=None):
    sq, rq, d = x3.shape
    nb = rq // rb
    tc = sb * rb
    blk_off = tok_off // tc
    final = final_gain is not None
    fn = (final_gain if final else jnp.ones((d,), F32)).reshape(1, d)
    blocks = 2 * _nbytes((tc, d)) + _nbytes((tc, 128)) + _nbytes((sb, 8, d))
    return pl.pallas_call(
        functools.partial(_combine_body, tok_off, final), name="moe_combine",
        grid_spec=pltpu.PrefetchScalarGridSpec(
            num_scalar_prefetch=1,
            grid=((sq * rq) // tc,),
            in_specs=[pl.BlockSpec(memory_space=pl.ANY),
                      pl.BlockSpec((tc, V7X_LANES), lambda i, p: (blk_off + i, 0)),
                      pl.BlockSpec((sb, rb, d), lambda i, p: (i // nb, i % nb, 0)),
                      pl.BlockSpec((sb, 1, d), lambda i, p: (i // nb, 0, chunk_gate)),
                      pl.BlockSpec((1, d), lambda i, p: (0, 0))],
            out_specs=pl.BlockSpec((sb, rb, d), lambda i, p: (i // nb, i % nb, 0)),
            scratch_shapes=[pltpu.VMEM((2, tc, d), F32), pltpu.SemaphoreType.DMA(())]),
        out_shape=jax.ShapeDtypeStruct(x3.shape, F32),
        compiler_params=_cparams(1, blocks, 2 * _nbytes((tc, d)), 4 * _nbytes((tc, d)),
                                 disable_bounds_checks=True),
    )(pos_flat, eo, route, x3, m3, fn)


def _moe_layer(xp, xs_, mp, ms, l, norm_moe, w_group, w_expert, w_gate, w_up, w_down, final_gain):
    bp, tp, d = xp.shape
    bs, ts, _ = xs_.shape
    n_p, n_s = bp * tp, bs * ts
    n_all = n_p + n_s
    wr = jnp.concatenate([w_group[l], jnp.transpose(w_expert[l], (1, 0, 2)).reshape(d, MOE_EXPERTS)],
                         axis=1)
    wr = jnp.pad(wr, ((0, 0), (0, V7X_LANES - wr.shape[1])))
    tm = MOE_TILE
    h_all, route = _moe_pre(xp, mp, xs_, ms, 3, 4, norm_moe[l], wr, tm)
    rank, cnt = _rank(route)
    counts = cnt[0, :MOE_EXPERTS].astype(I32)
    tiles_per = (counts + tm - 1) // tm
    tile_end = jnp.cumsum(tiles_per)
    row_start = (tile_end - tiles_per) * tm
    ntiles = tile_end[-1]
    e_idx = route[:, 0:2].astype(I32)
    pos = row_start[e_idx] + rank[:, 0:2].astype(I32)
    nt_max = (2 * n_all) // tm + MOE_EXPERTS
    p_pad = nt_max * tm
    tile_ids = jnp.arange(nt_max, dtype=I32)
    te = jnp.minimum(jnp.sum((tile_ids[:, None] >= tile_end[None, :]).astype(I32), axis=1),
                     MOE_EXPERTS - 1)
    te = jnp.where(tile_ids < ntiles, te, te[jnp.maximum(ntiles - 1, 0)])
    pos_flat = pos.reshape(-1)
    inv = jnp.zeros((p_pad,), I32).at[pos_flat].set(jnp.arange(2 * n_all, dtype=I32) // 2)
    nt_arr = ntiles.reshape(1).astype(I32)
    xsorted = _gather_rows(inv, nt_arr, h_all, p_pad)
    eo = _experts(te, nt_arr, xsorted, w_gate, w_up, w_down, l)
    xp_new = _combine(pos_flat, eo, route, xp, mp, 5, 1, tm, 0, final_gain)
    xs_new = _combine(pos_flat, eo, route, xs_, ms, 5, tm // ts, ts, n_p, final_gain)
    return xp_new, xs_new


def kernel(x_prompt, x_sample, c_prompt, c_sample, state_ret, state_gla, state_ssm, cache_conv,
           w_ada, b_ada, norm_mix, norm_moe,
           a_w_in, a_ret_gn, a_gla_w2, a_gla_b2, a_gla_gn, a_w_out,
           c_w_in, c_conv_w, c_conv_b, c_dt_bias, c_a_log, c_d, c_norm, c_w_out,
           moe_w_group, moe_w_expert, moe_w_gate, moe_w_up, moe_w_down, final_norm):
    bp, tp, d = x_prompt.shape
    bs, ts, _ = x_sample.shape
    n_p, n_s = bp * tp, bs * ts

    pad = (-bp) % V7X_SUBLANES
    c_all = jnp.concatenate([c_prompt, jnp.zeros((pad, d), F32), c_sample], axis=0)
    m_all = _ada(c_all, w_ada, b_ada)
    mp = [m_all[l, :bp].reshape(bp, 1, 6 * d) for l in range(2)]
    ms = [m_all[l, bp + pad:].reshape(bs, 1, 6 * d) for l in range(2)]

    tm_p, tm_s = 512, 512

    wa = a_w_in[0]
    n_main_a = (wa.shape[1] // 512) * 512
    wa_low = jnp.pad(wa[:, n_main_a:], ((0, 0), (0, V7X_LANES - (wa.shape[1] - n_main_a))))
    w2p = jnp.pad(a_gla_w2[0], ((0, V7X_LANES - GLA_RANK), (0, 0)))
    b2 = a_gla_b2[0].reshape(1, -1)
    gnr = a_ret_gn[0].reshape(1, -1)
    gng = a_gla_gn[0].reshape(1, -1)
    half = 128
    inv_freq = ROPE_BASE ** (-jnp.arange(half, dtype=F32) / half)
    ang_p = jnp.arange(tp, dtype=F32)[:, None] * inv_freq[None, :]
    ang_s = (PAST_LEN + jnp.arange(ts, dtype=F32))[:, None] * inv_freq[None, :]
    log_gamma = jnp.log(1.0 - 2.0 ** (-5.0 - jnp.arange(RET_HEADS, dtype=F32)))
    lam = jnp.broadcast_to(log_gamma[:, None, None], (RET_HEADS, 1, 256))

    wa_parts, wa_low_parts, wo_parts = _split_bf16(wa), _split_bf16(wa_low), _split_bf16(a_w_out[0])
    pa_p, low_p = _inproj(x_prompt, mp[0], 0, 1, norm_mix[0], wa_parts, n_main_a, wa_low_parts, 1, tm_p)
    pa_s, low_s = _inproj(x_sample, ms[0], 0, 1, norm_mix[0], wa_parts, n_main_a, wa_low_parts,
                          tm_s // ts, ts)
    or_p, og_p, ret_p, gla_p = _mixa_prompt(pa_p, low_p, w2p, b2, jnp.cos(ang_p), jnp.sin(ang_p),
                                            gnr, gng, lam, bp, tp)
    or_s, og_s, ret_s, gla_s = _mixa_sample(pa_s, low_s, w2p, b2, jnp.cos(ang_s), jnp.sin(ang_s),
                                            gnr, gng, lam, state_ret[0], state_gla[0])
    xp = _outproj([or_p, og_p], wo_parts, x_prompt, mp[0], 2, 1, tm_p)
    xs_ = _outproj([or_s, og_s], wo_parts, x_sample, ms[0], 2, tm_s // ts, ts)
    xp, xs_ = _moe_layer(xp, xs_, mp[0], ms[0], 0, norm_moe, moe_w_group, moe_w_expert,
                         moe_w_gate, moe_w_up, moe_w_down, None)

    wc = c_w_in[0]
    n_main_c = (wc.shape[1] // 512) * 512
    heads = wc.shape[1] - n_main_c
    hpg = heads // SSM_GROUPS
    d_inner = heads * SSM_HEADDIM
    conv_dim = n_main_c - d_inner
    wdt = jnp.pad(wc[:, n_main_c:].reshape(d, SSM_GROUPS, hpg),
                  ((0, 0), (0, 0), (0, V7X_LANES - hpg))).reshape(d, SSM_GROUPS * V7X_LANES).astype(BF16)
    wc_b = wc.astype(BF16)
    pc_p, dt_p = _inproj(x3=xp, m3=mp[1], chunk_shift=0, chunk_scale=1, gain=norm_mix[1], w=wc_b,
                         n_main=n_main_c, ws=wdt, sb=1, rb=2 * tm_p)
    pc_s, dt_s = _inproj(x3=xs_, m3=ms[1], chunk_shift=0, chunk_scale=1, gain=norm_mix[1], w=wc_b,
                         n_main=n_main_c, ws=wdt, sb=tm_s // ts, rb=ts)
    act_p = _conv_prompt(pc_p, c_conv_w[0], c_conv_b[0], bp, tp, d_inner, conv_dim)
    act_s = _conv_sample(pc_s.reshape(bs, ts, n_main_c), cache_conv[0], c_conv_w[0], c_conv_b[0],
                         d_inner, conv_dim).reshape(n_s, conv_dim)
    conv_p = pc_p.reshape(bp, tp, n_main_c)[:, tp - (CONV_W - 1):, d_inner:]
    conv_s = pc_s.reshape(bs, ts, n_main_c)[:, ts - (CONV_W - 1):, d_inner:]

    def lanes_pad(v):
        return jnp.pad(v.reshape(SSM_GROUPS, 1, hpg), ((0, 0), (0, 0), (0, V7X_LANES - hpg)))

    bias_c, alog_c = lanes_pad(c_dt_bias[0]), lanes_pad(c_a_log[0])
    bias_r = jnp.broadcast_to(c_dt_bias[0].reshape(SSM_GROUPS, hpg, 1), (SSM_GROUPS, hpg, MIX_CHUNK))
    alog_r = jnp.broadcast_to(c_a_log[0].reshape(SSM_GROUPS, hpg, 1), (SSM_GROUPS, hpg, MIX_CHUNK))
    dfull = jnp.repeat(c_d[0], SSM_HEADDIM).reshape(SSM_GROUPS, 1, hpg * SSM_HEADDIM)
    dt_t = jnp.transpose(dt_p.reshape(bp, tp, SSM_GROUPS, V7X_LANES)[..., :hpg].reshape(bp, tp, heads),
                         (0, 2, 1))
    yg_p, ssm_p = _ssd_prompt(act_p, pc_p, dt_p, dt_t, bias_c, alog_c, bias_r, alog_r, dfull, bp, tp)
    yg_s, ssm_s = _ssd_sample(act_s, pc_s, dt_s, bias_c, alog_c, dfull, state_ssm[0])
    wco_b = c_w_out[0].astype(BF16)
    xp = _outproj([yg_p], wco_b, xp, mp[1], 2, 1, 512, norm_gain=c_norm[0])
    xs_ = _outproj([yg_s], wco_b, xs_, ms[1], 2, 512 // ts, ts, norm_gain=c_norm[0])
    yp, ys = _moe_layer(xp, xs_, mp[1], ms[1], 1, norm_moe, moe_w_group, moe_w_expert,
                        moe_w_gate, moe_w_up, moe_w_down, final_norm)

    return (yp, ys, ret_p[None], ret_s[None], gla_p[None], gla_s[None],
            ssm_p[None], ssm_s[None], conv_p[None], conv_s[None])
```

```python
import functools
import math

import jax
import jax.numpy as jnp
from jax import lax
from jax.experimental import pallas as pl
from jax.experimental.pallas import tpu as pltpu

F32 = jnp.float32
BF16 = jnp.bfloat16
I32 = jnp.int32

EPS = 1e-6
PAST_LEN = 16384
ROPE_BASE = 10000.0
GLA_TEMP = 16.0
GLA_RANK = 16
RET_HEADS = 4
GLA_HEADS = 4
SSM_GROUPS = 8
SSM_HEADDIM = 64
SSM_STATE = 128
CONV_W = 4
MOE_GROUPS = 4
MOE_PER_GROUP = 8
MOE_EXPERTS = MOE_GROUPS * MOE_PER_GROUP
MOE_TOPK = 2

V7X_LANES = 128
V7X_SUBLANES = 8
V7X_VMEM_BYTES = 64 * 2 ** 20
V7X_VMEM_RESERVE = 6 * 2 ** 20

MIX_CHUNK = 128
GLA_BLOCK = 16
MOE_TILE = 256
DMA_UNROLL = 8


def _cparams(n_grid, block_bytes, scratch_bytes=0, temp_bytes=0, **kwargs):
    need = 2 * block_bytes + scratch_bytes + temp_bytes + 8 * 2 ** 20
    limit = int(min(max(need, 24 * 2 ** 20), V7X_VMEM_BYTES - V7X_VMEM_RESERVE))
    return pltpu.CompilerParams(dimension_semantics=("arbitrary",) * n_grid,
                                vmem_limit_bytes=limit, **kwargs)


def _nbytes(shape, dtype=F32):
    return math.prod(shape) * jnp.dtype(dtype).itemsize


def _silu(x):
    return x * (1.0 / (1.0 + jnp.exp(-x)))


def _softplus(x):
    return jnp.maximum(x, 0.0) + jnp.log1p(jnp.exp(-jnp.abs(x)))


def _log_sigmoid(x):
    return -_softplus(-x)


def _dot(a, b):
    return jnp.dot(a, b, preferred_element_type=F32)


def _dot_nt(a, b):
    return lax.dot_general(a, b, (((1,), (1,)), ((), ())), preferred_element_type=F32)


def _split2(x):
    hi = x.astype(BF16)
    return hi, (x - hi.astype(F32)).astype(BF16)


def _dot3p(a, b):
    return _dot(a[0], b[0]) + _dot(a[0], b[1]) + _dot(a[1], b[0])


def _dot3(a, b):
    return _dot3p(_split2(a), _split2(b))


def _dot3_nt(a, b):
    ah, al = _split2(a)
    bh, bl = _split2(b)
    return _dot_nt(ah, bh) + _dot_nt(ah, bl) + _dot_nt(al, bh)


def _split3(x):
    a = x.astype(BF16)
    r = x - a.astype(F32)
    b = r.astype(BF16)
    r = r - b.astype(F32)
    return a, b, r.astype(BF16)


def _dot01_left(m01, x):
    a, b, c = _split3(x)
    return _dot(m01, a) + _dot(m01, b) + _dot(m01, c)


def _dot01_right(x, m01):
    a, b, c = _split3(x)
    return _dot(a, m01) + _dot(b, m01) + _dot(c, m01)


def _iota(shape, dim, dtype=I32):
    return lax.broadcasted_iota(dtype, shape, dim)


def _tril01(n, block=None, strict=False):
    i = _iota((n, n), 0)
    j = _iota((n, n), 1)
    m = (i > j) if strict else (i >= j)
    if block is not None:
        m = m & ((i // block) == (j // block))
    return jnp.where(m, 1.0, 0.0).astype(BF16)


def _rms_rows(x, g):
    return x * lax.rsqrt(jnp.mean(x * x, axis=-1, keepdims=True) + EPS) * g


def _rot(x, cos, sin):
    half = x.shape[-1] // 2
    x1, x2 = x[:, :half], x[:, half:]
    return jnp.concatenate([x1 * cos - x2 * sin, x1 * sin + x2 * cos], axis=-1)


def _head_norm(o, gain, gate, center):
    if center:
        o = o - jnp.mean(o, axis=-1, keepdims=True)
    o = o * lax.rsqrt(jnp.mean(o * o, axis=-1, keepdims=True) + EPS)
    return o * gain * _silu(gate)


def _pad_rows(x, rows):
    return jnp.concatenate([x, jnp.zeros((rows - x.shape[0], x.shape[1]), x.dtype)], axis=0)


def _ada_body(c_ref, w_ref, b_ref, o_ref):
    o_ref[0] = _dot3(_silu(c_ref[...]), w_ref[0]) + b_ref[0]


def _ada(c_all, w_ada, b_ada):
    depth, d, n6 = w_ada.shape
    m = c_all.shape[0]
    tn = 1024
    blocks = _nbytes((m, d)) + _nbytes((d, tn)) + _nbytes((m, tn))
    return pl.pallas_call(
        _ada_body, name="ada",
        grid=(depth, n6 // tn),
        in_specs=[pl.BlockSpec((m, d), lambda l, j: (0, 0)),
                  pl.BlockSpec((1, d, tn), lambda l, j: (l, 0, j)),
                  pl.BlockSpec((1, 1, tn), lambda l, j: (l, 0, j))],
        out_specs=pl.BlockSpec((1, m, tn), lambda l, j: (l, 0, j)),
        out_shape=jax.ShapeDtypeStruct((depth, m, n6), F32),
        compiler_params=_cparams(2, blocks, temp_bytes=3 * _nbytes((d, tn))),
    )(c_all, w_ada, b_ada.reshape(depth, 1, n6))


def _wparts(x):
    return x if isinstance(x, tuple) else (x,)


def _split_bf16(w):
    hi = lax.bitcast_convert_type(
        lax.bitcast_convert_type(w, jnp.uint32) & jnp.uint32(0xFFFF0000), F32)
    return hi.astype(BF16), (w - hi).astype(BF16)


def _dot_parts(a, b):
    if len(b) == 1:
        return _dot(a[0], b[0])
    return _dot3p(a, b)


INPROJ_SUB = 512


def _inproj_body(n_w, x_ref, sh_ref, sc_ref, g_ref, *refs):
    w_refs, ws_refs = refs[:n_w], refs[n_w:2 * n_w]
    o_ref, os_ref, h_scr = refs[2 * n_w:]
    tm = h_scr.shape[1]
    sb, rb, _ = x_ref.shape

    @pl.when(pl.program_id(1) == 0)
    def _():
        for r0 in range(0, tm, INPROJ_SUB):
            rows = min(INPROJ_SUB, tm)
            if sb == 1:
                x = x_ref[:, r0:r0 + rows, :]
                sc, sh = sc_ref[...], sh_ref[...]
            else:
                s0, s1 = r0 // rb, (r0 + rows) // rb
                x = x_ref[s0:s1]
                sc, sh = sc_ref[s0:s1], sh_ref[s0:s1]
            h = _rms_rows(x, g_ref[...]) * (1.0 + sc) + sh
            h = h.reshape(rows, h.shape[-1])
            hs = _split2(h) if n_w == 2 else (h.astype(BF16),)
            for k in range(n_w):
                h_scr[k, r0:r0 + rows, :] = hs[k]
            os_ref[r0:r0 + rows, :] = _dot_parts(hs, tuple(r[...] for r in ws_refs))

    o_ref[...] = _dot_parts(tuple(h_scr[k] for k in range(n_w)), tuple(r[...] for r in w_refs))


def _inproj(x3, m3, chunk_shift, chunk_scale, gain, w, n_main, ws, sb, rb, tn=512):
    w, ws = _wparts(w), _wparts(ws)
    n_w = len(w)
    sq, rq, d = x3.shape
    nb = rq // rb
    tm = sb * rb
    n_tok = sq * rq
    ns = ws[0].shape[1]
    blocks = (_nbytes((tm, d)) + 2 * _nbytes((sb, V7X_SUBLANES, d))
              + n_w * (_nbytes((d, tn), BF16) + _nbytes((d, ns), BF16))
              + _nbytes((tm, tn)) + _nbytes((tm, ns)))
    w_specs = [pl.BlockSpec((d, tn), lambda i, j: (0, j)) for _ in w]
    ws_specs = [pl.BlockSpec((d, ns), lambda i, j: (0, 0)) for _ in ws]
    return pl.pallas_call(
        functools.partial(_inproj_body, n_w), name="inproj",
        grid=(n_tok // tm, n_main // tn),
        in_specs=[pl.BlockSpec((sb, rb, d), lambda i, j: (i // nb, i % nb, 0)),
                  pl.BlockSpec((sb, 1, d), lambda i, j: (i // nb, 0, chunk_shift)),
                  pl.BlockSpec((sb, 1, d), lambda i, j: (i // nb, 0, chunk_scale)),
                  pl.BlockSpec((1, d), lambda i, j: (0, 0))] + w_specs + ws_specs,
        out_specs=[pl.BlockSpec((tm, tn), lambda i, j: (i, j)),
                   pl.BlockSpec((tm, ns), lambda i, j: (i, 0))],
        out_shape=[jax.ShapeDtypeStruct((n_tok, n_main), F32),
                   jax.ShapeDtypeStruct((n_tok, ns), F32)],
        scratch_shapes=[pltpu.VMEM((n_w, tm, d), BF16)],
        compiler_params=_cparams(2, blocks, n_w * _nbytes((tm, d), BF16),
                                 4 * _nbytes((min(tm, INPROJ_SUB), d))),
    )(x3, m3, m3, gain.reshape(1, d), *w, *ws)


def _outproj_body(n_in, n_w, norm, *refs):
    o_refs = refs[:n_in]
    w_refs = refs[n_in:n_in + n_in * n_w]
    pos = n_in + n_in * n_w
    if norm:
        gn_ref = refs[pos]
        pos += 1
    x_ref, gate_ref, out_ref, o_scr = refs[pos:pos + 4]

    @pl.when(pl.program_id(1) == 0)
    def _():
        for k in range(n_in):
            o = o_refs[k][...]
            if norm:
                o = _rms_rows(o, gn_ref[...])
            parts = _split2(o) if n_w == 2 else (o.astype(BF16),)
            for p in range(n_w):
                o_scr[k, p] = parts[p]

    acc = None
    for k in range(n_in):
        part = _dot_parts(tuple(o_scr[k, p] for p in range(n_w)),
                          tuple(w_refs[k * n_w + p][...] for p in range(n_w)))
        acc = part if acc is None else acc + part
    x = x_ref[...]
    out_ref[...] = x + gate_ref[...] * acc.reshape(x.shape)


def _outproj(o_list, w, x3, m3, chunk_gate, sb, rb, norm_gain=None, tn=512):
    w = _wparts(w)
    n_w = len(w)
    sq, rq, d = x3.shape
    nb = rq // rb
    tm = sb * rb
    n_in = len(o_list)
    nj = d // tn
    widths = [o.shape[1] for o in o_list]
    assert all(wd == widths[0] for wd in widths) and sum(widths) == w[0].shape[0]
    kw = widths[0]
    in_specs = [pl.BlockSpec((tm, kw), lambda i, j: (i, 0)) for _ in o_list]
    args = list(o_list)
    for k in range(n_in):
        for part in w:
            in_specs.append(pl.BlockSpec((kw, tn), lambda i, j, k=k: (k, j)))
            args.append(part)
    if norm_gain is not None:
        in_specs.append(pl.BlockSpec((1, kw), lambda i, j: (0, 0)))
        args.append(norm_gain.reshape(1, kw))
    in_specs += [pl.BlockSpec((sb, rb, tn), lambda i, j: (i // nb, i % nb, j)),
                 pl.BlockSpec((sb, 1, tn), lambda i, j: (i // nb, 0, chunk_gate * nj + j))]
    args += [x3, m3]
    blocks = (n_in * (_nbytes((tm, kw)) + n_w * _nbytes((kw, tn), BF16))
              + 2 * _nbytes((tm, tn)) + _nbytes((sb, V7X_SUBLANES, tn)))
    scr = n_in * n_w * _nbytes((tm, kw), BF16)
    return pl.pallas_call(
        functools.partial(_outproj_body, n_in, n_w, norm_gain is not None), name="outproj",
        grid=((sq * rq) // tm, nj),
        in_specs=in_specs,
        out_specs=pl.BlockSpec((sb, rb, tn), lambda i, j: (i // nb, i % nb, j)),
        out_shape=jax.ShapeDtypeStruct(x3.shape, F32),
        scratch_shapes=[pltpu.VMEM((n_in, n_w, tm, kw), BF16)],
        compiler_params=_cparams(2, blocks, scr, 3 * _nbytes((tm, kw))),
    )(*args)


def _ret_tables(lam, c, width):
    lam1 = lam[:, :1]
    rel = (_iota((c, c), 0) - _iota((c, c), 1)).astype(F32)
    din = jnp.where(rel >= 0, jnp.exp(lam1 * jnp.maximum(rel, 0.0)), 0.0)
    r = _iota((c, width), 0).astype(F32)
    dq = jnp.exp(lam * (r + 1.0))
    dk = jnp.exp(lam * (c - 1.0 - r))
    gc = jnp.exp(lam * float(c))
    return din, dq, dk, gc


def _gla_log_alpha(alow, w2, b2):
    xa = _dot3(alow, w2) + b2
    return _log_sigmoid(xa) * (1.0 / GLA_TEMP)


def _mixa_prompt_body(qr, kr, vr, gr, qg, kg, vg, gg, alow, w2, b2, cos, sin, gnr, gng, lam,
                      o_r, o_g, sret_out, sgla_out,
                      s_ret, st_gla, din_s, dq_s, dk_s, b_s, a_s):
    c_id = pl.program_id(2)
    n_c = pl.num_programs(2)
    c = qr.shape[0]
    dk_r = qr.shape[1]
    dk_g = qg.shape[1]
    lam_row = lam[0]

    @pl.when(c_id == 0)
    def _():
        s_ret[...] = jnp.zeros_like(s_ret)
        st_gla[...] = jnp.zeros_like(st_gla)
        din, dq, dk, _ = _ret_tables(lam_row, c, dk_r)
        din_s[...] = din
        dq_s[...] = dq
        dk_s[...] = dk

    cs, sn = cos[...], sin[...]
    q = _rot(qr[...], cs, sn)
    k = _rot(kr[...], cs, sn) * (dk_r ** -0.5)
    qs, vs = _split2(q), _split2(vr[...])
    sc = _dot3_nt(q, k) * din_s[...]
    s_old = s_ret[...]
    o = _dot3p(_split2(sc), vs) + _dot3p(qs, _split2(s_old)) * dq_s[...]
    kd = k * dk_s[...]
    s_ret[...] = s_old * jnp.exp(lam_row * float(c)) + _dot3p(_split2(kd.T), vs)
    o_r[...] = _head_norm(o, gnr[...], gr[...], True)

    la = _gla_log_alpha(alow[...], w2[...], b2[...])
    b_s[...] = _dot01_left(_tril01(c), la)
    bc = b_s[...]
    qg_v = qg[...] * (dk_g ** -0.5)
    kg_v = kg[...]
    rows = _iota((c, dk_g), 0)
    nblk = c // GLA_BLOCK
    row16 = _iota((GLA_BLOCK, dk_g), 0)
    lane16 = _iota((GLA_BLOCK, c), 1)
    for blk in range(nblk):
        r0 = blk * GLA_BLOCK
        q_i = qg_v[r0:r0 + GLA_BLOCK]
        b_i = bc[r0:r0 + GLA_BLOCK]
        if blk == 0:
            a_i = jnp.zeros((GLA_BLOCK, c), F32)
        else:
            ref_b = bc[r0 - 1:r0]
            qt = q_i * jnp.exp(b_i - ref_b)
            kt = jnp.where(rows < r0, kg_v * jnp.exp(jnp.minimum(ref_b - bc, 0.0)), 0.0)
            a_i = _dot3_nt(qt, kt)
        for jj in range(GLA_BLOCK):
            j = r0 + jj
            e = jnp.exp(jnp.minimum(b_i - bc[j:j + 1], 0.0))
            col = jnp.sum(q_i * e * kg_v[j:j + 1], axis=-1, keepdims=True)
            col = jnp.where(row16[:, :1] >= jj, col, 0.0)
            a_i = a_i + jnp.where(lane16 == j, col, 0.0)
        a_s[r0:r0 + GLA_BLOCK, :] = a_i
    st_old = st_gla[...]
    og = _dot3(a_s[...], vg[...]) + _dot3_nt(qg_v * jnp.exp(bc), st_old)
    b_last = bc[c - 1:c]
    kdg = kg_v * jnp.exp(b_last - bc)
    st_gla[...] = st_old * jnp.exp(b_last) + _dot3(vg[...].T, kdg)
    o_g[...] = _head_norm(og, gng[...], gg[...], False)

    @pl.when(c_id == n_c - 1)
    def _():
        sret_out[0, 0] = s_ret[...]
        sgla_out[0, 0] = st_gla[...].T


def _mixa_prompt(pa, pa_low, w2p, b2, cos, sin, gnr, gng, lam, batch, seq):
    c = MIX_CHUNK
    nc = seq // c
    h = RET_HEADS
    dkr = 256
    dkg = 128
    dv = 256

    def rows(b, hh, cc):
        return b * nc + cc

    def col(width, base):
        return pl.BlockSpec((c, width), lambda b, hh, cc: (rows(b, hh, cc), base // width + hh))

    in_specs = [col(dkr, 0), col(dkr, 1024), col(dv, 2048), col(dv, 3072),
                col(dkg, 4096), col(dkg, 4608), col(dv, 5120), col(dv, 6144),
                pl.BlockSpec((c, V7X_LANES), lambda b, hh, cc: (rows(b, hh, cc), 0)),
                pl.BlockSpec((V7X_LANES, dkg), lambda b, hh, cc: (0, hh)),
                pl.BlockSpec((1, dkg), lambda b, hh, cc: (0, hh)),
                pl.BlockSpec((c, dkr // 2), lambda b, hh, cc: (cc, 0)),
                pl.BlockSpec((c, dkr // 2), lambda b, hh, cc: (cc, 0)),
                pl.BlockSpec((1, dv), lambda b, hh, cc: (0, hh)),
                pl.BlockSpec((1, dv), lambda b, hh, cc: (0, hh)),
                pl.BlockSpec((1, 1, dkr), lambda b, hh, cc: (hh, 0, 0))]
    out_specs = [pl.BlockSpec((c, dv), lambda b, hh, cc: (rows(b, hh, cc), hh)),
                 pl.BlockSpec((c, dv), lambda b, hh, cc: (rows(b, hh, cc), hh)),
                 pl.BlockSpec((1, 1, dkr, dv), lambda b, hh, cc: (b, hh, 0, 0)),
                 pl.BlockSpec((1, 1, dkg, dv), lambda b, hh, cc: (b, hh, 0, 0))]
    n_tok = batch * seq
    out_shape = [jax.ShapeDtypeStruct((n_tok, h * dv), F32),
                 jax.ShapeDtypeStruct((n_tok, h * dv), F32),
                 jax.ShapeDtypeStruct((batch, h, dkr, dv), F32),
                 jax.ShapeDtypeStruct((batch, h, dkg, dv), F32)]
    scratch = [pltpu.VMEM((dkr, dv), F32), pltpu.VMEM((dv, dkg), F32),
               pltpu.VMEM((c, c), F32), pltpu.VMEM((c, dkr), F32), pltpu.VMEM((c, dkr), F32),
               pltpu.VMEM((c, dkg), F32), pltpu.VMEM((c, c), F32)]
    blocks = 12 * _nbytes((c, dv)) + 2 * _nbytes((dkr, dv))
    return pl.pallas_call(
        _mixa_prompt_body, name="mixa_prompt",
        grid=(batch, h, nc),
        in_specs=in_specs, out_specs=out_specs, out_shape=out_shape,
        scratch_shapes=scratch,
        compiler_params=_cparams(3, blocks, 8 * _nbytes((dkr, dv)), 16 * 2 ** 20),
    )(pa, pa, pa, pa, pa, pa, pa, pa, pa_low, w2p, b2, cos, sin, gnr, gng, lam)


def _direct_intra(q, k, v, decay_fn):
    t = q.shape[0]
    row = _iota((t, 1), 0)
    o = jnp.zeros((t, v.shape[1]), F32)
    for j in range(t):
        col = jnp.sum(q * decay_fn(j) * k[j:j + 1], axis=-1, keepdims=True)
        col = jnp.where(row >= j, col, 0.0)
        o = o + col * v[j:j + 1]
    return o


def _mixa_sample_body(sb, pa, alow, w2, b2, cos, sin, gnr, gng, lam, sret_in, sgla_in,
                      o_r, o_g, sret_out, sgla_out):
    t = 8
    dkr, dkg, dv = 256, 128, 256
    cs, sn = cos[...], sin[...]
    la = _gla_log_alpha(alow[...], w2[...], b2[...])
    bc_all = _dot01_left(_tril01(sb * t, block=t), la)
    rel = (_iota((t, 1), 0)).astype(F32)
    for hh in range(RET_HEADS):
        lam_row = lam[hh]
        dq = jnp.exp(lam_row * (rel + 1.0))
        dk = jnp.exp(lam_row * (t - 1.0 - rel))
        gc = jnp.exp(lam_row * float(t))
        for s in range(sb):
            r0 = s * t
            q = _rot(pa[r0:r0 + t, hh * dkr:(hh + 1) * dkr], cs, sn)
            k = _rot(pa[r0:r0 + t, 1024 + hh * dkr:1024 + (hh + 1) * dkr], cs, sn) * (dkr ** -0.5)
            v = pa[r0:r0 + t, 2048 + hh * dv:2048 + (hh + 1) * dv]
            g = pa[r0:r0 + t, 3072 + hh * dv:3072 + (hh + 1) * dv]
            s_old = sret_in[s, hh]
            o = _direct_intra(q, k, v, lambda j: jnp.exp(lam_row * jnp.maximum(rel - float(j), 0.0)))
            o = o + _dot3(q, s_old) * dq
            kdt = _pad_rows(k * dk, V7X_LANES).T
            sret_out[s, hh] = s_old * gc + _dot3(kdt, _pad_rows(v, V7X_LANES))
            o_r[r0:r0 + t, hh * dv:(hh + 1) * dv] = _head_norm(
                o, gnr[:, hh * dv:(hh + 1) * dv], g, True)
            qg = pa[r0:r0 + t, 4096 + hh * dkg:4096 + (hh + 1) * dkg] * (dkg ** -0.5)
            kg = pa[r0:r0 + t, 4608 + hh * dkg:4608 + (hh + 1) * dkg]
            vg = pa[r0:r0 + t, 5120 + hh * dv:5120 + (hh + 1) * dv]
            gg = pa[r0:r0 + t, 6144 + hh * dv:6144 + (hh + 1) * dv]
            bc = bc_all[r0:r0 + t, hh * dkg:(hh + 1) * dkg]
            sg_old = sgla_in[s, hh]
            og = _direct_intra(qg, kg, vg, lambda j: jnp.exp(jnp.minimum(bc - bc[j:j + 1], 0.0)))
            og = og + _dot3(qg * jnp.exp(bc), sg_old)
            b_last = bc[t - 1:t]
            m = _pad_rows(jnp.concatenate([kg * jnp.exp(b_last - bc), jnp.exp(b_last)], axis=0),
                          V7X_LANES).T
            sgla_out[s, hh] = sg_old * m[:, t:t + 1] + _dot3(m, _pad_rows(vg, V7X_LANES))
            o_g[r0:r0 + t, hh * dv:(hh + 1) * dv] = _head_norm(
                og, gng[:, hh * dv:(hh + 1) * dv], gg, False)


def _mixa_sample(pa, pa_low, w2p, b2, cos, sin, gnr, gng, lam, sret, sgla, sb=4):
    nseq = sret.shape[0]
    t = 8
    h, dkr, dkg, dv = RET_HEADS, 256, 128, 256
    width = pa.shape[1]
    const2 = lambda i: (0, 0)
    in_specs = [pl.BlockSpec((sb * t, width), lambda i: (i, 0)),
                pl.BlockSpec((sb * t, V7X_LANES), lambda i: (i, 0)),
                pl.BlockSpec(w2p.shape, const2), pl.BlockSpec(b2.shape, const2),
                pl.BlockSpec(cos.shape, const2), pl.BlockSpec(sin.shape, const2),
                pl.BlockSpec(gnr.shape, const2), pl.BlockSpec(gng.shape, const2),
                pl.BlockSpec(lam.shape, lambda i: (0, 0, 0)),
                pl.BlockSpec((sb, h, dkr, dv), lambda i: (i, 0, 0, 0)),
                pl.BlockSpec((sb, h, dkg, dv), lambda i: (i, 0, 0, 0))]
    out_specs = [pl.BlockSpec((sb * t, h * dv), lambda i: (i, 0)),
                 pl.BlockSpec((sb * t, h * dv), lambda i: (i, 0)),
                 pl.BlockSpec((sb, h, dkr, dv), lambda i: (i, 0, 0, 0)),
                 pl.BlockSpec((sb, h, dkg, dv), lambda i: (i, 0, 0, 0))]
    out_shape = [jax.ShapeDtypeStruct((nseq * t, h * dv), F32),
                 jax.ShapeDtypeStruct((nseq * t, h * dv), F32),
                 jax.ShapeDtypeStruct(sret.shape, F32),
                 jax.ShapeDtypeStruct(sgla.shape, F32)]
    blocks = (_nbytes((sb * t, width)) + 2 * _nbytes((sb, h, dkr, dv))
              + 2 * _nbytes((sb, h, dkg, dv)) + 2 * 2 ** 20)
    return pl.pallas_call(
        functools.partial(_mixa_sample_body, sb), name="mixa_sample",
        grid=(nseq // sb,),
        in_specs=in_specs, out_specs=out_specs, out_shape=out_shape,
        compiler_params=_cparams(1, blocks, 0, 4 * 2 ** 20),
    )(pa, pa_low, w2p, b2, cos, sin, gnr, gng, lam, sret, sgla)


def _conv_prompt_body(x_ref, prev_ref, w_ref, b_ref, o_ref, scr):
    t = x_ref.shape[0]
    first = pl.program_id(1) == 0
    scr[0:V7X_SUBLANES, :] = jnp.where(first, 0.0, prev_ref[...])
    scr[V7X_SUBLANES:V7X_SUBLANES + t, :] = x_ref[...]
    acc = b_ref[...] + scr[V7X_SUBLANES - 3:V7X_SUBLANES - 3 + t, :] * w_ref[0:1, :]
    for j in range(1, CONV_W):
        off = V7X_SUBLANES - 3 + j
        acc = acc + scr[off:off + t, :] * w_ref[j:j + 1, :]
    o_ref[...] = _silu(acc)


def _conv_prompt(pc, conv_w, conv_b, batch, seq, col0, width, tc=512, tn=512):
    nc = seq // tc
    cb0 = col0 // tn
    blocks = 2 * _nbytes((tc, tn)) + _nbytes((V7X_SUBLANES, tn)) * 3
    return pl.pallas_call(
        _conv_prompt_body, name="conv_prompt",
        grid=(batch, nc, width // tn),
        in_specs=[pl.BlockSpec((tc, tn), lambda b, c, j: (b * nc + c, cb0 + j)),
                  pl.BlockSpec((V7X_SUBLANES, tn),
                               lambda b, c, j: (jnp.maximum((b * nc + c) * (tc // V7X_SUBLANES) - 1, 0),
                                                cb0 + j)),
                  pl.BlockSpec((CONV_W, tn), lambda b, c, j: (0, j)),
                  pl.BlockSpec((1, tn), lambda b, c, j: (0, j))],
        out_specs=pl.BlockSpec((tc, tn), lambda b, c, j: (b * nc + c, j)),
        out_shape=jax.ShapeDtypeStruct((batch * seq, width), F32),
        scratch_shapes=[pltpu.VMEM((tc + V7X_SUBLANES, tn), F32)],
        compiler_params=_cparams(3, blocks, _nbytes((tc + 8, tn)), 4 * _nbytes((tc, tn))),
    )(pc, pc, conv_w, conv_b.reshape(1, width))


def _conv_sample_body(x_ref, cache_ref, w_ref, b_ref, o_ref, scr):
    t = x_ref.shape[1]
    hist = CONV_W - 1
    scr[:, t - hist:t, :] = cache_ref[...]
    scr[:, t:2 * t, :] = x_ref[...]
    acc = b_ref[...] + scr[:, t - hist:2 * t - hist, :] * w_ref[0:1, :]
    for j in range(1, CONV_W):
        off = t - hist + j
        acc = acc + scr[:, off:off + t, :] * w_ref[j:j + 1, :]
    o_ref[...] = _silu(acc)


def _conv_sample(pc3, cache, conv_w, conv_b, col0, width, sbc=32, tn=512):
    nseq, t, _ = pc3.shape
    cb0 = col0 // tn
    blocks = 2 * _nbytes((sbc, t, tn)) + _nbytes((sbc, 8, tn))
    return pl.pallas_call(
        _conv_sample_body, name="conv_sample",
        grid=(nseq // sbc, width // tn),
        in_specs=[pl.BlockSpec((sbc, t, tn), lambda i, j: (i, 0, cb0 + j)),
                  pl.BlockSpec((sbc, CONV_W - 1, tn), lambda i, j: (i, 0, j)),
                  pl.BlockSpec((CONV_W, tn), lambda i, j: (0, j)),
                  pl.BlockSpec((1, tn), lambda i, j: (0, j))],
        out_specs=pl.BlockSpec((sbc, t, tn), lambda i, j: (i, 0, j)),
        out_shape=jax.ShapeDtypeStruct((nseq, t, width), F32),
        scratch_shapes=[pltpu.VMEM((sbc, 2 * t, tn), F32)],
        compiler_params=_cparams(2, blocks, _nbytes((sbc, 2 * t, tn)), 4 * _nbytes((sbc, t, tn))),
    )(pc3, cache, conv_w, conv_b.reshape(1, width))


def _expand01(heads, width):
    r = _iota((V7X_LANES, heads * width), 0)
    l = _iota((V7X_LANES, heads * width), 1)
    return jnp.where((l // width) == r, 1.0, 0.0).astype(BF16)


def _ssd_prompt_body(xs, bm, cm, z, dtc, dtr, bias_c, alog_c, bias_r, alog_r, dfull,
                     y_out, s_out, st):
    c_id = pl.program_id(2)
    n_c = pl.num_programs(2)
    l = xs.shape[0]
    heads = dtr.shape[1]
    p = xs.shape[1] // heads

    @pl.when(c_id == 0)
    def _():
        st[...] = jnp.zeros_like(st)

    dt_c = _softplus(dtc[...] + bias_c[0])
    cum_c = _dot01_left(_tril01(l), dt_c * (-jnp.exp(alog_c[0])))
    dt_r = _softplus(dtr[0] + bias_r[0])
    tri_u = jnp.where(_iota((l, l), 0) <= _iota((l, l), 1), 1.0, 0.0).astype(BF16)
    cum_r = _dot01_right(dt_r * (-jnp.exp(alog_r[0])), tri_u)

    x = xs[...]
    cb = _dot_nt(cm[...].astype(BF16), bm[...].astype(BF16))
    causal = _iota((l, l), 0) >= _iota((l, l), 1)
    lane = _iota((l, 2 * p), 1)
    y_tiles = []
    for pair in range(heads // 2):
        xt = x[:, pair * 2 * p:(pair + 1) * 2 * p]
        acc = None
        for sub in range(2):
            r = 2 * pair + sub
            seg = cum_c[:, r:r + 1] - cum_r[r:r + 1, :]
            w = jnp.where(causal, cb * jnp.exp(jnp.minimum(seg, 0.0)) * dt_r[r:r + 1, :], 0.0)
            xm = jnp.where((lane // p) == sub, xt, 0.0)
            part = _dot(w.astype(BF16), xm.astype(BF16))
            acc = part if acc is None else acc + part
        y_tiles.append(acc)
    y = jnp.concatenate(y_tiles, axis=-1)

    ex = _expand01(heads, p)
    cum_f = _dot01_right(cum_c, ex)
    dt_f = _dot01_right(dt_c, ex)
    st_old = st[...]
    y = y + _dot(cm[...].astype(BF16), st_old.astype(BF16)) * jnp.exp(cum_f)
    cl_f = cum_f[l - 1:l]
    tail = jnp.exp(cl_f - cum_f) * dt_f
    st[...] = st_old * jnp.exp(cl_f) + _dot(bm[...].T.astype(BF16), (x * tail).astype(BF16))
    y = y + dfull[0] * x
    y_out[...] = y * _silu(z[...])

    @pl.when(c_id == n_c - 1)
    def _():
        s_out[0] = st[...].T.reshape(s_out.shape[1:])


def _ssd_prompt(act, pc, pcs, dt_t, bias_c, alog_c, bias_r, alog_r, dfull, batch, seq):
    l = MIX_CHUNK
    nc = seq // l
    g = SSM_GROUPS
    hp = act.shape[1]
    n = SSM_STATE
    p = SSM_HEADDIM
    d_inner = pc.shape[1] - 6144
    gw = d_inner // g
    hpg = gw // p
    xb0 = 0
    bb0 = d_inner // n
    cb0 = (d_inner + g * n) // n

    def row(b, gg, cc):
        return b * nc + cc

    in_specs = [pl.BlockSpec((l, gw), lambda b, gg, cc: (row(b, gg, cc), gg)),
                pl.BlockSpec((l, n), lambda b, gg, cc: (row(b, gg, cc), bb0 + gg)),
                pl.BlockSpec((l, n), lambda b, gg, cc: (row(b, gg, cc), cb0 + gg)),
                pl.BlockSpec((l, gw), lambda b, gg, cc: (row(b, gg, cc), gg)),
                pl.BlockSpec((l, V7X_LANES), lambda b, gg, cc: (row(b, gg, cc), gg)),
                pl.BlockSpec((1, hpg, l), lambda b, gg, cc: (b, gg, cc)),
                pl.BlockSpec((1, 1, V7X_LANES), lambda b, gg, cc: (gg, 0, 0)),
                pl.BlockSpec((1, 1, V7X_LANES), lambda b, gg, cc: (gg, 0, 0)),
                pl.BlockSpec((1, hpg, l), lambda b, gg, cc: (gg, 0, 0)),
                pl.BlockSpec((1, hpg, l), lambda b, gg, cc: (gg, 0, 0)),
                pl.BlockSpec((1, 1, gw), lambda b, gg, cc: (gg, 0, 0))]
    out_specs = [pl.BlockSpec((l, gw), lambda b, gg, cc: (row(b, gg, cc), gg)),
                 pl.BlockSpec((1, hpg, p, n), lambda b, gg, cc: (b, gg, 0, 0))]
    out_shape = [jax.ShapeDtypeStruct((batch * seq, d_inner), F32),
                 jax.ShapeDtypeStruct((batch, g * hpg, p, n), F32)]
    del hp, xb0
    blocks = 3 * _nbytes((l, gw)) + 3 * _nbytes((l, n)) + _nbytes((hpg, p, n))
    return pl.pallas_call(
        _ssd_prompt_body, name="ssd_prompt",
        grid=(batch, g, nc),
        in_specs=in_specs, out_specs=out_specs, out_shape=out_shape,
        scratch_shapes=[pltpu.VMEM((n, gw), F32)],
        compiler_params=_cparams(3, blocks, _nbytes((n, gw)), 16 * 2 ** 20),
    )(act, act, act, pc, pcs, dt_t, bias_c, alog_c, bias_r, alog_r, dfull)


def _ssd_sample_body(sb, xs, bm, cm, z, dtc, bias_c, alog_c, dfull, s_in, y_out, s_out, y_s):
    t = 8
    rows = sb * t
    gw = xs.shape[1]
    hpg = s_in.shape[1]
    p = gw // hpg
    n = bm.shape[1]
    dt_c = _softplus(dtc[...] + bias_c[0])
    cum_c = _dot01_left(_tril01(rows, block=t), dt_c * (-jnp.exp(alog_c[0])))
    ex = _expand01(hpg, p)
    cum_f = _dot01_right(cum_c, ex)
    dt_f = _dot01_right(dt_c, ex)
    x = xs[...]
    x3 = x.reshape(sb, t, gw)
    c3 = cm[...].reshape(sb, t, n)
    b3 = bm[...].reshape(sb, t, n)
    cum3 = cum_f.reshape(sb, t, gw)
    dt3 = dt_f.reshape(sb, t, gw)
    rowi = _iota((sb, t, 1), 1)
    y3 = jnp.zeros((sb, t, gw), F32)
    for j in range(t):
        cbj = jnp.sum(c3 * b3[:, j:j + 1, :], axis=-1, keepdims=True)
        dec = jnp.exp(jnp.minimum(cum3 - cum3[:, j:j + 1, :], 0.0))
        w = jnp.where(rowi >= j, cbj * dec * dt3[:, j:j + 1, :], 0.0)
        y3 = y3 + w * x3[:, j:j + 1, :]
    y_s[...] = y3.reshape(rows, gw)
    cmv = cm[...]
    bmv = bm[...]
    for s in range(sb):
        r0 = s * t
        sg = s_in[s].reshape(hpg * p, n)
        cf = cum_f[r0:r0 + t]
        yi = _dot_nt(cmv[r0:r0 + t].astype(BF16), sg.astype(BF16)) * jnp.exp(cf)
        cl = cf[t - 1:t]
        xt = x[r0:r0 + t] * (jnp.exp(cl - cf) * dt_f[r0:r0 + t])
        m = _pad_rows(jnp.concatenate([xt, jnp.exp(cl)], axis=0), V7X_LANES).T
        s_new = sg * m[:, t:t + 1] + _dot(m.astype(BF16), _pad_rows(bmv[r0:r0 + t], V7X_LANES).astype(BF16))
        s_out[s] = s_new.reshape(hpg, p, n)
        y_s[r0:r0 + t, :] = y_s[r0:r0 + t, :] + yi
    y_out[...] = (y_s[...] + dfull[0] * x) * _silu(z[...])


def _ssd_sample(act, pc, pcs, bias_c, alog_c, dfull, s_ssm, sb=8):
    t = 8
    nseq = s_ssm.shape[0]
    g = SSM_GROUPS
    n = SSM_STATE
    p = SSM_HEADDIM
    d_inner = pc.shape[1] - 6144
    gw = d_inner // g
    hpg = gw // p
    bb0 = d_inner // n
    cb0 = (d_inner + g * n) // n
    rows = sb * t
    in_specs = [pl.BlockSpec((rows, gw), lambda i, gg: (i, gg)),
                pl.BlockSpec((rows, n), lambda i, gg: (i, bb0 + gg)),
                pl.BlockSpec((rows, n), lambda i, gg: (i, cb0 + gg)),
                pl.BlockSpec((rows, gw), lambda i, gg: (i, gg)),
                pl.BlockSpec((rows, V7X_LANES), lambda i, gg: (i, gg)),
                pl.BlockSpec((1, 1, V7X_LANES), lambda i, gg: (gg, 0, 0)),
                pl.BlockSpec((1, 1, V7X_LANES), lambda i, gg: (gg, 0, 0)),
                pl.BlockSpec((1, 1, gw), lambda i, gg: (gg, 0, 0)),
                pl.BlockSpec((sb, hpg, p, n), lambda i, gg: (i, gg, 0, 0))]
    out_specs = [pl.BlockSpec((rows, gw), lambda i, gg: (i, gg)),
                 pl.BlockSpec((sb, hpg, p, n), lambda i, gg: (i, gg, 0, 0))]
    out_shape = [jax.ShapeDtypeStruct((nseq * t, d_inner), F32),
                 jax.ShapeDtypeStruct(s_ssm.shape, F32)]
    blocks = 3 * _nbytes((rows, gw)) + 3 * _nbytes((rows, n)) + 2 * _nbytes((sb, hpg, p, n))
    return pl.pallas_call(
        functools.partial(_ssd_sample_body, sb), name="ssd_sample",
        grid=(nseq // sb, g),
        in_specs=in_specs, out_specs=out_specs, out_shape=out_shape,
        scratch_shapes=[pltpu.VMEM((rows, gw), F32)],
        compiler_params=_cparams(2, blocks, _nbytes((rows, gw)), 8 * 2 ** 20),
    )(act, act, act, pc, pcs, bias_c, alog_c, dfull, s_ssm)


def _moe_pre_body(n_pt, xp_ref, shp_ref, scp_ref, xs_ref, shs_ref, scs_ref, g_ref, wr_ref,
                  h_out, route_out):
    i = pl.program_id(0)

    @pl.when(i < n_pt)
    def _():
        _moe_route(xp_ref, shp_ref, scp_ref, g_ref, wr_ref, h_out, route_out)

    @pl.when(i >= n_pt)
    def _():
        _moe_route(xs_ref, shs_ref, scs_ref, g_ref, wr_ref, h_out, route_out)


def _moe_route(x_ref, sh_ref, sc_ref, g_ref, wr_ref, h_out, route_out):
    x = x_ref[...]
    h = _rms_rows(x, g_ref[...]) * (1.0 + sc_ref[...]) + sh_ref[...]
    h = h.reshape(h_out.shape[0], x.shape[-1])
    h_out[...] = h.reshape(h_out.shape).astype(BF16)
    logits = jnp.dot(h, wr_ref[...], precision=lax.Precision.HIGHEST, preferred_element_type=F32)
    tm = logits.shape[0]
    lane = _iota((tm, V7X_LANES), 1).astype(F32)
    neg = -jnp.inf
    lg = jnp.where(lane < MOE_GROUPS, logits, neg)
    mg = jnp.max(lg, axis=-1, keepdims=True)
    top_g = jnp.min(jnp.where(lg == mg, lane, float(V7X_LANES)), axis=-1, keepdims=True)
    p_top = 1.0 / jnp.sum(jnp.where(lane < MOE_GROUPS, jnp.exp(logits - mg), 0.0),
                          axis=-1, keepdims=True)
    lo = MOE_GROUPS + MOE_PER_GROUP * top_g
    le = jnp.where((lane >= lo) & (lane < lo + MOE_PER_GROUP), logits, neg)
    v1 = jnp.max(le, axis=-1, keepdims=True)
    i1 = jnp.min(jnp.where(le == v1, lane, float(V7X_LANES)), axis=-1, keepdims=True)
    le2 = jnp.where(lane == i1, neg, le)
    v2 = jnp.max(le2, axis=-1, keepdims=True)
    i2 = jnp.min(jnp.where(le2 == v2, lane, float(V7X_LANES)), axis=-1, keepdims=True)
    tt = jnp.exp(v2 - v1)
    w1 = p_top / (1.0 + tt)
    w2 = p_top * tt / (1.0 + tt)
    route_out[...] = jnp.where(lane == 0, i1 - MOE_GROUPS,
                     jnp.where(lane == 1, i2 - MOE_GROUPS,
                     jnp.where(lane == 2, w1, jnp.where(lane == 3, w2, 0.0))))


def _moe_pre(xp, mp, xs_, ms, chunk_shift, chunk_scale, gain, wr, tm):
    bp, tp, d = xp.shape
    bs, ts, _ = xs_.shape
    nb = tp // tm
    n_pt = bp * nb
    sbs = tm // ts
    n_st = bs // sbs
    n_all = bp * tp + bs * ts

    def pi(i):
        return jnp.minimum(i, n_pt - 1)

    def si(i):
        return jnp.maximum(i - n_pt, 0)

    in_specs = [pl.BlockSpec((1, tm, d), lambda i: (pi(i) // nb, pi(i) % nb, 0)),
                pl.BlockSpec((1, 1, d), lambda i: (pi(i) // nb, 0, chunk_shift)),
                pl.BlockSpec((1, 1, d), lambda i: (pi(i) // nb, 0, chunk_scale)),
                pl.BlockSpec((sbs, ts, d), lambda i: (si(i), 0, 0)),
                pl.BlockSpec((sbs, 1, d), lambda i: (si(i), 0, chunk_shift)),
                pl.BlockSpec((sbs, 1, d), lambda i: (si(i), 0, chunk_scale)),
                pl.BlockSpec((1, d), lambda i: (0, 0)),
                pl.BlockSpec((d, V7X_LANES), lambda i: (0, 0))]
    blocks = (3 * _nbytes((tm, d)) + 2 * _nbytes((sbs, 8, d)) + _nbytes((d, V7X_LANES))
              + _nbytes((tm, 128)))
    return pl.pallas_call(
        functools.partial(_moe_pre_body, n_pt), name="moe_pre",
        grid=(n_pt + n_st,),
        in_specs=in_specs,
        out_specs=[pl.BlockSpec((tm, d // V7X_LANES, V7X_LANES), lambda i: (i, 0, 0)),
                   pl.BlockSpec((tm, V7X_LANES), lambda i: (i, 0))],
        out_shape=[jax.ShapeDtypeStruct((n_all, d // V7X_LANES, V7X_LANES), BF16),
                   jax.ShapeDtypeStruct((n_all, V7X_LANES), F32)],
        compiler_params=_cparams(1, blocks, 0, 4 * _nbytes((tm, d))),
    )(xp, mp, mp, xs_, ms, ms, gain.reshape(1, d), wr)


def _rank_body(route_ref, rank_ref, cnt_ref, carry):
    @pl.when(pl.program_id(0) == 0)
    def _():
        carry[...] = jnp.zeros_like(carry)

    r = route_ref[...]
    tm = r.shape[0]
    lane = _iota((tm, V7X_LANES), 1).astype(F32)
    o1 = lane == r[:, 0:1]
    o2 = lane == r[:, 1:2]
    oh = jnp.where(o1, 1.0, 0.0) + jnp.where(o2, 1.0, 0.0)
    pex = _dot(_tril01(tm, strict=True), oh.astype(BF16)) + carry[0:1, :]
    r1 = jnp.sum(jnp.where(o1, pex, 0.0), axis=-1, keepdims=True)
    r2 = jnp.sum(jnp.where(o2, pex, 0.0), axis=-1, keepdims=True)
    rank_ref[...] = jnp.where(lane == 0, r1, jnp.where(lane == 1, r2, 0.0))
    carry[...] = carry[...] + jnp.sum(oh, axis=0, keepdims=True)
    cnt_ref[...] = carry[...]


def _rank(route, tm=256):
    n = route.shape[0]
    return pl.pallas_call(
        _rank_body, name="moe_rank",
        grid=(n // tm,),
        in_specs=[pl.BlockSpec((tm, V7X_LANES), lambda i: (i, 0))],
        out_specs=[pl.BlockSpec((tm, V7X_LANES), lambda i: (i, 0)),
                   pl.BlockSpec((V7X_SUBLANES, V7X_LANES), lambda i: (0, 0))],
        out_shape=[jax.ShapeDtypeStruct((n, V7X_LANES), F32),
                   jax.ShapeDtypeStruct((V7X_SUBLANES, V7X_LANES), F32)],
        scratch_shapes=[pltpu.VMEM((V7X_SUBLANES, V7X_LANES), F32)],
        compiler_params=_cparams(1, 2 * _nbytes((tm, 128))),
    )(route)


def _gather_tokens(src_hbm, dst, sem, n, row_of):
    def issue(r, carry):
        pltpu.make_async_copy(src_hbm.at[row_of(r)], dst.at[r], sem).start()
        return carry

    lax.fori_loop(0, n, issue, 0, unroll=DMA_UNROLL)


def _gather_wait(src_hbm, dst, sem):
    pltpu.make_async_copy(src_hbm.at[pl.ds(0, dst.shape[0])], dst, sem).wait()


def _expert_body(layer, inv_ref, te_ref, first_ref, wslot_ref, nxt_ref, nt_ref,
                 h3_hbm, wg_hbm, wu_hbm, wd_hbm, o_ref,
                 xbuf, wg_f, wu_f, wd_f, wg_b, wu_b, wd_b, xsem, wsem):
    i = pl.program_id(0)
    nt = nt_ref[0]
    tm = xbuf.shape[1]
    active = i < nt

    def w_copies(e, slot):
        return [pltpu.make_async_copy(src.at[layer, e], dst.at[slot], wsem.at[k])
                for k, (src, dst) in enumerate(((wg_hbm, wg_f), (wu_hbm, wu_f), (wd_hbm, wd_f)))]

    def fetch_rows(tile, slot):
        _gather_tokens(h3_hbm, xbuf.at[slot], xsem.at[slot], tm, lambda r: inv_ref[tile * tm + r])

    @pl.when(i == 0)
    def _():
        for cp in w_copies(te_ref[0], wslot_ref[0]):
            cp.start()
        fetch_rows(0, 0)

    @pl.when(active)
    def _():
        slot = i % 2
        ws = wslot_ref[i]

        @pl.when(first_ref[i] == 1)
        def _():
            for cp in w_copies(te_ref[i], ws):
                cp.wait()
            wg_b[...] = wg_f[ws].astype(BF16)
            wu_b[...] = wu_f[ws].astype(BF16)
            wd_b[...] = wd_f[ws].astype(BF16)

            @pl.when(nxt_ref[i] >= 0)
            def _():
                for cp in w_copies(nxt_ref[i], 1 - ws):
                    cp.start()

        @pl.when(i + 1 < nt)
        def _():
            fetch_rows(i + 1, 1 - slot)

        _gather_wait(h3_hbm, xbuf.at[slot], xsem.at[slot])
        xb = xbuf[slot].reshape(tm, wg_b.shape[0])
        hid = _silu(_dot(xb, wg_b[...])) * _dot(xb, wu_b[...])
        o_ref[...] = _dot(hid.astype(BF16), wd_b[...]).reshape(o_ref.shape)

    @pl.when(jnp.logical_not(active))
    def _():
        o_ref[...] = jnp.zeros_like(o_ref)


def _experts(inv, tile_expert, first, wslot, nxt, ntiles, h3, wg, wu, wd, layer):
    _, dc, ln = h3.shape
    d = dc * ln
    ff = wg.shape[3]
    tm = MOE_TILE
    p_pad = inv.shape[0]
    scratch = [pltpu.VMEM((2, tm, dc, ln), BF16),
               pltpu.VMEM((2, d, ff), F32), pltpu.VMEM((2, d, ff), F32), pltpu.VMEM((2, ff, d), F32),
               pltpu.VMEM((d, ff), BF16), pltpu.VMEM((d, ff), BF16), pltpu.VMEM((ff, d), BF16),
               pltpu.SemaphoreType.DMA((2,)), pltpu.SemaphoreType.DMA((3,))]
    scr_bytes = 2 * _nbytes((tm, d), BF16) + 6 * _nbytes((d, ff)) + 3 * _nbytes((d, ff), BF16)
    any_spec = pl.BlockSpec(memory_space=pl.ANY)
    return pl.pallas_call(
        functools.partial(_expert_body, layer), name="moe_experts",
        grid_spec=pltpu.PrefetchScalarGridSpec(
            num_scalar_prefetch=6,
            grid=(p_pad // tm,),
            in_specs=[any_spec, any_spec, any_spec, any_spec],
            out_specs=pl.BlockSpec((tm, dc, ln), lambda i, *_: (i, 0, 0)),
            scratch_shapes=scratch),
        out_shape=jax.ShapeDtypeStruct((p_pad, dc, ln), F32),
        compiler_params=_cparams(1, _nbytes((tm, d)), scr_bytes, 6 * _nbytes((tm, d)),
                                 disable_bounds_checks=True),
    )(inv, tile_expert, first, wslot, nxt, ntiles, h3, wg, wu, wd)


def _combine_body(tok_off, final, pos_ref, eo_hbm, route_ref, x_ref, gate_ref, fn_ref, out_ref,
                  buf, sem):
    i = pl.program_id(0)
    n = pl.num_programs(0)
    tc = route_ref.shape[0]
    d = x_ref.shape[-1]

    def fetch(step, slot):
        base = 2 * (tok_off + step * tc)
        for k in range(MOE_TOPK):
            _gather_tokens(eo_hbm, buf.at[slot, k], sem.at[slot], tc,
                           lambda r, k=k: pos_ref[base + MOE_TOPK * r + k])

    @pl.when(i == 0)
    def _():
        fetch(0, 0)

    slot = i % 2

    @pl.when(i + 1 < n)
    def _():
        fetch(i + 1, 1 - slot)

    for k in range(MOE_TOPK):
        _gather_wait(eo_hbm, buf.at[slot, k], sem.at[slot])
    rt = route_ref[...]
    y = rt[:, 2:3] * buf[slot, 0].reshape(tc, d) + rt[:, 3:4] * buf[slot, 1].reshape(tc, d)
    x = x_ref[...]
    xn = x + gate_ref[...] * y.reshape(x.shape)
    if final:
        xn = _rms_rows(xn, fn_ref[...])
    out_ref[...] = xn


def _combine(pos_flat, eo, route, x3, m3, chunk_gate, sb, rb, tok_off, final_gain=None):
    sq, rq, d = x3.shape
    nb = rq // rb
    tc = sb * rb
    blk_off = tok_off // tc
    final = final_gain is not None
    fn = (final_gain if final else jnp.ones((d,), F32)).reshape(1, d)
    blocks = 2 * _nbytes((tc, d)) + _nbytes((tc, 128)) + _nbytes((sb, 8, d))
    return pl.pallas_call(
        functools.partial(_combine_body, tok_off, final), name="moe_combine",
        grid_spec=pltpu.PrefetchScalarGridSpec(
            num_scalar_prefetch=1,
            grid=((sq * rq) // tc,),
            in_specs=[pl.BlockSpec(memory_space=pl.ANY),
                      pl.BlockSpec((tc, V7X_LANES), lambda i, p: (blk_off + i, 0)),
                      pl.BlockSpec((sb, rb, d), lambda i, p: (i // nb, i % nb, 0)),
                      pl.BlockSpec((sb, 1, d), lambda i, p: (i // nb, 0, chunk_gate)),
                      pl.BlockSpec((1, d), lambda i, p: (0, 0))],
            out_specs=pl.BlockSpec((sb, rb, d), lambda i, p: (i // nb, i % nb, 0)),
            scratch_shapes=[pltpu.VMEM((2, MOE_TOPK, tc, d // V7X_LANES, V7X_LANES), F32),
                            pltpu.SemaphoreType.DMA((2,))]),
        out_shape=jax.ShapeDtypeStruct(x3.shape, F32),
        compiler_params=_cparams(1, blocks, 2 * MOE_TOPK * _nbytes((tc, d)), 4 * _nbytes((tc, d)),
                                 disable_bounds_checks=True),
    )(pos_flat, eo, route, x3, m3, fn)


def _moe_layer(xp, xs_, mp, ms, l, norm_moe, w_group, w_expert, w_gate, w_up, w_down, final_gain):
    bp, tp, d = xp.shape
    bs, ts, _ = xs_.shape
    n_p, n_s = bp * tp, bs * ts
    n_all = n_p + n_s
    wr = jnp.concatenate([w_group[l], jnp.transpose(w_expert[l], (1, 0, 2)).reshape(d, MOE_EXPERTS)],
                         axis=1)
    wr = jnp.pad(wr, ((0, 0), (0, V7X_LANES - wr.shape[1])))
    tm = MOE_TILE
    h_all, route = _moe_pre(xp, mp, xs_, ms, 3, 4, norm_moe[l], wr, tm)
    rank, cnt = _rank(route)
    counts = cnt[0, :MOE_EXPERTS].astype(I32)
    tiles_per = (counts + tm - 1) // tm
    tile_end = jnp.cumsum(tiles_per)
    row_start = (tile_end - tiles_per) * tm
    ntiles = tile_end[-1]
    e_idx = route[:, 0:2].astype(I32)
    pos = row_start[e_idx] + rank[:, 0:2].astype(I32)
    nt_max = (2 * n_all) // tm + MOE_EXPERTS
    p_pad = nt_max * tm
    tile_ids = jnp.arange(nt_max, dtype=I32)
    te = jnp.minimum(jnp.sum((tile_ids[:, None] >= tile_end[None, :]).astype(I32), axis=1),
                     MOE_EXPERTS - 1)
    active = tile_ids < ntiles
    te = jnp.where(active, te, te[jnp.maximum(ntiles - 1, 0)])
    first = (active & ((tile_ids == 0) | (te != jnp.roll(te, 1)))).astype(I32)
    wslot = (jnp.cumsum(first) - 1) % 2
    grp_end = tile_end[te]
    nxt = jnp.where(grp_end < ntiles, te[jnp.minimum(grp_end, nt_max - 1)], -1).astype(I32)
    pos_flat = pos.reshape(-1)
    inv = jnp.zeros((p_pad,), I32).at[pos_flat].set(jnp.arange(2 * n_all, dtype=I32) // 2)
    nt_arr = ntiles.reshape(1).astype(I32)
    eo = _experts(inv, te, first, wslot.astype(I32), nxt, nt_arr, h_all, w_gate, w_up, w_down, l)
    xp_new = _combine(pos_flat, eo, route, xp, mp, 5, 1, tm, 0, final_gain)
    xs_new = _combine(pos_flat, eo, route, xs_, ms, 5, tm // ts, ts, n_p, final_gain)
    return xp_new, xs_new


def kernel(x_prompt, x_sample, c_prompt, c_sample, state_ret, state_gla, state_ssm, cache_conv,
           w_ada, b_ada, norm_mix, norm_moe,
           a_w_in, a_ret_gn, a_gla_w2, a_gla_b2, a_gla_gn, a_w_out,
           c_w_in, c_conv_w, c_conv_b, c_dt_bias, c_a_log, c_d, c_norm, c_w_out,
           moe_w_group, moe_w_expert, moe_w_gate, moe_w_up, moe_w_down, final_norm):
    bp, tp, d = x_prompt.shape
    bs, ts, _ = x_sample.shape
    n_p, n_s = bp * tp, bs * ts

    pad = (-bp) % V7X_SUBLANES
    c_all = jnp.concatenate([c_prompt, jnp.zeros((pad, d), F32), c_sample], axis=0)
    m_all = _ada(c_all, w_ada, b_ada)
    mp = [m_all[l, :bp].reshape(bp, 1, 6 * d) for l in range(2)]
    ms = [m_all[l, bp + pad:].reshape(bs, 1, 6 * d) for l in range(2)]

    tm_p, tm_s = 512, 512

    wa = a_w_in[0]
    n_main_a = (wa.shape[1] // 512) * 512
    wa_low = jnp.pad(wa[:, n_main_a:], ((0, 0), (0, V7X_LANES - (wa.shape[1] - n_main_a))))
    w2p = jnp.pad(a_gla_w2[0], ((0, V7X_LANES - GLA_RANK), (0, 0)))
    b2 = a_gla_b2[0].reshape(1, -1)
    gnr = a_ret_gn[0].reshape(1, -1)
    gng = a_gla_gn[0].reshape(1, -1)
    half = 128
    inv_freq = ROPE_BASE ** (-jnp.arange(half, dtype=F32) / half)
    ang_p = jnp.arange(tp, dtype=F32)[:, None] * inv_freq[None, :]
    ang_s = (PAST_LEN + jnp.arange(ts, dtype=F32))[:, None] * inv_freq[None, :]
    log_gamma = jnp.log(1.0 - 2.0 ** (-5.0 - jnp.arange(RET_HEADS, dtype=F32)))
    lam = jnp.broadcast_to(log_gamma[:, None, None], (RET_HEADS, 1, 256))

    wa_parts, wa_low_parts, wo_parts = _split_bf16(wa), _split_bf16(wa_low), _split_bf16(a_w_out[0])
    pa_p, low_p = _inproj(x_prompt, mp[0], 0, 1, norm_mix[0], wa_parts, n_main_a, wa_low_parts, 1, tm_p)
    pa_s, low_s = _inproj(x_sample, ms[0], 0, 1, norm_mix[0], wa_parts, n_main_a, wa_low_parts,
                          tm_s // ts, ts)
    or_p, og_p, ret_p, gla_p = _mixa_prompt(pa_p, low_p, w2p, b2, jnp.cos(ang_p), jnp.sin(ang_p),
                                            gnr, gng, lam, bp, tp)
    or_s, og_s, ret_s, gla_s = _mixa_sample(pa_s, low_s, w2p, b2, jnp.cos(ang_s), jnp.sin(ang_s),
                                            gnr, gng, lam, state_ret[0], state_gla[0])
    xp = _outproj([or_p, og_p], wo_parts, x_prompt, mp[0], 2, 1, tm_p)
    xs_ = _outproj([or_s, og_s], wo_parts, x_sample, ms[0], 2, tm_s // ts, ts)
    xp, xs_ = _moe_layer(xp, xs_, mp[0], ms[0], 0, norm_moe, moe_w_group, moe_w_expert,
                         moe_w_gate, moe_w_up, moe_w_down, None)

    wc = c_w_in[0]
    n_main_c = (wc.shape[1] // 512) * 512
    heads = wc.shape[1] - n_main_c
    hpg = heads // SSM_GROUPS
    d_inner = heads * SSM_HEADDIM
    conv_dim = n_main_c - d_inner
    wdt = jnp.pad(wc[:, n_main_c:].reshape(d, SSM_GROUPS, hpg),
                  ((0, 0), (0, 0), (0, V7X_LANES - hpg))).reshape(d, SSM_GROUPS * V7X_LANES).astype(BF16)
    wc_b = wc.astype(BF16)
    pc_p, dt_p = _inproj(x3=xp, m3=mp[1], chunk_shift=0, chunk_scale=1, gain=norm_mix[1], w=wc_b,
                         n_main=n_main_c, ws=wdt, sb=1, rb=2 * tm_p)
    pc_s, dt_s = _inproj(x3=xs_, m3=ms[1], chunk_shift=0, chunk_scale=1, gain=norm_mix[1], w=wc_b,
                         n_main=n_main_c, ws=wdt, sb=tm_s // ts, rb=ts)
    act_p = _conv_prompt(pc_p, c_conv_w[0], c_conv_b[0], bp, tp, d_inner, conv_dim)
    act_s = _conv_sample(pc_s.reshape(bs, ts, n_main_c), cache_conv[0], c_conv_w[0], c_conv_b[0],
                         d_inner, conv_dim).reshape(n_s, conv_dim)
    conv_p = pc_p.reshape(bp, tp, n_main_c)[:, tp - (CONV_W - 1):, d_inner:]
    conv_s = pc_s.reshape(bs, ts, n_main_c)[:, ts - (CONV_W - 1):, d_inner:]

    def lanes_pad(v):
        return jnp.pad(v.reshape(SSM_GROUPS, 1, hpg), ((0, 0), (0, 0), (0, V7X_LANES - hpg)))

    bias_c, alog_c = lanes_pad(c_dt_bias[0]), lanes_pad(c_a_log[0])
    bias_r = jnp.broadcast_to(c_dt_bias[0].reshape(SSM_GROUPS, hpg, 1), (SSM_GROUPS, hpg, MIX_CHUNK))
    alog_r = jnp.broadcast_to(c_a_log[0].reshape(SSM_GROUPS, hpg, 1), (SSM_GROUPS, hpg, MIX_CHUNK))
    dfull = jnp.repeat(c_d[0], SSM_HEADDIM).reshape(SSM_GROUPS, 1, hpg * SSM_HEADDIM)
    dt_t = jnp.transpose(dt_p.reshape(bp, tp, SSM_GROUPS, V7X_LANES)[..., :hpg].reshape(bp, tp, heads),
                         (0, 2, 1))
    yg_p, ssm_p = _ssd_prompt(act_p, pc_p, dt_p, dt_t, bias_c, alog_c, bias_r, alog_r, dfull, bp, tp)
    yg_s, ssm_s = _ssd_sample(act_s, pc_s, dt_s, bias_c, alog_c, dfull, state_ssm[0])
    wco_b = c_w_out[0].astype(BF16)
    xp = _outproj([yg_p], wco_b, xp, mp[1], 2, 1, 512, norm_gain=c_norm[0])
    xs_ = _outproj([yg_s], wco_b, xs_, ms[1], 2, 512 // ts, ts, norm_gain=c_norm[0])
    yp, ys = _moe_layer(xp, xs_, mp[1], ms[1], 1, norm_moe, moe_w_group, moe_w_expert,
                        moe_w_gate, moe_w_up, moe_w_down, final_norm)

    return (yp, ys, ret_p[None], ret_s[None], gla_p[None], gla_s[None],
            ssm_p[None], ssm_s[None], conv_p[None], conv_s[None])
```

```python
import functools
import math

import jax
import jax.numpy as jnp
from jax import lax
from jax.experimental import pallas as pl
from jax.experimental.pallas import tpu as pltpu

F32 = jnp.float32
BF16 = jnp.bfloat16
I32 = jnp.int32

EPS = 1e-6
PAST_LEN = 16384
ROPE_BASE = 10000.0
GLA_TEMP = 16.0
GLA_RANK = 16
RET_HEADS = 4
GLA_HEADS = 4
SSM_GROUPS = 8
SSM_HEADDIM = 64
SSM_STATE = 128
CONV_W = 4
MOE_GROUPS = 4
MOE_PER_GROUP = 8
MOE_EXPERTS = MOE_GROUPS * MOE_PER_GROUP
MOE_TOPK = 2

V7X_LANES = 128
V7X_SUBLANES = 8
V7X_VMEM_BYTES = 64 * 2 ** 20
V7X_VMEM_RESERVE = 6 * 2 ** 20

A_RET_DK, A_GLA_DK, A_DV = 256, 128, 256
A_QR = 0
A_KR = A_QR + RET_HEADS * A_RET_DK
A_VR = A_KR + RET_HEADS * A_RET_DK
A_GR = A_VR + RET_HEADS * A_DV
A_QG = A_GR + RET_HEADS * A_DV
A_KG = A_QG + GLA_HEADS * A_GLA_DK
A_VG = A_KG + GLA_HEADS * A_GLA_DK
A_GG = A_VG + GLA_HEADS * A_DV

MIX_CHUNK = 128
GLA_BLOCK = 16
MOE_TILE = 256
DMA_UNROLL = 8
WEIGHT_DMA_PRIORITY = 1


def _cparams(n_grid, block_bytes, scratch_bytes=0, temp_bytes=0, **kwargs):
    need = 2 * block_bytes + scratch_bytes + temp_bytes + 8 * 2 ** 20
    limit = int(min(max(need, 24 * 2 ** 20), V7X_VMEM_BYTES - V7X_VMEM_RESERVE))
    return pltpu.CompilerParams(dimension_semantics=("arbitrary",) * n_grid,
                                vmem_limit_bytes=limit, **kwargs)


def _nbytes(shape, dtype=F32):
    return math.prod(shape) * jnp.dtype(dtype).itemsize


def _silu(x):
    return x * (1.0 / (1.0 + jnp.exp(-x)))


def _softplus(x):
    return jnp.maximum(x, 0.0) + jnp.log1p(jnp.exp(-jnp.abs(x)))


def _log_sigmoid(x):
    return -_softplus(-x)


def _dot(a, b):
    return jnp.dot(a, b, preferred_element_type=F32)


def _dot_nt(a, b):
    return lax.dot_general(a, b, (((1,), (1,)), ((), ())), preferred_element_type=F32)


def _split2(x):
    hi = x.astype(BF16)
    return hi, (x - hi.astype(F32)).astype(BF16)


def _dot3p(a, b):
    return _dot(a[0], b[0]) + _dot(a[0], b[1]) + _dot(a[1], b[0])


def _dot3(a, b):
    return _dot3p(_split2(a), _split2(b))


def _dot3_nt(a, b):
    ah, al = _split2(a)
    bh, bl = _split2(b)
    return _dot_nt(ah, bh) + _dot_nt(ah, bl) + _dot_nt(al, bh)


def _split3(x):
    a = x.astype(BF16)
    r = x - a.astype(F32)
    b = r.astype(BF16)
    r = r - b.astype(F32)
    return a, b, r.astype(BF16)


def _dot01_left(m01, x):
    a, b, c = _split3(x)
    return _dot(m01, a) + _dot(m01, b) + _dot(m01, c)


def _dot01_right(x, m01):
    a, b, c = _split3(x)
    return _dot(a, m01) + _dot(b, m01) + _dot(c, m01)


def _iota(shape, dim, dtype=I32):
    return lax.broadcasted_iota(dtype, shape, dim)


def _tril01(n, block=None, strict=False):
    i = _iota((n, n), 0)
    j = _iota((n, n), 1)
    m = (i > j) if strict else (i >= j)
    if block is not None:
        m = m & ((i // block) == (j // block))
    return jnp.where(m, 1.0, 0.0).astype(BF16)


def _rms_rows(x, g):
    return x * lax.rsqrt(jnp.mean(x * x, axis=-1, keepdims=True) + EPS) * g


def _rot(x, cos, sin):
    half = x.shape[-1] // 2
    x1, x2 = x[:, :half], x[:, half:]
    return jnp.concatenate([x1 * cos - x2 * sin, x1 * sin + x2 * cos], axis=-1)


def _head_norm(o, gain, gate, center):
    if center:
        o = o - jnp.mean(o, axis=-1, keepdims=True)
    o = o * lax.rsqrt(jnp.mean(o * o, axis=-1, keepdims=True) + EPS)
    return o * gain * _silu(gate)


def _pad_rows(x, rows):
    return jnp.concatenate([x, jnp.zeros((rows - x.shape[0], x.shape[1]), x.dtype)], axis=0)


def _ada_body(c_ref, w_ref, b_ref, o_ref):
    o_ref[0] = _dot3(_silu(c_ref[...]), w_ref[0]) + b_ref[0]


def _ada(c_all, w_ada, b_ada):
    depth, d, n6 = w_ada.shape
    m = c_all.shape[0]
    tn = 1024
    blocks = _nbytes((m, d)) + _nbytes((d, tn)) + _nbytes((m, tn))
    return pl.pallas_call(
        _ada_body, name="ada",
        grid=(depth, n6 // tn),
        in_specs=[pl.BlockSpec((m, d), lambda l, j: (0, 0)),
                  pl.BlockSpec((1, d, tn), lambda l, j: (l, 0, j)),
                  pl.BlockSpec((1, 1, tn), lambda l, j: (l, 0, j))],
        out_specs=pl.BlockSpec((1, m, tn), lambda l, j: (l, 0, j)),
        out_shape=jax.ShapeDtypeStruct((depth, m, n6), F32),
        compiler_params=_cparams(2, blocks, temp_bytes=3 * _nbytes((d, tn))),
    )(c_all, w_ada, b_ada.reshape(depth, 1, n6))


def _wparts(x):
    return x if isinstance(x, tuple) else (x,)


def _split_bf16(w):
    hi = lax.bitcast_convert_type(
        lax.bitcast_convert_type(w, jnp.uint32) & jnp.uint32(0xFFFF0000), F32)
    return hi.astype(BF16), (w - hi).astype(BF16)


def _dot_parts(a, b):
    if len(b) == 1:
        return _dot(a[0], b[0])
    return _dot3p(a, b)


INPROJ_SUB = 512


def _inproj_body(n_w, x_ref, sh_ref, sc_ref, g_ref, *refs):
    w_refs, ws_refs = refs[:n_w], refs[n_w:2 * n_w]
    o_ref, os_ref, h_scr = refs[2 * n_w:]
    tm = h_scr.shape[1]
    sb, rb, _ = x_ref.shape

    @pl.when(pl.program_id(1) == 0)
    def _():
        for r0 in range(0, tm, INPROJ_SUB):
            rows = min(INPROJ_SUB, tm)
            if sb == 1:
                x = x_ref[:, r0:r0 + rows, :]
                sc, sh = sc_ref[...], sh_ref[...]
            else:
                s0, s1 = r0 // rb, (r0 + rows) // rb
                x = x_ref[s0:s1]
                sc, sh = sc_ref[s0:s1], sh_ref[s0:s1]
            h = _rms_rows(x, g_ref[...]) * (1.0 + sc) + sh
            h = h.reshape(rows, h.shape[-1])
            hs = _split2(h) if n_w == 2 else (h.astype(BF16),)
            for k in range(n_w):
                h_scr[k, r0:r0 + rows, :] = hs[k]
            os_ref[r0:r0 + rows, :] = _dot_parts(hs, tuple(r[...] for r in ws_refs))

    o_ref[...] = _dot_parts(tuple(h_scr[k] for k in range(n_w)), tuple(r[...] for r in w_refs))


def _inproj(x3, m3, chunk_shift, chunk_scale, gain, w, n_main, ws, sb, rb, tn=512):
    w, ws = _wparts(w), _wparts(ws)
    n_w = len(w)
    sq, rq, d = x3.shape
    nb = rq // rb
    tm = sb * rb
    n_tok = sq * rq
    ns = ws[0].shape[1]
    blocks = (_nbytes((tm, d)) + 2 * _nbytes((sb, V7X_SUBLANES, d))
              + n_w * (_nbytes((d, tn), BF16) + _nbytes((d, ns), BF16))
              + _nbytes((tm, tn)) + _nbytes((tm, ns)))
    w_specs = [pl.BlockSpec((d, tn), lambda i, j: (0, j)) for _ in w]
    ws_specs = [pl.BlockSpec((d, ns), lambda i, j: (0, 0)) for _ in ws]
    return pl.pallas_call(
        functools.partial(_inproj_body, n_w), name="inproj",
        grid=(n_tok // tm, n_main // tn),
        in_specs=[pl.BlockSpec((sb, rb, d), lambda i, j: (i // nb, i % nb, 0)),
                  pl.BlockSpec((sb, 1, d), lambda i, j: (i // nb, 0, chunk_shift)),
                  pl.BlockSpec((sb, 1, d), lambda i, j: (i // nb, 0, chunk_scale)),
                  pl.BlockSpec((1, d), lambda i, j: (0, 0))] + w_specs + ws_specs,
        out_specs=[pl.BlockSpec((tm, tn), lambda i, j: (i, j)),
                   pl.BlockSpec((tm, ns), lambda i, j: (i, 0))],
        out_shape=[jax.ShapeDtypeStruct((n_tok, n_main), F32),
                   jax.ShapeDtypeStruct((n_tok, ns), F32)],
        scratch_shapes=[pltpu.VMEM((n_w, tm, d), BF16)],
        compiler_params=_cparams(2, blocks, n_w * _nbytes((tm, d), BF16),
                                 4 * _nbytes((min(tm, INPROJ_SUB), d))),
    )(x3, m3, m3, gain.reshape(1, d), *w, *ws)


def _outproj_body(n_in, n_w, norm, *refs):
    o_refs = refs[:n_in]
    w_refs = refs[n_in:n_in + n_in * n_w]
    pos = n_in + n_in * n_w
    if norm:
        gn_ref = refs[pos]
        pos += 1
    x_ref, gate_ref, out_ref, o_scr = refs[pos:pos + 4]

    @pl.when(pl.program_id(1) == 0)
    def _():
        for k in range(n_in):
            o = o_refs[k][...]
            if norm:
                o = _rms_rows(o, gn_ref[...])
            parts = _split2(o) if n_w == 2 else (o.astype(BF16),)
            for p in range(n_w):
                o_scr[k, p] = parts[p]

    acc = None
    for k in range(n_in):
        part = _dot_parts(tuple(o_scr[k, p] for p in range(n_w)),
                          tuple(w_refs[k * n_w + p][...] for p in range(n_w)))
        acc = part if acc is None else acc + part
    x = x_ref[...]
    out_ref[...] = x + gate_ref[...] * acc.reshape(x.shape)


def _outproj(o_list, w, x3, m3, chunk_gate, sb, rb, norm_gain=None, tn=512):
    w = _wparts(w)
    n_w = len(w)
    sq, rq, d = x3.shape
    nb = rq // rb
    tm = sb * rb
    n_in = len(o_list)
    nj = d // tn
    widths = [o.shape[1] for o in o_list]
    assert all(wd == widths[0] for wd in widths) and sum(widths) == w[0].shape[0]
    kw = widths[0]
    in_specs = [pl.BlockSpec((tm, kw), lambda i, j: (i, 0)) for _ in o_list]
    args = list(o_list)
    for k in range(n_in):
        for part in w:
            in_specs.append(pl.BlockSpec((kw, tn), lambda i, j, k=k: (k, j)))
            args.append(part)
    if norm_gain is not None:
        in_specs.append(pl.BlockSpec((1, kw), lambda i, j: (0, 0)))
        args.append(norm_gain.reshape(1, kw))
    in_specs += [pl.BlockSpec((sb, rb, tn), lambda i, j: (i // nb, i % nb, j)),
                 pl.BlockSpec((sb, 1, tn), lambda i, j: (i // nb, 0, chunk_gate * nj + j))]
    args += [x3, m3]
    blocks = (n_in * (_nbytes((tm, kw)) + n_w * _nbytes((kw, tn), BF16))
              + 2 * _nbytes((tm, tn)) + _nbytes((sb, V7X_SUBLANES, tn)))
    scr = n_in * n_w * _nbytes((tm, kw), BF16)
    return pl.pallas_call(
        functools.partial(_outproj_body, n_in, n_w, norm_gain is not None), name="outproj",
        grid=((sq * rq) // tm, nj),
        in_specs=in_specs,
        out_specs=pl.BlockSpec((sb, rb, tn), lambda i, j: (i // nb, i % nb, j)),
        out_shape=jax.ShapeDtypeStruct(x3.shape, F32),
        scratch_shapes=[pltpu.VMEM((n_in, n_w, tm, kw), BF16)],
        compiler_params=_cparams(2, blocks, scr, 3 * _nbytes((tm, kw))),
    )(*args)


def _ret_tables(lam, c, width):
    lam1 = lam[:, :1]
    rel = (_iota((c, c), 0) - _iota((c, c), 1)).astype(F32)
    din = jnp.where(rel >= 0, jnp.exp(lam1 * jnp.maximum(rel, 0.0)), 0.0)
    r = _iota((c, width), 0).astype(F32)
    dq = jnp.exp(lam * (r + 1.0))
    dk = jnp.exp(lam * (c - 1.0 - r))
    gc = jnp.exp(lam * float(c))
    return din, dq, dk, gc


def _gla_log_alpha(alow, w2, b2):
    xa = _dot3(alow, w2) + b2
    return _log_sigmoid(xa) * (1.0 / GLA_TEMP)


def _mixa_prompt_body(pa, alow, w2, b2, cos, sin, gnr, gng, lam,
                      o_r, o_g, sret_out, sgla_out,
                      s_ret, st_gla, din_s, dq_s, dk_s, a_s):
    c_id = pl.program_id(1)
    n_c = pl.num_programs(1)
    c = pa.shape[0]
    dk_r, dk_g, dv = A_RET_DK, A_GLA_DK, A_DV

    @pl.when(c_id == 0)
    def _():
        s_ret[...] = jnp.zeros_like(s_ret)
        st_gla[...] = jnp.zeros_like(st_gla)
        for hh in range(RET_HEADS):
            din, dq, dk, _ = _ret_tables(lam[hh], c, dk_r)
            din_s[hh] = din
            dq_s[hh] = dq
            dk_s[hh] = dk

    cs, sn = cos[...], sin[...]
    la = _gla_log_alpha(alow[...], w2[...], b2[...])
    bc_all = _dot01_left(_tril01(c), la)
    rows = _iota((c, dk_g), 0)
    nblk = c // GLA_BLOCK
    row16 = _iota((GLA_BLOCK, dk_g), 0)
    lane16 = _iota((GLA_BLOCK, c), 1)
    for hh in range(RET_HEADS):
        q = _rot(pa[:, A_QR + hh * dk_r:A_QR + (hh + 1) * dk_r], cs, sn)
        k = _rot(pa[:, A_KR + hh * dk_r:A_KR + (hh + 1) * dk_r], cs, sn) * (dk_r ** -0.5)
        qs, vs = _split2(q), _split2(pa[:, A_VR + hh * dv:A_VR + (hh + 1) * dv])
        sc = _dot3_nt(q, k) * din_s[hh]
        s_old = s_ret[hh]
        o = _dot3p(_split2(sc), vs) + _dot3p(qs, _split2(s_old)) * dq_s[hh]
        kd = k * dk_s[hh]
        s_ret[hh] = s_old * jnp.exp(lam[hh] * float(c)) + _dot3p(_split2(kd.T), vs)
        o_r[:, hh * dv:(hh + 1) * dv] = _head_norm(
            o, gnr[:, hh * dv:(hh + 1) * dv], pa[:, A_GR + hh * dv:A_GR + (hh + 1) * dv], True)

        bc = bc_all[:, hh * dk_g:(hh + 1) * dk_g]
        qg_v = pa[:, A_QG + hh * dk_g:A_QG + (hh + 1) * dk_g] * (dk_g ** -0.5)
        kg_v = pa[:, A_KG + hh * dk_g:A_KG + (hh + 1) * dk_g]
        vg_v = pa[:, A_VG + hh * dv:A_VG + (hh + 1) * dv]
        for blk in range(nblk):
            r0 = blk * GLA_BLOCK
            q_i = qg_v[r0:r0 + GLA_BLOCK]
            b_i = bc[r0:r0 + GLA_BLOCK]
            if blk == 0:
                a_i = jnp.zeros((GLA_BLOCK, c), F32)
            else:
                ref_b = bc[r0 - 1:r0]
                qt = q_i * jnp.exp(b_i - ref_b)
                kt = jnp.where(rows < r0, kg_v * jnp.exp(jnp.minimum(ref_b - bc, 0.0)), 0.0)
                a_i = _dot3_nt(qt, kt)
            for jj in range(GLA_BLOCK):
                j = r0 + jj
                e = jnp.exp(jnp.minimum(b_i - bc[j:j + 1], 0.0))
                col = jnp.sum(q_i * e * kg_v[j:j + 1], axis=-1, keepdims=True)
                col = jnp.where(row16[:, :1] >= jj, col, 0.0)
                a_i = a_i + jnp.where(lane16 == j, col, 0.0)
            a_s[hh, r0:r0 + GLA_BLOCK, :] = a_i
        st_old = st_gla[hh]
        og = _dot3(a_s[hh], vg_v) + _dot3_nt(qg_v * jnp.exp(bc), st_old)
        b_last = bc[c - 1:c]
        kdg = kg_v * jnp.exp(b_last - bc)
        st_gla[hh] = st_old * jnp.exp(b_last) + _dot3(vg_v.T, kdg)
        o_g[:, hh * dv:(hh + 1) * dv] = _head_norm(
            og, gng[:, hh * dv:(hh + 1) * dv], pa[:, A_GG + hh * dv:A_GG + (hh + 1) * dv], False)

    @pl.when(c_id == n_c - 1)
    def _():
        for hh in range(RET_HEADS):
            sret_out[0, hh] = s_ret[hh]
            sgla_out[0, hh] = st_gla[hh].T


def _mixa_prompt(pa, pa_low, w2p, b2, cos, sin, gnr, gng, lam, batch, seq):
    c = MIX_CHUNK
    nc = seq // c
    h = RET_HEADS
    dkr, dkg, dv = A_RET_DK, A_GLA_DK, A_DV
    width = pa.shape[1]
    const2 = lambda b, cc: (0, 0)
    in_specs = [pl.BlockSpec((c, width), lambda b, cc: (b * nc + cc, 0)),
                pl.BlockSpec((c, V7X_LANES), lambda b, cc: (b * nc + cc, 0)),
                pl.BlockSpec(w2p.shape, const2), pl.BlockSpec(b2.shape, const2),
                pl.BlockSpec((c, dkr // 2), lambda b, cc: (cc, 0)),
                pl.BlockSpec((c, dkr // 2), lambda b, cc: (cc, 0)),
                pl.BlockSpec(gnr.shape, const2), pl.BlockSpec(gng.shape, const2),
                pl.BlockSpec(lam.shape, lambda b, cc: (0, 0, 0))]
    out_specs = [pl.BlockSpec((c, h * dv), lambda b, cc: (b * nc + cc, 0)),
                 pl.BlockSpec((c, h * dv), lambda b, cc: (b * nc + cc, 0)),
                 pl.BlockSpec((1, h, dkr, dv), lambda b, cc: (b, 0, 0, 0)),
                 pl.BlockSpec((1, h, dkg, dv), lambda b, cc: (b, 0, 0, 0))]
    n_tok = batch * seq
    out_shape = [jax.ShapeDtypeStruct((n_tok, h * dv), F32),
                 jax.ShapeDtypeStruct((n_tok, h * dv), F32),
                 jax.ShapeDtypeStruct((batch, h, dkr, dv), F32),
                 jax.ShapeDtypeStruct((batch, h, dkg, dv), F32)]
    scratch = [pltpu.VMEM((h, dkr, dv), F32), pltpu.VMEM((h, dv, dkg), F32),
               pltpu.VMEM((h, c, c), F32), pltpu.VMEM((h, c, dkr), F32), pltpu.VMEM((h, c, dkr), F32),
               pltpu.VMEM((h, c, c), F32)]
    blocks = _nbytes((c, width)) + 2 * _nbytes((c, h * dv)) + h * (_nbytes((dkr, dv)) + _nbytes((dkg, dv)))
    scr = h * (_nbytes((dkr, dv)) + _nbytes((dv, dkg)) + 2 * _nbytes((c, c)) + 2 * _nbytes((c, dkr)))
    return pl.pallas_call(
        _mixa_prompt_body, name="mixa_prompt",
        grid=(batch, nc),
        in_specs=in_specs, out_specs=out_specs, out_shape=out_shape,
        scratch_shapes=scratch,
        compiler_params=_cparams(2, blocks, scr, 16 * 2 ** 20),
    )(pa, pa_low, w2p, b2, cos, sin, gnr, gng, lam)


def _direct_intra(q, k, v, decay_fn):
    t = q.shape[0]
    row = _iota((t, 1), 0)
    o = jnp.zeros((t, v.shape[1]), F32)
    for j in range(t):
        col = jnp.sum(q * decay_fn(j) * k[j:j + 1], axis=-1, keepdims=True)
        col = jnp.where(row >= j, col, 0.0)
        o = o + col * v[j:j + 1]
    return o


def _mixa_sample_body(sb, pa, alow, w2, b2, cos, sin, gnr, gng, lam, sret_in, sgla_in,
                      o_r, o_g, sret_out, sgla_out):
    t = 8
    dkr, dkg, dv = 256, 128, 256
    cs, sn = cos[...], sin[...]
    la = _gla_log_alpha(alow[...], w2[...], b2[...])
    bc_all = _dot01_left(_tril01(sb * t, block=t), la)
    rel = (_iota((t, 1), 0)).astype(F32)
    for hh in range(RET_HEADS):
        lam_row = lam[hh]
        dq = jnp.exp(lam_row * (rel + 1.0))
        dk = jnp.exp(lam_row * (t - 1.0 - rel))
        gc = jnp.exp(lam_row * float(t))
        for s in range(sb):
            r0 = s * t
            q = _rot(pa[r0:r0 + t, hh * dkr:(hh + 1) * dkr], cs, sn)
            k = _rot(pa[r0:r0 + t, 1024 + hh * dkr:1024 + (hh + 1) * dkr], cs, sn) * (dkr ** -0.5)
            v = pa[r0:r0 + t, 2048 + hh * dv:2048 + (hh + 1) * dv]
            g = pa[r0:r0 + t, 3072 + hh * dv:3072 + (hh + 1) * dv]
            s_old = sret_in[s, hh]
            o = _direct_intra(q, k, v, lambda j: jnp.exp(lam_row * jnp.maximum(rel - float(j), 0.0)))
            o = o + _dot3(q, s_old) * dq
            kdt = _pad_rows(k * dk, V7X_LANES).T
            sret_out[s, hh] = s_old * gc + _dot3(kdt, _pad_rows(v, V7X_LANES))
            o_r[r0:r0 + t, hh * dv:(hh + 1) * dv] = _head_norm(
                o, gnr[:, hh * dv:(hh + 1) * dv], g, True)
            qg = pa[r0:r0 + t, 4096 + hh * dkg:4096 + (hh + 1) * dkg] * (dkg ** -0.5)
            kg = pa[r0:r0 + t, 4608 + hh * dkg:4608 + (hh + 1) * dkg]
            vg = pa[r0:r0 + t, 5120 + hh * dv:5120 + (hh + 1) * dv]
            gg = pa[r0:r0 + t, 6144 + hh * dv:6144 + (hh + 1) * dv]
            bc = bc_all[r0:r0 + t, hh * dkg:(hh + 1) * dkg]
            sg_old = sgla_in[s, hh]
            og = _direct_intra(qg, kg, vg, lambda j: jnp.exp(jnp.minimum(bc - bc[j:j + 1], 0.0)))
            og = og + _dot3(qg * jnp.exp(bc), sg_old)
            b_last = bc[t - 1:t]
            m = _pad_rows(jnp.concatenate([kg * jnp.exp(b_last - bc), jnp.exp(b_last)], axis=0),
                          V7X_LANES).T
            sgla_out[s, hh] = sg_old * m[:, t:t + 1] + _dot3(m, _pad_rows(vg, V7X_LANES))
            o_g[r0:r0 + t, hh * dv:(hh + 1) * dv] = _head_norm(
                og, gng[:, hh * dv:(hh + 1) * dv], gg, False)


def _mixa_sample(pa, pa_low, w2p, b2, cos, sin, gnr, gng, lam, sret, sgla, sb=4):
    nseq = sret.shape[0]
    t = 8
    h, dkr, dkg, dv = RET_HEADS, 256, 128, 256
    width = pa.shape[1]
    const2 = lambda i: (0, 0)
    in_specs = [pl.BlockSpec((sb * t, width), lambda i: (i, 0)),
                pl.BlockSpec((sb * t, V7X_LANES), lambda i: (i, 0)),
                pl.BlockSpec(w2p.shape, const2), pl.BlockSpec(b2.shape, const2),
                pl.BlockSpec(cos.shape, const2), pl.BlockSpec(sin.shape, const2),
                pl.BlockSpec(gnr.shape, const2), pl.BlockSpec(gng.shape, const2),
                pl.BlockSpec(lam.shape, lambda i: (0, 0, 0)),
                pl.BlockSpec((sb, h, dkr, dv), lambda i: (i, 0, 0, 0)),
                pl.BlockSpec((sb, h, dkg, dv), lambda i: (i, 0, 0, 0))]
    out_specs = [pl.BlockSpec((sb * t, h * dv), lambda i: (i, 0)),
                 pl.BlockSpec((sb * t, h * dv), lambda i: (i, 0)),
                 pl.BlockSpec((sb, h, dkr, dv), lambda i: (i, 0, 0, 0)),
                 pl.BlockSpec((sb, h, dkg, dv), lambda i: (i, 0, 0, 0))]
    out_shape = [jax.ShapeDtypeStruct((nseq * t, h * dv), F32),
                 jax.ShapeDtypeStruct((nseq * t, h * dv), F32),
                 jax.ShapeDtypeStruct(sret.shape, F32),
                 jax.ShapeDtypeStruct(sgla.shape, F32)]
    blocks = (_nbytes((sb * t, width)) + 2 * _nbytes((sb, h, dkr, dv))
              + 2 * _nbytes((sb, h, dkg, dv)) + 2 * 2 ** 20)
    return pl.pallas_call(
        functools.partial(_mixa_sample_body, sb), name="mixa_sample",
        grid=(nseq // sb,),
        in_specs=in_specs, out_specs=out_specs, out_shape=out_shape,
        compiler_params=_cparams(1, blocks, 0, 4 * 2 ** 20),
    )(pa, pa_low, w2p, b2, cos, sin, gnr, gng, lam, sret, sgla)


def _conv_prompt_body(x_ref, prev_ref, w_ref, b_ref, o_ref, scr):
    t = x_ref.shape[0]
    first = pl.program_id(1) == 0
    scr[0:V7X_SUBLANES, :] = jnp.where(first, 0.0, prev_ref[...])
    scr[V7X_SUBLANES:V7X_SUBLANES + t, :] = x_ref[...]
    acc = b_ref[...] + scr[V7X_SUBLANES - 3:V7X_SUBLANES - 3 + t, :] * w_ref[0:1, :]
    for j in range(1, CONV_W):
        off = V7X_SUBLANES - 3 + j
        acc = acc + scr[off:off + t, :] * w_ref[j:j + 1, :]
    o_ref[...] = _silu(acc)


def _conv_prompt(pc, conv_w, conv_b, batch, seq, col0, width, tc=512, tn=512):
    nc = seq // tc
    cb0 = col0 // tn
    blocks = 2 * _nbytes((tc, tn)) + _nbytes((V7X_SUBLANES, tn)) * 3
    return pl.pallas_call(
        _conv_prompt_body, name="conv_prompt",
        grid=(batch, nc, width // tn),
        in_specs=[pl.BlockSpec((tc, tn), lambda b, c, j: (b * nc + c, cb0 + j)),
                  pl.BlockSpec((V7X_SUBLANES, tn),
                               lambda b, c, j: (jnp.maximum((b * nc + c) * (tc // V7X_SUBLANES) - 1, 0),
                                                cb0 + j)),
                  pl.BlockSpec((CONV_W, tn), lambda b, c, j: (0, j)),
                  pl.BlockSpec((1, tn), lambda b, c, j: (0, j))],
        out_specs=pl.BlockSpec((tc, tn), lambda b, c, j: (b * nc + c, j)),
        out_shape=jax.ShapeDtypeStruct((batch * seq, width), F32),
        scratch_shapes=[pltpu.VMEM((tc + V7X_SUBLANES, tn), F32)],
        compiler_params=_cparams(3, blocks, _nbytes((tc + 8, tn)), 4 * _nbytes((tc, tn))),
    )(pc, pc, conv_w, conv_b.reshape(1, width))


def _conv_sample_body(x_ref, cache_ref, w_ref, b_ref, o_ref, scr):
    t = x_ref.shape[1]
    hist = CONV_W - 1
    scr[:, t - hist:t, :] = cache_ref[...]
    scr[:, t:2 * t, :] = x_ref[...]
    acc = b_ref[...] + scr[:, t - hist:2 * t - hist, :] * w_ref[0:1, :]
    for j in range(1, CONV_W):
        off = t - hist + j
        acc = acc + scr[:, off:off + t, :] * w_ref[j:j + 1, :]
    o_ref[...] = _silu(acc)


def _conv_sample(pc3, cache, conv_w, conv_b, col0, width, sbc=32, tn=512):
    nseq, t, _ = pc3.shape
    cb0 = col0 // tn
    blocks = 2 * _nbytes((sbc, t, tn)) + _nbytes((sbc, 8, tn))
    return pl.pallas_call(
        _conv_sample_body, name="conv_sample",
        grid=(nseq // sbc, width // tn),
        in_specs=[pl.BlockSpec((sbc, t, tn), lambda i, j: (i, 0, cb0 + j)),
                  pl.BlockSpec((sbc, CONV_W - 1, tn), lambda i, j: (i, 0, j)),
                  pl.BlockSpec((CONV_W, tn), lambda i, j: (0, j)),
                  pl.BlockSpec((1, tn), lambda i, j: (0, j))],
        out_specs=pl.BlockSpec((sbc, t, tn), lambda i, j: (i, 0, j)),
        out_shape=jax.ShapeDtypeStruct((nseq, t, width), F32),
        scratch_shapes=[pltpu.VMEM((sbc, 2 * t, tn), F32)],
        compiler_params=_cparams(2, blocks, _nbytes((sbc, 2 * t, tn)), 4 * _nbytes((sbc, t, tn))),
    )(pc3, cache, conv_w, conv_b.reshape(1, width))


def _expand01(heads, width, first=0):
    r = _iota((V7X_LANES, heads * width), 0)
    l = _iota((V7X_LANES, heads * width), 1)
    return jnp.where((l // width) == r - first, 1.0, 0.0).astype(BF16)


def _ssd_prompt_body(xs, bm, cm, z, dtc, dtr, bias_c, alog_c, bias_r, alog_r, dfull,
                     y_out, s_out, st):
    c_id = pl.program_id(1)
    n_c = pl.num_programs(1)
    l = xs.shape[0]
    groups, n, gw = st.shape
    heads = dtr.shape[1]
    hpg = heads // groups
    p = gw // hpg

    @pl.when(c_id == 0)
    def _():
        st[...] = jnp.zeros_like(st)

    dt_c = _softplus(dtc[...] + bias_c[...])
    cum_c = _dot01_left(_tril01(l), dt_c * (-jnp.exp(alog_c[...])))
    cum_parts, dt_parts = _split3(cum_c), _split3(dt_c)
    dt_r = _softplus(dtr[0] + bias_r[...])
    tri_u = jnp.where(_iota((l, l), 0) <= _iota((l, l), 1), 1.0, 0.0).astype(BF16)
    cum_r = _dot01_right(dt_r * (-jnp.exp(alog_r[...])), tri_u)

    causal = _iota((l, l), 0) >= _iota((l, l), 1)
    lane = _iota((l, 2 * p), 1)
    for g in range(groups):
        x = xs[:, g * gw:(g + 1) * gw]
        bmg = bm[:, g * n:(g + 1) * n]
        cmb = cm[:, g * n:(g + 1) * n].astype(BF16)
        cb = _dot_nt(cmb, bmg.astype(BF16))
        y_tiles = []
        for pair in range(hpg // 2):
            xt = x[:, pair * 2 * p:(pair + 1) * 2 * p]
            acc = None
            for sub in range(2):
                r = g * hpg + 2 * pair + sub
                seg = cum_c[:, r:r + 1] - cum_r[r:r + 1, :]
                w = jnp.where(causal, cb * jnp.exp(jnp.minimum(seg, 0.0)) * dt_r[r:r + 1, :], 0.0)
                xm = jnp.where((lane // p) == sub, xt, 0.0)
                part = _dot(w.astype(BF16), xm.astype(BF16))
                acc = part if acc is None else acc + part
            y_tiles.append(acc)
        y = jnp.concatenate(y_tiles, axis=-1)

        ex = _expand01(hpg, p, g * hpg)
        cum_f = _dot(cum_parts[0], ex) + _dot(cum_parts[1], ex) + _dot(cum_parts[2], ex)
        dt_f = _dot(dt_parts[0], ex) + _dot(dt_parts[1], ex) + _dot(dt_parts[2], ex)
        st_old = st[g]
        y = y + _dot(cmb, st_old.astype(BF16)) * jnp.exp(cum_f)
        cl_f = cum_f[l - 1:l]
        tail = jnp.exp(cl_f - cum_f) * dt_f
        st[g] = st_old * jnp.exp(cl_f) + _dot(bmg.T.astype(BF16), (x * tail).astype(BF16))
        y = y + dfull[:, g * gw:(g + 1) * gw] * x
        y_out[:, g * gw:(g + 1) * gw] = y * _silu(z[:, g * gw:(g + 1) * gw])

    @pl.when(c_id == n_c - 1)
    def _():
        for g in range(groups):
            s_out[0, g * hpg:(g + 1) * hpg] = st[g].T.reshape(hpg, p, n)


def _ssd_prompt(act, pc, dt_c, dt_t, bias_c, alog_c, bias_r, alog_r, dfull, batch, seq):
    l = MIX_CHUNK
    nc = seq // l
    g = SSM_GROUPS
    n = SSM_STATE
    p = SSM_HEADDIM
    heads = dt_t.shape[1]
    d_inner = heads * p
    gw = d_inner // g
    gn = g * n

    def row(b, cc):
        return b * nc + cc

    in_specs = [pl.BlockSpec((l, d_inner), lambda b, cc: (row(b, cc), 0)),
                pl.BlockSpec((l, gn), lambda b, cc: (row(b, cc), d_inner // gn)),
                pl.BlockSpec((l, gn), lambda b, cc: (row(b, cc), d_inner // gn + 1)),
                pl.BlockSpec((l, d_inner), lambda b, cc: (row(b, cc), 0)),
                pl.BlockSpec((l, V7X_LANES), lambda b, cc: (row(b, cc), 0)),
                pl.BlockSpec((1, heads, l), lambda b, cc: (b, 0, cc)),
                pl.BlockSpec((1, V7X_LANES), lambda b, cc: (0, 0)),
                pl.BlockSpec((1, V7X_LANES), lambda b, cc: (0, 0)),
                pl.BlockSpec((heads, l), lambda b, cc: (0, 0)),
                pl.BlockSpec((heads, l), lambda b, cc: (0, 0)),
                pl.BlockSpec((1, d_inner), lambda b, cc: (0, 0))]
    out_specs = [pl.BlockSpec((l, d_inner), lambda b, cc: (row(b, cc), 0)),
                 pl.BlockSpec((1, heads, p, n), lambda b, cc: (b, 0, 0, 0))]
    out_shape = [jax.ShapeDtypeStruct((batch * seq, d_inner), F32),
                 jax.ShapeDtypeStruct((batch, heads, p, n), F32)]
    blocks = 3 * _nbytes((l, d_inner)) + 2 * _nbytes((l, gn)) + _nbytes((heads, p, n))
    return pl.pallas_call(
        _ssd_prompt_body, name="ssd_prompt",
        grid=(batch, nc),
        in_specs=in_specs, out_specs=out_specs, out_shape=out_shape,
        scratch_shapes=[pltpu.VMEM((g, n, gw), F32)],
        compiler_params=_cparams(2, blocks, _nbytes((g, n, gw)), 16 * 2 ** 20),
    )(act, act, act, pc, dt_c, dt_t, bias_c, alog_c, bias_r, alog_r, dfull)


def _ssd_sample_body(sb, xs, bm, cm, z, dtc, bias_c, alog_c, dfull, s_in, y_out, s_out, y_s):
    t = 8
    rows = sb * t
    gw = xs.shape[1]
    hpg = s_in.shape[1]
    p = gw // hpg
    n = bm.shape[1]
    dt_c = _softplus(dtc[...] + bias_c[0])
    cum_c = _dot01_left(_tril01(rows, block=t), dt_c * (-jnp.exp(alog_c[0])))
    ex = _expand01(hpg, p)
    cum_f = _dot01_right(cum_c, ex)
    dt_f = _dot01_right(dt_c, ex)
    x = xs[...]
    x3 = x.reshape(sb, t, gw)
    c3 = cm[...].reshape(sb, t, n)
    b3 = bm[...].reshape(sb, t, n)
    cum3 = cum_f.reshape(sb, t, gw)
    dt3 = dt_f.reshape(sb, t, gw)
    rowi = _iota((sb, t, 1), 1)
    y3 = jnp.zeros((sb, t, gw), F32)
    for j in range(t):
        cbj = jnp.sum(c3 * b3[:, j:j + 1, :], axis=-1, keepdims=True)
        dec = jnp.exp(jnp.minimum(cum3 - cum3[:, j:j + 1, :], 0.0))
        w = jnp.where(rowi >= j, cbj * dec * dt3[:, j:j + 1, :], 0.0)
        y3 = y3 + w * x3[:, j:j + 1, :]
    y_s[...] = y3.reshape(rows, gw)
    cmv = cm[...]
    bmv = bm[...]
    for s in range(sb):
        r0 = s * t
        sg = s_in[s].reshape(hpg * p, n)
        cf = cum_f[r0:r0 + t]
        yi = _dot_nt(cmv[r0:r0 + t].astype(BF16), sg.astype(BF16)) * jnp.exp(cf)
        cl = cf[t - 1:t]
        xt = x[r0:r0 + t] * (jnp.exp(cl - cf) * dt_f[r0:r0 + t])
        m = _pad_rows(jnp.concatenate([xt, jnp.exp(cl)], axis=0), V7X_LANES).T
        s_new = sg * m[:, t:t + 1] + _dot(m.astype(BF16), _pad_rows(bmv[r0:r0 + t], V7X_LANES).astype(BF16))
        s_out[s] = s_new.reshape(hpg, p, n)
        y_s[r0:r0 + t, :] = y_s[r0:r0 + t, :] + yi
    y_out[...] = (y_s[...] + dfull[0] * x) * _silu(z[...])


def _ssd_sample(act, pc, pcs, bias_c, alog_c, dfull, s_ssm, sb=8):
    t = 8
    nseq = s_ssm.shape[0]
    g = SSM_GROUPS
    n = SSM_STATE
    p = SSM_HEADDIM
    d_inner = pc.shape[1] - 6144
    gw = d_inner // g
    hpg = gw // p
    bb0 = d_inner // n
    cb0 = (d_inner + g * n) // n
    rows = sb * t
    in_specs = [pl.BlockSpec((rows, gw), lambda i, gg: (i, gg)),
                pl.BlockSpec((rows, n), lambda i, gg: (i, bb0 + gg)),
                pl.BlockSpec((rows, n), lambda i, gg: (i, cb0 + gg)),
                pl.BlockSpec((rows, gw), lambda i, gg: (i, gg)),
                pl.BlockSpec((rows, V7X_LANES), lambda i, gg: (i, gg)),
                pl.BlockSpec((1, 1, V7X_LANES), lambda i, gg: (gg, 0, 0)),
                pl.BlockSpec((1, 1, V7X_LANES), lambda i, gg: (gg, 0, 0)),
                pl.BlockSpec((1, 1, gw), lambda i, gg: (gg, 0, 0)),
                pl.BlockSpec((sb, hpg, p, n), lambda i, gg: (i, gg, 0, 0))]
    out_specs = [pl.BlockSpec((rows, gw), lambda i, gg: (i, gg)),
                 pl.BlockSpec((sb, hpg, p, n), lambda i, gg: (i, gg, 0, 0))]
    out_shape = [jax.ShapeDtypeStruct((nseq * t, d_inner), F32),
                 jax.ShapeDtypeStruct(s_ssm.shape, F32)]
    blocks = 3 * _nbytes((rows, gw)) + 3 * _nbytes((rows, n)) + 2 * _nbytes((sb, hpg, p, n))
    return pl.pallas_call(
        functools.partial(_ssd_sample_body, sb), name="ssd_sample",
        grid=(nseq // sb, g),
        in_specs=in_specs, out_specs=out_specs, out_shape=out_shape,
        scratch_shapes=[pltpu.VMEM((rows, gw), F32)],
        compiler_params=_cparams(2, blocks, _nbytes((rows, gw)), 8 * 2 ** 20),
    )(act, act, act, pc, pcs, bias_c, alog_c, dfull, s_ssm)


def _moe_pre_body(n_pt, xp_ref, shp_ref, scp_ref, xs_ref, shs_ref, scs_ref, g_ref, wr_ref,
                  h_out, route_out):
    i = pl.program_id(0)

    @pl.when(i < n_pt)
    def _():
        _moe_route(xp_ref, shp_ref, scp_ref, g_ref, wr_ref, h_out, route_out)

    @pl.when(i >= n_pt)
    def _():
        _moe_route(xs_ref, shs_ref, scs_ref, g_ref, wr_ref, h_out, route_out)


def _moe_route(x_ref, sh_ref, sc_ref, g_ref, wr_ref, h_out, route_out):
    x = x_ref[...]
    h = _rms_rows(x, g_ref[...]) * (1.0 + sc_ref[...]) + sh_ref[...]
    h = h.reshape(h_out.shape[0], x.shape[-1])
    h_out[...] = h.reshape(h_out.shape).astype(BF16)
    logits = jnp.dot(h, wr_ref[...], precision=lax.Precision.HIGHEST, preferred_element_type=F32)
    tm = logits.shape[0]
    lane = _iota((tm, V7X_LANES), 1).astype(F32)
    neg = -jnp.inf
    lg = jnp.where(lane < MOE_GROUPS, logits, neg)
    mg = jnp.max(lg, axis=-1, keepdims=True)
    top_g = jnp.min(jnp.where(lg == mg, lane, float(V7X_LANES)), axis=-1, keepdims=True)
    p_top = 1.0 / jnp.sum(jnp.where(lane < MOE_GROUPS, jnp.exp(logits - mg), 0.0),
                          axis=-1, keepdims=True)
    lo = MOE_GROUPS + MOE_PER_GROUP * top_g
    le = jnp.where((lane >= lo) & (lane < lo + MOE_PER_GROUP), logits, neg)
    v1 = jnp.max(le, axis=-1, keepdims=True)
    i1 = jnp.min(jnp.where(le == v1, lane, float(V7X_LANES)), axis=-1, keepdims=True)
    le2 = jnp.where(lane == i1, neg, le)
    v2 = jnp.max(le2, axis=-1, keepdims=True)
    i2 = jnp.min(jnp.where(le2 == v2, lane, float(V7X_LANES)), axis=-1, keepdims=True)
    tt = jnp.exp(v2 - v1)
    w1 = p_top / (1.0 + tt)
    w2 = p_top * tt / (1.0 + tt)
    route_out[...] = jnp.where(lane == 0, i1 - MOE_GROUPS,
                     jnp.where(lane == 1, i2 - MOE_GROUPS,
                     jnp.where(lane == 2, w1, jnp.where(lane == 3, w2, 0.0))))


def _moe_pre(xp, mp, xs_, ms, chunk_shift, chunk_scale, gain, wr, tm):
    bp, tp, d = xp.shape
    bs, ts, _ = xs_.shape
    nb = tp // tm
    n_pt = bp * nb
    sbs = tm // ts
    n_st = bs // sbs
    n_all = bp * tp + bs * ts

    def pi(i):
        return jnp.minimum(i, n_pt - 1)

    def si(i):
        return jnp.maximum(i - n_pt, 0)

    in_specs = [pl.BlockSpec((1, tm, d), lambda i: (pi(i) // nb, pi(i) % nb, 0)),
                pl.BlockSpec((1, 1, d), lambda i: (pi(i) // nb, 0, chunk_shift)),
                pl.BlockSpec((1, 1, d), lambda i: (pi(i) // nb, 0, chunk_scale)),
                pl.BlockSpec((sbs, ts, d), lambda i: (si(i), 0, 0)),
                pl.BlockSpec((sbs, 1, d), lambda i: (si(i), 0, chunk_shift)),
                pl.BlockSpec((sbs, 1, d), lambda i: (si(i), 0, chunk_scale)),
                pl.BlockSpec((1, d), lambda i: (0, 0)),
                pl.BlockSpec((d, V7X_LANES), lambda i: (0, 0))]
    blocks = (3 * _nbytes((tm, d)) + 2 * _nbytes((sbs, 8, d)) + _nbytes((d, V7X_LANES))
              + _nbytes((tm, 128)))
    return pl.pallas_call(
        functools.partial(_moe_pre_body, n_pt), name="moe_pre",
        grid=(n_pt + n_st,),
        in_specs=in_specs,
        out_specs=[pl.BlockSpec((tm, d // V7X_LANES, V7X_LANES), lambda i: (i, 0, 0)),
                   pl.BlockSpec((tm, V7X_LANES), lambda i: (i, 0))],
        out_shape=[jax.ShapeDtypeStruct((n_all, d // V7X_LANES, V7X_LANES), BF16),
                   jax.ShapeDtypeStruct((n_all, V7X_LANES), F32)],
        compiler_params=_cparams(1, blocks, 0, 4 * _nbytes((tm, d))),
    )(xp, mp, mp, xs_, ms, ms, gain.reshape(1, d), wr)


def _rank_body(route_ref, rank_ref, cnt_ref, carry):
    @pl.when(pl.program_id(0) == 0)
    def _():
        carry[...] = jnp.zeros_like(carry)

    r = route_ref[...]
    tm = r.shape[0]
    lane = _iota((tm, V7X_LANES), 1).astype(F32)
    o1 = lane == r[:, 0:1]
    o2 = lane == r[:, 1:2]
    oh = jnp.where(o1, 1.0, 0.0) + jnp.where(o2, 1.0, 0.0)
    pex = _dot(_tril01(tm, strict=True), oh.astype(BF16)) + carry[0:1, :]
    r1 = jnp.sum(jnp.where(o1, pex, 0.0), axis=-1, keepdims=True)
    r2 = jnp.sum(jnp.where(o2, pex, 0.0), axis=-1, keepdims=True)
    rank_ref[...] = jnp.where(lane == 0, r1, jnp.where(lane == 1, r2, 0.0))
    carry[...] = carry[...] + jnp.sum(oh, axis=0, keepdims=True)
    cnt_ref[...] = carry[...]


def _rank(route, tm=256):
    n = route.shape[0]
    return pl.pallas_call(
        _rank_body, name="moe_rank",
        grid=(n // tm,),
        in_specs=[pl.BlockSpec((tm, V7X_LANES), lambda i: (i, 0))],
        out_specs=[pl.BlockSpec((tm, V7X_LANES), lambda i: (i, 0)),
                   pl.BlockSpec((V7X_SUBLANES, V7X_LANES), lambda i: (0, 0))],
        out_shape=[jax.ShapeDtypeStruct((n, V7X_LANES), F32),
                   jax.ShapeDtypeStruct((V7X_SUBLANES, V7X_LANES), F32)],
        scratch_shapes=[pltpu.VMEM((V7X_SUBLANES, V7X_LANES), F32)],
        compiler_params=_cparams(1, 2 * _nbytes((tm, 128))),
    )(route)


def _gather_tokens(src_hbm, dst, sem, n, row_of):
    def issue(r, carry):
        pltpu.make_async_copy(src_hbm.at[row_of(r)], dst.at[r], sem).start()
        return carry

    lax.fori_loop(0, n, issue, 0, unroll=DMA_UNROLL)


def _gather_wait(src_hbm, dst, sem):
    pltpu.make_async_copy(src_hbm.at[pl.ds(0, dst.shape[0])], dst, sem).wait()


def _expert_body(layer, inv_ref, te_ref, first_ref, wslot_ref, nxt_ref, nt_ref,
                 h3_hbm, wg_hbm, wu_hbm, wd_hbm, o_ref,
                 xbuf, wg_f, wu_f, wd_f, wg_b, wu_b, wd_b, xsem, wsem):
    i = pl.program_id(0)
    nt = nt_ref[0]
    tm = xbuf.shape[1]
    active = i < nt

    def w_copies(e, slot):
        return [pltpu.make_async_copy(src.at[layer, e], dst.at[slot], wsem.at[k])
                for k, (src, dst) in enumerate(((wg_hbm, wg_f), (wu_hbm, wu_f), (wd_hbm, wd_f)))]

    def fetch_rows(tile, slot):
        _gather_tokens(h3_hbm, xbuf.at[slot], xsem.at[slot], tm, lambda r: inv_ref[tile * tm + r])

    @pl.when(i == 0)
    def _():
        for cp in w_copies(te_ref[0], wslot_ref[0]):
            cp.start(priority=WEIGHT_DMA_PRIORITY)
        fetch_rows(0, 0)

    @pl.when(active)
    def _():
        slot = i % 2
        ws = wslot_ref[i]

        @pl.when(first_ref[i] == 1)
        def _():
            for cp in w_copies(te_ref[i], ws):
                cp.wait()
            wg_b[...] = wg_f[ws].astype(BF16)
            wu_b[...] = wu_f[ws].astype(BF16)
            wd_b[...] = wd_f[ws].astype(BF16)

            @pl.when(nxt_ref[i] >= 0)
            def _():
                for cp in w_copies(nxt_ref[i], 1 - ws):
                    cp.start(priority=WEIGHT_DMA_PRIORITY)

        @pl.when(i + 1 < nt)
        def _():
            fetch_rows(i + 1, 1 - slot)

        _gather_wait(h3_hbm, xbuf.at[slot], xsem.at[slot])
        xb = xbuf[slot].reshape(tm, wg_b.shape[0])
        hid = _silu(_dot(xb, wg_b[...])) * _dot(xb, wu_b[...])
        o_ref[...] = _dot(hid.astype(BF16), wd_b[...]).reshape(o_ref.shape)

    @pl.when(jnp.logical_not(active))
    def _():
        o_ref[...] = jnp.zeros_like(o_ref)


def _experts(inv, tile_expert, first, wslot, nxt, ntiles, h3, wg, wu, wd, layer):
    _, dc, ln = h3.shape
    d = dc * ln
    ff = wg.shape[3]
    tm = MOE_TILE
    p_pad = inv.shape[0]
    scratch = [pltpu.VMEM((2, tm, dc, ln), BF16),
               pltpu.VMEM((2, d, ff), F32), pltpu.VMEM((2, d, ff), F32), pltpu.VMEM((2, ff, d), F32),
               pltpu.VMEM((d, ff), BF16), pltpu.VMEM((d, ff), BF16), pltpu.VMEM((ff, d), BF16),
               pltpu.SemaphoreType.DMA((2,)), pltpu.SemaphoreType.DMA((3,))]
    scr_bytes = 2 * _nbytes((tm, d), BF16) + 6 * _nbytes((d, ff)) + 3 * _nbytes((d, ff), BF16)
    any_spec = pl.BlockSpec(memory_space=pl.ANY)
    return pl.pallas_call(
        functools.partial(_expert_body, layer), name="moe_experts",
        grid_spec=pltpu.PrefetchScalarGridSpec(
            num_scalar_prefetch=6,
            grid=(p_pad // tm,),
            in_specs=[any_spec, any_spec, any_spec, any_spec],
            out_specs=pl.BlockSpec((tm, dc, ln), lambda i, *_: (i, 0, 0)),
            scratch_shapes=scratch),
        out_shape=jax.ShapeDtypeStruct((p_pad, dc, ln), F32),
        compiler_params=_cparams(1, _nbytes((tm, d)), scr_bytes, 6 * _nbytes((tm, d)),
                                 disable_bounds_checks=True),
    )(inv, tile_expert, first, wslot, nxt, ntiles, h3, wg, wu, wd)


def _combine_body(tok_off, final, pos_ref, eo_hbm, route_ref, x_ref, gate_ref, fn_ref, out_ref,
                  buf, sem):
    i = pl.program_id(0)
    n = pl.num_programs(0)
    tc = route_ref.shape[0]
    d = x_ref.shape[-1]

    def fetch(step, slot):
        base = 2 * (tok_off + step * tc)
        for k in range(MOE_TOPK):
            _gather_tokens(eo_hbm, buf.at[slot, k], sem.at[slot], tc,
                           lambda r, k=k: pos_ref[base + MOE_TOPK * r + k])

    @pl.when(i == 0)
    def _():
        fetch(0, 0)

    slot = i % 2

    @pl.when(i + 1 < n)
    def _():
        fetch(i + 1, 1 - slot)

    for k in range(MOE_TOPK):
        _gather_wait(eo_hbm, buf.at[slot, k], sem.at[slot])
    rt = route_ref[...]
    y = rt[:, 2:3] * buf[slot, 0].reshape(tc, d) + rt[:, 3:4] * buf[slot, 1].reshape(tc, d)
    x = x_ref[...]
    xn = x + gate_ref[...] * y.reshape(x.shape)
    if final:
        xn = _rms_rows(xn, fn_ref[...])
    out_ref[...] = xn


def _combine(pos_flat, eo, route, x3, m3, chunk_gate, sb, rb, tok_off, final_gain=None):
    sq, rq, d = x3.shape
    nb = rq // rb
    tc = sb * rb
    blk_off = tok_off // tc
    final = final_gain is not None
    fn = (final_gain if final else jnp.ones((d,), F32)).reshape(1, d)
    blocks = 2 * _nbytes((tc, d)) + _nbytes((tc, 128)) + _nbytes((sb, 8, d))
    return pl.pallas_call(
        functools.partial(_combine_body, tok_off, final), name="moe_combine",
        grid_spec=pltpu.PrefetchScalarGridSpec(
            num_scalar_prefetch=1,
            grid=((sq * rq) // tc,),
            in_specs=[pl.BlockSpec(memory_space=pl.ANY),
                      pl.BlockSpec((tc, V7X_LANES), lambda i, p: (blk_off + i, 0)),
                      pl.BlockSpec((sb, rb, d), lambda i, p: (i // nb, i % nb, 0)),
                      pl.BlockSpec((sb, 1, d), lambda i, p: (i // nb, 0, chunk_gate)),
                      pl.BlockSpec((1, d), lambda i, p: (0, 0))],
            out_specs=pl.BlockSpec((sb, rb, d), lambda i, p: (i // nb, i % nb, 0)),
            scratch_shapes=[pltpu.VMEM((2, MOE_TOPK, tc, d // V7X_LANES, V7X_LANES), F32),
                            pltpu.SemaphoreType.DMA((2,))]),
        out_shape=jax.ShapeDtypeStruct(x3.shape, F32),
        compiler_params=_cparams(1, blocks, 2 * MOE_TOPK * _nbytes((tc, d)), 4 * _nbytes((tc, d)),
                                 disable_bounds_checks=True),
    )(pos_flat, eo, route, x3, m3, fn)


def _moe_layer(xp, xs_, mp, ms, l, norm_moe, w_group, w_expert, w_gate, w_up, w_down, final_gain):
    bp, tp, d = xp.shape
    bs, ts, _ = xs_.shape
    n_p, n_s = bp * tp, bs * ts
    n_all = n_p + n_s
    wr = jnp.concatenate([w_group[l], jnp.transpose(w_expert[l], (1, 0, 2)).reshape(d, MOE_EXPERTS)],
                         axis=1)
    wr = jnp.pad(wr, ((0, 0), (0, V7X_LANES - wr.shape[1])))
    tm = MOE_TILE
    h_all, route = _moe_pre(xp, mp, xs_, ms, 3, 4, norm_moe[l], wr, tm)
    rank, cnt = _rank(route)
    counts = cnt[0, :MOE_EXPERTS].astype(I32)
    tiles_per = (counts + tm - 1) // tm
    tile_end = jnp.cumsum(tiles_per)
    row_start = (tile_end - tiles_per) * tm
    ntiles = tile_end[-1]
    e_idx = route[:, 0:2].astype(I32)
    pos = row_start[e_idx] + rank[:, 0:2].astype(I32)
    nt_max = (2 * n_all) // tm + MOE_EXPERTS
    p_pad = nt_max * tm
    tile_ids = jnp.arange(nt_max, dtype=I32)
    te = jnp.minimum(jnp.sum((tile_ids[:, None] >= tile_end[None, :]).astype(I32), axis=1),
                     MOE_EXPERTS - 1)
    active = tile_ids < ntiles
    te = jnp.where(active, te, te[jnp.maximum(ntiles - 1, 0)])
    first = (active & ((tile_ids == 0) | (te != jnp.roll(te, 1)))).astype(I32)
    wslot = (jnp.cumsum(first) - 1) % 2
    grp_end = tile_end[te]
    nxt = jnp.where(grp_end < ntiles, te[jnp.minimum(grp_end, nt_max - 1)], -1).astype(I32)
    pos_flat = pos.reshape(-1)
    inv = jnp.zeros((p_pad,), I32).at[pos_flat].set(jnp.arange(2 * n_all, dtype=I32) // 2)
    nt_arr = ntiles.reshape(1).astype(I32)
    eo = _experts(inv, te, first, wslot.astype(I32), nxt, nt_arr, h_all, w_gate, w_up, w_down, l)
    xp_new = _combine(pos_flat, eo, route, xp, mp, 5, 1, tm, 0, final_gain)
    xs_new = _combine(pos_flat, eo, route, xs_, ms, 5, tm // ts, ts, n_p, final_gain)
    return xp_new, xs_new


def kernel(x_prompt, x_sample, c_prompt, c_sample, state_ret, state_gla, state_ssm, cache_conv,
           w_ada, b_ada, norm_mix, norm_moe,
           a_w_in, a_ret_gn, a_gla_w2, a_gla_b2, a_gla_gn, a_w_out,
           c_w_in, c_conv_w, c_conv_b, c_dt_bias, c_a_log, c_d, c_norm, c_w_out,
           moe_w_group, moe_w_expert, moe_w_gate, moe_w_up, moe_w_down, final_norm):
    bp, tp, d = x_prompt.shape
    bs, ts, _ = x_sample.shape
    n_p, n_s = bp * tp, bs * ts

    pad = (-bp) % V7X_SUBLANES
    c_all = jnp.concatenate([c_prompt, jnp.zeros((pad, d), F32), c_sample], axis=0)
    m_all = _ada(c_all, w_ada, b_ada)
    mp = [m_all[l, :bp].reshape(bp, 1, 6 * d) for l in range(2)]
    ms = [m_all[l, bp + pad:].reshape(bs, 1, 6 * d) for l in range(2)]

    tm_p, tm_s = 512, 512

    wa = a_w_in[0]
    n_main_a = (wa.shape[1] // 512) * 512
    wa_low = jnp.pad(wa[:, n_main_a:], ((0, 0), (0, V7X_LANES - (wa.shape[1] - n_main_a))))
    w2p = jnp.pad(a_gla_w2[0], ((0, V7X_LANES - GLA_RANK), (0, 0)))
    b2 = a_gla_b2[0].reshape(1, -1)
    gnr = a_ret_gn[0].reshape(1, -1)
    gng = a_gla_gn[0].reshape(1, -1)
    half = 128
    inv_freq = ROPE_BASE ** (-jnp.arange(half, dtype=F32) / half)
    ang_p = jnp.arange(tp, dtype=F32)[:, None] * inv_freq[None, :]
    ang_s = (PAST_LEN + jnp.arange(ts, dtype=F32))[:, None] * inv_freq[None, :]
    log_gamma = jnp.log(1.0 - 2.0 ** (-5.0 - jnp.arange(RET_HEADS, dtype=F32)))
    lam = jnp.broadcast_to(log_gamma[:, None, None], (RET_HEADS, 1, 256))

    wa_parts, wa_low_parts, wo_parts = _split_bf16(wa), _split_bf16(wa_low), _split_bf16(a_w_out[0])
    pa_p, low_p = _inproj(x_prompt, mp[0], 0, 1, norm_mix[0], wa_parts, n_main_a, wa_low_parts, 1, tm_p)
    pa_s, low_s = _inproj(x_sample, ms[0], 0, 1, norm_mix[0], wa_parts, n_main_a, wa_low_parts,
                          tm_s // ts, ts)
    or_p, og_p, ret_p, gla_p = _mixa_prompt(pa_p, low_p, w2p, b2, jnp.cos(ang_p), jnp.sin(ang_p),
                                            gnr, gng, lam, bp, tp)
    or_s, og_s, ret_s, gla_s = _mixa_sample(pa_s, low_s, w2p, b2, jnp.cos(ang_s), jnp.sin(ang_s),
                                            gnr, gng, lam, state_ret[0], state_gla[0])
    xp = _outproj([or_p, og_p], wo_parts, x_prompt, mp[0], 2, 1, tm_p)
    xs_ = _outproj([or_s, og_s], wo_parts, x_sample, ms[0], 2, tm_s // ts, ts)
    xp, xs_ = _moe_layer(xp, xs_, mp[0], ms[0], 0, norm_moe, moe_w_group, moe_w_expert,
                         moe_w_gate, moe_w_up, moe_w_down, None)

    wc = c_w_in[0]
    n_main_c = (wc.shape[1] // 512) * 512
    heads = wc.shape[1] - n_main_c
    hpg = heads // SSM_GROUPS
    d_inner = heads * SSM_HEADDIM
    conv_dim = n_main_c - d_inner
    wdt = jnp.pad(wc[:, n_main_c:].reshape(d, SSM_GROUPS, hpg),
                  ((0, 0), (0, 0), (0, V7X_LANES - hpg))).reshape(d, SSM_GROUPS * V7X_LANES).astype(BF16)
    wc_b = wc.astype(BF16)
    wdt_all = jnp.pad(wc[:, n_main_c:], ((0, 0), (0, V7X_LANES - heads))).astype(BF16)
    pc_p, dt_p = _inproj(x3=xp, m3=mp[1], chunk_shift=0, chunk_scale=1, gain=norm_mix[1], w=wc_b,
                         n_main=n_main_c, ws=wdt_all, sb=1, rb=2 * tm_p)
    pc_s, dt_s = _inproj(x3=xs_, m3=ms[1], chunk_shift=0, chunk_scale=1, gain=norm_mix[1], w=wc_b,
                         n_main=n_main_c, ws=wdt, sb=tm_s // ts, rb=ts)
    act_p = _conv_prompt(pc_p, c_conv_w[0], c_conv_b[0], bp, tp, d_inner, conv_dim)
    act_s = _conv_sample(pc_s.reshape(bs, ts, n_main_c), cache_conv[0], c_conv_w[0], c_conv_b[0],
                         d_inner, conv_dim).reshape(n_s, conv_dim)
    conv_p = pc_p.reshape(bp, tp, n_main_c)[:, tp - (CONV_W - 1):, d_inner:]
    conv_s = pc_s.reshape(bs, ts, n_main_c)[:, ts - (CONV_W - 1):, d_inner:]

    def lanes_pad(v):
        return jnp.pad(v.reshape(SSM_GROUPS, 1, hpg), ((0, 0), (0, 0), (0, V7X_LANES - hpg)))

    bias_c, alog_c = lanes_pad(c_dt_bias[0]), lanes_pad(c_a_log[0])
    dfull = jnp.repeat(c_d[0], SSM_HEADDIM).reshape(SSM_GROUPS, 1, hpg * SSM_HEADDIM)

    def lanes_all(v):
        return jnp.pad(v.reshape(1, heads), ((0, 0), (0, V7X_LANES - heads)))

    bias_r = jnp.broadcast_to(c_dt_bias[0].reshape(heads, 1), (heads, MIX_CHUNK))
    alog_r = jnp.broadcast_to(c_a_log[0].reshape(heads, 1), (heads, MIX_CHUNK))
    dt_t = jnp.transpose(dt_p[:, :heads].reshape(bp, tp, heads), (0, 2, 1))
    yg_p, ssm_p = _ssd_prompt(act_p, pc_p, dt_p, dt_t, lanes_all(c_dt_bias[0]), lanes_all(c_a_log[0]),
                              bias_r, alog_r, dfull.reshape(1, d_inner), bp, tp)
    yg_s, ssm_s = _ssd_sample(act_s, pc_s, dt_s, bias_c, alog_c, dfull, state_ssm[0])
    wco_b = c_w_out[0].astype(BF16)
    xp = _outproj([yg_p], wco_b, xp, mp[1], 2, 1, 512, norm_gain=c_norm[0])
    xs_ = _outproj([yg_s], wco_b, xs_, ms[1], 2, 512 // ts, ts, norm_gain=c_norm[0])
    yp, ys = _moe_layer(xp, xs_, mp[1], ms[1], 1, norm_moe, moe_w_group, moe_w_expert,
                        moe_w_gate, moe_w_up, moe_w_down, final_norm)

    return (yp, ys, ret_p[None], ret_s[None], gla_p[None], gla_s[None],
            ssm_p[None], ssm_s[None], conv_p[None], conv_s[None])
```

```python
import functools
import math

import jax
import jax.numpy as jnp
from jax import lax
from jax.experimental import pallas as pl
from jax.experimental.pallas import tpu as pltpu

F32 = jnp.float32
BF16 = jnp.bfloat16
I32 = jnp.int32

EPS = 1e-6
PAST_LEN = 16384
ROPE_BASE = 10000.0
GLA_TEMP = 16.0
GLA_RANK = 16
RET_HEADS = 4
GLA_HEADS = 4
SSM_GROUPS = 8
SSM_HEADDIM = 64
SSM_STATE = 128
CONV_W = 4
MOE_GROUPS = 4
MOE_PER_GROUP = 8
MOE_EXPERTS = MOE_GROUPS * MOE_PER_GROUP
MOE_TOPK = 2

V7X_LANES = 128
V7X_SUBLANES = 8
V7X_VMEM_BYTES = 64 * 2 ** 20
V7X_VMEM_RESERVE = 6 * 2 ** 20

A_RET_DK, A_GLA_DK, A_DV = 256, 128, 256
A_QR = 0
A_KR = A_QR + RET_HEADS * A_RET_DK
A_VR = A_KR + RET_HEADS * A_RET_DK
A_GR = A_VR + RET_HEADS * A_DV
A_QG = A_GR + RET_HEADS * A_DV
A_KG = A_QG + GLA_HEADS * A_GLA_DK
A_VG = A_KG + GLA_HEADS * A_GLA_DK
A_GG = A_VG + GLA_HEADS * A_DV

MIX_CHUNK = 128
GLA_BLOCK = 16
MOE_TILE = 256
DMA_UNROLL = 8
ROW_SLOTS = 3
WEIGHT_DMA_PRIORITY = 1


def _cparams(n_grid, block_bytes, scratch_bytes=0, temp_bytes=0, **kwargs):
    need = 2 * block_bytes + scratch_bytes + temp_bytes + 8 * 2 ** 20
    limit = int(min(max(need, 24 * 2 ** 20), V7X_VMEM_BYTES - V7X_VMEM_RESERVE))
    return pltpu.CompilerParams(dimension_semantics=("arbitrary",) * n_grid,
                                vmem_limit_bytes=limit, **kwargs)


def _nbytes(shape, dtype=F32):
    return math.prod(shape) * jnp.dtype(dtype).itemsize


def _silu(x):
    return x * (1.0 / (1.0 + jnp.exp(-x)))


def _softplus(x):
    return jnp.maximum(x, 0.0) + jnp.log1p(jnp.exp(-jnp.abs(x)))


def _log_sigmoid(x):
    return -_softplus(-x)


def _dot(a, b):
    return jnp.dot(a, b, preferred_element_type=F32)


def _dot_nt(a, b):
    return lax.dot_general(a, b, (((1,), (1,)), ((), ())), preferred_element_type=F32)


def _split2(x):
    hi = x.astype(BF16)
    return hi, (x - hi.astype(F32)).astype(BF16)


def _dot3p(a, b):
    return _dot(a[0], b[0]) + _dot(a[0], b[1]) + _dot(a[1], b[0])


def _dot3(a, b):
    return _dot3p(_split2(a), _split2(b))


def _dot3_nt(a, b):
    ah, al = _split2(a)
    bh, bl = _split2(b)
    return _dot_nt(ah, bh) + _dot_nt(ah, bl) + _dot_nt(al, bh)


def _split3(x):
    a = x.astype(BF16)
    r = x - a.astype(F32)
    b = r.astype(BF16)
    r = r - b.astype(F32)
    return a, b, r.astype(BF16)


def _dot01_left(m01, x):
    a, b, c = _split3(x)
    return _dot(m01, a) + _dot(m01, b) + _dot(m01, c)


def _dot01_right(x, m01):
    a, b, c = _split3(x)
    return _dot(a, m01) + _dot(b, m01) + _dot(c, m01)


def _iota(shape, dim, dtype=I32):
    return lax.broadcasted_iota(dtype, shape, dim)


def _tril01(n, block=None, strict=False):
    i = _iota((n, n), 0)
    j = _iota((n, n), 1)
    m = (i > j) if strict else (i >= j)
    if block is not None:
        m = m & ((i // block) == (j // block))
    return jnp.where(m, 1.0, 0.0).astype(BF16)


def _rms_rows(x, g):
    return x * lax.rsqrt(jnp.mean(x * x, axis=-1, keepdims=True) + EPS) * g


def _rot(x, cos, sin):
    half = x.shape[-1] // 2
    x1, x2 = x[:, :half], x[:, half:]
    return jnp.concatenate([x1 * cos - x2 * sin, x1 * sin + x2 * cos], axis=-1)


def _head_norm(o, gain, gate, center):
    if center:
        o = o - jnp.mean(o, axis=-1, keepdims=True)
    o = o * lax.rsqrt(jnp.mean(o * o, axis=-1, keepdims=True) + EPS)
    return o * gain * _silu(gate)


def _pad_rows(x, rows):
    return jnp.concatenate([x, jnp.zeros((rows - x.shape[0], x.shape[1]), x.dtype)], axis=0)


def _ada_body(c_ref, w_ref, b_ref, o_ref):
    o_ref[0] = _dot3(_silu(c_ref[...]), w_ref[0]) + b_ref[0]


def _ada(c_all, w_ada, b_ada):
    depth, d, n6 = w_ada.shape
    m = c_all.shape[0]
    tn = 1024
    blocks = _nbytes((m, d)) + _nbytes((d, tn)) + _nbytes((m, tn))
    return pl.pallas_call(
        _ada_body, name="ada",
        grid=(depth, n6 // tn),
        in_specs=[pl.BlockSpec((m, d), lambda l, j: (0, 0)),
                  pl.BlockSpec((1, d, tn), lambda l, j: (l, 0, j)),
                  pl.BlockSpec((1, 1, tn), lambda l, j: (l, 0, j))],
        out_specs=pl.BlockSpec((1, m, tn), lambda l, j: (l, 0, j)),
        out_shape=jax.ShapeDtypeStruct((depth, m, n6), F32),
        compiler_params=_cparams(2, blocks, temp_bytes=3 * _nbytes((d, tn))),
    )(c_all, w_ada, b_ada.reshape(depth, 1, n6))


def _wparts(x):
    return x if isinstance(x, tuple) else (x,)


def _split_bf16(w):
    hi = lax.bitcast_convert_type(
        lax.bitcast_convert_type(w, jnp.uint32) & jnp.uint32(0xFFFF0000), F32)
    return hi.astype(BF16), (w - hi).astype(BF16)


def _dot_parts(a, b):
    if len(b) == 1:
        return _dot(a[0], b[0])
    return _dot3p(a, b)


INPROJ_SUB = 512


def _inproj_body(n_w, x_ref, sh_ref, sc_ref, g_ref, *refs):
    w_refs, ws_refs = refs[:n_w], refs[n_w:2 * n_w]
    o_ref, os_ref, h_scr = refs[2 * n_w:]
    tm = h_scr.shape[1]
    sb, rb, _ = x_ref.shape

    @pl.when(pl.program_id(1) == 0)
    def _():
        for r0 in range(0, tm, INPROJ_SUB):
            rows = min(INPROJ_SUB, tm)
            if sb == 1:
                x = x_ref[:, r0:r0 + rows, :]
                sc, sh = sc_ref[...], sh_ref[...]
            else:
                s0, s1 = r0 // rb, (r0 + rows) // rb
                x = x_ref[s0:s1]
                sc, sh = sc_ref[s0:s1], sh_ref[s0:s1]
            h = _rms_rows(x, g_ref[...]) * (1.0 + sc) + sh
            h = h.reshape(rows, h.shape[-1])
            hs = _split2(h) if n_w == 2 else (h.astype(BF16),)
            for k in range(n_w):
                h_scr[k, r0:r0 + rows, :] = hs[k]
            os_ref[r0:r0 + rows, :] = _dot_parts(hs, tuple(r[...] for r in ws_refs))

    o_ref[...] = _dot_parts(tuple(h_scr[k] for k in range(n_w)), tuple(r[...] for r in w_refs))


def _inproj(x3, m3, chunk_shift, chunk_scale, gain, w, n_main, ws, sb, rb, tn=512):
    w, ws = _wparts(w), _wparts(ws)
    n_w = len(w)
    sq, rq, d = x3.shape
    nb = rq // rb
    tm = sb * rb
    n_tok = sq * rq
    ns = ws[0].shape[1]
    blocks = (_nbytes((tm, d)) + 2 * _nbytes((sb, V7X_SUBLANES, d))
              + n_w * (_nbytes((d, tn), BF16) + _nbytes((d, ns), BF16))
              + _nbytes((tm, tn)) + _nbytes((tm, ns)))
    w_specs = [pl.BlockSpec((d, tn), lambda i, j: (0, j)) for _ in w]
    ws_specs = [pl.BlockSpec((d, ns), lambda i, j: (0, 0)) for _ in ws]
    return pl.pallas_call(
        functools.partial(_inproj_body, n_w), name="inproj",
        grid=(n_tok // tm, n_main // tn),
        in_specs=[pl.BlockSpec((sb, rb, d), lambda i, j: (i // nb, i % nb, 0)),
                  pl.BlockSpec((sb, 1, d), lambda i, j: (i // nb, 0, chunk_shift)),
                  pl.BlockSpec((sb, 1, d), lambda i, j: (i // nb, 0, chunk_scale)),
                  pl.BlockSpec((1, d), lambda i, j: (0, 0))] + w_specs + ws_specs,
        out_specs=[pl.BlockSpec((tm, tn), lambda i, j: (i, j)),
                   pl.BlockSpec((tm, ns), lambda i, j: (i, 0))],
        out_shape=[jax.ShapeDtypeStruct((n_tok, n_main), F32),
                   jax.ShapeDtypeStruct((n_tok, ns), F32)],
        scratch_shapes=[pltpu.VMEM((n_w, tm, d), BF16)],
        compiler_params=_cparams(2, blocks, n_w * _nbytes((tm, d), BF16),
                                 4 * _nbytes((min(tm, INPROJ_SUB), d))),
    )(x3, m3, m3, gain.reshape(1, d), *w, *ws)


def _outproj_body(n_in, n_w, norm, *refs):
    o_refs = refs[:n_in]
    w_refs = refs[n_in:n_in + n_in * n_w]
    pos = n_in + n_in * n_w
    if norm:
        gn_ref = refs[pos]
        pos += 1
    x_ref, gate_ref, out_ref, o_scr = refs[pos:pos + 4]

    @pl.when(pl.program_id(1) == 0)
    def _():
        for k in range(n_in):
            o = o_refs[k][...]
            if norm:
                o = _rms_rows(o, gn_ref[...])
            parts = _split2(o) if n_w == 2 else (o.astype(BF16),)
            for p in range(n_w):
                o_scr[k, p] = parts[p]

    acc = None
    for k in range(n_in):
        part = _dot_parts(tuple(o_scr[k, p] for p in range(n_w)),
                          tuple(w_refs[k * n_w + p][...] for p in range(n_w)))
        acc = part if acc is None else acc + part
    x = x_ref[...]
    out_ref[...] = x + gate_ref[...] * acc.reshape(x.shape)


def _outproj(o_list, w, x3, m3, chunk_gate, sb, rb, norm_gain=None, tn=512):
    w = _wparts(w)
    n_w = len(w)
    sq, rq, d = x3.shape
    nb = rq // rb
    tm = sb * rb
    n_in = len(o_list)
    nj = d // tn
    widths = [o.shape[1] for o in o_list]
    assert all(wd == widths[0] for wd in widths) and sum(widths) == w[0].shape[0]
    kw = widths[0]
    in_specs = [pl.BlockSpec((tm, kw), lambda i, j: (i, 0)) for _ in o_list]
    args = list(o_list)
    for k in range(n_in):
        for part in w:
            in_specs.append(pl.BlockSpec((kw, tn), lambda i, j, k=k: (k, j)))
            args.append(part)
    if norm_gain is not None:
        in_specs.append(pl.BlockSpec((1, kw), lambda i, j: (0, 0)))
        args.append(norm_gain.reshape(1, kw))
    in_specs += [pl.BlockSpec((sb, rb, tn), lambda i, j: (i // nb, i % nb, j)),
                 pl.BlockSpec((sb, 1, tn), lambda i, j: (i // nb, 0, chunk_gate * nj + j))]
    args += [x3, m3]
    blocks = (n_in * (_nbytes((tm, kw)) + n_w * _nbytes((kw, tn), BF16))
              + 2 * _nbytes((tm, tn)) + _nbytes((sb, V7X_SUBLANES, tn)))
    scr = n_in * n_w * _nbytes((tm, kw), BF16)
    return pl.pallas_call(
        functools.partial(_outproj_body, n_in, n_w, norm_gain is not None), name="outproj",
        grid=((sq * rq) // tm, nj),
        in_specs=in_specs,
        out_specs=pl.BlockSpec((sb, rb, tn), lambda i, j: (i // nb, i % nb, j)),
        out_shape=jax.ShapeDtypeStruct(x3.shape, F32),
        scratch_shapes=[pltpu.VMEM((n_in, n_w, tm, kw), BF16)],
        compiler_params=_cparams(2, blocks, scr, 3 * _nbytes((tm, kw))),
    )(*args)


def _ret_tables(lam, c, width):
    lam1 = lam[:, :1]
    rel = (_iota((c, c), 0) - _iota((c, c), 1)).astype(F32)
    din = jnp.where(rel >= 0, jnp.exp(lam1 * jnp.maximum(rel, 0.0)), 0.0)
    r = _iota((c, width), 0).astype(F32)
    dq = jnp.exp(lam * (r + 1.0))
    dk = jnp.exp(lam * (c - 1.0 - r))
    gc = jnp.exp(lam * float(c))
    return din, dq, dk, gc


def _gla_log_alpha(alow, w2, b2):
    xa = _dot3(alow, w2) + b2
    return _log_sigmoid(xa) * (1.0 / GLA_TEMP)


def _mixa_prompt_body(pa, alow, w2, b2, cos, sin, gnr, gng, lam,
                      o_r, o_g, sret_out, sgla_out,
                      s_ret, st_gla, din_s, dq_s, dk_s, a_s):
    c_id = pl.program_id(1)
    n_c = pl.num_programs(1)
    c = pa.shape[0]
    dk_r, dk_g, dv = A_RET_DK, A_GLA_DK, A_DV

    @pl.when(c_id == 0)
    def _():
        s_ret[...] = jnp.zeros_like(s_ret)
        st_gla[...] = jnp.zeros_like(st_gla)
        for hh in range(RET_HEADS):
            din, dq, dk, _ = _ret_tables(lam[hh], c, dk_r)
            din_s[hh] = din
            dq_s[hh] = dq
            dk_s[hh] = dk

    cs, sn = cos[...], sin[...]
    la = _gla_log_alpha(alow[...], w2[...], b2[...])
    bc_all = _dot01_left(_tril01(c), la)
    rows = _iota((c, dk_g), 0)
    nblk = c // GLA_BLOCK
    row16 = _iota((GLA_BLOCK, dk_g), 0)
    lane16 = _iota((GLA_BLOCK, c), 1)
    for hh in range(RET_HEADS):
        q = _rot(pa[:, A_QR + hh * dk_r:A_QR + (hh + 1) * dk_r], cs, sn)
        k = _rot(pa[:, A_KR + hh * dk_r:A_KR + (hh + 1) * dk_r], cs, sn) * (dk_r ** -0.5)
        qs, vs = _split2(q), _split2(pa[:, A_VR + hh * dv:A_VR + (hh + 1) * dv])
        sc = _dot3_nt(q, k) * din_s[hh]
        s_old = s_ret[hh]
        o = _dot3p(_split2(sc), vs) + _dot3p(qs, _split2(s_old)) * dq_s[hh]
        kd = k * dk_s[hh]
        s_ret[hh] = s_old * jnp.exp(lam[hh] * float(c)) + _dot3p(_split2(kd.T), vs)
        o_r[:, hh * dv:(hh + 1) * dv] = _head_norm(
            o, gnr[:, hh * dv:(hh + 1) * dv], pa[:, A_GR + hh * dv:A_GR + (hh + 1) * dv], True)

        bc = bc_all[:, hh * dk_g:(hh + 1) * dk_g]
        qg_v = pa[:, A_QG + hh * dk_g:A_QG + (hh + 1) * dk_g] * (dk_g ** -0.5)
        kg_v = pa[:, A_KG + hh * dk_g:A_KG + (hh + 1) * dk_g]
        vg_v = pa[:, A_VG + hh * dv:A_VG + (hh + 1) * dv]
        for blk in range(nblk):
            r0 = blk * GLA_BLOCK
            q_i = qg_v[r0:r0 + GLA_BLOCK]
            b_i = bc[r0:r0 + GLA_BLOCK]
            if blk == 0:
                a_i = jnp.zeros((GLA_BLOCK, c), F32)
            else:
                ref_b = bc[r0 - 1:r0]
                qt = q_i * jnp.exp(b_i - ref_b)
                kt = jnp.where(rows < r0, kg_v * jnp.exp(jnp.minimum(ref_b - bc, 0.0)), 0.0)
                a_i = _dot3_nt(qt, kt)
            for jj in range(GLA_BLOCK):
                j = r0 + jj
                e = jnp.exp(jnp.minimum(b_i - bc[j:j + 1], 0.0))
                col = jnp.sum(q_i * e * kg_v[j:j + 1], axis=-1, keepdims=True)
                col = jnp.where(row16[:, :1] >= jj, col, 0.0)
                a_i = a_i + jnp.where(lane16 == j, col, 0.0)
            a_s[hh, r0:r0 + GLA_BLOCK, :] = a_i
        st_old = st_gla[hh]
        og = _dot3(a_s[hh], vg_v) + _dot3_nt(qg_v * jnp.exp(bc), st_old)
        b_last = bc[c - 1:c]
        kdg = kg_v * jnp.exp(b_last - bc)
        st_gla[hh] = st_old * jnp.exp(b_last) + _dot3(vg_v.T, kdg)
        o_g[:, hh * dv:(hh + 1) * dv] = _head_norm(
            og, gng[:, hh * dv:(hh + 1) * dv], pa[:, A_GG + hh * dv:A_GG + (hh + 1) * dv], False)

    @pl.when(c_id == n_c - 1)
    def _():
        for hh in range(RET_HEADS):
            sret_out[0, hh] = s_ret[hh]
            sgla_out[0, hh] = st_gla[hh].T


def _mixa_prompt(pa, pa_low, w2p, b2, cos, sin, gnr, gng, lam, batch, seq):
    c = MIX_CHUNK
    nc = seq // c
    h = RET_HEADS
    dkr, dkg, dv = A_RET_DK, A_GLA_DK, A_DV
    width = pa.shape[1]
    const2 = lambda b, cc: (0, 0)
    in_specs = [pl.BlockSpec((c, width), lambda b, cc: (b * nc + cc, 0)),
                pl.BlockSpec((c, V7X_LANES), lambda b, cc: (b * nc + cc, 0)),
                pl.BlockSpec(w2p.shape, const2), pl.BlockSpec(b2.shape, const2),
                pl.BlockSpec((c, dkr // 2), lambda b, cc: (cc, 0)),
                pl.BlockSpec((c, dkr // 2), lambda b, cc: (cc, 0)),
                pl.BlockSpec(gnr.shape, const2), pl.BlockSpec(gng.shape, const2),
                pl.BlockSpec(lam.shape, lambda b, cc: (0, 0, 0))]
    out_specs = [pl.BlockSpec((c, h * dv), lambda b, cc: (b * nc + cc, 0)),
                 pl.BlockSpec((c, h * dv), lambda b, cc: (b * nc + cc, 0)),
                 pl.BlockSpec((1, h, dkr, dv), lambda b, cc: (b, 0, 0, 0)),
                 pl.BlockSpec((1, h, dkg, dv), lambda b, cc: (b, 0, 0, 0))]
    n_tok = batch * seq
    out_shape = [jax.ShapeDtypeStruct((n_tok, h * dv), F32),
                 jax.ShapeDtypeStruct((n_tok, h * dv), F32),
                 jax.ShapeDtypeStruct((batch, h, dkr, dv), F32),
                 jax.ShapeDtypeStruct((batch, h, dkg, dv), F32)]
    scratch = [pltpu.VMEM((h, dkr, dv), F32), pltpu.VMEM((h, dv, dkg), F32),
               pltpu.VMEM((h, c, c), F32), pltpu.VMEM((h, c, dkr), F32), pltpu.VMEM((h, c, dkr), F32),
               pltpu.VMEM((h, c, c), F32)]
    blocks = _nbytes((c, width)) + 2 * _nbytes((c, h * dv)) + h * (_nbytes((dkr, dv)) + _nbytes((dkg, dv)))
    scr = h * (_nbytes((dkr, dv)) + _nbytes((dv, dkg)) + 2 * _nbytes((c, c)) + 2 * _nbytes((c, dkr)))
    return pl.pallas_call(
        _mixa_prompt_body, name="mixa_prompt",
        grid=(batch, nc),
        in_specs=in_specs, out_specs=out_specs, out_shape=out_shape,
        scratch_shapes=scratch,
        compiler_params=_cparams(2, blocks, scr, 16 * 2 ** 20),
    )(pa, pa_low, w2p, b2, cos, sin, gnr, gng, lam)


def _direct_intra(q, k, v, decay_fn):
    t = q.shape[0]
    row = _iota((t, 1), 0)
    o = jnp.zeros((t, v.shape[1]), F32)
    for j in range(t):
        col = jnp.sum(q * decay_fn(j) * k[j:j + 1], axis=-1, keepdims=True)
        col = jnp.where(row >= j, col, 0.0)
        o = o + col * v[j:j + 1]
    return o


def _mixa_sample_body(sb, pa, alow, w2, b2, cos, sin, gnr, gng, lam, sret_in, sgla_in,
                      o_r, o_g, sret_out, sgla_out):
    t = 8
    dkr, dkg, dv = 256, 128, 256
    cs, sn = cos[...], sin[...]
    la = _gla_log_alpha(alow[...], w2[...], b2[...])
    bc_all = _dot01_left(_tril01(sb * t, block=t), la)
    rel = (_iota((t, 1), 0)).astype(F32)
    for hh in range(RET_HEADS):
        lam_row = lam[hh]
        dq = jnp.exp(lam_row * (rel + 1.0))
        dk = jnp.exp(lam_row * (t - 1.0 - rel))
        gc = jnp.exp(lam_row * float(t))
        for s in range(sb):
            r0 = s * t
            q = _rot(pa[r0:r0 + t, hh * dkr:(hh + 1) * dkr], cs, sn)
            k = _rot(pa[r0:r0 + t, 1024 + hh * dkr:1024 + (hh + 1) * dkr], cs, sn) * (dkr ** -0.5)
            v = pa[r0:r0 + t, 2048 + hh * dv:2048 + (hh + 1) * dv]
            g = pa[r0:r0 + t, 3072 + hh * dv:3072 + (hh + 1) * dv]
            s_old = sret_in[s, hh]
            o = _direct_intra(q, k, v, lambda j: jnp.exp(lam_row * jnp.maximum(rel - float(j), 0.0)))
            o = o + _dot3(q, s_old) * dq
            kdt = _pad_rows(k * dk, V7X_LANES).T
            sret_out[s, hh] = s_old * gc + _dot3(kdt, _pad_rows(v, V7X_LANES))
            o_r[r0:r0 + t, hh * dv:(hh + 1) * dv] = _head_norm(
                o, gnr[:, hh * dv:(hh + 1) * dv], g, True)
            qg = pa[r0:r0 + t, 4096 + hh * dkg:4096 + (hh + 1) * dkg] * (dkg ** -0.5)
            kg = pa[r0:r0 + t, 4608 + hh * dkg:4608 + (hh + 1) * dkg]
            vg = pa[r0:r0 + t, 5120 + hh * dv:5120 + (hh + 1) * dv]
            gg = pa[r0:r0 + t, 6144 + hh * dv:6144 + (hh + 1) * dv]
            bc = bc_all[r0:r0 + t, hh * dkg:(hh + 1) * dkg]
            sg_old = sgla_in[s, hh]
            og = _direct_intra(qg, kg, vg, lambda j: jnp.exp(jnp.minimum(bc - bc[j:j + 1], 0.0)))
            og = og + _dot3(qg * jnp.exp(bc), sg_old)
            b_last = bc[t - 1:t]
            m = _pad_rows(jnp.concatenate([kg * jnp.exp(b_last - bc), jnp.exp(b_last)], axis=0),
                          V7X_LANES).T
            sgla_out[s, hh] = sg_old * m[:, t:t + 1] + _dot3(m, _pad_rows(vg, V7X_LANES))
            o_g[r0:r0 + t, hh * dv:(hh + 1) * dv] = _head_norm(
                og, gng[:, hh * dv:(hh + 1) * dv], gg, False)


def _mixa_sample(pa, pa_low, w2p, b2, cos, sin, gnr, gng, lam, sret, sgla, sb=4):
    nseq = sret.shape[0]
    t = 8
    h, dkr, dkg, dv = RET_HEADS, 256, 128, 256
    width = pa.shape[1]
    const2 = lambda i: (0, 0)
    in_specs = [pl.BlockSpec((sb * t, width), lambda i: (i, 0)),
                pl.BlockSpec((sb * t, V7X_LANES), lambda i: (i, 0)),
                pl.BlockSpec(w2p.shape, const2), pl.BlockSpec(b2.shape, const2),
                pl.BlockSpec(cos.shape, const2), pl.BlockSpec(sin.shape, const2),
                pl.BlockSpec(gnr.shape, const2), pl.BlockSpec(gng.shape, const2),
                pl.BlockSpec(lam.shape, lambda i: (0, 0, 0)),
                pl.BlockSpec((sb, h, dkr, dv), lambda i: (i, 0, 0, 0)),
                pl.BlockSpec((sb, h, dkg, dv), lambda i: (i, 0, 0, 0))]
    out_specs = [pl.BlockSpec((sb * t, h * dv), lambda i: (i, 0)),
                 pl.BlockSpec((sb * t, h * dv), lambda i: (i, 0)),
                 pl.BlockSpec((sb, h, dkr, dv), lambda i: (i, 0, 0, 0)),
                 pl.BlockSpec((sb, h, dkg, dv), lambda i: (i, 0, 0, 0))]
    out_shape = [jax.ShapeDtypeStruct((nseq * t, h * dv), F32),
                 jax.ShapeDtypeStruct((nseq * t, h * dv), F32),
                 jax.ShapeDtypeStruct(sret.shape, F32),
                 jax.ShapeDtypeStruct(sgla.shape, F32)]
    blocks = (_nbytes((sb * t, width)) + 2 * _nbytes((sb, h, dkr, dv))
              + 2 * _nbytes((sb, h, dkg, dv)) + 2 * 2 ** 20)
    return pl.pallas_call(
        functools.partial(_mixa_sample_body, sb), name="mixa_sample",
        grid=(nseq // sb,),
        in_specs=in_specs, out_specs=out_specs, out_shape=out_shape,
        compiler_params=_cparams(1, blocks, 0, 4 * 2 ** 20),
    )(pa, pa_low, w2p, b2, cos, sin, gnr, gng, lam, sret, sgla)


def _conv_prompt_body(x_ref, prev_ref, w_ref, b_ref, o_ref, scr):
    t = x_ref.shape[0]
    first = pl.program_id(1) == 0
    scr[0:V7X_SUBLANES, :] = jnp.where(first, 0.0, prev_ref[...])
    scr[V7X_SUBLANES:V7X_SUBLANES + t, :] = x_ref[...]
    acc = b_ref[...] + scr[V7X_SUBLANES - 3:V7X_SUBLANES - 3 + t, :] * w_ref[0:1, :]
    for j in range(1, CONV_W):
        off = V7X_SUBLANES - 3 + j
        acc = acc + scr[off:off + t, :] * w_ref[j:j + 1, :]
    o_ref[...] = _silu(acc)


def _conv_prompt(pc, conv_w, conv_b, batch, seq, col0, width, tc=512, tn=512):
    nc = seq // tc
    cb0 = col0 // tn
    blocks = 2 * _nbytes((tc, tn)) + _nbytes((V7X_SUBLANES, tn)) * 3
    return pl.pallas_call(
        _conv_prompt_body, name="conv_prompt",
        grid=(batch, nc, width // tn),
        in_specs=[pl.BlockSpec((tc, tn), lambda b, c, j: (b * nc + c, cb0 + j)),
                  pl.BlockSpec((V7X_SUBLANES, tn),
                               lambda b, c, j: (jnp.maximum((b * nc + c) * (tc // V7X_SUBLANES) - 1, 0),
                                                cb0 + j)),
                  pl.BlockSpec((CONV_W, tn), lambda b, c, j: (0, j)),
                  pl.BlockSpec((1, tn), lambda b, c, j: (0, j))],
        out_specs=pl.BlockSpec((tc, tn), lambda b, c, j: (b * nc + c, j)),
        out_shape=jax.ShapeDtypeStruct((batch * seq, width), F32),
        scratch_shapes=[pltpu.VMEM((tc + V7X_SUBLANES, tn), F32)],
        compiler_params=_cparams(3, blocks, _nbytes((tc + 8, tn)), 4 * _nbytes((tc, tn))),
    )(pc, pc, conv_w, conv_b.reshape(1, width))


def _conv_sample_body(x_ref, cache_ref, w_ref, b_ref, o_ref, scr):
    t = x_ref.shape[1]
    hist = CONV_W - 1
    scr[:, t - hist:t, :] = cache_ref[...]
    scr[:, t:2 * t, :] = x_ref[...]
    acc = b_ref[...] + scr[:, t - hist:2 * t - hist, :] * w_ref[0:1, :]
    for j in range(1, CONV_W):
        off = t - hist + j
        acc = acc + scr[:, off:off + t, :] * w_ref[j:j + 1, :]
    o_ref[...] = _silu(acc)


def _conv_sample(pc3, cache, conv_w, conv_b, col0, width, sbc=32, tn=512):
    nseq, t, _ = pc3.shape
    cb0 = col0 // tn
    blocks = 2 * _nbytes((sbc, t, tn)) + _nbytes((sbc, 8, tn))
    return pl.pallas_call(
        _conv_sample_body, name="conv_sample",
        grid=(nseq // sbc, width // tn),
        in_specs=[pl.BlockSpec((sbc, t, tn), lambda i, j: (i, 0, cb0 + j)),
                  pl.BlockSpec((sbc, CONV_W - 1, tn), lambda i, j: (i, 0, j)),
                  pl.BlockSpec((CONV_W, tn), lambda i, j: (0, j)),
                  pl.BlockSpec((1, tn), lambda i, j: (0, j))],
        out_specs=pl.BlockSpec((sbc, t, tn), lambda i, j: (i, 0, j)),
        out_shape=jax.ShapeDtypeStruct((nseq, t, width), F32),
        scratch_shapes=[pltpu.VMEM((sbc, 2 * t, tn), F32)],
        compiler_params=_cparams(2, blocks, _nbytes((sbc, 2 * t, tn)), 4 * _nbytes((sbc, t, tn))),
    )(pc3, cache, conv_w, conv_b.reshape(1, width))


def _expand01(heads, width, first=0):
    r = _iota((V7X_LANES, heads * width), 0)
    l = _iota((V7X_LANES, heads * width), 1)
    return jnp.where((l // width) == r - first, 1.0, 0.0).astype(BF16)


def _ssd_prompt_body(xs, bm, cm, z, dtc, dtr, bias_c, alog_c, bias_r, alog_r, dfull,
                     y_out, s_out, st):
    c_id = pl.program_id(1)
    n_c = pl.num_programs(1)
    l = xs.shape[0]
    groups, n, gw = st.shape
    heads = dtr.shape[1]
    hpg = heads // groups
    p = gw // hpg

    @pl.when(c_id == 0)
    def _():
        st[...] = jnp.zeros_like(st)

    dt_c = _softplus(dtc[...] + bias_c[...])
    cum_c = _dot01_left(_tril01(l), dt_c * (-jnp.exp(alog_c[...])))
    cum_parts, dt_parts = _split3(cum_c), _split3(dt_c)
    dt_r = _softplus(dtr[0] + bias_r[...])
    tri_u = jnp.where(_iota((l, l), 0) <= _iota((l, l), 1), 1.0, 0.0).astype(BF16)
    cum_r = _dot01_right(dt_r * (-jnp.exp(alog_r[...])), tri_u)

    causal = _iota((l, l), 0) >= _iota((l, l), 1)
    lane = _iota((l, 2 * p), 1)
    for g in range(groups):
        x = xs[:, g * gw:(g + 1) * gw]
        bmg = bm[:, g * n:(g + 1) * n]
        cmb = cm[:, g * n:(g + 1) * n].astype(BF16)
        cb = _dot_nt(cmb, bmg.astype(BF16))
        y_tiles = []
        for pair in range(hpg // 2):
            xt = x[:, pair * 2 * p:(pair + 1) * 2 * p]
            acc = None
            for sub in range(2):
                r = g * hpg + 2 * pair + sub
                seg = cum_c[:, r:r + 1] - cum_r[r:r + 1, :]
                w = jnp.where(causal, cb * jnp.exp(jnp.minimum(seg, 0.0)) * dt_r[r:r + 1, :], 0.0)
                xm = jnp.where((lane // p) == sub, xt, 0.0)
                part = _dot(w.astype(BF16), xm.astype(BF16))
                acc = part if acc is None else acc + part
            y_tiles.append(acc)
        y = jnp.concatenate(y_tiles, axis=-1)

        ex = _expand01(hpg, p, g * hpg)
        cum_f = _dot(cum_parts[0], ex) + _dot(cum_parts[1], ex) + _dot(cum_parts[2], ex)
        dt_f = _dot(dt_parts[0], ex) + _dot(dt_parts[1], ex) + _dot(dt_parts[2], ex)
        st_old = st[g]
        y = y + _dot(cmb, st_old.astype(BF16)) * jnp.exp(cum_f)
        cl_f = cum_f[l - 1:l]
        tail = jnp.exp(cl_f - cum_f) * dt_f
        st[g] = st_old * jnp.exp(cl_f) + _dot(bmg.T.astype(BF16), (x * tail).astype(BF16))
        y = y + dfull[:, g * gw:(g + 1) * gw] * x
        y_out[:, g * gw:(g + 1) * gw] = y * _silu(z[:, g * gw:(g + 1) * gw])

    @pl.when(c_id == n_c - 1)
    def _():
        for g in range(groups):
            s_out[0, g * hpg:(g + 1) * hpg] = st[g].T.reshape(hpg, p, n)


def _ssd_prompt(act, pc, dt_c, dt_t, bias_c, alog_c, bias_r, alog_r, dfull, batch, seq):
    l = MIX_CHUNK
    nc = seq // l
    g = SSM_GROUPS
    n = SSM_STATE
    p = SSM_HEADDIM
    heads = dt_t.shape[1]
    d_inner = heads * p
    gw = d_inner // g
    gn = g * n

    def row(b, cc):
        return b * nc + cc

    in_specs = [pl.BlockSpec((l, d_inner), lambda b, cc: (row(b, cc), 0)),
                pl.BlockSpec((l, gn), lambda b, cc: (row(b, cc), d_inner // gn)),
                pl.BlockSpec((l, gn), lambda b, cc: (row(b, cc), d_inner // gn + 1)),
                pl.BlockSpec((l, d_inner), lambda b, cc: (row(b, cc), 0)),
                pl.BlockSpec((l, V7X_LANES), lambda b, cc: (row(b, cc), 0)),
                pl.BlockSpec((1, heads, l), lambda b, cc: (b, 0, cc)),
                pl.BlockSpec((1, V7X_LANES), lambda b, cc: (0, 0)),
                pl.BlockSpec((1, V7X_LANES), lambda b, cc: (0, 0)),
                pl.BlockSpec((heads, l), lambda b, cc: (0, 0)),
                pl.BlockSpec((heads, l), lambda b, cc: (0, 0)),
                pl.BlockSpec((1, d_inner), lambda b, cc: (0, 0))]
    out_specs = [pl.BlockSpec((l, d_inner), lambda b, cc: (row(b, cc), 0)),
                 pl.BlockSpec((1, heads, p, n), lambda b, cc: (b, 0, 0, 0))]
    out_shape = [jax.ShapeDtypeStruct((batch * seq, d_inner), F32),
                 jax.ShapeDtypeStruct((batch, heads, p, n), F32)]
    blocks = 3 * _nbytes((l, d_inner)) + 2 * _nbytes((l, gn)) + _nbytes((heads, p, n))
    return pl.pallas_call(
        _ssd_prompt_body, name="ssd_prompt",
        grid=(batch, nc),
        in_specs=in_specs, out_specs=out_specs, out_shape=out_shape,
        scratch_shapes=[pltpu.VMEM((g, n, gw), F32)],
        compiler_params=_cparams(2, blocks, _nbytes((g, n, gw)), 16 * 2 ** 20),
    )(act, act, act, pc, dt_c, dt_t, bias_c, alog_c, bias_r, alog_r, dfull)


def _ssd_sample_body(sb, xs, bm, cm, z, dtc, bias_c, alog_c, dfull, s_in, y_out, s_out, y_s):
    t = 8
    rows = sb * t
    gw = xs.shape[1]
    hpg = s_in.shape[1]
    p = gw // hpg
    n = bm.shape[1]
    dt_c = _softplus(dtc[...] + bias_c[0])
    cum_c = _dot01_left(_tril01(rows, block=t), dt_c * (-jnp.exp(alog_c[0])))
    ex = _expand01(hpg, p)
    cum_f = _dot01_right(cum_c, ex)
    dt_f = _dot01_right(dt_c, ex)
    x = xs[...]
    x3 = x.reshape(sb, t, gw)
    c3 = cm[...].reshape(sb, t, n)
    b3 = bm[...].reshape(sb, t, n)
    cum3 = cum_f.reshape(sb, t, gw)
    dt3 = dt_f.reshape(sb, t, gw)
    rowi = _iota((sb, t, 1), 1)
    y3 = jnp.zeros((sb, t, gw), F32)
    for j in range(t):
        cbj = jnp.sum(c3 * b3[:, j:j + 1, :], axis=-1, keepdims=True)
        dec = jnp.exp(jnp.minimum(cum3 - cum3[:, j:j + 1, :], 0.0))
        w = jnp.where(rowi >= j, cbj * dec * dt3[:, j:j + 1, :], 0.0)
        y3 = y3 + w * x3[:, j:j + 1, :]
    y_s[...] = y3.reshape(rows, gw)
    cmv = cm[...]
    bmv = bm[...]
    for s in range(sb):
        r0 = s * t
        sg = s_in[s].reshape(hpg * p, n)
        cf = cum_f[r0:r0 + t]
        yi = _dot_nt(cmv[r0:r0 + t].astype(BF16), sg.astype(BF16)) * jnp.exp(cf)
        cl = cf[t - 1:t]
        xt = x[r0:r0 + t] * (jnp.exp(cl - cf) * dt_f[r0:r0 + t])
        m = _pad_rows(jnp.concatenate([xt, jnp.exp(cl)], axis=0), V7X_LANES).T
        s_new = sg * m[:, t:t + 1] + _dot(m.astype(BF16), _pad_rows(bmv[r0:r0 + t], V7X_LANES).astype(BF16))
        s_out[s] = s_new.reshape(hpg, p, n)
        y_s[r0:r0 + t, :] = y_s[r0:r0 + t, :] + yi
    y_out[...] = (y_s[...] + dfull[0] * x) * _silu(z[...])


def _ssd_sample(act, pc, pcs, bias_c, alog_c, dfull, s_ssm, sb=8):
    t = 8
    nseq = s_ssm.shape[0]
    g = SSM_GROUPS
    n = SSM_STATE
    p = SSM_HEADDIM
    d_inner = pc.shape[1] - 6144
    gw = d_inner // g
    hpg = gw // p
    bb0 = d_inner // n
    cb0 = (d_inner + g * n) // n
    rows = sb * t
    in_specs = [pl.BlockSpec((rows, gw), lambda i, gg: (i, gg)),
                pl.BlockSpec((rows, n), lambda i, gg: (i, bb0 + gg)),
                pl.BlockSpec((rows, n), lambda i, gg: (i, cb0 + gg)),
                pl.BlockSpec((rows, gw), lambda i, gg: (i, gg)),
                pl.BlockSpec((rows, V7X_LANES), lambda i, gg: (i, gg)),
                pl.BlockSpec((1, 1, V7X_LANES), lambda i, gg: (gg, 0, 0)),
                pl.BlockSpec((1, 1, V7X_LANES), lambda i, gg: (gg, 0, 0)),
                pl.BlockSpec((1, 1, gw), lambda i, gg: (gg, 0, 0)),
                pl.BlockSpec((sb, hpg, p, n), lambda i, gg: (i, gg, 0, 0))]
    out_specs = [pl.BlockSpec((rows, gw), lambda i, gg: (i, gg)),
                 pl.BlockSpec((sb, hpg, p, n), lambda i, gg: (i, gg, 0, 0))]
    out_shape = [jax.ShapeDtypeStruct((nseq * t, d_inner), F32),
                 jax.ShapeDtypeStruct(s_ssm.shape, F32)]
    blocks = 3 * _nbytes((rows, gw)) + 3 * _nbytes((rows, n)) + 2 * _nbytes((sb, hpg, p, n))
    return pl.pallas_call(
        functools.partial(_ssd_sample_body, sb), name="ssd_sample",
        grid=(nseq // sb, g),
        in_specs=in_specs, out_specs=out_specs, out_shape=out_shape,
        scratch_shapes=[pltpu.VMEM((rows, gw), F32)],
        compiler_params=_cparams(2, blocks, _nbytes((rows, gw)), 8 * 2 ** 20),
    )(act, act, act, pc, pcs, bias_c, alog_c, dfull, s_ssm)


def _moe_pre_body(n_pt, xp_ref, shp_ref, scp_ref, xs_ref, shs_ref, scs_ref, g_ref, wr_ref,
                  h_out, route_out):
    i = pl.program_id(0)

    @pl.when(i < n_pt)
    def _():
        _moe_route(xp_ref, shp_ref, scp_ref, g_ref, wr_ref, h_out, route_out)

    @pl.when(i >= n_pt)
    def _():
        _moe_route(xs_ref, shs_ref, scs_ref, g_ref, wr_ref, h_out, route_out)


def _moe_route(x_ref, sh_ref, sc_ref, g_ref, wr_ref, h_out, route_out):
    x = x_ref[...]
    h = _rms_rows(x, g_ref[...]) * (1.0 + sc_ref[...]) + sh_ref[...]
    h = h.reshape(h_out.shape[0], x.shape[-1])
    h_out[...] = h.reshape(h_out.shape)
    logits = jnp.dot(h, wr_ref[...], precision=lax.Precision.HIGHEST, preferred_element_type=F32)
    tm = logits.shape[0]
    lane = _iota((tm, V7X_LANES), 1).astype(F32)
    neg = -jnp.inf
    lg = jnp.where(lane < MOE_GROUPS, logits, neg)
    mg = jnp.max(lg, axis=-1, keepdims=True)
    top_g = jnp.min(jnp.where(lg == mg, lane, float(V7X_LANES)), axis=-1, keepdims=True)
    p_top = 1.0 / jnp.sum(jnp.where(lane < MOE_GROUPS, jnp.exp(logits - mg), 0.0),
                          axis=-1, keepdims=True)
    lo = MOE_GROUPS + MOE_PER_GROUP * top_g
    le = jnp.where((lane >= lo) & (lane < lo + MOE_PER_GROUP), logits, neg)
    v1 = jnp.max(le, axis=-1, keepdims=True)
    i1 = jnp.min(jnp.where(le == v1, lane, float(V7X_LANES)), axis=-1, keepdims=True)
    le2 = jnp.where(lane == i1, neg, le)
    v2 = jnp.max(le2, axis=-1, keepdims=True)
    i2 = jnp.min(jnp.where(le2 == v2, lane, float(V7X_LANES)), axis=-1, keepdims=True)
    tt = jnp.exp(v2 - v1)
    w1 = p_top / (1.0 + tt)
    w2 = p_top * tt / (1.0 + tt)
    route_out[...] = jnp.where(lane == 0, i1 - MOE_GROUPS,
                     jnp.where(lane == 1, i2 - MOE_GROUPS,
                     jnp.where(lane == 2, w1, jnp.where(lane == 3, w2, 0.0))))


def _moe_pre(xp, mp, xs_, ms, chunk_shift, chunk_scale, gain, wr, tm):
    bp, tp, d = xp.shape
    bs, ts, _ = xs_.shape
    nb = tp // tm
    n_pt = bp * nb
    sbs = tm // ts
    n_st = bs // sbs
    n_all = bp * tp + bs * ts

    def pi(i):
        return jnp.minimum(i, n_pt - 1)

    def si(i):
        return jnp.maximum(i - n_pt, 0)

    in_specs = [pl.BlockSpec((1, tm, d), lambda i: (pi(i) // nb, pi(i) % nb, 0)),
                pl.BlockSpec((1, 1, d), lambda i: (pi(i) // nb, 0, chunk_shift)),
                pl.BlockSpec((1, 1, d), lambda i: (pi(i) // nb, 0, chunk_scale)),
                pl.BlockSpec((sbs, ts, d), lambda i: (si(i), 0, 0)),
                pl.BlockSpec((sbs, 1, d), lambda i: (si(i), 0, chunk_shift)),
                pl.BlockSpec((sbs, 1, d), lambda i: (si(i), 0, chunk_scale)),
                pl.BlockSpec((1, d), lambda i: (0, 0)),
                pl.BlockSpec((d, V7X_LANES), lambda i: (0, 0))]
    blocks = (3 * _nbytes((tm, d)) + 2 * _nbytes((sbs, 8, d)) + _nbytes((d, V7X_LANES))
              + _nbytes((tm, 128)))
    return pl.pallas_call(
        functools.partial(_moe_pre_body, n_pt), name="moe_pre",
        grid=(n_pt + n_st,),
        in_specs=in_specs,
        out_specs=[pl.BlockSpec((tm, d // V7X_LANES, V7X_LANES), lambda i: (i, 0, 0)),
                   pl.BlockSpec((tm, V7X_LANES), lambda i: (i, 0))],
        out_shape=[jax.ShapeDtypeStruct((n_all, d // V7X_LANES, V7X_LANES), F32),
                   jax.ShapeDtypeStruct((n_all, V7X_LANES), F32)],
        compiler_params=_cparams(1, blocks, 0, 4 * _nbytes((tm, d))),
    )(xp, mp, mp, xs_, ms, ms, gain.reshape(1, d), wr)


def _rank_body(route_ref, rank_ref, cnt_ref, carry):
    @pl.when(pl.program_id(0) == 0)
    def _():
        carry[...] = jnp.zeros_like(carry)

    r = route_ref[...]
    tm = r.shape[0]
    lane = _iota((tm, V7X_LANES), 1).astype(F32)
    o1 = lane == r[:, 0:1]
    o2 = lane == r[:, 1:2]
    oh = jnp.where(o1, 1.0, 0.0) + jnp.where(o2, 1.0, 0.0)
    pex = _dot(_tril01(tm, strict=True), oh.astype(BF16)) + carry[0:1, :]
    r1 = jnp.sum(jnp.where(o1, pex, 0.0), axis=-1, keepdims=True)
    r2 = jnp.sum(jnp.where(o2, pex, 0.0), axis=-1, keepdims=True)
    rank_ref[...] = jnp.where(lane == 0, r1, jnp.where(lane == 1, r2, 0.0))
    carry[...] = carry[...] + jnp.sum(oh, axis=0, keepdims=True)
    cnt_ref[...] = carry[...]


def _rank(route, tm=256):
    n = route.shape[0]
    return pl.pallas_call(
        _rank_body, name="moe_rank",
        grid=(n // tm,),
        in_specs=[pl.BlockSpec((tm, V7X_LANES), lambda i: (i, 0))],
        out_specs=[pl.BlockSpec((tm, V7X_LANES), lambda i: (i, 0)),
                   pl.BlockSpec((V7X_SUBLANES, V7X_LANES), lambda i: (0, 0))],
        out_shape=[jax.ShapeDtypeStruct((n, V7X_LANES), F32),
                   jax.ShapeDtypeStruct((V7X_SUBLANES, V7X_LANES), F32)],
        scratch_shapes=[pltpu.VMEM((V7X_SUBLANES, V7X_LANES), F32)],
        compiler_params=_cparams(1, 2 * _nbytes((tm, 128))),
    )(route)


def _gather_tokens(src_hbm, dst, sem, n, row_of):
    def issue(r, carry):
        pltpu.make_async_copy(src_hbm.at[row_of(r)], dst.at[r], sem).start()
        return carry

    lax.fori_loop(0, n, issue, 0, unroll=DMA_UNROLL)


def _gather_wait(src_hbm, dst, sem):
    pltpu.make_async_copy(src_hbm.at[pl.ds(0, dst.shape[0])], dst, sem).wait()


def _expert_body(layer, inv_ref, te_ref, first_ref, wslot_ref, nxt_ref, nt_ref,
                 h3_hbm, wg_hbm, wu_hbm, wd_hbm, o_ref,
                 xbuf, wg_f, wu_f, wd_f, wg_b, wu_b, wd_b, xsem, wsem):
    i = pl.program_id(0)
    nt = nt_ref[0]
    tm = xbuf.shape[1]
    active = i < nt

    def w_copies(e, slot):
        return [pltpu.make_async_copy(src.at[layer, e], dst.at[slot], wsem.at[k])
                for k, (src, dst) in enumerate(((wg_hbm, wg_f), (wu_hbm, wu_f), (wd_hbm, wd_f)))]

    def fetch_rows(tile, slot):
        _gather_tokens(h3_hbm, xbuf.at[slot], xsem.at[slot], tm, lambda r: inv_ref[tile * tm + r])

    @pl.when(i == 0)
    def _():
        for cp in w_copies(te_ref[0], wslot_ref[0]):
            cp.start(priority=WEIGHT_DMA_PRIORITY)
        fetch_rows(0, 0)

        @pl.when(nt > 1)
        def _():
            fetch_rows(1, 1)

    @pl.when(active)
    def _():
        slot = i % ROW_SLOTS
        ws = wslot_ref[i]

        @pl.when(first_ref[i] == 1)
        def _():
            for cp in w_copies(te_ref[i], ws):
                cp.wait()
            wg_b[...] = wg_f[ws].astype(BF16)
            wu_b[...] = wu_f[ws].astype(BF16)
            wd_b[...] = wd_f[ws].astype(BF16)

            @pl.when(nxt_ref[i] >= 0)
            def _():
                for cp in w_copies(nxt_ref[i], 1 - ws):
                    cp.start(priority=WEIGHT_DMA_PRIORITY)

        @pl.when(i + 2 < nt)
        def _():
            fetch_rows(i + 2, (i + 2) % ROW_SLOTS)

        _gather_wait(h3_hbm, xbuf.at[slot], xsem.at[slot])
        xb = xbuf[slot].reshape(tm, wg_b.shape[0]).astype(BF16)
        hid = _silu(_dot(xb, wg_b[...])) * _dot(xb, wu_b[...])
        o_ref[...] = _dot(hid.astype(BF16), wd_b[...]).reshape(o_ref.shape)

    @pl.when(jnp.logical_not(active))
    def _():
        o_ref[...] = jnp.zeros_like(o_ref)


def _experts(inv, tile_expert, first, wslot, nxt, ntiles, h3, wg, wu, wd, layer):
    _, dc, ln = h3.shape
    d = dc * ln
    ff = wg.shape[3]
    tm = MOE_TILE
    p_pad = inv.shape[0]
    scratch = [pltpu.VMEM((ROW_SLOTS, tm, dc, ln), F32),
               pltpu.VMEM((2, d, ff), F32), pltpu.VMEM((2, d, ff), F32), pltpu.VMEM((2, ff, d), F32),
               pltpu.VMEM((d, ff), BF16), pltpu.VMEM((d, ff), BF16), pltpu.VMEM((ff, d), BF16),
               pltpu.SemaphoreType.DMA((ROW_SLOTS,)), pltpu.SemaphoreType.DMA((3,))]
    scr_bytes = ROW_SLOTS * _nbytes((tm, d)) + 6 * _nbytes((d, ff)) + 3 * _nbytes((d, ff), BF16)
    any_spec = pl.BlockSpec(memory_space=pl.ANY)
    return pl.pallas_call(
        functools.partial(_expert_body, layer), name="moe_experts",
        grid_spec=pltpu.PrefetchScalarGridSpec(
            num_scalar_prefetch=6,
            grid=(p_pad // tm,),
            in_specs=[any_spec, any_spec, any_spec, any_spec],
            out_specs=pl.BlockSpec((tm, dc, ln), lambda i, *_: (i, 0, 0)),
            scratch_shapes=scratch),
        out_shape=jax.ShapeDtypeStruct((p_pad, dc, ln), F32),
        compiler_params=_cparams(1, _nbytes((tm, d)), scr_bytes, 3 * _nbytes((tm, d)),
                                 disable_bounds_checks=True),
    )(inv, tile_expert, first, wslot, nxt, ntiles, h3, wg, wu, wd)


def _combine_body(tok_off, final, pos_ref, eo_hbm, route_ref, x_ref, gate_ref, fn_ref, out_ref,
                  buf, sem):
    i = pl.program_id(0)
    n = pl.num_programs(0)
    tc = route_ref.shape[0]
    d = x_ref.shape[-1]

    def fetch(step, slot):
        base = 2 * (tok_off + step * tc)
        for k in range(MOE_TOPK):
            _gather_tokens(eo_hbm, buf.at[slot, k], sem.at[slot], tc,
                           lambda r, k=k: pos_ref[base + MOE_TOPK * r + k])

    @pl.when(i == 0)
    def _():
        fetch(0, 0)

    slot = i % 2

    @pl.when(i + 1 < n)
    def _():
        fetch(i + 1, 1 - slot)

    for k in range(MOE_TOPK):
        _gather_wait(eo_hbm, buf.at[slot, k], sem.at[slot])
    rt = route_ref[...]
    y = rt[:, 2:3] * buf[slot, 0].reshape(tc, d) + rt[:, 3:4] * buf[slot, 1].reshape(tc, d)
    x = x_ref[...]
    xn = x + gate_ref[...] * y.reshape(x.shape)
    if final:
        xn = _rms_rows(xn, fn_ref[...])
    out_ref[...] = xn


def _combine(pos_flat, eo, route, x3, m3, chunk_gate, sb, rb, tok_off, final_gain=None):
    sq, rq, d = x3.shape
    nb = rq // rb
    tc = sb * rb
    blk_off = tok_off // tc
    final = final_gain is not None
    fn = (final_gain if final else jnp.ones((d,), F32)).reshape(1, d)
    blocks = 2 * _nbytes((tc, d)) + _nbytes((tc, 128)) + _nbytes((sb, 8, d))
    return pl.pallas_call(
        functools.partial(_combine_body, tok_off, final), name="moe_combine",
        grid_spec=pltpu.PrefetchScalarGridSpec(
            num_scalar_prefetch=1,
            grid=((sq * rq) // tc,),
            in_specs=[pl.BlockSpec(memory_space=pl.ANY),
                      pl.BlockSpec((tc, V7X_LANES), lambda i, p: (blk_off + i, 0)),
                      pl.BlockSpec((sb, rb, d), lambda i, p: (i // nb, i % nb, 0)),
                      pl.BlockSpec((sb, 1, d), lambda i, p: (i // nb, 0, chunk_gate)),
                      pl.BlockSpec((1, d), lambda i, p: (0, 0))],
            out_specs=pl.BlockSpec((sb, rb, d), lambda i, p: (i // nb, i % nb, 0)),
            scratch_shapes=[pltpu.VMEM((2, MOE_TOPK, tc, d // V7X_LANES, V7X_LANES), F32),
                            pltpu.SemaphoreType.DMA((2,))]),
        out_shape=jax.ShapeDtypeStruct(x3.shape, F32),
        compiler_params=_cparams(1, blocks, 2 * MOE_TOPK * _nbytes((tc, d)), 4 * _nbytes((tc, d)),
                                 disable_bounds_checks=True),
    )(pos_flat, eo, route, x3, m3, fn)


def _moe_layer(xp, xs_, mp, ms, l, norm_moe, w_group, w_expert, w_gate, w_up, w_down, final_gain):
    bp, tp, d = xp.shape
    bs, ts, _ = xs_.shape
    n_p, n_s = bp * tp, bs * ts
    n_all = n_p + n_s
    wr = jnp.concatenate([w_group[l], jnp.transpose(w_expert[l], (1, 0, 2)).reshape(d, MOE_EXPERTS)],
                         axis=1)
    wr = jnp.pad(wr, ((0, 0), (0, V7X_LANES - wr.shape[1])))
    tm = MOE_TILE
    h_all, route = _moe_pre(xp, mp, xs_, ms, 3, 4, norm_moe[l], wr, tm)
    rank, cnt = _rank(route)
    counts = cnt[0, :MOE_EXPERTS].astype(I32)
    tiles_per = (counts + tm - 1) // tm
    tile_end = jnp.cumsum(tiles_per)
    row_start = (tile_end - tiles_per) * tm
    ntiles = tile_end[-1]
    e_idx = route[:, 0:2].astype(I32)
    pos = row_start[e_idx] + rank[:, 0:2].astype(I32)
    nt_max = (2 * n_all) // tm + MOE_EXPERTS
    p_pad = nt_max * tm
    tile_ids = jnp.arange(nt_max, dtype=I32)
    te = jnp.minimum(jnp.sum((tile_ids[:, None] >= tile_end[None, :]).astype(I32), axis=1),
                     MOE_EXPERTS - 1)
    active = tile_ids < ntiles
    te = jnp.where(active, te, te[jnp.maximum(ntiles - 1, 0)])
    first = (active & ((tile_ids == 0) | (te != jnp.roll(te, 1)))).astype(I32)
    wslot = (jnp.cumsum(first) - 1) % 2
    grp_end = tile_end[te]
    nxt = jnp.where(grp_end < ntiles, te[jnp.minimum(grp_end, nt_max - 1)], -1).astype(I32)
    pos_flat = pos.reshape(-1)
    inv = (jnp.arange(p_pad, dtype=I32) % n_all).at[pos_flat].set(jnp.arange(2 * n_all, dtype=I32) // 2)
    nt_arr = ntiles.reshape(1).astype(I32)
    eo = _experts(inv, te, first, wslot.astype(I32), nxt, nt_arr, h_all, w_gate, w_up, w_down, l)
    xp_new = _combine(pos_flat, eo, route, xp, mp, 5, 1, tm, 0, final_gain)
    xs_new = _combine(pos_flat, eo, route, xs_, ms, 5, tm // ts, ts, n_p, final_gain)
    return xp_new, xs_new


def kernel(x_prompt, x_sample, c_prompt, c_sample, state_ret, state_gla, state_ssm, cache_conv,
           w_ada, b_ada, norm_mix, norm_moe,
           a_w_in, a_ret_gn, a_gla_w2, a_gla_b2, a_gla_gn, a_w_out,
           c_w_in, c_conv_w, c_conv_b, c_dt_bias, c_a_log, c_d, c_norm, c_w_out,
           moe_w_group, moe_w_expert, moe_w_gate, moe_w_up, moe_w_down, final_norm):
    bp, tp, d = x_prompt.shape
    bs, ts, _ = x_sample.shape
    n_p, n_s = bp * tp, bs * ts

    pad = (-bp) % V7X_SUBLANES
    c_all = jnp.concatenate([c_prompt, jnp.zeros((pad, d), F32), c_sample], axis=0)
    m_all = _ada(c_all, w_ada, b_ada)
    mp = [m_all[l, :bp].reshape(bp, 1, 6 * d) for l in range(2)]
    ms = [m_all[l, bp + pad:].reshape(bs, 1, 6 * d) for l in range(2)]

    tm_p, tm_s = 512, 512

    wa = a_w_in[0]
    n_main_a = (wa.shape[1] // 512) * 512
    wa_low = jnp.pad(wa[:, n_main_a:], ((0, 0), (0, V7X_LANES - (wa.shape[1] - n_main_a))))
    w2p = jnp.pad(a_gla_w2[0], ((0, V7X_LANES - GLA_RANK), (0, 0)))
    b2 = a_gla_b2[0].reshape(1, -1)
    gnr = a_ret_gn[0].reshape(1, -1)
    gng = a_gla_gn[0].reshape(1, -1)
    half = 128
    inv_freq = ROPE_BASE ** (-jnp.arange(half, dtype=F32) / half)
    ang_p = jnp.arange(tp, dtype=F32)[:, None] * inv_freq[None, :]
    ang_s = (PAST_LEN + jnp.arange(ts, dtype=F32))[:, None] * inv_freq[None, :]
    log_gamma = jnp.log(1.0 - 2.0 ** (-5.0 - jnp.arange(RET_HEADS, dtype=F32)))
    lam = jnp.broadcast_to(log_gamma[:, None, None], (RET_HEADS, 1, 256))

    wa_parts, wa_low_parts, wo_parts = _split_bf16(wa), _split_bf16(wa_low), _split_bf16(a_w_out[0])
    pa_p, low_p = _inproj(x_prompt, mp[0], 0, 1, norm_mix[0], wa_parts, n_main_a, wa_low_parts, 1, tm_p)
    pa_s, low_s = _inproj(x_sample, ms[0], 0, 1, norm_mix[0], wa_parts, n_main_a, wa_low_parts,
                          tm_s // ts, ts)
    or_p, og_p, ret_p, gla_p = _mixa_prompt(pa_p, low_p, w2p, b2, jnp.cos(ang_p), jnp.sin(ang_p),
                                            gnr, gng, lam, bp, tp)
    or_s, og_s, ret_s, gla_s = _mixa_sample(pa_s, low_s, w2p, b2, jnp.cos(ang_s), jnp.sin(ang_s),
                                            gnr, gng, lam, state_ret[0], state_gla[0])
    xp = _outproj([or_p, og_p], wo_parts, x_prompt, mp[0], 2, 1, tm_p)
    xs_ = _outproj([or_s, og_s], wo_parts, x_sample, ms[0], 2, tm_s // ts, ts)
    xp, xs_ = _moe_layer(xp, xs_, mp[0], ms[0], 0, norm_moe, moe_w_group, moe_w_expert,
                         moe_w_gate, moe_w_up, moe_w_down, None)

    wc = c_w_in[0]
    n_main_c = (wc.shape[1] // 512) * 512
    heads = wc.shape[1] - n_main_c
    hpg = heads // SSM_GROUPS
    d_inner = heads * SSM_HEADDIM
    conv_dim = n_main_c - d_inner
    wdt = jnp.pad(wc[:, n_main_c:].reshape(d, SSM_GROUPS, hpg),
                  ((0, 0), (0, 0), (0, V7X_LANES - hpg))).reshape(d, SSM_GROUPS * V7X_LANES).astype(BF16)
    wc_b = wc.astype(BF16)
    wdt_all = jnp.pad(wc[:, n_main_c:], ((0, 0), (0, V7X_LANES - heads))).astype(BF16)
    pc_p, dt_p = _inproj(x3=xp, m3=mp[1], chunk_shift=0, chunk_scale=1, gain=norm_mix[1], w=wc_b,
                         n_main=n_main_c, ws=wdt_all, sb=1, rb=2 * tm_p)
    pc_s, dt_s = _inproj(x3=xs_, m3=ms[1], chunk_shift=0, chunk_scale=1, gain=norm_mix[1], w=wc_b,
                         n_main=n_main_c, ws=wdt, sb=tm_s // ts, rb=ts)
    act_p = _conv_prompt(pc_p, c_conv_w[0], c_conv_b[0], bp, tp, d_inner, conv_dim)
    act_s = _conv_sample(pc_s.reshape(bs, ts, n_main_c), cache_conv[0], c_conv_w[0], c_conv_b[0],
                         d_inner, conv_dim).reshape(n_s, conv_dim)
    conv_p = pc_p.reshape(bp, tp, n_main_c)[:, tp - (CONV_W - 1):, d_inner:]
    conv_s = pc_s.reshape(bs, ts, n_main_c)[:, ts - (CONV_W - 1):, d_inner:]

    def lanes_pad(v):
        return jnp.pad(v.reshape(SSM_GROUPS, 1, hpg), ((0, 0), (0, 0), (0, V7X_LANES - hpg)))

    bias_c, alog_c = lanes_pad(c_dt_bias[0]), lanes_pad(c_a_log[0])
    dfull = jnp.repeat(c_d[0], SSM_HEADDIM).reshape(SSM_GROUPS, 1, hpg * SSM_HEADDIM)

    def lanes_all(v):
        return jnp.pad(v.reshape(1, heads), ((0, 0), (0, V7X_LANES - heads)))

    bias_r = jnp.broadcast_to(c_dt_bias[0].reshape(heads, 1), (heads, MIX_CHUNK))
    alog_r = jnp.broadcast_to(c_a_log[0].reshape(heads, 1), (heads, MIX_CHUNK))
    dt_t = jnp.transpose(dt_p[:, :heads].reshape(bp, tp, heads), (0, 2, 1))
    yg_p, ssm_p = _ssd_prompt(act_p, pc_p, dt_p, dt_t, lanes_all(c_dt_bias[0]), lanes_all(c_a_log[0]),
                              bias_r, alog_r, dfull.reshape(1, d_inner), bp, tp)
    yg_s, ssm_s = _ssd_sample(act_s, pc_s, dt_s, bias_c, alog_c, dfull, state_ssm[0])
    wco_b = c_w_out[0].astype(BF16)
    xp = _outproj([yg_p], wco_b, xp, mp[1], 2, 1, 512, norm_gain=c_norm[0])
    xs_ = _outproj([yg_s], wco_b, xs_, ms[1], 2, 512 // ts, ts, norm_gain=c_norm[0])
    yp, ys = _moe_layer(xp, xs_, mp[1], ms[1], 1, norm_moe, moe_w_group, moe_w_expert,
                        moe_w_gate, moe_w_up, moe_w_down, final_norm)

    return (yp, ys, ret_p[None], ret_s[None], gla_p[None], gla_s[None],
            ssm_p[None], ssm_s[None], conv_p[None], conv_s[None])
```

```python
import functools
import math

import jax
import jax.numpy as jnp
from jax import lax
from jax.experimental import pallas as pl
from jax.experimental.pallas import tpu as pltpu

F32 = jnp.float32
BF16 = jnp.bfloat16
I32 = jnp.int32

EPS = 1e-6
PAST_LEN = 16384
ROPE_BASE = 10000.0
GLA_TEMP = 16.0
GLA_RANK = 16
RET_HEADS = 4
GLA_HEADS = 4
SSM_GROUPS = 8
SSM_HEADDIM = 64
SSM_STATE = 128
CONV_W = 4
MOE_GROUPS = 4
MOE_PER_GROUP = 8
MOE_EXPERTS = MOE_GROUPS * MOE_PER_GROUP
MOE_TOPK = 2

V7X_LANES = 128
V7X_SUBLANES = 8
V7X_VMEM_BYTES = 64 * 2 ** 20
V7X_VMEM_RESERVE = 6 * 2 ** 20

A_RET_DK, A_GLA_DK, A_DV = 256, 128, 256
A_QR = 0
A_KR = A_QR + RET_HEADS * A_RET_DK
A_VR = A_KR + RET_HEADS * A_RET_DK
A_GR = A_VR + RET_HEADS * A_DV
A_QG = A_GR + RET_HEADS * A_DV
A_KG = A_QG + GLA_HEADS * A_GLA_DK
A_VG = A_KG + GLA_HEADS * A_GLA_DK
A_GG = A_VG + GLA_HEADS * A_DV

MIX_CHUNK = 128
GLA_BLOCK = 16
MOE_TILE = 256
DMA_UNROLL = 8
ROW_SLOTS = 3
WEIGHT_DMA_PRIORITY = 1


def _cparams(n_grid, block_bytes, scratch_bytes=0, temp_bytes=0, **kwargs):
    need = 2 * block_bytes + scratch_bytes + temp_bytes + 8 * 2 ** 20
    limit = int(min(max(need, 24 * 2 ** 20), V7X_VMEM_BYTES - V7X_VMEM_RESERVE))
    return pltpu.CompilerParams(dimension_semantics=("arbitrary",) * n_grid,
                                vmem_limit_bytes=limit, **kwargs)


def _nbytes(shape, dtype=F32):
    return math.prod(shape) * jnp.dtype(dtype).itemsize


def _silu(x):
    return x * (1.0 / (1.0 + jnp.exp(-x)))


def _softplus(x):
    return jnp.maximum(x, 0.0) + jnp.log1p(jnp.exp(-jnp.abs(x)))


def _log_sigmoid(x):
    return -_softplus(-x)


def _dot(a, b):
    return jnp.dot(a, b, preferred_element_type=F32)


def _dot_nt(a, b):
    return lax.dot_general(a, b, (((1,), (1,)), ((), ())), preferred_element_type=F32)


def _split2(x):
    hi = x.astype(BF16)
    return hi, (x - hi.astype(F32)).astype(BF16)


def _dot3p(a, b):
    return _dot(a[0], b[0]) + _dot(a[0], b[1]) + _dot(a[1], b[0])


def _dot3(a, b):
    return _dot3p(_split2(a), _split2(b))


def _dot3_nt(a, b):
    ah, al = _split2(a)
    bh, bl = _split2(b)
    return _dot_nt(ah, bh) + _dot_nt(ah, bl) + _dot_nt(al, bh)


def _split3(x):
    a = x.astype(BF16)
    r = x - a.astype(F32)
    b = r.astype(BF16)
    r = r - b.astype(F32)
    return a, b, r.astype(BF16)


def _dot01_left(m01, x):
    a, b, c = _split3(x)
    return _dot(m01, a) + _dot(m01, b) + _dot(m01, c)


def _dot01_right(x, m01):
    a, b, c = _split3(x)
    return _dot(a, m01) + _dot(b, m01) + _dot(c, m01)


def _iota(shape, dim, dtype=I32):
    return lax.broadcasted_iota(dtype, shape, dim)


def _tril01(n, block=None, strict=False):
    i = _iota((n, n), 0)
    j = _iota((n, n), 1)
    m = (i > j) if strict else (i >= j)
    if block is not None:
        m = m & ((i // block) == (j // block))
    return jnp.where(m, 1.0, 0.0).astype(BF16)


def _rms_rows(x, g):
    return x * lax.rsqrt(jnp.mean(x * x, axis=-1, keepdims=True) + EPS) * g


def _rot(x, cos, sin):
    half = x.shape[-1] // 2
    x1, x2 = x[:, :half], x[:, half:]
    return jnp.concatenate([x1 * cos - x2 * sin, x1 * sin + x2 * cos], axis=-1)


def _head_norm(o, gain, gate, center):
    if center:
        o = o - jnp.mean(o, axis=-1, keepdims=True)
    o = o * lax.rsqrt(jnp.mean(o * o, axis=-1, keepdims=True) + EPS)
    return o * gain * _silu(gate)


def _pad_rows(x, rows):
    return jnp.concatenate([x, jnp.zeros((rows - x.shape[0], x.shape[1]), x.dtype)], axis=0)


def _ada_body(c_ref, w_ref, b_ref, o_ref):
    o_ref[0] = _dot3(_silu(c_ref[...]), w_ref[0]) + b_ref[0]


def _ada(c_all, w_ada, b_ada):
    depth, d, n6 = w_ada.shape
    m = c_all.shape[0]
    tn = 1024
    blocks = _nbytes((m, d)) + _nbytes((d, tn)) + _nbytes((m, tn))
    return pl.pallas_call(
        _ada_body, name="ada",
        grid=(depth, n6 // tn),
        in_specs=[pl.BlockSpec((m, d), lambda l, j: (0, 0)),
                  pl.BlockSpec((1, d, tn), lambda l, j: (l, 0, j)),
                  pl.BlockSpec((1, 1, tn), lambda l, j: (l, 0, j))],
        out_specs=pl.BlockSpec((1, m, tn), lambda l, j: (l, 0, j)),
        out_shape=jax.ShapeDtypeStruct((depth, m, n6), F32),
        compiler_params=_cparams(2, blocks, temp_bytes=3 * _nbytes((d, tn))),
    )(c_all, w_ada, b_ada.reshape(depth, 1, n6))


def _wparts(x):
    return x if isinstance(x, tuple) else (x,)


def _split_bf16(w):
    hi = lax.bitcast_convert_type(
        lax.bitcast_convert_type(w, jnp.uint32) & jnp.uint32(0xFFFF0000), F32)
    return hi.astype(BF16), (w - hi).astype(BF16)


def _dot_parts(a, b):
    if len(b) == 1:
        return _dot(a[0], b[0])
    return _dot3p(a, b)


INPROJ_SUB = 512


def _inproj_body(n_w, x_ref, sh_ref, sc_ref, g_ref, *refs):
    w_refs, ws_refs = refs[:n_w], refs[n_w:2 * n_w]
    o_ref, os_ref, h_scr = refs[2 * n_w:]
    tm = h_scr.shape[1]
    sb, rb, _ = x_ref.shape

    @pl.when(pl.program_id(1) == 0)
    def _():
        for r0 in range(0, tm, INPROJ_SUB):
            rows = min(INPROJ_SUB, tm)
            if sb == 1:
                x = x_ref[:, r0:r0 + rows, :]
                sc, sh = sc_ref[...], sh_ref[...]
            else:
                s0, s1 = r0 // rb, (r0 + rows) // rb
                x = x_ref[s0:s1]
                sc, sh = sc_ref[s0:s1], sh_ref[s0:s1]
            h = _rms_rows(x, g_ref[...]) * (1.0 + sc) + sh
            h = h.reshape(rows, h.shape[-1])
            hs = _split2(h) if n_w == 2 else (h.astype(BF16),)
            for k in range(n_w):
                h_scr[k, r0:r0 + rows, :] = hs[k]
            os_ref[r0:r0 + rows, :] = _dot_parts(hs, tuple(r[...] for r in ws_refs))

    o_ref[...] = _dot_parts(tuple(h_scr[k] for k in range(n_w)), tuple(r[...] for r in w_refs))


def _inproj(x3, m3, chunk_shift, chunk_scale, gain, w, n_main, ws, sb, rb, tn=512):
    w, ws = _wparts(w), _wparts(ws)
    n_w = len(w)
    sq, rq, d = x3.shape
    nb = rq // rb
    tm = sb * rb
    n_tok = sq * rq
    ns = ws[0].shape[1]
    blocks = (_nbytes((tm, d)) + 2 * _nbytes((sb, V7X_SUBLANES, d))
              + n_w * (_nbytes((d, tn), BF16) + _nbytes((d, ns), BF16))
              + _nbytes((tm, tn)) + _nbytes((tm, ns)))
    w_specs = [pl.BlockSpec((d, tn), lambda i, j: (0, j)) for _ in w]
    ws_specs = [pl.BlockSpec((d, ns), lambda i, j: (0, 0)) for _ in ws]
    return pl.pallas_call(
        functools.partial(_inproj_body, n_w), name="inproj",
        grid=(n_tok // tm, n_main // tn),
        in_specs=[pl.BlockSpec((sb, rb, d), lambda i, j: (i // nb, i % nb, 0)),
                  pl.BlockSpec((sb, 1, d), lambda i, j: (i // nb, 0, chunk_shift)),
                  pl.BlockSpec((sb, 1, d), lambda i, j: (i // nb, 0, chunk_scale)),
                  pl.BlockSpec((1, d), lambda i, j: (0, 0))] + w_specs + ws_specs,
        out_specs=[pl.BlockSpec((tm, tn), lambda i, j: (i, j)),
                   pl.BlockSpec((tm, ns), lambda i, j: (i, 0))],
        out_shape=[jax.ShapeDtypeStruct((n_tok, n_main), F32),
                   jax.ShapeDtypeStruct((n_tok, ns), F32)],
        scratch_shapes=[pltpu.VMEM((n_w, tm, d), BF16)],
        compiler_params=_cparams(2, blocks, n_w * _nbytes((tm, d), BF16),
                                 4 * _nbytes((min(tm, INPROJ_SUB), d))),
    )(x3, m3, m3, gain.reshape(1, d), *w, *ws)


def _outproj_body(n_in, n_w, norm, *refs):
    o_refs = refs[:n_in]
    w_refs = refs[n_in:n_in + n_in * n_w]
    pos = n_in + n_in * n_w
    if norm:
        gn_ref = refs[pos]
        pos += 1
    x_ref, gate_ref, out_ref, o_scr = refs[pos:pos + 4]

    @pl.when(pl.program_id(1) == 0)
    def _():
        for k in range(n_in):
            o = o_refs[k][...]
            if norm:
                o = _rms_rows(o, gn_ref[...])
            parts = _split2(o) if n_w == 2 else (o.astype(BF16),)
            for p in range(n_w):
                o_scr[k, p] = parts[p]

    acc = None
    for k in range(n_in):
        part = _dot_parts(tuple(o_scr[k, p] for p in range(n_w)),
                          tuple(w_refs[k * n_w + p][...] for p in range(n_w)))
        acc = part if acc is None else acc + part
    x = x_ref[...]
    out_ref[...] = x + gate_ref[...] * acc.reshape(x.shape)


def _outproj(o_list, w, x3, m3, chunk_gate, sb, rb, norm_gain=None, tn=512):
    w = _wparts(w)
    n_w = len(w)
    sq, rq, d = x3.shape
    nb = rq // rb
    tm = sb * rb
    n_in = len(o_list)
    nj = d // tn
    widths = [o.shape[1] for o in o_list]
    assert all(wd == widths[0] for wd in widths) and sum(widths) == w[0].shape[0]
    kw = widths[0]
    in_specs = [pl.BlockSpec((tm, kw), lambda i, j: (i, 0)) for _ in o_list]
    args = list(o_list)
    for k in range(n_in):
        for part in w:
            mode = pl.Buffered(1) if (nj == 1 and n_in == 1) else None
            in_specs.append(pl.BlockSpec((kw, tn), lambda i, j, k=k: (k, j), pipeline_mode=mode))
            args.append(part)
    if norm_gain is not None:
        in_specs.append(pl.BlockSpec((1, kw), lambda i, j: (0, 0)))
        args.append(norm_gain.reshape(1, kw))
    in_specs += [pl.BlockSpec((sb, rb, tn), lambda i, j: (i // nb, i % nb, j)),
                 pl.BlockSpec((sb, 1, tn), lambda i, j: (i // nb, 0, chunk_gate * nj + j))]
    args += [x3, m3]
    blocks = (n_in * (_nbytes((tm, kw)) + n_w * _nbytes((kw, tn), BF16))
              + 2 * _nbytes((tm, tn)) + _nbytes((sb, V7X_SUBLANES, tn)))
    scr = n_in * n_w * _nbytes((tm, kw), BF16)
    return pl.pallas_call(
        functools.partial(_outproj_body, n_in, n_w, norm_gain is not None), name="outproj",
        grid=((sq * rq) // tm, nj),
        in_specs=in_specs,
        out_specs=pl.BlockSpec((sb, rb, tn), lambda i, j: (i // nb, i % nb, j)),
        out_shape=jax.ShapeDtypeStruct(x3.shape, F32),
        scratch_shapes=[pltpu.VMEM((n_in, n_w, tm, kw), BF16)],
        compiler_params=_cparams(2, blocks, scr, 3 * _nbytes((tm, kw))),
    )(*args)


def _ret_tables(lam, c, width):
    lam1 = lam[:, :1]
    rel = (_iota((c, c), 0) - _iota((c, c), 1)).astype(F32)
    din = jnp.where(rel >= 0, jnp.exp(lam1 * jnp.maximum(rel, 0.0)), 0.0)
    r = _iota((c, width), 0).astype(F32)
    dq = jnp.exp(lam * (r + 1.0))
    dk = jnp.exp(lam * (c - 1.0 - r))
    gc = jnp.exp(lam * float(c))
    return din, dq, dk, gc


def _gla_log_alpha(alow, w2, b2):
    xa = _dot3(alow, w2) + b2
    return _log_sigmoid(xa) * (1.0 / GLA_TEMP)


def _mixa_prompt_body(pa, alow, w2, b2, cos, sin, gnr, gng, lam,
                      o_r, o_g, sret_out, sgla_out,
                      s_ret, st_gla, din_s, dq_s, dk_s, a_s):
    c_id = pl.program_id(1)
    n_c = pl.num_programs(1)
    c = pa.shape[0]
    dk_r, dk_g, dv = A_RET_DK, A_GLA_DK, A_DV

    @pl.when(c_id == 0)
    def _():
        s_ret[...] = jnp.zeros_like(s_ret)
        st_gla[...] = jnp.zeros_like(st_gla)
        for hh in range(RET_HEADS):
            din, dq, dk, _ = _ret_tables(lam[hh], c, dk_r)
            din_s[hh] = din
            dq_s[hh] = dq
            dk_s[hh] = dk

    cs, sn = cos[...], sin[...]
    la = _gla_log_alpha(alow[...], w2[...], b2[...])
    bc_all = _dot01_left(_tril01(c), la)
    rows = _iota((c, dk_g), 0)
    nblk = c // GLA_BLOCK
    row16 = _iota((GLA_BLOCK, dk_g), 0)
    lane16 = _iota((GLA_BLOCK, c), 1)
    for hh in range(RET_HEADS):
        q = _rot(pa[:, A_QR + hh * dk_r:A_QR + (hh + 1) * dk_r], cs, sn)
        k = _rot(pa[:, A_KR + hh * dk_r:A_KR + (hh + 1) * dk_r], cs, sn) * (dk_r ** -0.5)
        qs, vs = _split2(q), _split2(pa[:, A_VR + hh * dv:A_VR + (hh + 1) * dv])
        sc = _dot3_nt(q, k) * din_s[hh]
        s_old = s_ret[hh]
        o = _dot3p(_split2(sc), vs) + _dot3p(qs, _split2(s_old)) * dq_s[hh]
        kd = k * dk_s[hh]
        s_ret[hh] = s_old * jnp.exp(lam[hh] * float(c)) + _dot3p(_split2(kd.T), vs)
        o_r[:, hh * dv:(hh + 1) * dv] = _head_norm(
            o, gnr[:, hh * dv:(hh + 1) * dv], pa[:, A_GR + hh * dv:A_GR + (hh + 1) * dv], True)

        bc = bc_all[:, hh * dk_g:(hh + 1) * dk_g]
        qg_v = pa[:, A_QG + hh * dk_g:A_QG + (hh + 1) * dk_g] * (dk_g ** -0.5)
        kg_v = pa[:, A_KG + hh * dk_g:A_KG + (hh + 1) * dk_g]
        vg_v = pa[:, A_VG + hh * dv:A_VG + (hh + 1) * dv]
        for blk in range(nblk):
            r0 = blk * GLA_BLOCK
            q_i = qg_v[r0:r0 + GLA_BLOCK]
            b_i = bc[r0:r0 + GLA_BLOCK]
            if blk == 0:
                a_i = jnp.zeros((GLA_BLOCK, c), F32)
            else:
                ref_b = bc[r0 - 1:r0]
                qt = q_i * jnp.exp(b_i - ref_b)
                kt = jnp.where(rows < r0, kg_v * jnp.exp(jnp.minimum(ref_b - bc, 0.0)), 0.0)
                a_i = _dot3_nt(qt, kt)
            for jj in range(GLA_BLOCK):
                j = r0 + jj
                e = jnp.exp(jnp.minimum(b_i - bc[j:j + 1], 0.0))
                col = jnp.sum(q_i * e * kg_v[j:j + 1], axis=-1, keepdims=True)
                col = jnp.where(row16[:, :1] >= jj, col, 0.0)
                a_i = a_i + jnp.where(lane16 == j, col, 0.0)
            a_s[hh, r0:r0 + GLA_BLOCK, :] = a_i
        st_old = st_gla[hh]
        og = _dot3(a_s[hh], vg_v) + _dot3_nt(qg_v * jnp.exp(bc), st_old)
        b_last = bc[c - 1:c]
        kdg = kg_v * jnp.exp(b_last - bc)
        st_gla[hh] = st_old * jnp.exp(b_last) + _dot3(vg_v.T, kdg)
        o_g[:, hh * dv:(hh + 1) * dv] = _head_norm(
            og, gng[:, hh * dv:(hh + 1) * dv], pa[:, A_GG + hh * dv:A_GG + (hh + 1) * dv], False)

    @pl.when(c_id == n_c - 1)
    def _():
        for hh in range(RET_HEADS):
            sret_out[0, hh] = s_ret[hh]
            sgla_out[0, hh] = st_gla[hh].T


def _mixa_prompt(pa, pa_low, w2p, b2, cos, sin, gnr, gng, lam, batch, seq):
    c = MIX_CHUNK
    nc = seq // c
    h = RET_HEADS
    dkr, dkg, dv = A_RET_DK, A_GLA_DK, A_DV
    width = pa.shape[1]
    const2 = lambda b, cc: (0, 0)
    in_specs = [pl.BlockSpec((c, width), lambda b, cc: (b * nc + cc, 0)),
                pl.BlockSpec((c, V7X_LANES), lambda b, cc: (b * nc + cc, 0)),
                pl.BlockSpec(w2p.shape, const2), pl.BlockSpec(b2.shape, const2),
                pl.BlockSpec((c, dkr // 2), lambda b, cc: (cc, 0)),
                pl.BlockSpec((c, dkr // 2), lambda b, cc: (cc, 0)),
                pl.BlockSpec(gnr.shape, const2), pl.BlockSpec(gng.shape, const2),
                pl.BlockSpec(lam.shape, lambda b, cc: (0, 0, 0))]
    out_specs = [pl.BlockSpec((c, h * dv), lambda b, cc: (b * nc + cc, 0)),
                 pl.BlockSpec((c, h * dv), lambda b, cc: (b * nc + cc, 0)),
                 pl.BlockSpec((1, h, dkr, dv), lambda b, cc: (b, 0, 0, 0)),
                 pl.BlockSpec((1, h, dkg, dv), lambda b, cc: (b, 0, 0, 0))]
    n_tok = batch * seq
    out_shape = [jax.ShapeDtypeStruct((n_tok, h * dv), F32),
                 jax.ShapeDtypeStruct((n_tok, h * dv), F32),
                 jax.ShapeDtypeStruct((batch, h, dkr, dv), F32),
                 jax.ShapeDtypeStruct((batch, h, dkg, dv), F32)]
    scratch = [pltpu.VMEM((h, dkr, dv), F32), pltpu.VMEM((h, dv, dkg), F32),
               pltpu.VMEM((h, c, c), F32), pltpu.VMEM((h, c, dkr), F32), pltpu.VMEM((h, c, dkr), F32),
               pltpu.VMEM((h, c, c), F32)]
    blocks = _nbytes((c, width)) + 2 * _nbytes((c, h * dv)) + h * (_nbytes((dkr, dv)) + _nbytes((dkg, dv)))
    scr = h * (_nbytes((dkr, dv)) + _nbytes((dv, dkg)) + 2 * _nbytes((c, c)) + 2 * _nbytes((c, dkr)))
    return pl.pallas_call(
        _mixa_prompt_body, name="mixa_prompt",
        grid=(batch, nc),
        in_specs=in_specs, out_specs=out_specs, out_shape=out_shape,
        scratch_shapes=scratch,
        compiler_params=_cparams(2, blocks, scr, 16 * 2 ** 20),
    )(pa, pa_low, w2p, b2, cos, sin, gnr, gng, lam)


def _direct_intra(q, k, v, decay_fn):
    t = q.shape[0]
    row = _iota((t, 1), 0)
    o = jnp.zeros((t, v.shape[1]), F32)
    for j in range(t):
        col = jnp.sum(q * decay_fn(j) * k[j:j + 1], axis=-1, keepdims=True)
        col = jnp.where(row >= j, col, 0.0)
        o = o + col * v[j:j + 1]
    return o


def _mixa_sample_body(sb, pa, alow, w2, b2, cos, sin, gnr, gng, lam, sret_in, sgla_in,
                      o_r, o_g, sret_out, sgla_out):
    t = 8
    dkr, dkg, dv = 256, 128, 256
    cs, sn = cos[...], sin[...]
    la = _gla_log_alpha(alow[...], w2[...], b2[...])
    bc_all = _dot01_left(_tril01(sb * t, block=t), la)
    rel = (_iota((t, 1), 0)).astype(F32)
    for hh in range(RET_HEADS):
        lam_row = lam[hh]
        dq = jnp.exp(lam_row * (rel + 1.0))
        dk = jnp.exp(lam_row * (t - 1.0 - rel))
        gc = jnp.exp(lam_row * float(t))
        for s in range(sb):
            r0 = s * t
            q = _rot(pa[r0:r0 + t, hh * dkr:(hh + 1) * dkr], cs, sn)
            k = _rot(pa[r0:r0 + t, 1024 + hh * dkr:1024 + (hh + 1) * dkr], cs, sn) * (dkr ** -0.5)
            v = pa[r0:r0 + t, 2048 + hh * dv:2048 + (hh + 1) * dv]
            g = pa[r0:r0 + t, 3072 + hh * dv:3072 + (hh + 1) * dv]
            s_old = sret_in[s, hh]
            o = _direct_intra(q, k, v, lambda j: jnp.exp(lam_row * jnp.maximum(rel - float(j), 0.0)))
            o = o + _dot3(q, s_old) * dq
            kdt = _pad_rows(k * dk, V7X_LANES).T
            sret_out[s, hh] = s_old * gc + _dot3(kdt, _pad_rows(v, V7X_LANES))
            o_r[r0:r0 + t, hh * dv:(hh + 1) * dv] = _head_norm(
                o, gnr[:, hh * dv:(hh + 1) * dv], g, True)
            qg = pa[r0:r0 + t, 4096 + hh * dkg:4096 + (hh + 1) * dkg] * (dkg ** -0.5)
            kg = pa[r0:r0 + t, 4608 + hh * dkg:4608 + (hh + 1) * dkg]
            vg = pa[r0:r0 + t, 5120 + hh * dv:5120 + (hh + 1) * dv]
            gg = pa[r0:r0 + t, 6144 + hh * dv:6144 + (hh + 1) * dv]
            bc = bc_all[r0:r0 + t, hh * dkg:(hh + 1) * dkg]
            sg_old = sgla_in[s, hh]
            og = _direct_intra(qg, kg, vg, lambda j: jnp.exp(jnp.minimum(bc - bc[j:j + 1], 0.0)))
            og = og + _dot3(qg * jnp.exp(bc), sg_old)
            b_last = bc[t - 1:t]
            m = _pad_rows(jnp.concatenate([kg * jnp.exp(b_last - bc), jnp.exp(b_last)], axis=0),
                          V7X_LANES).T
            sgla_out[s, hh] = sg_old * m[:, t:t + 1] + _dot3(m, _pad_rows(vg, V7X_LANES))
            o_g[r0:r0 + t, hh * dv:(hh + 1) * dv] = _head_norm(
                og, gng[:, hh * dv:(hh + 1) * dv], gg, False)


def _mixa_sample(pa, pa_low, w2p, b2, cos, sin, gnr, gng, lam, sret, sgla, sb=4):
    nseq = sret.shape[0]
    t = 8
    h, dkr, dkg, dv = RET_HEADS, 256, 128, 256
    width = pa.shape[1]
    const2 = lambda i: (0, 0)
    in_specs = [pl.BlockSpec((sb * t, width), lambda i: (i, 0)),
                pl.BlockSpec((sb * t, V7X_LANES), lambda i: (i, 0)),
                pl.BlockSpec(w2p.shape, const2), pl.BlockSpec(b2.shape, const2),
                pl.BlockSpec(cos.shape, const2), pl.BlockSpec(sin.shape, const2),
                pl.BlockSpec(gnr.shape, const2), pl.BlockSpec(gng.shape, const2),
                pl.BlockSpec(lam.shape, lambda i: (0, 0, 0)),
                pl.BlockSpec((sb, h, dkr, dv), lambda i: (i, 0, 0, 0)),
                pl.BlockSpec((sb, h, dkg, dv), lambda i: (i, 0, 0, 0))]
    out_specs = [pl.BlockSpec((sb * t, h * dv), lambda i: (i, 0)),
                 pl.BlockSpec((sb * t, h * dv), lambda i: (i, 0)),
                 pl.BlockSpec((sb, h, dkr, dv), lambda i: (i, 0, 0, 0)),
                 pl.BlockSpec((sb, h, dkg, dv), lambda i: (i, 0, 0, 0))]
    out_shape = [jax.ShapeDtypeStruct((nseq * t, h * dv), F32),
                 jax.ShapeDtypeStruct((nseq * t, h * dv), F32),
                 jax.ShapeDtypeStruct(sret.shape, F32),
                 jax.ShapeDtypeStruct(sgla.shape, F32)]
    blocks = (_nbytes((sb * t, width)) + 2 * _nbytes((sb, h, dkr, dv))
              + 2 * _nbytes((sb, h, dkg, dv)) + 2 * 2 ** 20)
    return pl.pallas_call(
        functools.partial(_mixa_sample_body, sb), name="mixa_sample",
        grid=(nseq // sb,),
        in_specs=in_specs, out_specs=out_specs, out_shape=out_shape,
        compiler_params=_cparams(1, blocks, 0, 4 * 2 ** 20),
    )(pa, pa_low, w2p, b2, cos, sin, gnr, gng, lam, sret, sgla)


def _conv_prompt_body(x_ref, prev_ref, w_ref, b_ref, o_ref):
    first = pl.program_id(1) == 0
    ext = jnp.concatenate([jnp.where(first, 0.0, prev_ref[...]), x_ref[...]], axis=0)
    hist = CONV_W - 1
    acc = b_ref[...] + pltpu.roll(ext, hist, 0)[V7X_SUBLANES:] * w_ref[0:1, :]
    for j in range(1, CONV_W):
        delayed = pltpu.roll(ext, hist - j, 0)[V7X_SUBLANES:] if j < hist else x_ref[...]
        acc = acc + delayed * w_ref[j:j + 1, :]
    o_ref[...] = _silu(acc)


def _conv_prompt(pc, conv_w, conv_b, batch, seq, col0, width, tc=512, tn=512):
    nc = seq // tc
    cb0 = col0 // tn
    blocks = 2 * _nbytes((tc, tn)) + _nbytes((V7X_SUBLANES, tn)) * 3
    return pl.pallas_call(
        _conv_prompt_body, name="conv_prompt",
        grid=(batch, nc, width // tn),
        in_specs=[pl.BlockSpec((tc, tn), lambda b, c, j: (b * nc + c, cb0 + j)),
                  pl.BlockSpec((V7X_SUBLANES, tn),
                               lambda b, c, j: (jnp.maximum((b * nc + c) * (tc // V7X_SUBLANES) - 1, 0),
                                                cb0 + j)),
                  pl.BlockSpec((CONV_W, tn), lambda b, c, j: (0, j)),
                  pl.BlockSpec((1, tn), lambda b, c, j: (0, j))],
        out_specs=pl.BlockSpec((tc, tn), lambda b, c, j: (b * nc + c, j)),
        out_shape=jax.ShapeDtypeStruct((batch * seq, width), F32),
        compiler_params=_cparams(3, blocks, 0, 6 * _nbytes((tc, tn))),
    )(pc, pc, conv_w, conv_b.reshape(1, width))


def _conv_sample_body(x_ref, cache_ref, w_ref, b_ref, o_ref, scr):
    t = x_ref.shape[1]
    hist = CONV_W - 1
    scr[:, t - hist:t, :] = cache_ref[...]
    scr[:, t:2 * t, :] = x_ref[...]
    acc = b_ref[...] + scr[:, t - hist:2 * t - hist, :] * w_ref[0:1, :]
    for j in range(1, CONV_W):
        off = t - hist + j
        acc = acc + scr[:, off:off + t, :] * w_ref[j:j + 1, :]
    o_ref[...] = _silu(acc)


def _conv_sample(pc3, cache, conv_w, conv_b, col0, width, sbc=32, tn=512):
    nseq, t, _ = pc3.shape
    cb0 = col0 // tn
    blocks = 2 * _nbytes((sbc, t, tn)) + _nbytes((sbc, 8, tn))
    return pl.pallas_call(
        _conv_sample_body, name="conv_sample",
        grid=(nseq // sbc, width // tn),
        in_specs=[pl.BlockSpec((sbc, t, tn), lambda i, j: (i, 0, cb0 + j)),
                  pl.BlockSpec((sbc, CONV_W - 1, tn), lambda i, j: (i, 0, j)),
                  pl.BlockSpec((CONV_W, tn), lambda i, j: (0, j)),
                  pl.BlockSpec((1, tn), lambda i, j: (0, j))],
        out_specs=pl.BlockSpec((sbc, t, tn), lambda i, j: (i, 0, j)),
        out_shape=jax.ShapeDtypeStruct((nseq, t, width), F32),
        scratch_shapes=[pltpu.VMEM((sbc, 2 * t, tn), F32)],
        compiler_params=_cparams(2, blocks, _nbytes((sbc, 2 * t, tn)), 4 * _nbytes((sbc, t, tn))),
    )(pc3, cache, conv_w, conv_b.reshape(1, width))


def _expand01(heads, width, first=0):
    r = _iota((V7X_LANES, heads * width), 0)
    l = _iota((V7X_LANES, heads * width), 1)
    return jnp.where((l // width) == r - first, 1.0, 0.0).astype(BF16)


def _ssd_prompt_body(xs, bm, cm, z, dtc, dtr, bias_c, alog_c, bias_r, alog_r, dfull,
                     y_out, s_out, st):
    c_id = pl.program_id(1)
    n_c = pl.num_programs(1)
    l = xs.shape[0]
    groups, n, gw = st.shape
    heads = dtr.shape[1]
    hpg = heads // groups
    p = gw // hpg

    @pl.when(c_id == 0)
    def _():
        st[...] = jnp.zeros_like(st)

    dt_c = _softplus(dtc[...] + bias_c[...])
    cum_c = _dot01_left(_tril01(l), dt_c * (-jnp.exp(alog_c[...])))
    cum_parts, dt_parts = _split3(cum_c), _split3(dt_c)
    dt_r = _softplus(dtr[0] + bias_r[...])
    tri_u = jnp.where(_iota((l, l), 0) <= _iota((l, l), 1), 1.0, 0.0).astype(BF16)
    cum_r = _dot01_right(dt_r * (-jnp.exp(alog_r[...])), tri_u)

    causal = _iota((l, l), 0) >= _iota((l, l), 1)
    lane = _iota((l, 2 * p), 1)
    for g in range(groups):
        x = xs[:, g * gw:(g + 1) * gw]
        bmg = bm[:, g * n:(g + 1) * n]
        cmb = cm[:, g * n:(g + 1) * n].astype(BF16)
        cb = _dot_nt(cmb, bmg.astype(BF16))
        y_tiles = []
        for pair in range(hpg // 2):
            xt = x[:, pair * 2 * p:(pair + 1) * 2 * p]
            acc = None
            for sub in range(2):
                r = g * hpg + 2 * pair + sub
                seg = cum_c[:, r:r + 1] - cum_r[r:r + 1, :]
                w = jnp.where(causal, cb * jnp.exp(jnp.minimum(seg, 0.0)) * dt_r[r:r + 1, :], 0.0)
                xm = jnp.where((lane // p) == sub, xt, 0.0)
                part = _dot(w.astype(BF16), xm.astype(BF16))
                acc = part if acc is None else acc + part
            y_tiles.append(acc)
        y = jnp.concatenate(y_tiles, axis=-1)

        ex = _expand01(hpg, p, g * hpg)
        cum_f = _dot(cum_parts[0], ex) + _dot(cum_parts[1], ex) + _dot(cum_parts[2], ex)
        dt_f = _dot(dt_parts[0], ex) + _dot(dt_parts[1], ex) + _dot(dt_parts[2], ex)
        st_old = st[g]
        y = y + _dot(cmb, st_old.astype(BF16)) * jnp.exp(cum_f)
        cl_f = cum_f[l - 1:l]
        tail = jnp.exp(cl_f - cum_f) * dt_f
        st[g] = st_old * jnp.exp(cl_f) + _dot(bmg.T.astype(BF16), (x * tail).astype(BF16))
        y = y + dfull[:, g * gw:(g + 1) * gw] * x
        y_out[:, g * gw:(g + 1) * gw] = y * _silu(z[:, g * gw:(g + 1) * gw])

    @pl.when(c_id == n_c - 1)
    def _():
        for g in range(groups):
            s_out[0, g * hpg:(g + 1) * hpg] = st[g].T.reshape(hpg, p, n)


def _ssd_prompt(act, pc, dt_c, dt_t, bias_c, alog_c, bias_r, alog_r, dfull, batch, seq):
    l = MIX_CHUNK
    nc = seq // l
    g = SSM_GROUPS
    n = SSM_STATE
    p = SSM_HEADDIM
    heads = dt_t.shape[1]
    d_inner = heads * p
    gw = d_inner // g
    gn = g * n

    def row(b, cc):
        return b * nc + cc

    in_specs = [pl.BlockSpec((l, d_inner), lambda b, cc: (row(b, cc), 0)),
                pl.BlockSpec((l, gn), lambda b, cc: (row(b, cc), d_inner // gn)),
                pl.BlockSpec((l, gn), lambda b, cc: (row(b, cc), d_inner // gn + 1)),
                pl.BlockSpec((l, d_inner), lambda b, cc: (row(b, cc), 0)),
                pl.BlockSpec((l, V7X_LANES), lambda b, cc: (row(b, cc), 0)),
                pl.BlockSpec((1, heads, l), lambda b, cc: (b, 0, cc)),
                pl.BlockSpec((1, V7X_LANES), lambda b, cc: (0, 0)),
                pl.BlockSpec((1, V7X_LANES), lambda b, cc: (0, 0)),
                pl.BlockSpec((heads, l), lambda b, cc: (0, 0)),
                pl.BlockSpec((heads, l), lambda b, cc: (0, 0)),
                pl.BlockSpec((1, d_inner), lambda b, cc: (0, 0))]
    out_specs = [pl.BlockSpec((l, d_inner), lambda b, cc: (row(b, cc), 0)),
                 pl.BlockSpec((1, heads, p, n), lambda b, cc: (b, 0, 0, 0))]
    out_shape = [jax.ShapeDtypeStruct((batch * seq, d_inner), F32),
                 jax.ShapeDtypeStruct((batch, heads, p, n), F32)]
    blocks = 3 * _nbytes((l, d_inner)) + 2 * _nbytes((l, gn)) + _nbytes((heads, p, n))
    return pl.pallas_call(
        _ssd_prompt_body, name="ssd_prompt",
        grid=(batch, nc),
        in_specs=in_specs, out_specs=out_specs, out_shape=out_shape,
        scratch_shapes=[pltpu.VMEM((g, n, gw), F32)],
        compiler_params=_cparams(2, blocks, _nbytes((g, n, gw)), 16 * 2 ** 20),
    )(act, act, act, pc, dt_c, dt_t, bias_c, alog_c, bias_r, alog_r, dfull)


def _ssd_sample_body(sb, xs, bm, cm, z, dtc, bias_c, alog_c, dfull, s_in, y_out, s_out, y_s):
    t = 8
    rows = sb * t
    gw = xs.shape[1]
    hpg = s_in.shape[1]
    p = gw // hpg
    n = bm.shape[1]
    dt_c = _softplus(dtc[...] + bias_c[0])
    cum_c = _dot01_left(_tril01(rows, block=t), dt_c * (-jnp.exp(alog_c[0])))
    ex = _expand01(hpg, p)
    cum_f = _dot01_right(cum_c, ex)
    dt_f = _dot01_right(dt_c, ex)
    x = xs[...]
    x3 = x.reshape(sb, t, gw)
    c3 = cm[...].reshape(sb, t, n)
    b3 = bm[...].reshape(sb, t, n)
    cum3 = cum_f.reshape(sb, t, gw)
    dt3 = dt_f.reshape(sb, t, gw)
    rowi = _iota((sb, t, 1), 1)
    y3 = jnp.zeros((sb, t, gw), F32)
    for j in range(t):
        cbj = jnp.sum(c3 * b3[:, j:j + 1, :], axis=-1, keepdims=True)
        dec = jnp.exp(jnp.minimum(cum3 - cum3[:, j:j + 1, :], 0.0))
        w = jnp.where(rowi >= j, cbj * dec * dt3[:, j:j + 1, :], 0.0)
        y3 = y3 + w * x3[:, j:j + 1, :]
    y_s[...] = y3.reshape(rows, gw)
    cmv = cm[...]
    bmv = bm[...]
    for s in range(sb):
        r0 = s * t
        sg = s_in[s].reshape(hpg * p, n)
        cf = cum_f[r0:r0 + t]
        yi = _dot_nt(cmv[r0:r0 + t].astype(BF16), sg.astype(BF16)) * jnp.exp(cf)
        cl = cf[t - 1:t]
        xt = x[r0:r0 + t] * (jnp.exp(cl - cf) * dt_f[r0:r0 + t])
        m = _pad_rows(jnp.concatenate([xt, jnp.exp(cl)], axis=0), V7X_LANES).T
        s_new = sg * m[:, t:t + 1] + _dot(m.astype(BF16), _pad_rows(bmv[r0:r0 + t], V7X_LANES).astype(BF16))
        s_out[s] = s_new.reshape(hpg, p, n)
        y_s[r0:r0 + t, :] = y_s[r0:r0 + t, :] + yi
    y_out[...] = (y_s[...] + dfull[0] * x) * _silu(z[...])


def _ssd_sample(act, pc, pcs, bias_c, alog_c, dfull, s_ssm, sb=8):
    t = 8
    nseq = s_ssm.shape[0]
    g = SSM_GROUPS
    n = SSM_STATE
    p = SSM_HEADDIM
    d_inner = pc.shape[1] - 6144
    gw = d_inner // g
    hpg = gw // p
    bb0 = d_inner // n
    cb0 = (d_inner + g * n) // n
    rows = sb * t
    in_specs = [pl.BlockSpec((rows, gw), lambda i, gg: (i, gg)),
                pl.BlockSpec((rows, n), lambda i, gg: (i, bb0 + gg)),
                pl.BlockSpec((rows, n), lambda i, gg: (i, cb0 + gg)),
                pl.BlockSpec((rows, gw), lambda i, gg: (i, gg)),
                pl.BlockSpec((rows, V7X_LANES), lambda i, gg: (i, gg)),
                pl.BlockSpec((1, 1, V7X_LANES), lambda i, gg: (gg, 0, 0)),
                pl.BlockSpec((1, 1, V7X_LANES), lambda i, gg: (gg, 0, 0)),
                pl.BlockSpec((1, 1, gw), lambda i, gg: (gg, 0, 0)),
                pl.BlockSpec((sb, hpg, p, n), lambda i, gg: (i, gg, 0, 0))]
    out_specs = [pl.BlockSpec((rows, gw), lambda i, gg: (i, gg)),
                 pl.BlockSpec((sb, hpg, p, n), lambda i, gg: (i, gg, 0, 0))]
    out_shape = [jax.ShapeDtypeStruct((nseq * t, d_inner), F32),
                 jax.ShapeDtypeStruct(s_ssm.shape, F32)]
    blocks = 3 * _nbytes((rows, gw)) + 3 * _nbytes((rows, n)) + 2 * _nbytes((sb, hpg, p, n))
    return pl.pallas_call(
        functools.partial(_ssd_sample_body, sb), name="ssd_sample",
        grid=(nseq // sb, g),
        in_specs=in_specs, out_specs=out_specs, out_shape=out_shape,
        scratch_shapes=[pltpu.VMEM((rows, gw), F32)],
        compiler_params=_cparams(2, blocks, _nbytes((rows, gw)), 8 * 2 ** 20),
    )(act, act, act, pc, pcs, bias_c, alog_c, dfull, s_ssm)


def _moe_pre_body(n_pt, xp_ref, shp_ref, scp_ref, xs_ref, shs_ref, scs_ref, g_ref, wr_ref,
                  h_out, route_out):
    i = pl.program_id(0)

    @pl.when(i < n_pt)
    def _():
        _moe_route(xp_ref, shp_ref, scp_ref, g_ref, wr_ref, h_out, route_out)

    @pl.when(i >= n_pt)
    def _():
        _moe_route(xs_ref, shs_ref, scs_ref, g_ref, wr_ref, h_out, route_out)


def _moe_route(x_ref, sh_ref, sc_ref, g_ref, wr_ref, h_out, route_out):
    x = x_ref[...]
    h = _rms_rows(x, g_ref[...]) * (1.0 + sc_ref[...]) + sh_ref[...]
    h = h.reshape(h_out.shape[0], x.shape[-1])
    h_out[...] = h.reshape(h_out.shape)
    logits = _dot3(h, wr_ref[...])
    tm = logits.shape[0]
    lane = _iota((tm, V7X_LANES), 1).astype(F32)
    neg = -jnp.inf
    lg = jnp.where(lane < MOE_GROUPS, logits, neg)
    mg = jnp.max(lg, axis=-1, keepdims=True)
    top_g = jnp.min(jnp.where(lg == mg, lane, float(V7X_LANES)), axis=-1, keepdims=True)
    p_top = 1.0 / jnp.sum(jnp.where(lane < MOE_GROUPS, jnp.exp(logits - mg), 0.0),
                          axis=-1, keepdims=True)
    lo = MOE_GROUPS + MOE_PER_GROUP * top_g
    le = jnp.where((lane >= lo) & (lane < lo + MOE_PER_GROUP), logits, neg)
    v1 = jnp.max(le, axis=-1, keepdims=True)
    i1 = jnp.min(jnp.where(le == v1, lane, float(V7X_LANES)), axis=-1, keepdims=True)
    le2 = jnp.where(lane == i1, neg, le)
    v2 = jnp.max(le2, axis=-1, keepdims=True)
    i2 = jnp.min(jnp.where(le2 == v2, lane, float(V7X_LANES)), axis=-1, keepdims=True)
    tt = jnp.exp(v2 - v1)
    w1 = p_top / (1.0 + tt)
    w2 = p_top * tt / (1.0 + tt)
    route_out[...] = jnp.where(lane == 0, i1 - MOE_GROUPS,
                     jnp.where(lane == 1, i2 - MOE_GROUPS,
                     jnp.where(lane == 2, w1, jnp.where(lane == 3, w2, 0.0))))


def _moe_pre(xp, mp, xs_, ms, chunk_shift, chunk_scale, gain, wr, tm):
    bp, tp, d = xp.shape
    bs, ts, _ = xs_.shape
    nb = tp // tm
    n_pt = bp * nb
    sbs = tm // ts
    n_st = bs // sbs
    n_all = bp * tp + bs * ts

    def pi(i):
        return jnp.minimum(i, n_pt - 1)

    def si(i):
        return jnp.maximum(i - n_pt, 0)

    in_specs = [pl.BlockSpec((1, tm, d), lambda i: (pi(i) // nb, pi(i) % nb, 0)),
                pl.BlockSpec((1, 1, d), lambda i: (pi(i) // nb, 0, chunk_shift)),
                pl.BlockSpec((1, 1, d), lambda i: (pi(i) // nb, 0, chunk_scale)),
                pl.BlockSpec((sbs, ts, d), lambda i: (si(i), 0, 0)),
                pl.BlockSpec((sbs, 1, d), lambda i: (si(i), 0, chunk_shift)),
                pl.BlockSpec((sbs, 1, d), lambda i: (si(i), 0, chunk_scale)),
                pl.BlockSpec((1, d), lambda i: (0, 0)),
                pl.BlockSpec((d, V7X_LANES), lambda i: (0, 0))]
    blocks = (3 * _nbytes((tm, d)) + 2 * _nbytes((sbs, 8, d)) + _nbytes((d, V7X_LANES))
              + _nbytes((tm, 128)))
    return pl.pallas_call(
        functools.partial(_moe_pre_body, n_pt), name="moe_pre",
        grid=(n_pt + n_st,),
        in_specs=in_specs,
        out_specs=[pl.BlockSpec((tm, d // V7X_LANES, V7X_LANES), lambda i: (i, 0, 0)),
                   pl.BlockSpec((tm, V7X_LANES), lambda i: (i, 0))],
        out_shape=[jax.ShapeDtypeStruct((n_all, d // V7X_LANES, V7X_LANES), F32),
                   jax.ShapeDtypeStruct((n_all, V7X_LANES), F32)],
        compiler_params=_cparams(1, blocks, 0, 4 * _nbytes((tm, d))),
    )(xp, mp, mp, xs_, ms, ms, gain.reshape(1, d), wr)


def _rank_body(route_ref, rank_ref, cnt_ref, carry):
    @pl.when(pl.program_id(0) == 0)
    def _():
        carry[...] = jnp.zeros_like(carry)

    r = route_ref[...]
    tm = r.shape[0]
    lane = _iota((tm, V7X_LANES), 1).astype(F32)
    o1 = lane == r[:, 0:1]
    o2 = lane == r[:, 1:2]
    oh = jnp.where(o1, 1.0, 0.0) + jnp.where(o2, 1.0, 0.0)
    pex = _dot(_tril01(tm, strict=True), oh.astype(BF16)) + carry[0:1, :]
    r1 = jnp.sum(jnp.where(o1, pex, 0.0), axis=-1, keepdims=True)
    r2 = jnp.sum(jnp.where(o2, pex, 0.0), axis=-1, keepdims=True)
    rank_ref[...] = jnp.where(lane == 0, r1, jnp.where(lane == 1, r2, 0.0))
    carry[...] = carry[...] + jnp.sum(oh, axis=0, keepdims=True)
    cnt_ref[...] = carry[...]


def _rank(route, tm=256):
    n = route.shape[0]
    return pl.pallas_call(
        _rank_body, name="moe_rank",
        grid=(n // tm,),
        in_specs=[pl.BlockSpec((tm, V7X_LANES), lambda i: (i, 0))],
        out_specs=[pl.BlockSpec((tm, V7X_LANES), lambda i: (i, 0)),
                   pl.BlockSpec((V7X_SUBLANES, V7X_LANES), lambda i: (0, 0))],
        out_shape=[jax.ShapeDtypeStruct((n, V7X_LANES), F32),
                   jax.ShapeDtypeStruct((V7X_SUBLANES, V7X_LANES), F32)],
        scratch_shapes=[pltpu.VMEM((V7X_SUBLANES, V7X_LANES), F32)],
        compiler_params=_cparams(1, 2 * _nbytes((tm, 128))),
    )(route)


def _gather_tokens(src_hbm, dst, sem, n, row_of):
    def issue(r, carry):
        pltpu.make_async_copy(src_hbm.at[row_of(r)], dst.at[r], sem).start()
        return carry

    lax.fori_loop(0, n, issue, 0, unroll=DMA_UNROLL)


def _gather_wait(src_hbm, dst, sem):
    pltpu.make_async_copy(src_hbm.at[pl.ds(0, dst.shape[0])], dst, sem).wait()


def _expert_body(layer, inv_ref, te_ref, first_ref, wslot_ref, nxt_ref, nt_ref,
                 h3_hbm, wg_hbm, wu_hbm, wd_hbm, o_ref,
                 xbuf, wg_f, wu_f, wd_f, wg_b, wu_b, wd_b, xsem, wsem):
    i = pl.program_id(0)
    nt = nt_ref[0]
    tm = xbuf.shape[1]
    active = i < nt

    def w_copies(e, slot):
        return [pltpu.make_async_copy(src.at[layer, e], dst.at[slot], wsem.at[k])
                for k, (src, dst) in enumerate(((wg_hbm, wg_f), (wu_hbm, wu_f), (wd_hbm, wd_f)))]

    def fetch_rows(tile, slot):
        _gather_tokens(h3_hbm, xbuf.at[slot], xsem.at[slot], tm, lambda r: inv_ref[tile * tm + r])

    @pl.when(i == 0)
    def _():
        for cp in w_copies(te_ref[0], wslot_ref[0]):
            cp.start(priority=WEIGHT_DMA_PRIORITY)
        fetch_rows(0, 0)

        @pl.when(nt > 1)
        def _():
            fetch_rows(1, 1)

    @pl.when(active)
    def _():
        slot = i % ROW_SLOTS
        ws = wslot_ref[i]

        @pl.when(first_ref[i] == 1)
        def _():
            for cp in w_copies(te_ref[i], ws):
                cp.wait()
            wg_b[...] = wg_f[ws].astype(BF16)
            wu_b[...] = wu_f[ws].astype(BF16)
            wd_b[...] = wd_f[ws].astype(BF16)

            @pl.when(nxt_ref[i] >= 0)
            def _():
                for cp in w_copies(nxt_ref[i], 1 - ws):
                    cp.start(priority=WEIGHT_DMA_PRIORITY)

        @pl.when(i + 2 < nt)
        def _():
            fetch_rows(i + 2, (i + 2) % ROW_SLOTS)

        _gather_wait(h3_hbm, xbuf.at[slot], xsem.at[slot])
        xb = xbuf[slot].reshape(tm, wg_b.shape[0]).astype(BF16)
        hid = _silu(_dot(xb, wg_b[...])) * _dot(xb, wu_b[...])
        o_ref[...] = _dot(hid.astype(BF16), wd_b[...]).reshape(o_ref.shape)

    @pl.when(jnp.logical_not(active))
    def _():
        o_ref[...] = jnp.zeros_like(o_ref)


def _experts(inv, tile_expert, first, wslot, nxt, ntiles, h3, wg, wu, wd, layer):
    _, dc, ln = h3.shape
    d = dc * ln
    ff = wg.shape[3]
    tm = MOE_TILE
    p_pad = inv.shape[0]
    scratch = [pltpu.VMEM((ROW_SLOTS, tm, dc, ln), F32),
               pltpu.VMEM((2, d, ff), F32), pltpu.VMEM((2, d, ff), F32), pltpu.VMEM((2, ff, d), F32),
               pltpu.VMEM((d, ff), BF16), pltpu.VMEM((d, ff), BF16), pltpu.VMEM((ff, d), BF16),
               pltpu.SemaphoreType.DMA((ROW_SLOTS,)), pltpu.SemaphoreType.DMA((3,))]
    scr_bytes = ROW_SLOTS * _nbytes((tm, d)) + 6 * _nbytes((d, ff)) + 3 * _nbytes((d, ff), BF16)
    any_spec = pl.BlockSpec(memory_space=pl.ANY)
    return pl.pallas_call(
        functools.partial(_expert_body, layer), name="moe_experts",
        grid_spec=pltpu.PrefetchScalarGridSpec(
            num_scalar_prefetch=6,
            grid=(p_pad // tm,),
            in_specs=[any_spec, any_spec, any_spec, any_spec],
            out_specs=pl.BlockSpec((tm, dc, ln), lambda i, *_: (i, 0, 0)),
            scratch_shapes=scratch),
        out_shape=jax.ShapeDtypeStruct((p_pad, dc, ln), F32),
        compiler_params=_cparams(1, _nbytes((tm, d)), scr_bytes, 3 * _nbytes((tm, d)),
                                 disable_bounds_checks=True),
    )(inv, tile_expert, first, wslot, nxt, ntiles, h3, wg, wu, wd)


def _combine_body(tok_off, final, pos_ref, eo_hbm, route_ref, x_ref, gate_ref, fn_ref, out_ref,
                  buf, sem):
    i = pl.program_id(0)
    n = pl.num_programs(0)
    tc = route_ref.shape[0]
    d = x_ref.shape[-1]

    def fetch(step, slot):
        base = 2 * (tok_off + step * tc)
        for k in range(MOE_TOPK):
            _gather_tokens(eo_hbm, buf.at[slot, k], sem.at[slot], tc,
                           lambda r, k=k: pos_ref[base + MOE_TOPK * r + k])

    @pl.when(i == 0)
    def _():
        fetch(0, 0)

    slot = i % 2

    @pl.when(i + 1 < n)
    def _():
        fetch(i + 1, 1 - slot)

    for k in range(MOE_TOPK):
        _gather_wait(eo_hbm, buf.at[slot, k], sem.at[slot])
    rt = route_ref[...]
    y = rt[:, 2:3] * buf[slot, 0].reshape(tc, d) + rt[:, 3:4] * buf[slot, 1].reshape(tc, d)
    x = x_ref[...]
    xn = x + gate_ref[...] * y.reshape(x.shape)
    if final:
        xn = _rms_rows(xn, fn_ref[...])
    out_ref[...] = xn


def _combine(pos_flat, eo, route, x3, m3, chunk_gate, sb, rb, tok_off, final_gain=None):
    sq, rq, d = x3.shape
    nb = rq // rb
    tc = sb * rb
    blk_off = tok_off // tc
    final = final_gain is not None
    fn = (final_gain if final else jnp.ones((d,), F32)).reshape(1, d)
    blocks = 2 * _nbytes((tc, d)) + _nbytes((tc, 128)) + _nbytes((sb, 8, d))
    return pl.pallas_call(
        functools.partial(_combine_body, tok_off, final), name="moe_combine",
        grid_spec=pltpu.PrefetchScalarGridSpec(
            num_scalar_prefetch=1,
            grid=((sq * rq) // tc,),
            in_specs=[pl.BlockSpec(memory_space=pl.ANY),
                      pl.BlockSpec((tc, V7X_LANES), lambda i, p: (blk_off + i, 0)),
                      pl.BlockSpec((sb, rb, d), lambda i, p: (i // nb, i % nb, 0)),
                      pl.BlockSpec((sb, 1, d), lambda i, p: (i // nb, 0, chunk_gate)),
                      pl.BlockSpec((1, d), lambda i, p: (0, 0))],
            out_specs=pl.BlockSpec((sb, rb, d), lambda i, p: (i // nb, i % nb, 0)),
            scratch_shapes=[pltpu.VMEM((2, MOE_TOPK, tc, d // V7X_LANES, V7X_LANES), F32),
                            pltpu.SemaphoreType.DMA((2,))]),
        out_shape=jax.ShapeDtypeStruct(x3.shape, F32),
        compiler_params=_cparams(1, blocks, 2 * MOE_TOPK * _nbytes((tc, d)), 4 * _nbytes((tc, d)),
                                 disable_bounds_checks=True),
    )(pos_flat, eo, route, x3, m3, fn)


def _moe_layer(xp, xs_, mp, ms, l, norm_moe, w_group, w_expert, w_gate, w_up, w_down, final_gain):
    bp, tp, d = xp.shape
    bs, ts, _ = xs_.shape
    n_p, n_s = bp * tp, bs * ts
    n_all = n_p + n_s
    wr = jnp.concatenate([w_group[l], jnp.transpose(w_expert[l], (1, 0, 2)).reshape(d, MOE_EXPERTS)],
                         axis=1)
    wr = jnp.pad(wr, ((0, 0), (0, V7X_LANES - wr.shape[1])))
    tm = MOE_TILE
    h_all, route = _moe_pre(xp, mp, xs_, ms, 3, 4, norm_moe[l], wr, tm)
    rank, cnt = _rank(route)
    counts = cnt[0, :MOE_EXPERTS].astype(I32)
    tiles_per = (counts + tm - 1) // tm
    tile_end = jnp.cumsum(tiles_per)
    row_start = (tile_end - tiles_per) * tm
    ntiles = tile_end[-1]
    e_idx = route[:, 0:2].astype(I32)
    pos = row_start[e_idx] + rank[:, 0:2].astype(I32)
    nt_max = (2 * n_all) // tm + MOE_EXPERTS
    p_pad = nt_max * tm
    tile_ids = jnp.arange(nt_max, dtype=I32)
    te = jnp.minimum(jnp.sum((tile_ids[:, None] >= tile_end[None, :]).astype(I32), axis=1),
                     MOE_EXPERTS - 1)
    active = tile_ids < ntiles
    te = jnp.where(active, te, te[jnp.maximum(ntiles - 1, 0)])
    first = (active & ((tile_ids == 0) | (te != jnp.roll(te, 1)))).astype(I32)
    wslot = (jnp.cumsum(first) - 1) % 2
    grp_end = tile_end[te]
    nxt = jnp.where(grp_end < ntiles, te[jnp.minimum(grp_end, nt_max - 1)], -1).astype(I32)
    pos_flat = pos.reshape(-1)
    inv = (jnp.arange(p_pad, dtype=I32) % n_all).at[pos_flat].set(jnp.arange(2 * n_all, dtype=I32) // 2)
    nt_arr = ntiles.reshape(1).astype(I32)
    eo = _experts(inv, te, first, wslot.astype(I32), nxt, nt_arr, h_all, w_gate, w_up, w_down, l)
    xp_new = _combine(pos_flat, eo, route, xp, mp, 5, 1, tm, 0, final_gain)
    xs_new = _combine(pos_flat, eo, route, xs_, ms, 5, tm // ts, ts, n_p, final_gain)
    return xp_new, xs_new


def kernel(x_prompt, x_sample, c_prompt, c_sample, state_ret, state_gla, state_ssm, cache_conv,
           w_ada, b_ada, norm_mix, norm_moe,
           a_w_in, a_ret_gn, a_gla_w2, a_gla_b2, a_gla_gn, a_w_out,
           c_w_in, c_conv_w, c_conv_b, c_dt_bias, c_a_log, c_d, c_norm, c_w_out,
           moe_w_group, moe_w_expert, moe_w_gate, moe_w_up, moe_w_down, final_norm):
    bp, tp, d = x_prompt.shape
    bs, ts, _ = x_sample.shape
    n_p, n_s = bp * tp, bs * ts

    pad = (-bp) % V7X_SUBLANES
    c_all = jnp.concatenate([c_prompt, jnp.zeros((pad, d), F32), c_sample], axis=0)
    m_all = _ada(c_all, w_ada, b_ada)
    mp = [m_all[l, :bp].reshape(bp, 1, 6 * d) for l in range(2)]
    ms = [m_all[l, bp + pad:].reshape(bs, 1, 6 * d) for l in range(2)]

    tm_p, tm_s = 512, 512

    wa = a_w_in[0]
    n_main_a = (wa.shape[1] // 512) * 512
    wa_low = jnp.pad(wa[:, n_main_a:], ((0, 0), (0, V7X_LANES - (wa.shape[1] - n_main_a))))
    w2p = jnp.pad(a_gla_w2[0], ((0, V7X_LANES - GLA_RANK), (0, 0)))
    b2 = a_gla_b2[0].reshape(1, -1)
    gnr = a_ret_gn[0].reshape(1, -1)
    gng = a_gla_gn[0].reshape(1, -1)
    half = 128
    inv_freq = ROPE_BASE ** (-jnp.arange(half, dtype=F32) / half)
    ang_p = jnp.arange(tp, dtype=F32)[:, None] * inv_freq[None, :]
    ang_s = (PAST_LEN + jnp.arange(ts, dtype=F32))[:, None] * inv_freq[None, :]
    log_gamma = jnp.log(1.0 - 2.0 ** (-5.0 - jnp.arange(RET_HEADS, dtype=F32)))
    lam = jnp.broadcast_to(log_gamma[:, None, None], (RET_HEADS, 1, 256))

    wa_parts, wa_low_parts, wo_parts = _split_bf16(wa), _split_bf16(wa_low), _split_bf16(a_w_out[0])
    pa_p, low_p = _inproj(x_prompt, mp[0], 0, 1, norm_mix[0], wa_parts, n_main_a, wa_low_parts, 1, tm_p)
    pa_s, low_s = _inproj(x_sample, ms[0], 0, 1, norm_mix[0], wa_parts, n_main_a, wa_low_parts,
                          tm_s // ts, ts)
    or_p, og_p, ret_p, gla_p = _mixa_prompt(pa_p, low_p, w2p, b2, jnp.cos(ang_p), jnp.sin(ang_p),
                                            gnr, gng, lam, bp, tp)
    or_s, og_s, ret_s, gla_s = _mixa_sample(pa_s, low_s, w2p, b2, jnp.cos(ang_s), jnp.sin(ang_s),
                                            gnr, gng, lam, state_ret[0], state_gla[0])
    xp = _outproj([or_p, og_p], wo_parts, x_prompt, mp[0], 2, 1, tm_p)
    xs_ = _outproj([or_s, og_s], wo_parts, x_sample, ms[0], 2, tm_s // ts, ts)
    xp, xs_ = _moe_layer(xp, xs_, mp[0], ms[0], 0, norm_moe, moe_w_group, moe_w_expert,
                         moe_w_gate, moe_w_up, moe_w_down, None)

    wc = c_w_in[0]
    n_main_c = (wc.shape[1] // 512) * 512
    heads = wc.shape[1] - n_main_c
    hpg = heads // SSM_GROUPS
    d_inner = heads * SSM_HEADDIM
    conv_dim = n_main_c - d_inner
    wdt = jnp.pad(wc[:, n_main_c:].reshape(d, SSM_GROUPS, hpg),
                  ((0, 0), (0, 0), (0, V7X_LANES - hpg))).reshape(d, SSM_GROUPS * V7X_LANES).astype(BF16)
    wc_b = wc.astype(BF16)
    wdt_all = jnp.pad(wc[:, n_main_c:], ((0, 0), (0, V7X_LANES - heads))).astype(BF16)
    pc_p, dt_p = _inproj(x3=xp, m3=mp[1], chunk_shift=0, chunk_scale=1, gain=norm_mix[1], w=wc_b,
                         n_main=n_main_c, ws=wdt_all, sb=1, rb=2 * tm_p)
    pc_s, dt_s = _inproj(x3=xs_, m3=ms[1], chunk_shift=0, chunk_scale=1, gain=norm_mix[1], w=wc_b,
                         n_main=n_main_c, ws=wdt, sb=tm_s // ts, rb=ts)
    act_p = _conv_prompt(pc_p, c_conv_w[0], c_conv_b[0], bp, tp, d_inner, conv_dim)
    act_s = _conv_sample(pc_s.reshape(bs, ts, n_main_c), cache_conv[0], c_conv_w[0], c_conv_b[0],
                         d_inner, conv_dim).reshape(n_s, conv_dim)
    conv_p = pc_p.reshape(bp, tp, n_main_c)[:, tp - (CONV_W - 1):, d_inner:]
    conv_s = pc_s.reshape(bs, ts, n_main_c)[:, ts - (CONV_W - 1):, d_inner:]

    def lanes_pad(v):
        return jnp.pad(v.reshape(SSM_GROUPS, 1, hpg), ((0, 0), (0, 0), (0, V7X_LANES - hpg)))

    bias_c, alog_c = lanes_pad(c_dt_bias[0]), lanes_pad(c_a_log[0])
    dfull = jnp.repeat(c_d[0], SSM_HEADDIM).reshape(SSM_GROUPS, 1, hpg * SSM_HEADDIM)

    def lanes_all(v):
        return jnp.pad(v.reshape(1, heads), ((0, 0), (0, V7X_LANES - heads)))

    bias_r = jnp.broadcast_to(c_dt_bias[0].reshape(heads, 1), (heads, MIX_CHUNK))
    alog_r = jnp.broadcast_to(c_a_log[0].reshape(heads, 1), (heads, MIX_CHUNK))
    dt_t = jnp.transpose(dt_p[:, :heads].reshape(bp, tp, heads), (0, 2, 1))
    yg_p, ssm_p = _ssd_prompt(act_p, pc_p, dt_p, dt_t, lanes_all(c_dt_bias[0]), lanes_all(c_a_log[0]),
                              bias_r, alog_r, dfull.reshape(1, d_inner), bp, tp)
    yg_s, ssm_s = _ssd_sample(act_s, pc_s, dt_s, bias_c, alog_c, dfull, state_ssm[0])
    wco_b = c_w_out[0].astype(BF16)
    xp = _outproj([yg_p], wco_b, xp, mp[1], 2, 1, 256, norm_gain=c_norm[0], tn=d)
    xs_ = _outproj([yg_s], wco_b, xs_, ms[1], 2, 256 // ts, ts, norm_gain=c_norm[0], tn=d)
    yp, ys = _moe_layer(xp, xs_, mp[1], ms[1], 1, norm_moe, moe_w_group, moe_w_expert,
                        moe_w_gate, moe_w_up, moe_w_down, final_norm)

    return (yp, ys, ret_p[None], ret_s[None], gla_p[None], gla_s[None],
            ssm_p[None], ssm_s[None], conv_p[None], conv_s[None])
```

```python
import functools
import math

import jax
import jax.numpy as jnp
from jax import lax
from jax.experimental import pallas as pl
from jax.experimental.pallas import tpu as pltpu

F32 = jnp.float32
BF16 = jnp.bfloat16
I32 = jnp.int32

EPS = 1e-6
PAST_LEN = 16384
ROPE_BASE = 10000.0
GLA_TEMP = 16.0
GLA_RANK = 16
RET_HEADS = 4
GLA_HEADS = 4
SSM_GROUPS = 8
SSM_HEADDIM = 64
SSM_STATE = 128
CONV_W = 4
MOE_GROUPS = 4
MOE_PER_GROUP = 8
MOE_EXPERTS = MOE_GROUPS * MOE_PER_GROUP
MOE_TOPK = 2

V7X_LANES = 128
V7X_SUBLANES = 8
V7X_VMEM_BYTES = 64 * 2 ** 20
V7X_VMEM_RESERVE = 6 * 2 ** 20

A_RET_DK, A_GLA_DK, A_DV = 256, 128, 256
A_QR = 0
A_KR = A_QR + RET_HEADS * A_RET_DK
A_VR = A_KR + RET_HEADS * A_RET_DK
A_GR = A_VR + RET_HEADS * A_DV
A_QG = A_GR + RET_HEADS * A_DV
A_KG = A_QG + GLA_HEADS * A_GLA_DK
A_VG = A_KG + GLA_HEADS * A_GLA_DK
A_GG = A_VG + GLA_HEADS * A_DV

MIX_CHUNK = 128
GLA_BLOCK = 16
MOE_TILE = 256
DMA_UNROLL = 8
ROW_SLOTS = 3
WEIGHT_DMA_PRIORITY = 1


def _cparams(n_grid, block_bytes, scratch_bytes=0, temp_bytes=0, **kwargs):
    need = 2 * block_bytes + scratch_bytes + temp_bytes + 8 * 2 ** 20
    limit = int(min(max(need, 24 * 2 ** 20), V7X_VMEM_BYTES - V7X_VMEM_RESERVE))
    return pltpu.CompilerParams(dimension_semantics=("arbitrary",) * n_grid,
                                vmem_limit_bytes=limit, **kwargs)


def _nbytes(shape, dtype=F32):
    return math.prod(shape) * jnp.dtype(dtype).itemsize


def _silu(x):
    return x * (1.0 / (1.0 + jnp.exp(-x)))


def _softplus(x):
    return jnp.maximum(x, 0.0) + jnp.log1p(jnp.exp(-jnp.abs(x)))


def _log_sigmoid(x):
    return -_softplus(-x)


def _dot(a, b):
    return jnp.dot(a, b, preferred_element_type=F32)


def _dot_nt(a, b):
    return lax.dot_general(a, b, (((1,), (1,)), ((), ())), preferred_element_type=F32)


def _split2(x):
    hi = x.astype(BF16)
    return hi, (x - hi.astype(F32)).astype(BF16)


def _dot3p(a, b):
    return _dot(a[0], b[0]) + _dot(a[0], b[1]) + _dot(a[1], b[0])


def _dot3(a, b):
    return _dot3p(_split2(a), _split2(b))


def _dot3_nt(a, b):
    ah, al = _split2(a)
    bh, bl = _split2(b)
    return _dot_nt(ah, bh) + _dot_nt(ah, bl) + _dot_nt(al, bh)


def _split3(x):
    a = x.astype(BF16)
    r = x - a.astype(F32)
    b = r.astype(BF16)
    r = r - b.astype(F32)
    return a, b, r.astype(BF16)


def _dot01_left(m01, x):
    a, b, c = _split3(x)
    return _dot(m01, a) + _dot(m01, b) + _dot(m01, c)


def _dot01_right(x, m01):
    a, b, c = _split3(x)
    return _dot(a, m01) + _dot(b, m01) + _dot(c, m01)


def _iota(shape, dim, dtype=I32):
    return lax.broadcasted_iota(dtype, shape, dim)


def _tril01(n, block=None, strict=False):
    i = _iota((n, n), 0)
    j = _iota((n, n), 1)
    m = (i > j) if strict else (i >= j)
    if block is not None:
        m = m & ((i // block) == (j // block))
    return jnp.where(m, 1.0, 0.0).astype(BF16)


def _rms_rows(x, g):
    return x * lax.rsqrt(jnp.mean(x * x, axis=-1, keepdims=True) + EPS) * g


def _rot(x, cos, sin):
    half = x.shape[-1] // 2
    x1, x2 = x[:, :half], x[:, half:]
    return jnp.concatenate([x1 * cos - x2 * sin, x1 * sin + x2 * cos], axis=-1)


def _head_norm(o, gain, gate, center):
    if center:
        o = o - jnp.mean(o, axis=-1, keepdims=True)
    o = o * lax.rsqrt(jnp.mean(o * o, axis=-1, keepdims=True) + EPS)
    return o * gain * _silu(gate)


def _pad_rows(x, rows):
    return jnp.concatenate([x, jnp.zeros((rows - x.shape[0], x.shape[1]), x.dtype)], axis=0)


def _ada_body(c_ref, w_ref, b_ref, o_ref):
    o_ref[0] = _dot3(_silu(c_ref[...]), w_ref[0]) + b_ref[0]


def _ada(c_all, w_ada, b_ada):
    depth, d, n6 = w_ada.shape
    m = c_all.shape[0]
    tn = 1024
    blocks = _nbytes((m, d)) + _nbytes((d, tn)) + _nbytes((m, tn))
    return pl.pallas_call(
        _ada_body, name="ada",
        grid=(depth, n6 // tn),
        in_specs=[pl.BlockSpec((m, d), lambda l, j: (0, 0)),
                  pl.BlockSpec((1, d, tn), lambda l, j: (l, 0, j)),
                  pl.BlockSpec((1, 1, tn), lambda l, j: (l, 0, j))],
        out_specs=pl.BlockSpec((1, m, tn), lambda l, j: (l, 0, j)),
        out_shape=jax.ShapeDtypeStruct((depth, m, n6), F32),
        compiler_params=_cparams(2, blocks, temp_bytes=3 * _nbytes((d, tn))),
    )(c_all, w_ada, b_ada.reshape(depth, 1, n6))


def _wparts(x):
    return x if isinstance(x, tuple) else (x,)


def _split_bf16(w):
    hi = lax.bitcast_convert_type(
        lax.bitcast_convert_type(w, jnp.uint32) & jnp.uint32(0xFFFF0000), F32)
    return hi.astype(BF16), (w - hi).astype(BF16)


def _dot_parts(a, b):
    if len(b) == 1:
        return _dot(a[0], b[0])
    return _dot3p(a, b)


INPROJ_SUB = 512


def _inproj_body(n_w, x_ref, sh_ref, sc_ref, g_ref, *refs):
    w_refs, ws_refs = refs[:n_w], refs[n_w:2 * n_w]
    o_ref, os_ref, h_scr = refs[2 * n_w:]
    tm = h_scr.shape[1]
    sb, rb, _ = x_ref.shape

    @pl.when(pl.program_id(1) == 0)
    def _():
        for r0 in range(0, tm, INPROJ_SUB):
            rows = min(INPROJ_SUB, tm)
            if sb == 1:
                x = x_ref[:, r0:r0 + rows, :]
                sc, sh = sc_ref[...], sh_ref[...]
            else:
                s0, s1 = r0 // rb, (r0 + rows) // rb
                x = x_ref[s0:s1]
                sc, sh = sc_ref[s0:s1], sh_ref[s0:s1]
            h = _rms_rows(x, g_ref[...]) * (1.0 + sc) + sh
            h = h.reshape(rows, h.shape[-1])
            hs = _split2(h) if n_w == 2 else (h.astype(BF16),)
            for k in range(n_w):
                h_scr[k, r0:r0 + rows, :] = hs[k]
            os_ref[r0:r0 + rows, :] = _dot_parts(hs, tuple(r[...] for r in ws_refs))

    o_ref[...] = _dot_parts(tuple(h_scr[k] for k in range(n_w)), tuple(r[...] for r in w_refs))


def _inproj(x3, m3, chunk_shift, chunk_scale, gain, w, n_main, ws, sb, rb, tn=512):
    w, ws = _wparts(w), _wparts(ws)
    n_w = len(w)
    sq, rq, d = x3.shape
    nb = rq // rb
    tm = sb * rb
    n_tok = sq * rq
    ns = ws[0].shape[1]
    blocks = (_nbytes((tm, d)) + 2 * _nbytes((sb, V7X_SUBLANES, d))
              + n_w * (_nbytes((d, tn), BF16) + _nbytes((d, ns), BF16))
              + _nbytes((tm, tn)) + _nbytes((tm, ns)))
    w_specs = [pl.BlockSpec((d, tn), lambda i, j: (0, j)) for _ in w]
    ws_specs = [pl.BlockSpec((d, ns), lambda i, j: (0, 0)) for _ in ws]
    return pl.pallas_call(
        functools.partial(_inproj_body, n_w), name="inproj",
        grid=(n_tok // tm, n_main // tn),
        in_specs=[pl.BlockSpec((sb, rb, d), lambda i, j: (i // nb, i % nb, 0)),
                  pl.BlockSpec((sb, 1, d), lambda i, j: (i // nb, 0, chunk_shift)),
                  pl.BlockSpec((sb, 1, d), lambda i, j: (i // nb, 0, chunk_scale)),
                  pl.BlockSpec((1, d), lambda i, j: (0, 0))] + w_specs + ws_specs,
        out_specs=[pl.BlockSpec((tm, tn), lambda i, j: (i, j)),
                   pl.BlockSpec((tm, ns), lambda i, j: (i, 0))],
        out_shape=[jax.ShapeDtypeStruct((n_tok, n_main), F32),
                   jax.ShapeDtypeStruct((n_tok, ns), F32)],
        scratch_shapes=[pltpu.VMEM((n_w, tm, d), BF16)],
        compiler_params=_cparams(2, blocks, n_w * _nbytes((tm, d), BF16),
                                 4 * _nbytes((min(tm, INPROJ_SUB), d))),
    )(x3, m3, m3, gain.reshape(1, d), *w, *ws)


def _outproj_body(n_in, n_w, norm, *refs):
    o_refs = refs[:n_in]
    w_refs = refs[n_in:n_in + n_in * n_w]
    pos = n_in + n_in * n_w
    if norm:
        gn_ref = refs[pos]
        pos += 1
    x_ref, gate_ref, out_ref, o_scr = refs[pos:pos + 4]

    @pl.when(pl.program_id(1) == 0)
    def _():
        for k in range(n_in):
            o = o_refs[k][...]
            if norm:
                o = _rms_rows(o, gn_ref[...])
            parts = _split2(o) if n_w == 2 else (o.astype(BF16),)
            for p in range(n_w):
                o_scr[k, p] = parts[p]

    acc = None
    for k in range(n_in):
        part = _dot_parts(tuple(o_scr[k, p] for p in range(n_w)),
                          tuple(w_refs[k * n_w + p][...] for p in range(n_w)))
        acc = part if acc is None else acc + part
    x = x_ref[...]
    out_ref[...] = x + gate_ref[...] * acc.reshape(x.shape)


def _outproj(o_list, w, x3, m3, chunk_gate, sb, rb, norm_gain=None, tn=512):
    w = _wparts(w)
    n_w = len(w)
    sq, rq, d = x3.shape
    nb = rq // rb
    tm = sb * rb
    n_in = len(o_list)
    nj = d // tn
    widths = [o.shape[1] for o in o_list]
    assert all(wd == widths[0] for wd in widths) and sum(widths) == w[0].shape[0]
    kw = widths[0]
    in_specs = [pl.BlockSpec((tm, kw), lambda i, j: (i, 0)) for _ in o_list]
    args = list(o_list)
    for k in range(n_in):
        for part in w:
            mode = pl.Buffered(1) if nj == 1 else None
            in_specs.append(pl.BlockSpec((kw, tn), lambda i, j, k=k: (k, j), pipeline_mode=mode))
            args.append(part)
    if norm_gain is not None:
        in_specs.append(pl.BlockSpec((1, kw), lambda i, j: (0, 0)))
        args.append(norm_gain.reshape(1, kw))
    in_specs += [pl.BlockSpec((sb, rb, tn), lambda i, j: (i // nb, i % nb, j)),
                 pl.BlockSpec((sb, 1, tn), lambda i, j: (i // nb, 0, chunk_gate * nj + j))]
    args += [x3, m3]
    blocks = (n_in * (_nbytes((tm, kw)) + n_w * _nbytes((kw, tn), BF16))
              + 2 * _nbytes((tm, tn)) + _nbytes((sb, V7X_SUBLANES, tn)))
    scr = n_in * n_w * _nbytes((tm, kw), BF16)
    return pl.pallas_call(
        functools.partial(_outproj_body, n_in, n_w, norm_gain is not None), name="outproj",
        grid=((sq * rq) // tm, nj),
        in_specs=in_specs,
        out_specs=pl.BlockSpec((sb, rb, tn), lambda i, j: (i // nb, i % nb, j)),
        out_shape=jax.ShapeDtypeStruct(x3.shape, F32),
        scratch_shapes=[pltpu.VMEM((n_in, n_w, tm, kw), BF16)],
        compiler_params=_cparams(2, blocks, scr, 3 * _nbytes((tm, kw))),
    )(*args)


def _ret_tables(lam, c, width):
    lam1 = lam[:, :1]
    rel = (_iota((c, c), 0) - _iota((c, c), 1)).astype(F32)
    din = jnp.where(rel >= 0, jnp.exp(lam1 * jnp.maximum(rel, 0.0)), 0.0)
    r = _iota((c, width), 0).astype(F32)
    dq = jnp.exp(lam * (r + 1.0))
    dk = jnp.exp(lam * (c - 1.0 - r))
    gc = jnp.exp(lam * float(c))
    return din, dq, dk, gc


def _gla_log_alpha(alow, w2, b2):
    xa = _dot3(alow, w2) + b2
    return _log_sigmoid(xa) * (1.0 / GLA_TEMP)


def _mixa_prompt_body(pa, alow, w2, b2, cos, sin, gnr, gng, lam,
                      o_r, o_g, sret_out, sgla_out,
                      s_ret, st_gla, din_s, dq_s, dk_s, a_s):
    c_id = pl.program_id(1)
    n_c = pl.num_programs(1)
    c = pa.shape[0]
    dk_r, dk_g, dv = A_RET_DK, A_GLA_DK, A_DV

    @pl.when(c_id == 0)
    def _():
        s_ret[...] = jnp.zeros_like(s_ret)
        st_gla[...] = jnp.zeros_like(st_gla)
        for hh in range(RET_HEADS):
            din, dq, dk, _ = _ret_tables(lam[hh], c, dk_r)
            din_s[hh] = din
            dq_s[hh] = dq
            dk_s[hh] = dk

    cs, sn = cos[...], sin[...]
    la = _gla_log_alpha(alow[...], w2[...], b2[...])
    bc_all = _dot01_left(_tril01(c), la)
    rows = _iota((c, dk_g), 0)
    nblk = c // GLA_BLOCK
    row16 = _iota((GLA_BLOCK, dk_g), 0)
    lane16 = _iota((GLA_BLOCK, c), 1)
    for hh in range(RET_HEADS):
        q = _rot(pa[:, A_QR + hh * dk_r:A_QR + (hh + 1) * dk_r], cs, sn)
        k = _rot(pa[:, A_KR + hh * dk_r:A_KR + (hh + 1) * dk_r], cs, sn) * (dk_r ** -0.5)
        qs, vs = _split2(q), _split2(pa[:, A_VR + hh * dv:A_VR + (hh + 1) * dv])
        sc = _dot3_nt(q, k) * din_s[hh]
        s_old = s_ret[hh]
        o = _dot3p(_split2(sc), vs) + _dot3p(qs, _split2(s_old)) * dq_s[hh]
        kd = k * dk_s[hh]
        s_ret[hh] = s_old * jnp.exp(lam[hh] * float(c)) + _dot3p(_split2(kd.T), vs)
        o_r[:, hh * dv:(hh + 1) * dv] = _head_norm(
            o, gnr[:, hh * dv:(hh + 1) * dv], pa[:, A_GR + hh * dv:A_GR + (hh + 1) * dv], True)

        bc = bc_all[:, hh * dk_g:(hh + 1) * dk_g]
        qg_v = pa[:, A_QG + hh * dk_g:A_QG + (hh + 1) * dk_g] * (dk_g ** -0.5)
        kg_v = pa[:, A_KG + hh * dk_g:A_KG + (hh + 1) * dk_g]
        vg_v = pa[:, A_VG + hh * dv:A_VG + (hh + 1) * dv]
        for blk in range(nblk):
            r0 = blk * GLA_BLOCK
            q_i = qg_v[r0:r0 + GLA_BLOCK]
            b_i = bc[r0:r0 + GLA_BLOCK]
            if blk == 0:
                a_i = jnp.zeros((GLA_BLOCK, c), F32)
            else:
                ref_b = bc[r0 - 1:r0]
                qt = q_i * jnp.exp(b_i - ref_b)
                kt = jnp.where(rows < r0, kg_v * jnp.exp(jnp.minimum(ref_b - bc, 0.0)), 0.0)
                a_i = _dot3_nt(qt, kt)
            for jj in range(GLA_BLOCK):
                j = r0 + jj
                e = jnp.exp(jnp.minimum(b_i - bc[j:j + 1], 0.0))
                col = jnp.sum(q_i * e * kg_v[j:j + 1], axis=-1, keepdims=True)
                col = jnp.where(row16[:, :1] >= jj, col, 0.0)
                a_i = a_i + jnp.where(lane16 == j, col, 0.0)
            a_s[hh, r0:r0 + GLA_BLOCK, :] = a_i
        st_old = st_gla[hh]
        og = _dot3(a_s[hh], vg_v) + _dot3_nt(qg_v * jnp.exp(bc), st_old)
        b_last = bc[c - 1:c]
        kdg = kg_v * jnp.exp(b_last - bc)
        st_gla[hh] = st_old * jnp.exp(b_last) + _dot3(vg_v.T, kdg)
        o_g[:, hh * dv:(hh + 1) * dv] = _head_norm(
            og, gng[:, hh * dv:(hh + 1) * dv], pa[:, A_GG + hh * dv:A_GG + (hh + 1) * dv], False)

    @pl.when(c_id == n_c - 1)
    def _():
        for hh in range(RET_HEADS):
            sret_out[0, hh] = s_ret[hh]
            sgla_out[0, hh] = st_gla[hh].T


def _mixa_prompt(pa, pa_low, w2p, b2, cos, sin, gnr, gng, lam, batch, seq):
    c = MIX_CHUNK
    nc = seq // c
    h = RET_HEADS
    dkr, dkg, dv = A_RET_DK, A_GLA_DK, A_DV
    width = pa.shape[1]
    const2 = lambda b, cc: (0, 0)
    in_specs = [pl.BlockSpec((c, width), lambda b, cc: (b * nc + cc, 0)),
                pl.BlockSpec((c, V7X_LANES), lambda b, cc: (b * nc + cc, 0)),
                pl.BlockSpec(w2p.shape, const2), pl.BlockSpec(b2.shape, const2),
                pl.BlockSpec((c, dkr // 2), lambda b, cc: (cc, 0)),
                pl.BlockSpec((c, dkr // 2), lambda b, cc: (cc, 0)),
                pl.BlockSpec(gnr.shape, const2), pl.BlockSpec(gng.shape, const2),
                pl.BlockSpec(lam.shape, lambda b, cc: (0, 0, 0))]
    out_specs = [pl.BlockSpec((c, h * dv), lambda b, cc: (b * nc + cc, 0)),
                 pl.BlockSpec((c, h * dv), lambda b, cc: (b * nc + cc, 0)),
                 pl.BlockSpec((1, h, dkr, dv), lambda b, cc: (b, 0, 0, 0)),
                 pl.BlockSpec((1, h, dkg, dv), lambda b, cc: (b, 0, 0, 0))]
    n_tok = batch * seq
    out_shape = [jax.ShapeDtypeStruct((n_tok, h * dv), F32),
                 jax.ShapeDtypeStruct((n_tok, h * dv), F32),
                 jax.ShapeDtypeStruct((batch, h, dkr, dv), F32),
                 jax.ShapeDtypeStruct((batch, h, dkg, dv), F32)]
    scratch = [pltpu.VMEM((h, dkr, dv), F32), pltpu.VMEM((h, dv, dkg), F32),
               pltpu.VMEM((h, c, c), F32), pltpu.VMEM((h, c, dkr), F32), pltpu.VMEM((h, c, dkr), F32),
               pltpu.VMEM((h, c, c), F32)]
    blocks = _nbytes((c, width)) + 2 * _nbytes((c, h * dv)) + h * (_nbytes((dkr, dv)) + _nbytes((dkg, dv)))
    scr = h * (_nbytes((dkr, dv)) + _nbytes((dv, dkg)) + 2 * _nbytes((c, c)) + 2 * _nbytes((c, dkr)))
    return pl.pallas_call(
        _mixa_prompt_body, name="mixa_prompt",
        grid=(batch, nc),
        in_specs=in_specs, out_specs=out_specs, out_shape=out_shape,
        scratch_shapes=scratch,
        compiler_params=_cparams(2, blocks, scr, 16 * 2 ** 20),
    )(pa, pa_low, w2p, b2, cos, sin, gnr, gng, lam)


def _direct_intra(q, k, v, decay_fn):
    t = q.shape[0]
    row = _iota((t, 1), 0)
    o = jnp.zeros((t, v.shape[1]), F32)
    for j in range(t):
        col = jnp.sum(q * decay_fn(j) * k[j:j + 1], axis=-1, keepdims=True)
        col = jnp.where(row >= j, col, 0.0)
        o = o + col * v[j:j + 1]
    return o


def _mixa_sample_body(sb, pa, alow, w2, b2, cos, sin, gnr, gng, lam, sret_in, sgla_in,
                      o_r, o_g, sret_out, sgla_out):
    t = 8
    dkr, dkg, dv = 256, 128, 256
    cs, sn = cos[...], sin[...]
    la = _gla_log_alpha(alow[...], w2[...], b2[...])
    bc_all = _dot01_left(_tril01(sb * t, block=t), la)
    rel = (_iota((t, 1), 0)).astype(F32)
    for hh in range(RET_HEADS):
        lam_row = lam[hh]
        dq = jnp.exp(lam_row * (rel + 1.0))
        dk = jnp.exp(lam_row * (t - 1.0 - rel))
        gc = jnp.exp(lam_row * float(t))
        for s in range(sb):
            r0 = s * t
            q = _rot(pa[r0:r0 + t, hh * dkr:(hh + 1) * dkr], cs, sn)
            k = _rot(pa[r0:r0 + t, 1024 + hh * dkr:1024 + (hh + 1) * dkr], cs, sn) * (dkr ** -0.5)
            v = pa[r0:r0 + t, 2048 + hh * dv:2048 + (hh + 1) * dv]
            g = pa[r0:r0 + t, 3072 + hh * dv:3072 + (hh + 1) * dv]
            s_old = sret_in[s, hh]
            o = _direct_intra(q, k, v, lambda j: jnp.exp(lam_row * jnp.maximum(rel - float(j), 0.0)))
            o = o + _dot3(q, s_old) * dq
            kdt = _pad_rows(k * dk, V7X_LANES).T
            sret_out[s, hh] = s_old * gc + _dot3(kdt, _pad_rows(v, V7X_LANES))
            o_r[r0:r0 + t, hh * dv:(hh + 1) * dv] = _head_norm(
                o, gnr[:, hh * dv:(hh + 1) * dv], g, True)
            qg = pa[r0:r0 + t, 4096 + hh * dkg:4096 + (hh + 1) * dkg] * (dkg ** -0.5)
            kg = pa[r0:r0 + t, 4608 + hh * dkg:4608 + (hh + 1) * dkg]
            vg = pa[r0:r0 + t, 5120 + hh * dv:5120 + (hh + 1) * dv]
            gg = pa[r0:r0 + t, 6144 + hh * dv:6144 + (hh + 1) * dv]
            bc = bc_all[r0:r0 + t, hh * dkg:(hh + 1) * dkg]
            sg_old = sgla_in[s, hh]
            og = _direct_intra(qg, kg, vg, lambda j: jnp.exp(jnp.minimum(bc - bc[j:j + 1], 0.0)))
            og = og + _dot3(qg * jnp.exp(bc), sg_old)
            b_last = bc[t - 1:t]
            m = _pad_rows(jnp.concatenate([kg * jnp.exp(b_last - bc), jnp.exp(b_last)], axis=0),
                          V7X_LANES).T
            sgla_out[s, hh] = sg_old * m[:, t:t + 1] + _dot3(m, _pad_rows(vg, V7X_LANES))
            o_g[r0:r0 + t, hh * dv:(hh + 1) * dv] = _head_norm(
                og, gng[:, hh * dv:(hh + 1) * dv], gg, False)


def _mixa_sample(pa, pa_low, w2p, b2, cos, sin, gnr, gng, lam, sret, sgla, sb=4):
    nseq = sret.shape[0]
    t = 8
    h, dkr, dkg, dv = RET_HEADS, 256, 128, 256
    width = pa.shape[1]
    const2 = lambda i: (0, 0)
    in_specs = [pl.BlockSpec((sb * t, width), lambda i: (i, 0)),
                pl.BlockSpec((sb * t, V7X_LANES), lambda i: (i, 0)),
                pl.BlockSpec(w2p.shape, const2), pl.BlockSpec(b2.shape, const2),
                pl.BlockSpec(cos.shape, const2), pl.BlockSpec(sin.shape, const2),
                pl.BlockSpec(gnr.shape, const2), pl.BlockSpec(gng.shape, const2),
                pl.BlockSpec(lam.shape, lambda i: (0, 0, 0)),
                pl.BlockSpec((sb, h, dkr, dv), lambda i: (i, 0, 0, 0)),
                pl.BlockSpec((sb, h, dkg, dv), lambda i: (i, 0, 0, 0))]
    out_specs = [pl.BlockSpec((sb * t, h * dv), lambda i: (i, 0)),
                 pl.BlockSpec((sb * t, h * dv), lambda i: (i, 0)),
                 pl.BlockSpec((sb, h, dkr, dv), lambda i: (i, 0, 0, 0)),
                 pl.BlockSpec((sb, h, dkg, dv), lambda i: (i, 0, 0, 0))]
    out_shape = [jax.ShapeDtypeStruct((nseq * t, h * dv), F32),
                 jax.ShapeDtypeStruct((nseq * t, h * dv), F32),
                 jax.ShapeDtypeStruct(sret.shape, F32),
                 jax.ShapeDtypeStruct(sgla.shape, F32)]
    blocks = (_nbytes((sb * t, width)) + 2 * _nbytes((sb, h, dkr, dv))
              + 2 * _nbytes((sb, h, dkg, dv)) + 2 * 2 ** 20)
    return pl.pallas_call(
        functools.partial(_mixa_sample_body, sb), name="mixa_sample",
        grid=(nseq // sb,),
        in_specs=in_specs, out_specs=out_specs, out_shape=out_shape,
        compiler_params=_cparams(1, blocks, 0, 4 * 2 ** 20),
    )(pa, pa_low, w2p, b2, cos, sin, gnr, gng, lam, sret, sgla)


def _conv_prompt_body(x_ref, prev_ref, w_ref, b_ref, o_ref):
    first = pl.program_id(1) == 0
    ext = jnp.concatenate([jnp.where(first, 0.0, prev_ref[...]), x_ref[...]], axis=0)
    hist = CONV_W - 1
    acc = b_ref[...] + pltpu.roll(ext, hist, 0)[V7X_SUBLANES:] * w_ref[0:1, :]
    for j in range(1, CONV_W):
        delayed = pltpu.roll(ext, hist - j, 0)[V7X_SUBLANES:] if j < hist else x_ref[...]
        acc = acc + delayed * w_ref[j:j + 1, :]
    o_ref[...] = _silu(acc)


def _conv_prompt(pc, conv_w, conv_b, batch, seq, col0, width, tc=512, tn=1024):
    nc = seq // tc
    cb0 = col0 // tn
    blocks = 2 * _nbytes((tc, tn)) + _nbytes((V7X_SUBLANES, tn)) * 3
    return pl.pallas_call(
        _conv_prompt_body, name="conv_prompt",
        grid=(batch, nc, width // tn),
        in_specs=[pl.BlockSpec((tc, tn), lambda b, c, j: (b * nc + c, cb0 + j)),
                  pl.BlockSpec((V7X_SUBLANES, tn),
                               lambda b, c, j: (jnp.maximum((b * nc + c) * (tc // V7X_SUBLANES) - 1, 0),
                                                cb0 + j)),
                  pl.BlockSpec((CONV_W, tn), lambda b, c, j: (0, j)),
                  pl.BlockSpec((1, tn), lambda b, c, j: (0, j))],
        out_specs=pl.BlockSpec((tc, tn), lambda b, c, j: (b * nc + c, j)),
        out_shape=jax.ShapeDtypeStruct((batch * seq, width), F32),
        compiler_params=_cparams(3, blocks, 0, 6 * _nbytes((tc, tn))),
    )(pc, pc, conv_w, conv_b.reshape(1, width))


def _conv_sample_body(x_ref, cache_ref, w_ref, b_ref, o_ref, scr):
    t = x_ref.shape[1]
    hist = CONV_W - 1
    scr[:, t - hist:t, :] = cache_ref[...]
    scr[:, t:2 * t, :] = x_ref[...]
    acc = b_ref[...] + scr[:, t - hist:2 * t - hist, :] * w_ref[0:1, :]
    for j in range(1, CONV_W):
        off = t - hist + j
        acc = acc + scr[:, off:off + t, :] * w_ref[j:j + 1, :]
    o_ref[...] = _silu(acc)


def _conv_sample(pc3, cache, conv_w, conv_b, col0, width, sbc=32, tn=512):
    nseq, t, _ = pc3.shape
    cb0 = col0 // tn
    blocks = 2 * _nbytes((sbc, t, tn)) + _nbytes((sbc, 8, tn))
    return pl.pallas_call(
        _conv_sample_body, name="conv_sample",
        grid=(nseq // sbc, width // tn),
        in_specs=[pl.BlockSpec((sbc, t, tn), lambda i, j: (i, 0, cb0 + j)),
                  pl.BlockSpec((sbc, CONV_W - 1, tn), lambda i, j: (i, 0, j)),
                  pl.BlockSpec((CONV_W, tn), lambda i, j: (0, j)),
                  pl.BlockSpec((1, tn), lambda i, j: (0, j))],
        out_specs=pl.BlockSpec((sbc, t, tn), lambda i, j: (i, 0, j)),
        out_shape=jax.ShapeDtypeStruct((nseq, t, width), F32),
        scratch_shapes=[pltpu.VMEM((sbc, 2 * t, tn), F32)],
        compiler_params=_cparams(2, blocks, _nbytes((sbc, 2 * t, tn)), 4 * _nbytes((sbc, t, tn))),
    )(pc3, cache, conv_w, conv_b.reshape(1, width))


def _expand01(heads, width, first=0):
    r = _iota((V7X_LANES, heads * width), 0)
    l = _iota((V7X_LANES, heads * width), 1)
    return jnp.where((l // width) == r - first, 1.0, 0.0).astype(BF16)


def _ssd_prompt_body(xs, bm, cm, z, dtc, dtr, bias_c, alog_c, bias_r, alog_r, dfull,
                     y_out, s_out, st):
    c_id = pl.program_id(1)
    n_c = pl.num_programs(1)
    l = xs.shape[0]
    groups, n, gw = st.shape
    heads = dtr.shape[1]
    hpg = heads // groups
    p = gw // hpg

    @pl.when(c_id == 0)
    def _():
        st[...] = jnp.zeros_like(st)

    dt_c = _softplus(dtc[...] + bias_c[...])
    cum_c = _dot01_left(_tril01(l), dt_c * (-jnp.exp(alog_c[...])))
    cum_parts, dt_parts = _split3(cum_c), _split3(dt_c)
    dt_r = _softplus(dtr[0] + bias_r[...])
    tri_u = jnp.where(_iota((l, l), 0) <= _iota((l, l), 1), 1.0, 0.0).astype(BF16)
    cum_r = _dot01_right(dt_r * (-jnp.exp(alog_r[...])), tri_u)

    causal = _iota((l, l), 0) >= _iota((l, l), 1)
    lane = _iota((l, 2 * p), 1)
    for g in range(groups):
        x = xs[:, g * gw:(g + 1) * gw]
        bmg = bm[:, g * n:(g + 1) * n]
        cmb = cm[:, g * n:(g + 1) * n].astype(BF16)
        cb = _dot_nt(cmb, bmg.astype(BF16))
        y_tiles = []
        for pair in range(hpg // 2):
            xt = x[:, pair * 2 * p:(pair + 1) * 2 * p]
            acc = None
            for sub in range(2):
                r = g * hpg + 2 * pair + sub
                seg = cum_c[:, r:r + 1] - cum_r[r:r + 1, :]
                w = jnp.where(causal, cb * jnp.exp(jnp.minimum(seg, 0.0)) * dt_r[r:r + 1, :], 0.0)
                xm = jnp.where((lane // p) == sub, xt, 0.0)
                part = _dot(w.astype(BF16), xm.astype(BF16))
                acc = part if acc is None else acc + part
            y_tiles.append(acc)
        y = jnp.concatenate(y_tiles, axis=-1)

        ex = _expand01(hpg, p, g * hpg)
        cum_f = _dot(cum_parts[0], ex) + _dot(cum_parts[1], ex) + _dot(cum_parts[2], ex)
        dt_f = _dot(dt_parts[0], ex) + _dot(dt_parts[1], ex) + _dot(dt_parts[2], ex)
        st_old = st[g]
        y = y + _dot(cmb, st_old.astype(BF16)) * jnp.exp(cum_f)
        cl_f = cum_f[l - 1:l]
        tail = jnp.exp(cl_f - cum_f) * dt_f
        st[g] = st_old * jnp.exp(cl_f) + _dot(bmg.T.astype(BF16), (x * tail).astype(BF16))
        y = y + dfull[:, g * gw:(g + 1) * gw] * x
        y_out[:, g * gw:(g + 1) * gw] = y * _silu(z[:, g * gw:(g + 1) * gw])

    @pl.when(c_id == n_c - 1)
    def _():
        for g in range(groups):
            s_out[0, g * hpg:(g + 1) * hpg] = st[g].T.reshape(hpg, p, n)


def _ssd_prompt(act, pc, dt_c, dt_t, bias_c, alog_c, bias_r, alog_r, dfull, batch, seq):
    l = MIX_CHUNK
    nc = seq // l
    g = SSM_GROUPS
    n = SSM_STATE
    p = SSM_HEADDIM
    heads = dt_t.shape[1]
    d_inner = heads * p
    gw = d_inner // g
    gn = g * n

    def row(b, cc):
        return b * nc + cc

    in_specs = [pl.BlockSpec((l, d_inner), lambda b, cc: (row(b, cc), 0)),
                pl.BlockSpec((l, gn), lambda b, cc: (row(b, cc), d_inner // gn)),
                pl.BlockSpec((l, gn), lambda b, cc: (row(b, cc), d_inner // gn + 1)),
                pl.BlockSpec((l, d_inner), lambda b, cc: (row(b, cc), 0)),
                pl.BlockSpec((l, V7X_LANES), lambda b, cc: (row(b, cc), 0)),
                pl.BlockSpec((1, heads, l), lambda b, cc: (b, 0, cc)),
                pl.BlockSpec((1, V7X_LANES), lambda b, cc: (0, 0)),
                pl.BlockSpec((1, V7X_LANES), lambda b, cc: (0, 0)),
                pl.BlockSpec((heads, l), lambda b, cc: (0, 0)),
                pl.BlockSpec((heads, l), lambda b, cc: (0, 0)),
                pl.BlockSpec((1, d_inner), lambda b, cc: (0, 0))]
    out_specs = [pl.BlockSpec((l, d_inner), lambda b, cc: (row(b, cc), 0)),
                 pl.BlockSpec((1, heads, p, n), lambda b, cc: (b, 0, 0, 0))]
    out_shape = [jax.ShapeDtypeStruct((batch * seq, d_inner), F32),
                 jax.ShapeDtypeStruct((batch, heads, p, n), F32)]
    blocks = 3 * _nbytes((l, d_inner)) + 2 * _nbytes((l, gn)) + _nbytes((heads, p, n))
    return pl.pallas_call(
        _ssd_prompt_body, name="ssd_prompt",
        grid=(batch, nc),
        in_specs=in_specs, out_specs=out_specs, out_shape=out_shape,
        scratch_shapes=[pltpu.VMEM((g, n, gw), F32)],
        compiler_params=_cparams(2, blocks, _nbytes((g, n, gw)), 16 * 2 ** 20),
    )(act, act, act, pc, dt_c, dt_t, bias_c, alog_c, bias_r, alog_r, dfull)


def _ssd_sample_body(sb, xs, bm, cm, z, dtc, bias_c, alog_c, dfull, s_in, y_out, s_out, y_s):
    t = 8
    rows = sb * t
    gw = xs.shape[1]
    hpg = s_in.shape[1]
    p = gw // hpg
    n = bm.shape[1]
    dt_c = _softplus(dtc[...] + bias_c[0])
    cum_c = _dot01_left(_tril01(rows, block=t), dt_c * (-jnp.exp(alog_c[0])))
    ex = _expand01(hpg, p)
    cum_f = _dot01_right(cum_c, ex)
    dt_f = _dot01_right(dt_c, ex)
    x = xs[...]
    x3 = x.reshape(sb, t, gw)
    c3 = cm[...].reshape(sb, t, n)
    b3 = bm[...].reshape(sb, t, n)
    cum3 = cum_f.reshape(sb, t, gw)
    dt3 = dt_f.reshape(sb, t, gw)
    rowi = _iota((sb, t, 1), 1)
    y3 = jnp.zeros((sb, t, gw), F32)
    for j in range(t):
        cbj = jnp.sum(c3 * b3[:, j:j + 1, :], axis=-1, keepdims=True)
        dec = jnp.exp(jnp.minimum(cum3 - cum3[:, j:j + 1, :], 0.0))
        w = jnp.where(rowi >= j, cbj * dec * dt3[:, j:j + 1, :], 0.0)
        y3 = y3 + w * x3[:, j:j + 1, :]
    y_s[...] = y3.reshape(rows, gw)
    cmv = cm[...]
    bmv = bm[...]
    for s in range(sb):
        r0 = s * t
        sg = s_in[s].reshape(hpg * p, n)
        cf = cum_f[r0:r0 + t]
        yi = _dot_nt(cmv[r0:r0 + t].astype(BF16), sg.astype(BF16)) * jnp.exp(cf)
        cl = cf[t - 1:t]
        xt = x[r0:r0 + t] * (jnp.exp(cl - cf) * dt_f[r0:r0 + t])
        m = _pad_rows(jnp.concatenate([xt, jnp.exp(cl)], axis=0), V7X_LANES).T
        s_new = sg * m[:, t:t + 1] + _dot(m.astype(BF16), _pad_rows(bmv[r0:r0 + t], V7X_LANES).astype(BF16))
        s_out[s] = s_new.reshape(hpg, p, n)
        y_s[r0:r0 + t, :] = y_s[r0:r0 + t, :] + yi
    y_out[...] = (y_s[...] + dfull[0] * x) * _silu(z[...])


def _ssd_sample(act, pc, pcs, bias_c, alog_c, dfull, s_ssm, sb=8):
    t = 8
    nseq = s_ssm.shape[0]
    g = SSM_GROUPS
    n = SSM_STATE
    p = SSM_HEADDIM
    d_inner = pc.shape[1] - 6144
    gw = d_inner // g
    hpg = gw // p
    bb0 = d_inner // n
    cb0 = (d_inner + g * n) // n
    rows = sb * t
    in_specs = [pl.BlockSpec((rows, gw), lambda i, gg: (i, gg)),
                pl.BlockSpec((rows, n), lambda i, gg: (i, bb0 + gg)),
                pl.BlockSpec((rows, n), lambda i, gg: (i, cb0 + gg)),
                pl.BlockSpec((rows, gw), lambda i, gg: (i, gg)),
                pl.BlockSpec((rows, V7X_LANES), lambda i, gg: (i, gg)),
                pl.BlockSpec((1, 1, V7X_LANES), lambda i, gg: (gg, 0, 0)),
                pl.BlockSpec((1, 1, V7X_LANES), lambda i, gg: (gg, 0, 0)),
                pl.BlockSpec((1, 1, gw), lambda i, gg: (gg, 0, 0)),
                pl.BlockSpec((sb, hpg, p, n), lambda i, gg: (i, gg, 0, 0))]
    out_specs = [pl.BlockSpec((rows, gw), lambda i, gg: (i, gg)),
                 pl.BlockSpec((sb, hpg, p, n), lambda i, gg: (i, gg, 0, 0))]
    out_shape = [jax.ShapeDtypeStruct((nseq * t, d_inner), F32),
                 jax.ShapeDtypeStruct(s_ssm.shape, F32)]
    blocks = 3 * _nbytes((rows, gw)) + 3 * _nbytes((rows, n)) + 2 * _nbytes((sb, hpg, p, n))
    return pl.pallas_call(
        functools.partial(_ssd_sample_body, sb), name="ssd_sample",
        grid=(nseq // sb, g),
        in_specs=in_specs, out_specs=out_specs, out_shape=out_shape,
        scratch_shapes=[pltpu.VMEM((rows, gw), F32)],
        compiler_params=_cparams(2, blocks, _nbytes((rows, gw)), 8 * 2 ** 20),
    )(act, act, act, pc, pcs, bias_c, alog_c, dfull, s_ssm)


def _moe_pre_body(n_pt, xp_ref, shp_ref, scp_ref, xs_ref, shs_ref, scs_ref, g_ref, wr_ref,
                  h_out, route_out):
    i = pl.program_id(0)

    @pl.when(i < n_pt)
    def _():
        _moe_route(xp_ref, shp_ref, scp_ref, g_ref, wr_ref, h_out, route_out)

    @pl.when(i >= n_pt)
    def _():
        _moe_route(xs_ref, shs_ref, scs_ref, g_ref, wr_ref, h_out, route_out)


def _moe_route(x_ref, sh_ref, sc_ref, g_ref, wr_ref, h_out, route_out):
    x = x_ref[...]
    h = _rms_rows(x, g_ref[...]) * (1.0 + sc_ref[...]) + sh_ref[...]
    h = h.reshape(h_out.shape[0], x.shape[-1])
    h_out[...] = h.reshape(h_out.shape)
    logits = _dot3(h, wr_ref[...])
    tm = logits.shape[0]
    lane = _iota((tm, V7X_LANES), 1).astype(F32)
    neg = -jnp.inf
    lg = jnp.where(lane < MOE_GROUPS, logits, neg)
    mg = jnp.max(lg, axis=-1, keepdims=True)
    top_g = jnp.min(jnp.where(lg == mg, lane, float(V7X_LANES)), axis=-1, keepdims=True)
    p_top = 1.0 / jnp.sum(jnp.where(lane < MOE_GROUPS, jnp.exp(logits - mg), 0.0),
                          axis=-1, keepdims=True)
    lo = MOE_GROUPS + MOE_PER_GROUP * top_g
    le = jnp.where((lane >= lo) & (lane < lo + MOE_PER_GROUP), logits, neg)
    v1 = jnp.max(le, axis=-1, keepdims=True)
    i1 = jnp.min(jnp.where(le == v1, lane, float(V7X_LANES)), axis=-1, keepdims=True)
    le2 = jnp.where(lane == i1, neg, le)
    v2 = jnp.max(le2, axis=-1, keepdims=True)
    i2 = jnp.min(jnp.where(le2 == v2, lane, float(V7X_LANES)), axis=-1, keepdims=True)
    tt = jnp.exp(v2 - v1)
    w1 = p_top / (1.0 + tt)
    w2 = p_top * tt / (1.0 + tt)
    route_out[...] = jnp.where(lane == 0, i1 - MOE_GROUPS,
                     jnp.where(lane == 1, i2 - MOE_GROUPS,
                     jnp.where(lane == 2, w1, jnp.where(lane == 3, w2, 0.0))))


def _moe_pre(xp, mp, xs_, ms, chunk_shift, chunk_scale, gain, wr, tm):
    bp, tp, d = xp.shape
    bs, ts, _ = xs_.shape
    nb = tp // tm
    n_pt = bp * nb
    sbs = tm // ts
    n_st = bs // sbs
    n_all = bp * tp + bs * ts

    def pi(i):
        return jnp.minimum(i, n_pt - 1)

    def si(i):
        return jnp.maximum(i - n_pt, 0)

    in_specs = [pl.BlockSpec((1, tm, d), lambda i: (pi(i) // nb, pi(i) % nb, 0)),
                pl.BlockSpec((1, 1, d), lambda i: (pi(i) // nb, 0, chunk_shift)),
                pl.BlockSpec((1, 1, d), lambda i: (pi(i) // nb, 0, chunk_scale)),
                pl.BlockSpec((sbs, ts, d), lambda i: (si(i), 0, 0)),
                pl.BlockSpec((sbs, 1, d), lambda i: (si(i), 0, chunk_shift)),
                pl.BlockSpec((sbs, 1, d), lambda i: (si(i), 0, chunk_scale)),
                pl.BlockSpec((1, d), lambda i: (0, 0)),
                pl.BlockSpec((d, V7X_LANES), lambda i: (0, 0))]
    blocks = (3 * _nbytes((tm, d)) + 2 * _nbytes((sbs, 8, d)) + _nbytes((d, V7X_LANES))
              + _nbytes((tm, 128)))
    return pl.pallas_call(
        functools.partial(_moe_pre_body, n_pt), name="moe_pre",
        grid=(n_pt + n_st,),
        in_specs=in_specs,
        out_specs=[pl.BlockSpec((tm, d // V7X_LANES, V7X_LANES), lambda i: (i, 0, 0)),
                   pl.BlockSpec((tm, V7X_LANES), lambda i: (i, 0))],
        out_shape=[jax.ShapeDtypeStruct((n_all, d // V7X_LANES, V7X_LANES), F32),
                   jax.ShapeDtypeStruct((n_all, V7X_LANES), F32)],
        compiler_params=_cparams(1, blocks, 0, 4 * _nbytes((tm, d))),
    )(xp, mp, mp, xs_, ms, ms, gain.reshape(1, d), wr)


def _rank_body(route_ref, rank_ref, cnt_ref, carry):
    @pl.when(pl.program_id(0) == 0)
    def _():
        carry[...] = jnp.zeros_like(carry)

    r = route_ref[...]
    tm = r.shape[0]
    lane = _iota((tm, V7X_LANES), 1).astype(F32)
    o1 = lane == r[:, 0:1]
    o2 = lane == r[:, 1:2]
    oh = jnp.where(o1, 1.0, 0.0) + jnp.where(o2, 1.0, 0.0)
    pex = _dot(_tril01(tm, strict=True), oh.astype(BF16)) + carry[0:1, :]
    r1 = jnp.sum(jnp.where(o1, pex, 0.0), axis=-1, keepdims=True)
    r2 = jnp.sum(jnp.where(o2, pex, 0.0), axis=-1, keepdims=True)
    rank_ref[...] = jnp.where(lane == 0, r1, jnp.where(lane == 1, r2, 0.0))
    carry[...] = carry[...] + jnp.sum(oh, axis=0, keepdims=True)
    cnt_ref[...] = carry[...]


def _rank(route, tm=256):
    n = route.shape[0]
    return pl.pallas_call(
        _rank_body, name="moe_rank",
        grid=(n // tm,),
        in_specs=[pl.BlockSpec((tm, V7X_LANES), lambda i: (i, 0))],
        out_specs=[pl.BlockSpec((tm, V7X_LANES), lambda i: (i, 0)),
                   pl.BlockSpec((V7X_SUBLANES, V7X_LANES), lambda i: (0, 0))],
        out_shape=[jax.ShapeDtypeStruct((n, V7X_LANES), F32),
                   jax.ShapeDtypeStruct((V7X_SUBLANES, V7X_LANES), F32)],
        scratch_shapes=[pltpu.VMEM((V7X_SUBLANES, V7X_LANES), F32)],
        compiler_params=_cparams(1, 2 * _nbytes((tm, 128))),
    )(route)


def _gather_tokens(src_hbm, dst, sem, n, row_of):
    def issue(r, carry):
        pltpu.make_async_copy(src_hbm.at[row_of(r)], dst.at[r], sem).start()
        return carry

    lax.fori_loop(0, n, issue, 0, unroll=DMA_UNROLL)


def _gather_wait(src_hbm, dst, sem):
    pltpu.make_async_copy(src_hbm.at[pl.ds(0, dst.shape[0])], dst, sem).wait()


def _expert_body(layer, inv_ref, te_ref, first_ref, wslot_ref, nxt_ref, nt_ref,
                 h3_hbm, wg_hbm, wu_hbm, wd_hbm, o_ref,
                 xbuf, wg_f, wu_f, wd_f, wg_b, wu_b, wd_b, xsem, wsem):
    i = pl.program_id(0)
    nt = nt_ref[0]
    tm = xbuf.shape[1]
    active = i < nt

    def w_copies(e, slot):
        return [pltpu.make_async_copy(src.at[layer, e], dst.at[slot], wsem.at[k])
                for k, (src, dst) in enumerate(((wg_hbm, wg_f), (wu_hbm, wu_f), (wd_hbm, wd_f)))]

    def fetch_rows(tile, slot):
        _gather_tokens(h3_hbm, xbuf.at[slot], xsem.at[slot], tm, lambda r: inv_ref[tile * tm + r])

    @pl.when(i == 0)
    def _():
        for cp in w_copies(te_ref[0], wslot_ref[0]):
            cp.start(priority=WEIGHT_DMA_PRIORITY)
        fetch_rows(0, 0)

        @pl.when(nt > 1)
        def _():
            fetch_rows(1, 1)

    @pl.when(active)
    def _():
        slot = i % ROW_SLOTS
        ws = wslot_ref[i]

        @pl.when(first_ref[i] == 1)
        def _():
            for cp in w_copies(te_ref[i], ws):
                cp.wait()
            wg_b[...] = wg_f[ws].astype(BF16)
            wu_b[...] = wu_f[ws].astype(BF16)
            wd_b[...] = wd_f[ws].astype(BF16)

            @pl.when(nxt_ref[i] >= 0)
            def _():
                for cp in w_copies(nxt_ref[i], 1 - ws):
                    cp.start(priority=WEIGHT_DMA_PRIORITY)

        @pl.when(i + 2 < nt)
        def _():
            fetch_rows(i + 2, (i + 2) % ROW_SLOTS)

        _gather_wait(h3_hbm, xbuf.at[slot], xsem.at[slot])
        xb = xbuf[slot].reshape(tm, wg_b.shape[0]).astype(BF16)
        hid = _silu(_dot(xb, wg_b[...])) * _dot(xb, wu_b[...])
        o_ref[...] = _dot(hid.astype(BF16), wd_b[...]).reshape(o_ref.shape)

    @pl.when(jnp.logical_not(active))
    def _():
        o_ref[...] = jnp.zeros_like(o_ref)


def _experts(inv, tile_expert, first, wslot, nxt, ntiles, h3, wg, wu, wd, layer):
    _, dc, ln = h3.shape
    d = dc * ln
    ff = wg.shape[3]
    tm = MOE_TILE
    p_pad = inv.shape[0]
    scratch = [pltpu.VMEM((ROW_SLOTS, tm, dc, ln), F32),
               pltpu.VMEM((2, d, ff), F32), pltpu.VMEM((2, d, ff), F32), pltpu.VMEM((2, ff, d), F32),
               pltpu.VMEM((d, ff), BF16), pltpu.VMEM((d, ff), BF16), pltpu.VMEM((ff, d), BF16),
               pltpu.SemaphoreType.DMA((ROW_SLOTS,)), pltpu.SemaphoreType.DMA((3,))]
    scr_bytes = ROW_SLOTS * _nbytes((tm, d)) + 6 * _nbytes((d, ff)) + 3 * _nbytes((d, ff), BF16)
    any_spec = pl.BlockSpec(memory_space=pl.ANY)
    return pl.pallas_call(
        functools.partial(_expert_body, layer), name="moe_experts",
        grid_spec=pltpu.PrefetchScalarGridSpec(
            num_scalar_prefetch=6,
            grid=(p_pad // tm,),
            in_specs=[any_spec, any_spec, any_spec, any_spec],
            out_specs=pl.BlockSpec((tm, dc, ln), lambda i, *_: (i, 0, 0)),
            scratch_shapes=scratch),
        out_shape=jax.ShapeDtypeStruct((p_pad, dc, ln), F32),
        compiler_params=_cparams(1, _nbytes((tm, d)), scr_bytes, 3 * _nbytes((tm, d)),
                                 disable_bounds_checks=True),
    )(inv, tile_expert, first, wslot, nxt, ntiles, h3, wg, wu, wd)


def _combine_body(tok_off, final, pos_ref, eo_hbm, route_ref, x_ref, gate_ref, fn_ref, out_ref,
                  buf, sem):
    i = pl.program_id(0)
    n = pl.num_programs(0)
    tc = route_ref.shape[0]
    d = x_ref.shape[-1]

    def fetch(step, slot):
        base = 2 * (tok_off + step * tc)
        for k in range(MOE_TOPK):
            _gather_tokens(eo_hbm, buf.at[slot, k], sem.at[slot], tc,
                           lambda r, k=k: pos_ref[base + MOE_TOPK * r + k])

    @pl.when(i == 0)
    def _():
        fetch(0, 0)

    slot = i % 2

    @pl.when(i + 1 < n)
    def _():
        fetch(i + 1, 1 - slot)

    for k in range(MOE_TOPK):
        _gather_wait(eo_hbm, buf.at[slot, k], sem.at[slot])
    rt = route_ref[...]
    y = rt[:, 2:3] * buf[slot, 0].reshape(tc, d) + rt[:, 3:4] * buf[slot, 1].reshape(tc, d)
    x = x_ref[...]
    xn = x + gate_ref[...] * y.reshape(x.shape)
    if final:
        xn = _rms_rows(xn, fn_ref[...])
    out_ref[...] = xn


def _combine(pos_flat, eo, route, x3, m3, chunk_gate, sb, rb, tok_off, final_gain=None):
    sq, rq, d = x3.shape
    nb = rq // rb
    tc = sb * rb
    blk_off = tok_off // tc
    final = final_gain is not None
    fn = (final_gain if final else jnp.ones((d,), F32)).reshape(1, d)
    blocks = 2 * _nbytes((tc, d)) + _nbytes((tc, 128)) + _nbytes((sb, 8, d))
    return pl.pallas_call(
        functools.partial(_combine_body, tok_off, final), name="moe_combine",
        grid_spec=pltpu.PrefetchScalarGridSpec(
            num_scalar_prefetch=1,
            grid=((sq * rq) // tc,),
            in_specs=[pl.BlockSpec(memory_space=pl.ANY),
                      pl.BlockSpec((tc, V7X_LANES), lambda i, p: (blk_off + i, 0)),
                      pl.BlockSpec((sb, rb, d), lambda i, p: (i // nb, i % nb, 0)),
                      pl.BlockSpec((sb, 1, d), lambda i, p: (i // nb, 0, chunk_gate)),
                      pl.BlockSpec((1, d), lambda i, p: (0, 0))],
            out_specs=pl.BlockSpec((sb, rb, d), lambda i, p: (i // nb, i % nb, 0)),
            scratch_shapes=[pltpu.VMEM((2, MOE_TOPK, tc, d // V7X_LANES, V7X_LANES), F32),
                            pltpu.SemaphoreType.DMA((2,))]),
        out_shape=jax.ShapeDtypeStruct(x3.shape, F32),
        compiler_params=_cparams(1, blocks, 2 * MOE_TOPK * _nbytes((tc, d)), 4 * _nbytes((tc, d)),
                                 disable_bounds_checks=True),
    )(pos_flat, eo, route, x3, m3, fn)


def _moe_layer(xp, xs_, mp, ms, l, norm_moe, w_group, w_expert, w_gate, w_up, w_down, final_gain):
    bp, tp, d = xp.shape
    bs, ts, _ = xs_.shape
    n_p, n_s = bp * tp, bs * ts
    n_all = n_p + n_s
    wr = jnp.concatenate([w_group[l], jnp.transpose(w_expert[l], (1, 0, 2)).reshape(d, MOE_EXPERTS)],
                         axis=1)
    wr = jnp.pad(wr, ((0, 0), (0, V7X_LANES - wr.shape[1])))
    tm = MOE_TILE
    h_all, route = _moe_pre(xp, mp, xs_, ms, 3, 4, norm_moe[l], wr, tm)
    rank, cnt = _rank(route)
    counts = cnt[0, :MOE_EXPERTS].astype(I32)
    tiles_per = (counts + tm - 1) // tm
    tile_end = jnp.cumsum(tiles_per)
    row_start = (tile_end - tiles_per) * tm
    ntiles = tile_end[-1]
    pos = jnp.stack([row_start[route[:, k].astype(I32)] + rank[:, k].astype(I32)
                     for k in range(MOE_TOPK)], axis=1)
    nt_max = (2 * n_all) // tm + MOE_EXPERTS
    p_pad = nt_max * tm
    tile_ids = jnp.arange(nt_max, dtype=I32)
    te = jnp.minimum(jnp.sum((tile_ids[:, None] >= tile_end[None, :]).astype(I32), axis=1),
                     MOE_EXPERTS - 1)
    active = tile_ids < ntiles
    te = jnp.where(active, te, te[jnp.maximum(ntiles - 1, 0)])
    first = (active & ((tile_ids == 0) | (te != jnp.roll(te, 1)))).astype(I32)
    wslot = (jnp.cumsum(first) - 1) % 2
    grp_end = tile_end[te]
    nxt = jnp.where(grp_end < ntiles, te[jnp.minimum(grp_end, nt_max - 1)], -1).astype(I32)
    pos_flat = pos.reshape(-1)
    inv = (jnp.arange(p_pad, dtype=I32) % n_all).at[pos_flat].set(jnp.arange(2 * n_all, dtype=I32) // 2)
    nt_arr = ntiles.reshape(1).astype(I32)
    eo = _experts(inv, te, first, wslot.astype(I32), nxt, nt_arr, h_all, w_gate, w_up, w_down, l)
    xp_new = _combine(pos_flat, eo, route, xp, mp, 5, 1, tm, 0, final_gain)
    xs_new = _combine(pos_flat, eo, route, xs_, ms, 5, tm // ts, ts, n_p, final_gain)
    return xp_new, xs_new


def kernel(x_prompt, x_sample, c_prompt, c_sample, state_ret, state_gla, state_ssm, cache_conv,
           w_ada, b_ada, norm_mix, norm_moe,
           a_w_in, a_ret_gn, a_gla_w2, a_gla_b2, a_gla_gn, a_w_out,
           c_w_in, c_conv_w, c_conv_b, c_dt_bias, c_a_log, c_d, c_norm, c_w_out,
           moe_w_group, moe_w_expert, moe_w_gate, moe_w_up, moe_w_down, final_norm):
    bp, tp, d = x_prompt.shape
    bs, ts, _ = x_sample.shape
    n_p, n_s = bp * tp, bs * ts

    pad = (-bp) % V7X_SUBLANES
    c_all = jnp.concatenate([c_prompt, jnp.zeros((pad, d), F32), c_sample], axis=0)
    m_all = _ada(c_all, w_ada, b_ada)
    mp = [m_all[l, :bp].reshape(bp, 1, 6 * d) for l in range(2)]
    ms = [m_all[l, bp + pad:].reshape(bs, 1, 6 * d) for l in range(2)]

    tm_p, tm_s = 512, 512

    wa = a_w_in[0]
    n_main_a = (wa.shape[1] // 512) * 512
    wa_low = jnp.pad(wa[:, n_main_a:], ((0, 0), (0, V7X_LANES - (wa.shape[1] - n_main_a))))
    w2p = jnp.pad(a_gla_w2[0], ((0, V7X_LANES - GLA_RANK), (0, 0)))
    b2 = a_gla_b2[0].reshape(1, -1)
    gnr = a_ret_gn[0].reshape(1, -1)
    gng = a_gla_gn[0].reshape(1, -1)
    half = 128
    inv_freq = ROPE_BASE ** (-jnp.arange(half, dtype=F32) / half)
    ang_p = jnp.arange(tp, dtype=F32)[:, None] * inv_freq[None, :]
    ang_s = (PAST_LEN + jnp.arange(ts, dtype=F32))[:, None] * inv_freq[None, :]
    log_gamma = jnp.log(1.0 - 2.0 ** (-5.0 - jnp.arange(RET_HEADS, dtype=F32)))
    lam = jnp.broadcast_to(log_gamma[:, None, None], (RET_HEADS, 1, 256))

    wa_parts, wa_low_parts, wo_parts = _split_bf16(wa), _split_bf16(wa_low), _split_bf16(a_w_out[0])
    pa_p, low_p = _inproj(x_prompt, mp[0], 0, 1, norm_mix[0], wa_parts, n_main_a, wa_low_parts, 1, tm_p)
    pa_s, low_s = _inproj(x_sample, ms[0], 0, 1, norm_mix[0], wa_parts, n_main_a, wa_low_parts,
                          tm_s // ts, ts)
    or_p, og_p, ret_p, gla_p = _mixa_prompt(pa_p, low_p, w2p, b2, jnp.cos(ang_p), jnp.sin(ang_p),
                                            gnr, gng, lam, bp, tp)
    or_s, og_s, ret_s, gla_s = _mixa_sample(pa_s, low_s, w2p, b2, jnp.cos(ang_s), jnp.sin(ang_s),
                                            gnr, gng, lam, state_ret[0], state_gla[0])
    xp = _outproj([or_p, og_p], wo_parts, x_prompt, mp[0], 2, 1, 256, tn=d)
    xs_ = _outproj([or_s, og_s], wo_parts, x_sample, ms[0], 2, 256 // ts, ts, tn=d)
    xp, xs_ = _moe_layer(xp, xs_, mp[0], ms[0], 0, norm_moe, moe_w_group, moe_w_expert,
                         moe_w_gate, moe_w_up, moe_w_down, None)

    wc = c_w_in[0]
    n_main_c = (wc.shape[1] // 512) * 512
    heads = wc.shape[1] - n_main_c
    hpg = heads // SSM_GROUPS
    d_inner = heads * SSM_HEADDIM
    conv_dim = n_main_c - d_inner
    wdt = jnp.pad(wc[:, n_main_c:].reshape(d, SSM_GROUPS, hpg),
                  ((0, 0), (0, 0), (0, V7X_LANES - hpg))).reshape(d, SSM_GROUPS * V7X_LANES).astype(BF16)
    wc_b = wc.astype(BF16)
    wdt_all = jnp.pad(wc[:, n_main_c:], ((0, 0), (0, V7X_LANES - heads))).astype(BF16)
    pc_p, dt_p = _inproj(x3=xp, m3=mp[1], chunk_shift=0, chunk_scale=1, gain=norm_mix[1], w=wc_b,
                         n_main=n_main_c, ws=wdt_all, sb=1, rb=2 * tm_p)
    pc_s, dt_s = _inproj(x3=xs_, m3=ms[1], chunk_shift=0, chunk_scale=1, gain=norm_mix[1], w=wc_b,
                         n_main=n_main_c, ws=wdt, sb=tm_s // ts, rb=ts)
    act_p = _conv_prompt(pc_p, c_conv_w[0], c_conv_b[0], bp, tp, d_inner, conv_dim)
    act_s = _conv_sample(pc_s.reshape(bs, ts, n_main_c), cache_conv[0], c_conv_w[0], c_conv_b[0],
                         d_inner, conv_dim).reshape(n_s, conv_dim)
    conv_p = pc_p.reshape(bp, tp, n_main_c)[:, tp - (CONV_W - 1):, d_inner:]
    conv_s = pc_s.reshape(bs, ts, n_main_c)[:, ts - (CONV_W - 1):, d_inner:]

    def lanes_pad(v):
        return jnp.pad(v.reshape(SSM_GROUPS, 1, hpg), ((0, 0), (0, 0), (0, V7X_LANES - hpg)))

    bias_c, alog_c = lanes_pad(c_dt_bias[0]), lanes_pad(c_a_log[0])
    dfull = jnp.repeat(c_d[0], SSM_HEADDIM).reshape(SSM_GROUPS, 1, hpg * SSM_HEADDIM)

    def lanes_all(v):
        return jnp.pad(v.reshape(1, heads), ((0, 0), (0, V7X_LANES - heads)))

    bias_r = jnp.broadcast_to(c_dt_bias[0].reshape(heads, 1), (heads, MIX_CHUNK))
    alog_r = jnp.broadcast_to(c_a_log[0].reshape(heads, 1), (heads, MIX_CHUNK))
    dt_t = jnp.transpose(dt_p[:, :heads].reshape(bp, tp, heads), (0, 2, 1))
    yg_p, ssm_p = _ssd_prompt(act_p, pc_p, dt_p, dt_t, lanes_all(c_dt_bias[0]), lanes_all(c_a_log[0]),
                              bias_r, alog_r, dfull.reshape(1, d_inner), bp, tp)
    yg_s, ssm_s = _ssd_sample(act_s, pc_s, dt_s, bias_c, alog_c, dfull, state_ssm[0])
    wco_b = c_w_out[0].astype(BF16)
    xp = _outproj([yg_p], wco_b, xp, mp[1], 2, 1, 256, norm_gain=c_norm[0], tn=d)
    xs_ = _outproj([yg_s], wco_b, xs_, ms[1], 2, 256 // ts, ts, norm_gain=c_norm[0], tn=d)
    yp, ys = _moe_layer(xp, xs_, mp[1], ms[1], 1, norm_moe, moe_w_group, moe_w_expert,
                        moe_w_gate, moe_w_up, moe_w_down, final_norm)

    return (yp, ys, ret_p[None], ret_s[None], gla_p[None], gla_s[None],
            ssm_p[None], ssm_s[None], conv_p[None], conv_s[None])
```

```python
import functools
import math

import jax
import jax.numpy as jnp
from jax import lax
from jax.experimental import pallas as pl
from jax.experimental.pallas import tpu as pltpu

F32 = jnp.float32
BF16 = jnp.bfloat16
I32 = jnp.int32

EPS = 1e-6
PAST_LEN = 16384
ROPE_BASE = 10000.0
GLA_TEMP = 16.0
GLA_RANK = 16
RET_HEADS = 4
GLA_HEADS = 4
SSM_GROUPS = 8
SSM_HEADDIM = 64
SSM_STATE = 128
CONV_W = 4
MOE_GROUPS = 4
MOE_PER_GROUP = 8
MOE_EXPERTS = MOE_GROUPS * MOE_PER_GROUP
MOE_TOPK = 2

V7X_LANES = 128
V7X_SUBLANES = 8
V7X_VMEM_BYTES = 64 * 2 ** 20
V7X_VMEM_RESERVE = 6 * 2 ** 20

A_RET_DK, A_GLA_DK, A_DV = 256, 128, 256
A_QR = 0
A_KR = A_QR + RET_HEADS * A_RET_DK
A_VR = A_KR + RET_HEADS * A_RET_DK
A_GR = A_VR + RET_HEADS * A_DV
A_QG = A_GR + RET_HEADS * A_DV
A_KG = A_QG + GLA_HEADS * A_GLA_DK
A_VG = A_KG + GLA_HEADS * A_GLA_DK
A_GG = A_VG + GLA_HEADS * A_DV

MIX_CHUNK = 128
GLA_BLOCK = 16
MOE_TILE = 256
DMA_UNROLL = 8
ROW_SLOTS = 3
WEIGHT_DMA_PRIORITY = 1


def _cparams(n_grid, block_bytes, scratch_bytes=0, temp_bytes=0, **kwargs):
    need = 2 * block_bytes + scratch_bytes + temp_bytes + 8 * 2 ** 20
    limit = int(min(max(need, 24 * 2 ** 20), V7X_VMEM_BYTES - V7X_VMEM_RESERVE))
    return pltpu.CompilerParams(dimension_semantics=("arbitrary",) * n_grid,
                                vmem_limit_bytes=limit, **kwargs)


def _nbytes(shape, dtype=F32):
    return math.prod(shape) * jnp.dtype(dtype).itemsize


def _silu(x):
    return x * (1.0 / (1.0 + jnp.exp(-x)))


def _softplus(x):
    return jnp.maximum(x, 0.0) + jnp.log1p(jnp.exp(-jnp.abs(x)))


def _log_sigmoid(x):
    return -_softplus(-x)


def _dot(a, b):
    return jnp.dot(a, b, preferred_element_type=F32)


def _dot_nt(a, b):
    return lax.dot_general(a, b, (((1,), (1,)), ((), ())), preferred_element_type=F32)


def _split2(x):
    hi = x.astype(BF16)
    return hi, (x - hi.astype(F32)).astype(BF16)


def _dot3p(a, b):
    return _dot(a[0], b[0]) + _dot(a[0], b[1]) + _dot(a[1], b[0])


def _dot3(a, b):
    return _dot3p(_split2(a), _split2(b))


def _dot3_nt(a, b):
    ah, al = _split2(a)
    bh, bl = _split2(b)
    return _dot_nt(ah, bh) + _dot_nt(ah, bl) + _dot_nt(al, bh)


def _split3(x):
    a = x.astype(BF16)
    r = x - a.astype(F32)
    b = r.astype(BF16)
    r = r - b.astype(F32)
    return a, b, r.astype(BF16)


def _dot01_left(m01, x):
    a, b, c = _split3(x)
    return _dot(m01, a) + _dot(m01, b) + _dot(m01, c)


def _dot01_right(x, m01):
    a, b, c = _split3(x)
    return _dot(a, m01) + _dot(b, m01) + _dot(c, m01)


def _iota(shape, dim, dtype=I32):
    return lax.broadcasted_iota(dtype, shape, dim)


def _tril01(n, block=None, strict=False):
    i = _iota((n, n), 0)
    j = _iota((n, n), 1)
    m = (i > j) if strict else (i >= j)
    if block is not None:
        m = m & ((i // block) == (j // block))
    return jnp.where(m, 1.0, 0.0).astype(BF16)


def _rms_rows(x, g):
    return x * lax.rsqrt(jnp.mean(x * x, axis=-1, keepdims=True) + EPS) * g


def _rot(x, cos, sin):
    half = x.shape[-1] // 2
    x1, x2 = x[:, :half], x[:, half:]
    return jnp.concatenate([x1 * cos - x2 * sin, x1 * sin + x2 * cos], axis=-1)


def _head_norm(o, gain, gate, center):
    if center:
        o = o - jnp.mean(o, axis=-1, keepdims=True)
    o = o * lax.rsqrt(jnp.mean(o * o, axis=-1, keepdims=True) + EPS)
    return o * gain * _silu(gate)


def _pad_rows(x, rows):
    return jnp.concatenate([x, jnp.zeros((rows - x.shape[0], x.shape[1]), x.dtype)], axis=0)


def _ada_body(c_ref, w_ref, b_ref, o_ref):
    o_ref[0] = _dot3(_silu(c_ref[...]), w_ref[0]) + b_ref[0]


def _ada(c_all, w_ada, b_ada):
    depth, d, n6 = w_ada.shape
    m = c_all.shape[0]
    tn = 1024
    blocks = _nbytes((m, d)) + _nbytes((d, tn)) + _nbytes((m, tn))
    return pl.pallas_call(
        _ada_body, name="ada",
        grid=(depth, n6 // tn),
        in_specs=[pl.BlockSpec((m, d), lambda l, j: (0, 0)),
                  pl.BlockSpec((1, d, tn), lambda l, j: (l, 0, j)),
                  pl.BlockSpec((1, 1, tn), lambda l, j: (l, 0, j))],
        out_specs=pl.BlockSpec((1, m, tn), lambda l, j: (l, 0, j)),
        out_shape=jax.ShapeDtypeStruct((depth, m, n6), F32),
        compiler_params=_cparams(2, blocks, temp_bytes=3 * _nbytes((d, tn))),
    )(c_all, w_ada, b_ada.reshape(depth, 1, n6))


def _wparts(x):
    return x if isinstance(x, tuple) else (x,)


def _split_bf16(w):
    hi = lax.bitcast_convert_type(
        lax.bitcast_convert_type(w, jnp.uint32) & jnp.uint32(0xFFFF0000), F32)
    return hi.astype(BF16), (w - hi).astype(BF16)


def _dot_parts(a, b):
    if len(b) == 1:
        return _dot(a[0], b[0])
    return _dot3p(a, b)


INPROJ_SUB = 512


def _inproj_body(n_w, x_ref, sh_ref, sc_ref, g_ref, *refs):
    w_refs, ws_refs = refs[:n_w], refs[n_w:2 * n_w]
    o_ref, os_ref, h_scr = refs[2 * n_w:]
    tm = h_scr.shape[1]
    sb, rb, _ = x_ref.shape

    @pl.when(pl.program_id(1) == 0)
    def _():
        for r0 in range(0, tm, INPROJ_SUB):
            rows = min(INPROJ_SUB, tm)
            if sb == 1:
                x = x_ref[:, r0:r0 + rows, :]
                sc, sh = sc_ref[...], sh_ref[...]
            else:
                s0, s1 = r0 // rb, (r0 + rows) // rb
                x = x_ref[s0:s1]
                sc, sh = sc_ref[s0:s1], sh_ref[s0:s1]
            h = _rms_rows(x, g_ref[...]) * (1.0 + sc) + sh
            h = h.reshape(rows, h.shape[-1])
            hs = _split2(h) if n_w == 2 else (h.astype(BF16),)
            for k in range(n_w):
                h_scr[k, r0:r0 + rows, :] = hs[k]
            os_ref[r0:r0 + rows, :] = _dot_parts(hs, tuple(r[...] for r in ws_refs))

    o_ref[...] = _dot_parts(tuple(h_scr[k] for k in range(n_w)), tuple(r[...] for r in w_refs))


def _inproj(x3, m3, chunk_shift, chunk_scale, gain, w, n_main, ws, sb, rb, tn=512):
    w, ws = _wparts(w), _wparts(ws)
    n_w = len(w)
    sq, rq, d = x3.shape
    nb = rq // rb
    tm = sb * rb
    n_tok = sq * rq
    ns = ws[0].shape[1]
    blocks = (_nbytes((tm, d)) + 2 * _nbytes((sb, V7X_SUBLANES, d))
              + n_w * (_nbytes((d, tn), BF16) + _nbytes((d, ns), BF16))
              + _nbytes((tm, tn)) + _nbytes((tm, ns)))
    w_specs = [pl.BlockSpec((d, tn), lambda i, j: (0, j)) for _ in w]
    ws_specs = [pl.BlockSpec((d, ns), lambda i, j: (0, 0)) for _ in ws]
    return pl.pallas_call(
        functools.partial(_inproj_body, n_w), name="inproj",
        grid=(n_tok // tm, n_main // tn),
        in_specs=[pl.BlockSpec((sb, rb, d), lambda i, j: (i // nb, i % nb, 0)),
                  pl.BlockSpec((sb, 1, d), lambda i, j: (i // nb, 0, chunk_shift)),
                  pl.BlockSpec((sb, 1, d), lambda i, j: (i // nb, 0, chunk_scale)),
                  pl.BlockSpec((1, d), lambda i, j: (0, 0))] + w_specs + ws_specs,
        out_specs=[pl.BlockSpec((tm, tn), lambda i, j: (i, j)),
                   pl.BlockSpec((tm, ns), lambda i, j: (i, 0))],
        out_shape=[jax.ShapeDtypeStruct((n_tok, n_main), F32),
                   jax.ShapeDtypeStruct((n_tok, ns), F32)],
        scratch_shapes=[pltpu.VMEM((n_w, tm, d), BF16)],
        compiler_params=_cparams(2, blocks, n_w * _nbytes((tm, d), BF16),
                                 4 * _nbytes((min(tm, INPROJ_SUB), d))),
    )(x3, m3, m3, gain.reshape(1, d), *w, *ws)


def _outproj_body(n_in, n_w, norm, *refs):
    o_refs = refs[:n_in]
    w_refs = refs[n_in:n_in + n_in * n_w]
    pos = n_in + n_in * n_w
    if norm:
        gn_ref = refs[pos]
        pos += 1
    x_ref, gate_ref, out_ref, o_scr = refs[pos:pos + 4]

    @pl.when(pl.program_id(1) == 0)
    def _():
        for k in range(n_in):
            o = o_refs[k][...]
            if norm:
                o = _rms_rows(o, gn_ref[...])
            parts = _split2(o) if n_w == 2 else (o.astype(BF16),)
            for p in range(n_w):
                o_scr[k, p] = parts[p]

    acc = None
    for k in range(n_in):
        part = _dot_parts(tuple(o_scr[k, p] for p in range(n_w)),
                          tuple(w_refs[k * n_w + p][...] for p in range(n_w)))
        acc = part if acc is None else acc + part
    x = x_ref[...]
    out_ref[...] = x + gate_ref[...] * acc.reshape(x.shape)


def _outproj(o_list, w, x3, m3, chunk_gate, sb, rb, norm_gain=None, tn=512):
    w = _wparts(w)
    n_w = len(w)
    sq, rq, d = x3.shape
    nb = rq // rb
    tm = sb * rb
    n_in = len(o_list)
    nj = d // tn
    widths = [o.shape[1] for o in o_list]
    assert all(wd == widths[0] for wd in widths) and sum(widths) == w[0].shape[0]
    kw = widths[0]
    in_specs = [pl.BlockSpec((tm, kw), lambda i, j: (i, 0)) for _ in o_list]
    args = list(o_list)
    for k in range(n_in):
        for part in w:
            mode = pl.Buffered(1) if nj == 1 else None
            in_specs.append(pl.BlockSpec((kw, tn), lambda i, j, k=k: (k, j), pipeline_mode=mode))
            args.append(part)
    if norm_gain is not None:
        in_specs.append(pl.BlockSpec((1, kw), lambda i, j: (0, 0)))
        args.append(norm_gain.reshape(1, kw))
    in_specs += [pl.BlockSpec((sb, rb, tn), lambda i, j: (i // nb, i % nb, j)),
                 pl.BlockSpec((sb, 1, tn), lambda i, j: (i // nb, 0, chunk_gate * nj + j))]
    args += [x3, m3]
    blocks = (n_in * (_nbytes((tm, kw)) + n_w * _nbytes((kw, tn), BF16))
              + 2 * _nbytes((tm, tn)) + _nbytes((sb, V7X_SUBLANES, tn)))
    scr = n_in * n_w * _nbytes((tm, kw), BF16)
    return pl.pallas_call(
        functools.partial(_outproj_body, n_in, n_w, norm_gain is not None), name="outproj",
        grid=((sq * rq) // tm, nj),
        in_specs=in_specs,
        out_specs=pl.BlockSpec((sb, rb, tn), lambda i, j: (i // nb, i % nb, j)),
        out_shape=jax.ShapeDtypeStruct(x3.shape, F32),
        scratch_shapes=[pltpu.VMEM((n_in, n_w, tm, kw), BF16)],
        compiler_params=_cparams(2, blocks, scr, 3 * _nbytes((tm, kw))),
    )(*args)


def _ret_tables(lam, c, width):
    lam1 = lam[:, :1]
    rel = (_iota((c, c), 0) - _iota((c, c), 1)).astype(F32)
    din = jnp.where(rel >= 0, jnp.exp(lam1 * jnp.maximum(rel, 0.0)), 0.0)
    r = _iota((c, width), 0).astype(F32)
    dq = jnp.exp(lam * (r + 1.0))
    dk = jnp.exp(lam * (c - 1.0 - r))
    gc = jnp.exp(lam * float(c))
    return din, dq, dk, gc


def _gla_log_alpha(alow, w2, b2):
    xa = _dot3(alow, w2) + b2
    return _log_sigmoid(xa) * (1.0 / GLA_TEMP)


def _mixa_prompt_body(pa, alow, w2, b2, cos, sin, gnr, gng, lam,
                      o_r, o_g, sret_out, sgla_out,
                      s_ret, st_gla, din_s, dq_s, dk_s, a_s):
    c_id = pl.program_id(1)
    n_c = pl.num_programs(1)
    c = pa.shape[0]
    dk_r, dk_g, dv = A_RET_DK, A_GLA_DK, A_DV

    @pl.when(c_id == 0)
    def _():
        s_ret[...] = jnp.zeros_like(s_ret)
        st_gla[...] = jnp.zeros_like(st_gla)
        for hh in range(RET_HEADS):
            din, dq, dk, _ = _ret_tables(lam[hh], c, dk_r)
            din_s[hh] = din
            dq_s[hh] = dq
            dk_s[hh] = dk

    cs, sn = cos[...], sin[...]
    la = _gla_log_alpha(alow[...], w2[...], b2[...])
    bc_all = _dot01_left(_tril01(c), la)
    rows = _iota((c, dk_g), 0)
    nblk = c // GLA_BLOCK
    row16 = _iota((GLA_BLOCK, dk_g), 0)
    lane16 = _iota((GLA_BLOCK, c), 1)
    for hh in range(RET_HEADS):
        q = _rot(pa[:, A_QR + hh * dk_r:A_QR + (hh + 1) * dk_r], cs, sn)
        k = _rot(pa[:, A_KR + hh * dk_r:A_KR + (hh + 1) * dk_r], cs, sn) * (dk_r ** -0.5)
        qs, vs = _split2(q), _split2(pa[:, A_VR + hh * dv:A_VR + (hh + 1) * dv])
        sc = _dot3_nt(q, k) * din_s[hh]
        s_old = s_ret[hh]
        o = _dot3p(_split2(sc), vs) + _dot3p(qs, _split2(s_old)) * dq_s[hh]
        kd = k * dk_s[hh]
        s_ret[hh] = s_old * jnp.exp(lam[hh] * float(c)) + _dot3p(_split2(kd.T), vs)
        o_r[:, hh * dv:(hh + 1) * dv] = _head_norm(
            o, gnr[:, hh * dv:(hh + 1) * dv], pa[:, A_GR + hh * dv:A_GR + (hh + 1) * dv], True)

        bc = bc_all[:, hh * dk_g:(hh + 1) * dk_g]
        qg_v = pa[:, A_QG + hh * dk_g:A_QG + (hh + 1) * dk_g] * (dk_g ** -0.5)
        kg_v = pa[:, A_KG + hh * dk_g:A_KG + (hh + 1) * dk_g]
        vg_v = pa[:, A_VG + hh * dv:A_VG + (hh + 1) * dv]
        for blk in range(nblk):
            r0 = blk * GLA_BLOCK
            q_i = qg_v[r0:r0 + GLA_BLOCK]
            b_i = bc[r0:r0 + GLA_BLOCK]
            if blk == 0:
                a_i = jnp.zeros((GLA_BLOCK, c), F32)
            else:
                ref_b = bc[r0 - 1:r0]
                qt = q_i * jnp.exp(b_i - ref_b)
                kt = jnp.where(rows < r0, kg_v * jnp.exp(jnp.minimum(ref_b - bc, 0.0)), 0.0)
                a_i = _dot3_nt(qt, kt)
            for jj in range(GLA_BLOCK):
                j = r0 + jj
                e = jnp.exp(jnp.minimum(b_i - bc[j:j + 1], 0.0))
                col = jnp.sum(q_i * e * kg_v[j:j + 1], axis=-1, keepdims=True)
                col = jnp.where(row16[:, :1] >= jj, col, 0.0)
                a_i = a_i + jnp.where(lane16 == j, col, 0.0)
            a_s[hh, r0:r0 + GLA_BLOCK, :] = a_i
        st_old = st_gla[hh]
        og = _dot3(a_s[hh], vg_v) + _dot3_nt(qg_v * jnp.exp(bc), st_old)
        b_last = bc[c - 1:c]
        kdg = kg_v * jnp.exp(b_last - bc)
        st_gla[hh] = st_old * jnp.exp(b_last) + _dot3(vg_v.T, kdg)
        o_g[:, hh * dv:(hh + 1) * dv] = _head_norm(
            og, gng[:, hh * dv:(hh + 1) * dv], pa[:, A_GG + hh * dv:A_GG + (hh + 1) * dv], False)

    @pl.when(c_id == n_c - 1)
    def _():
        for hh in range(RET_HEADS):
            sret_out[0, hh] = s_ret[hh]
            sgla_out[0, hh] = st_gla[hh].T


def _mixa_prompt(pa, pa_low, w2p, b2, cos, sin, gnr, gng, lam, batch, seq):
    c = MIX_CHUNK
    nc = seq // c
    h = RET_HEADS
    dkr, dkg, dv = A_RET_DK, A_GLA_DK, A_DV
    width = pa.shape[1]
    const2 = lambda b, cc: (0, 0)
    in_specs = [pl.BlockSpec((c, width), lambda b, cc: (b * nc + cc, 0)),
                pl.BlockSpec((c, V7X_LANES), lambda b, cc: (b * nc + cc, 0)),
                pl.BlockSpec(w2p.shape, const2), pl.BlockSpec(b2.shape, const2),
                pl.BlockSpec((c, dkr // 2), lambda b, cc: (cc, 0)),
                pl.BlockSpec((c, dkr // 2), lambda b, cc: (cc, 0)),
                pl.BlockSpec(gnr.shape, const2), pl.BlockSpec(gng.shape, const2),
                pl.BlockSpec(lam.shape, lambda b, cc: (0, 0, 0))]
    out_specs = [pl.BlockSpec((c, h * dv), lambda b, cc: (b * nc + cc, 0)),
                 pl.BlockSpec((c, h * dv), lambda b, cc: (b * nc + cc, 0)),
                 pl.BlockSpec((1, h, dkr, dv), lambda b, cc: (b, 0, 0, 0)),
                 pl.BlockSpec((1, h, dkg, dv), lambda b, cc: (b, 0, 0, 0))]
    n_tok = batch * seq
    out_shape = [jax.ShapeDtypeStruct((n_tok, h * dv), F32),
                 jax.ShapeDtypeStruct((n_tok, h * dv), F32),
                 jax.ShapeDtypeStruct((batch, h, dkr, dv), F32),
                 jax.ShapeDtypeStruct((batch, h, dkg, dv), F32)]
    scratch = [pltpu.VMEM((h, dkr, dv), F32), pltpu.VMEM((h, dv, dkg), F32),
               pltpu.VMEM((h, c, c), F32), pltpu.VMEM((h, c, dkr), F32), pltpu.VMEM((h, c, dkr), F32),
               pltpu.VMEM((h, c, c), F32)]
    blocks = _nbytes((c, width)) + 2 * _nbytes((c, h * dv)) + h * (_nbytes((dkr, dv)) + _nbytes((dkg, dv)))
    scr = h * (_nbytes((dkr, dv)) + _nbytes((dv, dkg)) + 2 * _nbytes((c, c)) + 2 * _nbytes((c, dkr)))
    return pl.pallas_call(
        _mixa_prompt_body, name="mixa_prompt",
        grid=(batch, nc),
        in_specs=in_specs, out_specs=out_specs, out_shape=out_shape,
        scratch_shapes=scratch,
        compiler_params=_cparams(2, blocks, scr, 16 * 2 ** 20),
    )(pa, pa_low, w2p, b2, cos, sin, gnr, gng, lam)


def _direct_intra(q, k, v, decay_fn):
    t = q.shape[0]
    row = _iota((t, 1), 0)
    o = jnp.zeros((t, v.shape[1]), F32)
    for j in range(t):
        col = jnp.sum(q * decay_fn(j) * k[j:j + 1], axis=-1, keepdims=True)
        col = jnp.where(row >= j, col, 0.0)
        o = o + col * v[j:j + 1]
    return o


def _mixa_sample_body(sb, pa, alow, w2, b2, cos, sin, gnr, gng, lam, sret_in, sgla_in,
                      o_r, o_g, sret_out, sgla_out):
    t = 8
    dkr, dkg, dv = 256, 128, 256
    cs, sn = cos[...], sin[...]
    la = _gla_log_alpha(alow[...], w2[...], b2[...])
    bc_all = _dot01_left(_tril01(sb * t, block=t), la)
    rel = (_iota((t, 1), 0)).astype(F32)
    for hh in range(RET_HEADS):
        lam_row = lam[hh]
        dq = jnp.exp(lam_row * (rel + 1.0))
        dk = jnp.exp(lam_row * (t - 1.0 - rel))
        gc = jnp.exp(lam_row * float(t))
        for s in range(sb):
            r0 = s * t
            q = _rot(pa[r0:r0 + t, hh * dkr:(hh + 1) * dkr], cs, sn)
            k = _rot(pa[r0:r0 + t, 1024 + hh * dkr:1024 + (hh + 1) * dkr], cs, sn) * (dkr ** -0.5)
            v = pa[r0:r0 + t, 2048 + hh * dv:2048 + (hh + 1) * dv]
            g = pa[r0:r0 + t, 3072 + hh * dv:3072 + (hh + 1) * dv]
            s_old = sret_in[s, hh]
            o = _direct_intra(q, k, v, lambda j: jnp.exp(lam_row * jnp.maximum(rel - float(j), 0.0)))
            o = o + _dot3(q, s_old) * dq
            kdt = _pad_rows(k * dk, V7X_LANES).T
            sret_out[s, hh] = s_old * gc + _dot3(kdt, _pad_rows(v, V7X_LANES))
            o_r[r0:r0 + t, hh * dv:(hh + 1) * dv] = _head_norm(
                o, gnr[:, hh * dv:(hh + 1) * dv], g, True)
            qg = pa[r0:r0 + t, 4096 + hh * dkg:4096 + (hh + 1) * dkg] * (dkg ** -0.5)
            kg = pa[r0:r0 + t, 4608 + hh * dkg:4608 + (hh + 1) * dkg]
            vg = pa[r0:r0 + t, 5120 + hh * dv:5120 + (hh + 1) * dv]
            gg = pa[r0:r0 + t, 6144 + hh * dv:6144 + (hh + 1) * dv]
            bc = bc_all[r0:r0 + t, hh * dkg:(hh + 1) * dkg]
            sg_old = sgla_in[s, hh]
            og = _direct_intra(qg, kg, vg, lambda j: jnp.exp(jnp.minimum(bc - bc[j:j + 1], 0.0)))
            og = og + _dot3(qg * jnp.exp(bc), sg_old)
            b_last = bc[t - 1:t]
            m = _pad_rows(jnp.concatenate([kg * jnp.exp(b_last - bc), jnp.exp(b_last)], axis=0),
                          V7X_LANES).T
            sgla_out[s, hh] = sg_old * m[:, t:t + 1] + _dot3(m, _pad_rows(vg, V7X_LANES))
            o_g[r0:r0 + t, hh * dv:(hh + 1) * dv] = _head_norm(
                og, gng[:, hh * dv:(hh + 1) * dv], gg, False)


def _mixa_sample(pa, pa_low, w2p, b2, cos, sin, gnr, gng, lam, sret, sgla, sb=4):
    nseq = sret.shape[0]
    t = 8
    h, dkr, dkg, dv = RET_HEADS, 256, 128, 256
    width = pa.shape[1]
    const2 = lambda i: (0, 0)
    in_specs = [pl.BlockSpec((sb * t, width), lambda i: (i, 0)),
                pl.BlockSpec((sb * t, V7X_LANES), lambda i: (i, 0)),
                pl.BlockSpec(w2p.shape, const2), pl.BlockSpec(b2.shape, const2),
                pl.BlockSpec(cos.shape, const2), pl.BlockSpec(sin.shape, const2),
                pl.BlockSpec(gnr.shape, const2), pl.BlockSpec(gng.shape, const2),
                pl.BlockSpec(lam.shape, lambda i: (0, 0, 0)),
                pl.BlockSpec((sb, h, dkr, dv), lambda i: (i, 0, 0, 0)),
                pl.BlockSpec((sb, h, dkg, dv), lambda i: (i, 0, 0, 0))]
    out_specs = [pl.BlockSpec((sb * t, h * dv), lambda i: (i, 0)),
                 pl.BlockSpec((sb * t, h * dv), lambda i: (i, 0)),
                 pl.BlockSpec((sb, h, dkr, dv), lambda i: (i, 0, 0, 0)),
                 pl.BlockSpec((sb, h, dkg, dv), lambda i: (i, 0, 0, 0))]
    out_shape = [jax.ShapeDtypeStruct((nseq * t, h * dv), F32),
                 jax.ShapeDtypeStruct((nseq * t, h * dv), F32),
                 jax.ShapeDtypeStruct(sret.shape, F32),
                 jax.ShapeDtypeStruct(sgla.shape, F32)]
    blocks = (_nbytes((sb * t, width)) + 2 * _nbytes((sb, h, dkr, dv))
              + 2 * _nbytes((sb, h, dkg, dv)) + 2 * 2 ** 20)
    return pl.pallas_call(
        functools.partial(_mixa_sample_body, sb), name="mixa_sample",
        grid=(nseq // sb,),
        in_specs=in_specs, out_specs=out_specs, out_shape=out_shape,
        compiler_params=_cparams(1, blocks, 0, 4 * 2 ** 20),
    )(pa, pa_low, w2p, b2, cos, sin, gnr, gng, lam, sret, sgla)


def _conv_prompt_body(x_ref, prev_ref, w_ref, b_ref, o_ref):
    first = pl.program_id(1) == 0
    ext = jnp.concatenate([jnp.where(first, 0.0, prev_ref[...]), x_ref[...]], axis=0)
    hist = CONV_W - 1
    acc = b_ref[...] + pltpu.roll(ext, hist, 0)[V7X_SUBLANES:] * w_ref[0:1, :]
    for j in range(1, CONV_W):
        delayed = pltpu.roll(ext, hist - j, 0)[V7X_SUBLANES:] if j < hist else x_ref[...]
        acc = acc + delayed * w_ref[j:j + 1, :]
    o_ref[...] = _silu(acc)


def _conv_prompt(pc, conv_w, conv_b, batch, seq, col0, width, tc=512, tn=1024):
    nc = seq // tc
    cb0 = col0 // tn
    blocks = 2 * _nbytes((tc, tn)) + _nbytes((V7X_SUBLANES, tn)) * 3
    return pl.pallas_call(
        _conv_prompt_body, name="conv_prompt",
        grid=(batch, nc, width // tn),
        in_specs=[pl.BlockSpec((tc, tn), lambda b, c, j: (b * nc + c, cb0 + j)),
                  pl.BlockSpec((V7X_SUBLANES, tn),
                               lambda b, c, j: (jnp.maximum((b * nc + c) * (tc // V7X_SUBLANES) - 1, 0),
                                                cb0 + j)),
                  pl.BlockSpec((CONV_W, tn), lambda b, c, j: (0, j)),
                  pl.BlockSpec((1, tn), lambda b, c, j: (0, j))],
        out_specs=pl.BlockSpec((tc, tn), lambda b, c, j: (b * nc + c, j)),
        out_shape=jax.ShapeDtypeStruct((batch * seq, width), F32),
        compiler_params=_cparams(3, blocks, 0, 6 * _nbytes((tc, tn))),
    )(pc, pc, conv_w, conv_b.reshape(1, width))


def _conv_sample_body(x_ref, cache_ref, w_ref, b_ref, o_ref, scr):
    t = x_ref.shape[1]
    hist = CONV_W - 1
    scr[:, t - hist:t, :] = cache_ref[...]
    scr[:, t:2 * t, :] = x_ref[...]
    acc = b_ref[...] + scr[:, t - hist:2 * t - hist, :] * w_ref[0:1, :]
    for j in range(1, CONV_W):
        off = t - hist + j
        acc = acc + scr[:, off:off + t, :] * w_ref[j:j + 1, :]
    o_ref[...] = _silu(acc)


def _conv_sample(pc3, cache, conv_w, conv_b, col0, width, sbc=32, tn=512):
    nseq, t, _ = pc3.shape
    cb0 = col0 // tn
    blocks = 2 * _nbytes((sbc, t, tn)) + _nbytes((sbc, 8, tn))
    return pl.pallas_call(
        _conv_sample_body, name="conv_sample",
        grid=(nseq // sbc, width // tn),
        in_specs=[pl.BlockSpec((sbc, t, tn), lambda i, j: (i, 0, cb0 + j)),
                  pl.BlockSpec((sbc, CONV_W - 1, tn), lambda i, j: (i, 0, j)),
                  pl.BlockSpec((CONV_W, tn), lambda i, j: (0, j)),
                  pl.BlockSpec((1, tn), lambda i, j: (0, j))],
        out_specs=pl.BlockSpec((sbc, t, tn), lambda i, j: (i, 0, j)),
        out_shape=jax.ShapeDtypeStruct((nseq, t, width), F32),
        scratch_shapes=[pltpu.VMEM((sbc, 2 * t, tn), F32)],
        compiler_params=_cparams(2, blocks, _nbytes((sbc, 2 * t, tn)), 4 * _nbytes((sbc, t, tn))),
    )(pc3, cache, conv_w, conv_b.reshape(1, width))


def _expand01(heads, width, first=0):
    r = _iota((V7X_LANES, heads * width), 0)
    l = _iota((V7X_LANES, heads * width), 1)
    return jnp.where((l // width) == r - first, 1.0, 0.0).astype(BF16)


def _ssd_prompt_body(xs, bm, cm, z, dtc, dtr, bias_c, alog_c, bias_r, alog_r, dfull,
                     y_out, s_out, st):
    c_id = pl.program_id(1)
    n_c = pl.num_programs(1)
    l = xs.shape[0]
    groups, n, gw = st.shape
    heads = dtr.shape[1]
    hpg = heads // groups
    p = gw // hpg

    @pl.when(c_id == 0)
    def _():
        st[...] = jnp.zeros_like(st)

    dt_c = _softplus(dtc[...] + bias_c[...])
    cum_c = _dot01_left(_tril01(l), dt_c * (-jnp.exp(alog_c[...])))
    cum_parts, dt_parts = _split3(cum_c), _split3(dt_c)
    dt_r = _softplus(dtr[0] + bias_r[...])
    tri_u = jnp.where(_iota((l, l), 0) <= _iota((l, l), 1), 1.0, 0.0).astype(BF16)
    cum_r = _dot01_right(dt_r * (-jnp.exp(alog_r[...])), tri_u)

    causal = _iota((l, l), 0) >= _iota((l, l), 1)
    lane = _iota((l, 2 * p), 1)
    for g in range(groups):
        x = xs[:, g * gw:(g + 1) * gw]
        bmg = bm[:, g * n:(g + 1) * n]
        cmb = cm[:, g * n:(g + 1) * n].astype(BF16)
        cb = _dot_nt(cmb, bmg.astype(BF16))
        y_tiles = []
        for pair in range(hpg // 2):
            xt = x[:, pair * 2 * p:(pair + 1) * 2 * p]
            acc = None
            for sub in range(2):
                r = g * hpg + 2 * pair + sub
                seg = cum_c[:, r:r + 1] - cum_r[r:r + 1, :]
                w = jnp.where(causal, cb * jnp.exp(jnp.minimum(seg, 0.0)) * dt_r[r:r + 1, :], 0.0)
                xm = jnp.where((lane // p) == sub, xt, 0.0)
                part = _dot(w.astype(BF16), xm.astype(BF16))
                acc = part if acc is None else acc + part
            y_tiles.append(acc)
        y = jnp.concatenate(y_tiles, axis=-1)

        ex = _expand01(hpg, p, g * hpg)
        cum_f = _dot(cum_parts[0], ex) + _dot(cum_parts[1], ex) + _dot(cum_parts[2], ex)
        dt_f = _dot(dt_parts[0], ex) + _dot(dt_parts[1], ex) + _dot(dt_parts[2], ex)
        st_old = st[g]
        y = y + _dot(cmb, st_old.astype(BF16)) * jnp.exp(cum_f)
        cl_f = cum_f[l - 1:l]
        tail = jnp.exp(cl_f - cum_f) * dt_f
        st[g] = st_old * jnp.exp(cl_f) + _dot(bmg.T.astype(BF16), (x * tail).astype(BF16))
        y = y + dfull[:, g * gw:(g + 1) * gw] * x
        y_out[:, g * gw:(g + 1) * gw] = y * _silu(z[:, g * gw:(g + 1) * gw])

    @pl.when(c_id == n_c - 1)
    def _():
        for g in range(groups):
            s_out[0, g * hpg:(g + 1) * hpg] = st[g].T.reshape(hpg, p, n)


def _ssd_prompt(act, pc, dt_c, dt_t, bias_c, alog_c, bias_r, alog_r, dfull, batch, seq):
    l = MIX_CHUNK
    nc = seq // l
    g = SSM_GROUPS
    n = SSM_STATE
    p = SSM_HEADDIM
    heads = dt_t.shape[1]
    d_inner = heads * p
    gw = d_inner // g
    gn = g * n

    def row(b, cc):
        return b * nc + cc

    in_specs = [pl.BlockSpec((l, d_inner), lambda b, cc: (row(b, cc), 0)),
                pl.BlockSpec((l, gn), lambda b, cc: (row(b, cc), d_inner // gn)),
                pl.BlockSpec((l, gn), lambda b, cc: (row(b, cc), d_inner // gn + 1)),
                pl.BlockSpec((l, d_inner), lambda b, cc: (row(b, cc), 0)),
                pl.BlockSpec((l, V7X_LANES), lambda b, cc: (row(b, cc), 0)),
                pl.BlockSpec((1, heads, l), lambda b, cc: (b, 0, cc)),
                pl.BlockSpec((1, V7X_LANES), lambda b, cc: (0, 0)),
                pl.BlockSpec((1, V7X_LANES), lambda b, cc: (0, 0)),
                pl.BlockSpec((heads, l), lambda b, cc: (0, 0)),
                pl.BlockSpec((heads, l), lambda b, cc: (0, 0)),
                pl.BlockSpec((1, d_inner), lambda b, cc: (0, 0))]
    out_specs = [pl.BlockSpec((l, d_inner), lambda b, cc: (row(b, cc), 0)),
                 pl.BlockSpec((1, heads, p, n), lambda b, cc: (b, 0, 0, 0))]
    out_shape = [jax.ShapeDtypeStruct((batch * seq, d_inner), F32),
                 jax.ShapeDtypeStruct((batch, heads, p, n), F32)]
    blocks = 3 * _nbytes((l, d_inner)) + 2 * _nbytes((l, gn)) + _nbytes((heads, p, n))
    return pl.pallas_call(
        _ssd_prompt_body, name="ssd_prompt",
        grid=(batch, nc),
        in_specs=in_specs, out_specs=out_specs, out_shape=out_shape,
        scratch_shapes=[pltpu.VMEM((g, n, gw), F32)],
        compiler_params=_cparams(2, blocks, _nbytes((g, n, gw)), 16 * 2 ** 20),
    )(act, act, act, pc, dt_c, dt_t, bias_c, alog_c, bias_r, alog_r, dfull)


def _ssd_sample_body(sb, xs, bm, cm, z, dtc, bias_c, alog_c, dfull, s_in, y_out, s_out, y_s):
    t = 8
    rows = sb * t
    gw = xs.shape[1]
    hpg = s_in.shape[1]
    p = gw // hpg
    n = bm.shape[1]
    dt_c = _softplus(dtc[...] + bias_c[0])
    cum_c = _dot01_left(_tril01(rows, block=t), dt_c * (-jnp.exp(alog_c[0])))
    ex = _expand01(hpg, p)
    cum_f = _dot01_right(cum_c, ex)
    dt_f = _dot01_right(dt_c, ex)
    x = xs[...]
    x3 = x.reshape(sb, t, gw)
    c3 = cm[...].reshape(sb, t, n)
    b3 = bm[...].reshape(sb, t, n)
    cum3 = cum_f.reshape(sb, t, gw)
    dt3 = dt_f.reshape(sb, t, gw)
    rowi = _iota((sb, t, 1), 1)
    y3 = jnp.zeros((sb, t, gw), F32)
    for j in range(t):
        cbj = jnp.sum(c3 * b3[:, j:j + 1, :], axis=-1, keepdims=True)
        dec = jnp.exp(jnp.minimum(cum3 - cum3[:, j:j + 1, :], 0.0))
        w = jnp.where(rowi >= j, cbj * dec * dt3[:, j:j + 1, :], 0.0)
        y3 = y3 + w * x3[:, j:j + 1, :]
    y_s[...] = y3.reshape(rows, gw)
    cmv = cm[...]
    bmv = bm[...]
    for s in range(sb):
        r0 = s * t
        sg = s_in[s].reshape(hpg * p, n)
        cf = cum_f[r0:r0 + t]
        yi = _dot_nt(cmv[r0:r0 + t].astype(BF16), sg.astype(BF16)) * jnp.exp(cf)
        cl = cf[t - 1:t]
        xt = x[r0:r0 + t] * (jnp.exp(cl - cf) * dt_f[r0:r0 + t])
        m = _pad_rows(jnp.concatenate([xt, jnp.exp(cl)], axis=0), V7X_LANES).T
        s_new = sg * m[:, t:t + 1] + _dot(m.astype(BF16), _pad_rows(bmv[r0:r0 + t], V7X_LANES).astype(BF16))
        s_out[s] = s_new.reshape(hpg, p, n)
        y_s[r0:r0 + t, :] = y_s[r0:r0 + t, :] + yi
    y_out[...] = (y_s[...] + dfull[0] * x) * _silu(z[...])


def _ssd_sample(act, pc, pcs, bias_c, alog_c, dfull, s_ssm, sb=8):
    t = 8
    nseq = s_ssm.shape[0]
    g = SSM_GROUPS
    n = SSM_STATE
    p = SSM_HEADDIM
    d_inner = pc.shape[1] - 6144
    gw = d_inner // g
    hpg = gw // p
    bb0 = d_inner // n
    cb0 = (d_inner + g * n) // n
    rows = sb * t
    in_specs = [pl.BlockSpec((rows, gw), lambda i, gg: (i, gg)),
                pl.BlockSpec((rows, n), lambda i, gg: (i, bb0 + gg)),
                pl.BlockSpec((rows, n), lambda i, gg: (i, cb0 + gg)),
                pl.BlockSpec((rows, gw), lambda i, gg: (i, gg)),
                pl.BlockSpec((rows, V7X_LANES), lambda i, gg: (i, gg)),
                pl.BlockSpec((1, 1, V7X_LANES), lambda i, gg: (gg, 0, 0)),
                pl.BlockSpec((1, 1, V7X_LANES), lambda i, gg: (gg, 0, 0)),
                pl.BlockSpec((1, 1, gw), lambda i, gg: (gg, 0, 0)),
                pl.BlockSpec((sb, hpg, p, n), lambda i, gg: (i, gg, 0, 0))]
    out_specs = [pl.BlockSpec((rows, gw), lambda i, gg: (i, gg)),
                 pl.BlockSpec((sb, hpg, p, n), lambda i, gg: (i, gg, 0, 0))]
    out_shape = [jax.ShapeDtypeStruct((nseq * t, d_inner), F32),
                 jax.ShapeDtypeStruct(s_ssm.shape, F32)]
    blocks = 3 * _nbytes((rows, gw)) + 3 * _nbytes((rows, n)) + 2 * _nbytes((sb, hpg, p, n))
    return pl.pallas_call(
        functools.partial(_ssd_sample_body, sb), name="ssd_sample",
        grid=(nseq // sb, g),
        in_specs=in_specs, out_specs=out_specs, out_shape=out_shape,
        scratch_shapes=[pltpu.VMEM((rows, gw), F32)],
        compiler_params=_cparams(2, blocks, _nbytes((rows, gw)), 8 * 2 ** 20),
    )(act, act, act, pc, pcs, bias_c, alog_c, dfull, s_ssm)


def _moe_pre_body(n_pt, xp_ref, shp_ref, scp_ref, xs_ref, shs_ref, scs_ref, g_ref, wr_ref,
                  h_out, route_out):
    i = pl.program_id(0)

    @pl.when(i < n_pt)
    def _():
        _moe_route(xp_ref, shp_ref, scp_ref, g_ref, wr_ref, h_out, route_out)

    @pl.when(i >= n_pt)
    def _():
        _moe_route(xs_ref, shs_ref, scs_ref, g_ref, wr_ref, h_out, route_out)


def _moe_route(x_ref, sh_ref, sc_ref, g_ref, wr_ref, h_out, route_out):
    x = x_ref[...]
    h = _rms_rows(x, g_ref[...]) * (1.0 + sc_ref[...]) + sh_ref[...]
    h = h.reshape(h_out.shape[0], x.shape[-1])
    h_out[...] = h.reshape(h_out.shape)
    logits = _dot3(h, wr_ref[...])
    tm = logits.shape[0]
    lane = _iota((tm, V7X_LANES), 1).astype(F32)
    neg = -jnp.inf
    lg = jnp.where(lane < MOE_GROUPS, logits, neg)
    mg = jnp.max(lg, axis=-1, keepdims=True)
    top_g = jnp.min(jnp.where(lg == mg, lane, float(V7X_LANES)), axis=-1, keepdims=True)
    p_top = 1.0 / jnp.sum(jnp.where(lane < MOE_GROUPS, jnp.exp(logits - mg), 0.0),
                          axis=-1, keepdims=True)
    lo = MOE_GROUPS + MOE_PER_GROUP * top_g
    le = jnp.where((lane >= lo) & (lane < lo + MOE_PER_GROUP), logits, neg)
    v1 = jnp.max(le, axis=-1, keepdims=True)
    i1 = jnp.min(jnp.where(le == v1, lane, float(V7X_LANES)), axis=-1, keepdims=True)
    le2 = jnp.where(lane == i1, neg, le)
    v2 = jnp.max(le2, axis=-1, keepdims=True)
    i2 = jnp.min(jnp.where(le2 == v2, lane, float(V7X_LANES)), axis=-1, keepdims=True)
    tt = jnp.exp(v2 - v1)
    w1 = p_top / (1.0 + tt)
    w2 = p_top * tt / (1.0 + tt)
    route_out[...] = jnp.where(lane == 0, i1 - MOE_GROUPS,
                     jnp.where(lane == 1, i2 - MOE_GROUPS,
                     jnp.where(lane == 2, w1, jnp.where(lane == 3, w2, 0.0))))


def _moe_pre(xp, mp, xs_, ms, chunk_shift, chunk_scale, gain, wr, tm):
    bp, tp, d = xp.shape
    bs, ts, _ = xs_.shape
    nb = tp // tm
    n_pt = bp * nb
    sbs = tm // ts
    n_st = bs // sbs
    n_all = bp * tp + bs * ts

    def pi(i):
        return jnp.minimum(i, n_pt - 1)

    def si(i):
        return jnp.maximum(i - n_pt, 0)

    in_specs = [pl.BlockSpec((1, tm, d), lambda i: (pi(i) // nb, pi(i) % nb, 0)),
                pl.BlockSpec((1, 1, d), lambda i: (pi(i) // nb, 0, chunk_shift)),
                pl.BlockSpec((1, 1, d), lambda i: (pi(i) // nb, 0, chunk_scale)),
                pl.BlockSpec((sbs, ts, d), lambda i: (si(i), 0, 0)),
                pl.BlockSpec((sbs, 1, d), lambda i: (si(i), 0, chunk_shift)),
                pl.BlockSpec((sbs, 1, d), lambda i: (si(i), 0, chunk_scale)),
                pl.BlockSpec((1, d), lambda i: (0, 0)),
                pl.BlockSpec((d, V7X_LANES), lambda i: (0, 0))]
    blocks = (3 * _nbytes((tm, d)) + 2 * _nbytes((sbs, 8, d)) + _nbytes((d, V7X_LANES))
              + _nbytes((tm, 128)))
    return pl.pallas_call(
        functools.partial(_moe_pre_body, n_pt), name="moe_pre",
        grid=(n_pt + n_st,),
        in_specs=in_specs,
        out_specs=[pl.BlockSpec((tm, d // V7X_LANES, V7X_LANES), lambda i: (i, 0, 0)),
                   pl.BlockSpec((tm, V7X_LANES), lambda i: (i, 0))],
        out_shape=[jax.ShapeDtypeStruct((n_all, d // V7X_LANES, V7X_LANES), F32),
                   jax.ShapeDtypeStruct((n_all, V7X_LANES), F32)],
        compiler_params=_cparams(1, blocks, 0, 4 * _nbytes((tm, d))),
    )(xp, mp, mp, xs_, ms, ms, gain.reshape(1, d), wr)


def _rank_body(route_ref, rank_ref, cnt_ref, carry):
    @pl.when(pl.program_id(0) == 0)
    def _():
        carry[...] = jnp.zeros_like(carry)

    r = route_ref[...]
    tm = r.shape[0]
    lane = _iota((tm, V7X_LANES), 1).astype(F32)
    o1 = lane == r[:, 0:1]
    o2 = lane == r[:, 1:2]
    oh = jnp.where(o1, 1.0, 0.0) + jnp.where(o2, 1.0, 0.0)
    pex = _dot(_tril01(tm, strict=True), oh.astype(BF16)) + carry[0:1, :]
    r1 = jnp.sum(jnp.where(o1, pex, 0.0), axis=-1, keepdims=True)
    r2 = jnp.sum(jnp.where(o2, pex, 0.0), axis=-1, keepdims=True)
    rank_ref[...] = jnp.where(lane == 0, r1, jnp.where(lane == 1, r2, 0.0))
    carry[...] = carry[...] + jnp.sum(oh, axis=0, keepdims=True)
    cnt_ref[...] = carry[...]


def _rank(route, tm=256):
    n = route.shape[0]
    return pl.pallas_call(
        _rank_body, name="moe_rank",
        grid=(n // tm,),
        in_specs=[pl.BlockSpec((tm, V7X_LANES), lambda i: (i, 0))],
        out_specs=[pl.BlockSpec((tm, V7X_LANES), lambda i: (i, 0)),
                   pl.BlockSpec((V7X_SUBLANES, V7X_LANES), lambda i: (0, 0))],
        out_shape=[jax.ShapeDtypeStruct((n, V7X_LANES), F32),
                   jax.ShapeDtypeStruct((V7X_SUBLANES, V7X_LANES), F32)],
        scratch_shapes=[pltpu.VMEM((V7X_SUBLANES, V7X_LANES), F32)],
        compiler_params=_cparams(1, 2 * _nbytes((tm, 128))),
    )(route)


def _gather_tokens(src_hbm, dst, sem, n, row_of):
    def issue(r, carry):
        pltpu.make_async_copy(src_hbm.at[row_of(r)], dst.at[r], sem).start()
        return carry

    lax.fori_loop(0, n, issue, 0, unroll=DMA_UNROLL)


def _gather_wait(src_hbm, dst, sem):
    pltpu.make_async_copy(src_hbm.at[pl.ds(0, dst.shape[0])], dst, sem).wait()


def _expert_body(layer, inv_ref, te_ref, first_ref, wslot_ref, nxt_ref, nt_ref,
                 h3_hbm, wg_hbm, wu_hbm, wd_hbm, o_ref,
                 xbuf, wg_f, wu_f, wd_f, wg_b, wu_b, wd_b, xsem, wsem):
    i = pl.program_id(0)
    nt = nt_ref[0]
    tm = xbuf.shape[1]
    active = i < nt

    def w_copies(e, slot):
        return [pltpu.make_async_copy(src.at[layer, e], dst.at[slot], wsem.at[k])
                for k, (src, dst) in enumerate(((wg_hbm, wg_f), (wu_hbm, wu_f), (wd_hbm, wd_f)))]

    def fetch_rows(tile, slot):
        _gather_tokens(h3_hbm, xbuf.at[slot], xsem.at[slot], tm, lambda r: inv_ref[tile * tm + r])

    @pl.when(i == 0)
    def _():
        for cp in w_copies(te_ref[0], wslot_ref[0]):
            cp.start(priority=WEIGHT_DMA_PRIORITY)
        fetch_rows(0, 0)

        @pl.when(nt > 1)
        def _():
            fetch_rows(1, 1)

    @pl.when(active)
    def _():
        slot = i % ROW_SLOTS
        ws = wslot_ref[i]

        @pl.when(first_ref[i] == 1)
        def _():
            for cp in w_copies(te_ref[i], ws):
                cp.wait()
            wg_b[...] = wg_f[ws].astype(BF16)
            wu_b[...] = wu_f[ws].astype(BF16)
            wd_b[...] = wd_f[ws].astype(BF16)

            @pl.when(nxt_ref[i] >= 0)
            def _():
                for cp in w_copies(nxt_ref[i], 1 - ws):
                    cp.start(priority=WEIGHT_DMA_PRIORITY)

        @pl.when(i + 2 < nt)
        def _():
            fetch_rows(i + 2, (i + 2) % ROW_SLOTS)

        _gather_wait(h3_hbm, xbuf.at[slot], xsem.at[slot])
        xb = xbuf[slot].reshape(tm, wg_b.shape[0]).astype(BF16)
        hid = _silu(_dot(xb, wg_b[...])) * _dot(xb, wu_b[...])
        o_ref[...] = _dot(hid.astype(BF16), wd_b[...]).reshape(o_ref.shape)

    @pl.when(jnp.logical_not(active))
    def _():
        o_ref[...] = jnp.zeros_like(o_ref)


def _experts(inv, tile_expert, first, wslot, nxt, ntiles, h3, wg, wu, wd, layer):
    _, dc, ln = h3.shape
    d = dc * ln
    ff = wg.shape[3]
    tm = MOE_TILE
    p_pad = inv.shape[0]
    scratch = [pltpu.VMEM((ROW_SLOTS, tm, dc, ln), F32),
               pltpu.VMEM((2, d, ff), F32), pltpu.VMEM((2, d, ff), F32), pltpu.VMEM((2, ff, d), F32),
               pltpu.VMEM((d, ff), BF16), pltpu.VMEM((d, ff), BF16), pltpu.VMEM((ff, d), BF16),
               pltpu.SemaphoreType.DMA((ROW_SLOTS,)), pltpu.SemaphoreType.DMA((3,))]
    scr_bytes = ROW_SLOTS * _nbytes((tm, d)) + 6 * _nbytes((d, ff)) + 3 * _nbytes((d, ff), BF16)
    any_spec = pl.BlockSpec(memory_space=pl.ANY)
    return pl.pallas_call(
        functools.partial(_expert_body, layer), name="moe_experts",
        grid_spec=pltpu.PrefetchScalarGridSpec(
            num_scalar_prefetch=6,
            grid=(p_pad // tm,),
            in_specs=[any_spec, any_spec, any_spec, any_spec],
            out_specs=pl.BlockSpec((tm, dc, ln), lambda i, *_: (i, 0, 0)),
            scratch_shapes=scratch),
        out_shape=jax.ShapeDtypeStruct((p_pad, dc, ln), F32),
        compiler_params=_cparams(1, _nbytes((tm, d)), scr_bytes, 3 * _nbytes((tm, d)),
                                 disable_bounds_checks=True),
    )(inv, tile_expert, first, wslot, nxt, ntiles, h3, wg, wu, wd)


def _combine_body(tok_off, final, pos_ref, eo_hbm, route_ref, x_ref, gate_ref, fn_ref, out_ref,
                  buf, sem):
    i = pl.program_id(0)
    n = pl.num_programs(0)
    tc = route_ref.shape[0]
    d = x_ref.shape[-1]

    def fetch(step, slot):
        base = 2 * (tok_off + step * tc)
        for k in range(MOE_TOPK):
            _gather_tokens(eo_hbm, buf.at[slot, k], sem.at[slot], tc,
                           lambda r, k=k: pos_ref[base + MOE_TOPK * r + k])

    @pl.when(i == 0)
    def _():
        fetch(0, 0)

    slot = i % 2

    @pl.when(i + 1 < n)
    def _():
        fetch(i + 1, 1 - slot)

    for k in range(MOE_TOPK):
        _gather_wait(eo_hbm, buf.at[slot, k], sem.at[slot])
    rt = route_ref[...]
    y = rt[:, 2:3] * buf[slot, 0].reshape(tc, d) + rt[:, 3:4] * buf[slot, 1].reshape(tc, d)
    x = x_ref[...]
    xn = x + gate_ref[...] * y.reshape(x.shape)
    if final:
        xn = _rms_rows(xn, fn_ref[...])
    out_ref[...] = xn


def _combine(pos_flat, eo, route, x3, m3, chunk_gate, sb, rb, tok_off, final_gain=None):
    sq, rq, d = x3.shape
    nb = rq // rb
    tc = sb * rb
    blk_off = tok_off // tc
    final = final_gain is not None
    fn = (final_gain if final else jnp.ones((d,), F32)).reshape(1, d)
    blocks = 2 * _nbytes((tc, d)) + _nbytes((tc, 128)) + _nbytes((sb, 8, d))
    return pl.pallas_call(
        functools.partial(_combine_body, tok_off, final), name="moe_combine",
        grid_spec=pltpu.PrefetchScalarGridSpec(
            num_scalar_prefetch=1,
            grid=((sq * rq) // tc,),
            in_specs=[pl.BlockSpec(memory_space=pl.ANY),
                      pl.BlockSpec((tc, V7X_LANES), lambda i, p: (blk_off + i, 0)),
                      pl.BlockSpec((sb, rb, d), lambda i, p: (i // nb, i % nb, 0)),
                      pl.BlockSpec((sb, 1, d), lambda i, p: (i // nb, 0, chunk_gate)),
                      pl.BlockSpec((1, d), lambda i, p: (0, 0))],
            out_specs=pl.BlockSpec((sb, rb, d), lambda i, p: (i // nb, i % nb, 0)),
            scratch_shapes=[pltpu.VMEM((2, MOE_TOPK, tc, d // V7X_LANES, V7X_LANES), F32),
                            pltpu.SemaphoreType.DMA((2,))]),
        out_shape=jax.ShapeDtypeStruct(x3.shape, F32),
        compiler_params=_cparams(1, blocks, 2 * MOE_TOPK * _nbytes((tc, d)), 4 * _nbytes((tc, d)),
                                 disable_bounds_checks=True),
    )(pos_flat, eo, route, x3, m3, fn)


def _moe_layer(xp, xs_, mp, ms, l, norm_moe, w_group, w_expert, w_gate, w_up, w_down, final_gain):
    bp, tp, d = xp.shape
    bs, ts, _ = xs_.shape
    n_p, n_s = bp * tp, bs * ts
    n_all = n_p + n_s
    wr = jnp.concatenate([w_group[l], jnp.transpose(w_expert[l], (1, 0, 2)).reshape(d, MOE_EXPERTS)],
                         axis=1)
    wr = jnp.pad(wr, ((0, 0), (0, V7X_LANES - wr.shape[1])))
    tm = MOE_TILE
    h_all, route = _moe_pre(xp, mp, xs_, ms, 3, 4, norm_moe[l], wr, tm)
    rank, cnt = _rank(route)
    counts = cnt[0, :MOE_EXPERTS].astype(I32)
    tiles_per = (counts + tm - 1) // tm
    tile_end = jnp.cumsum(tiles_per)
    row_start = (tile_end - tiles_per) * tm
    ntiles = tile_end[-1]
    e_idx = route[:, 0:MOE_TOPK].astype(I32)
    pos = row_start[e_idx] + rank[:, 0:MOE_TOPK].astype(I32)
    nt_max = (2 * n_all) // tm + MOE_EXPERTS
    p_pad = nt_max * tm
    tile_ids = jnp.arange(nt_max, dtype=I32)
    te = jnp.minimum(jnp.sum((tile_ids[:, None] >= tile_end[None, :]).astype(I32), axis=1),
                     MOE_EXPERTS - 1)
    active = tile_ids < ntiles
    te = jnp.where(active, te, te[jnp.maximum(ntiles - 1, 0)])
    first = (active & ((tile_ids == 0) | (te != jnp.roll(te, 1)))).astype(I32)
    wslot = (jnp.cumsum(first) - 1) % 2
    grp_end = tile_end[te]
    nxt = jnp.where(grp_end < ntiles, te[jnp.minimum(grp_end, nt_max - 1)], -1).astype(I32)
    pos_flat = pos.reshape(-1)
    inv = (jnp.arange(p_pad, dtype=I32) % n_all).at[pos_flat].set(jnp.arange(2 * n_all, dtype=I32) // 2)
    nt_arr = ntiles.reshape(1).astype(I32)
    eo = _experts(inv, te, first, wslot.astype(I32), nxt, nt_arr, h_all, w_gate, w_up, w_down, l)
    xp_new = _combine(pos_flat, eo, route, xp, mp, 5, 1, tm, 0, final_gain)
    xs_new = _combine(pos_flat, eo, route, xs_, ms, 5, tm // ts, ts, n_p, final_gain)
    return xp_new, xs_new


def kernel(x_prompt, x_sample, c_prompt, c_sample, state_ret, state_gla, state_ssm, cache_conv,
           w_ada, b_ada, norm_mix, norm_moe,
           a_w_in, a_ret_gn, a_gla_w2, a_gla_b2, a_gla_gn, a_w_out,
           c_w_in, c_conv_w, c_conv_b, c_dt_bias, c_a_log, c_d, c_norm, c_w_out,
           moe_w_group, moe_w_expert, moe_w_gate, moe_w_up, moe_w_down, final_norm):
    bp, tp, d = x_prompt.shape
    bs, ts, _ = x_sample.shape
    n_p, n_s = bp * tp, bs * ts

    pad = (-bp) % V7X_SUBLANES
    c_all = jnp.concatenate([c_prompt, jnp.zeros((pad, d), F32), c_sample], axis=0)
    m_all = _ada(c_all, w_ada, b_ada)
    mp = [m_all[l, :bp].reshape(bp, 1, 6 * d) for l in range(2)]
    ms = [m_all[l, bp + pad:].reshape(bs, 1, 6 * d) for l in range(2)]

    tm_p, tm_s = 512, 512

    wa = a_w_in[0]
    n_main_a = (wa.shape[1] // 512) * 512
    wa_low = jnp.pad(wa[:, n_main_a:], ((0, 0), (0, V7X_LANES - (wa.shape[1] - n_main_a))))
    w2p = jnp.pad(a_gla_w2[0], ((0, V7X_LANES - GLA_RANK), (0, 0)))
    b2 = a_gla_b2[0].reshape(1, -1)
    gnr = a_ret_gn[0].reshape(1, -1)
    gng = a_gla_gn[0].reshape(1, -1)
    half = 128
    inv_freq = ROPE_BASE ** (-jnp.arange(half, dtype=F32) / half)
    ang_p = jnp.arange(tp, dtype=F32)[:, None] * inv_freq[None, :]
    ang_s = (PAST_LEN + jnp.arange(ts, dtype=F32))[:, None] * inv_freq[None, :]
    log_gamma = jnp.log(1.0 - 2.0 ** (-5.0 - jnp.arange(RET_HEADS, dtype=F32)))
    lam = jnp.broadcast_to(log_gamma[:, None, None], (RET_HEADS, 1, 256))

    wa_parts, wa_low_parts, wo_parts = _split_bf16(wa), _split_bf16(wa_low), _split_bf16(a_w_out[0])
    pa_p, low_p = _inproj(x_prompt, mp[0], 0, 1, norm_mix[0], wa_parts, n_main_a, wa_low_parts, 1, tm_p)
    pa_s, low_s = _inproj(x_sample, ms[0], 0, 1, norm_mix[0], wa_parts, n_main_a, wa_low_parts,
                          tm_s // ts, ts)
    or_p, og_p, ret_p, gla_p = _mixa_prompt(pa_p, low_p, w2p, b2, jnp.cos(ang_p), jnp.sin(ang_p),
                                            gnr, gng, lam, bp, tp)
    or_s, og_s, ret_s, gla_s = _mixa_sample(pa_s, low_s, w2p, b2, jnp.cos(ang_s), jnp.sin(ang_s),
                                            gnr, gng, lam, state_ret[0], state_gla[0])
    xp = _outproj([or_p, og_p], wo_parts, x_prompt, mp[0], 2, 1, 256, tn=d)
    xs_ = _outproj([or_s, og_s], wo_parts, x_sample, ms[0], 2, 256 // ts, ts, tn=d)
    xp, xs_ = _moe_layer(xp, xs_, mp[0], ms[0], 0, norm_moe, moe_w_group, moe_w_expert,
                         moe_w_gate, moe_w_up, moe_w_down, None)

    wc = c_w_in[0]
    n_main_c = (wc.shape[1] // 512) * 512
    heads = wc.shape[1] - n_main_c
    hpg = heads // SSM_GROUPS
    d_inner = heads * SSM_HEADDIM
    conv_dim = n_main_c - d_inner
    wdt = jnp.pad(wc[:, n_main_c:].reshape(d, SSM_GROUPS, hpg),
                  ((0, 0), (0, 0), (0, V7X_LANES - hpg))).reshape(d, SSM_GROUPS * V7X_LANES).astype(BF16)
    wc_b = wc.astype(BF16)
    wdt_all = jnp.pad(wc[:, n_main_c:], ((0, 0), (0, V7X_LANES - heads))).astype(BF16)
    pc_p, dt_p = _inproj(x3=xp, m3=mp[1], chunk_shift=0, chunk_scale=1, gain=norm_mix[1], w=wc_b,
                         n_main=n_main_c, ws=wdt_all, sb=1, rb=2 * tm_p)
    pc_s, dt_s = _inproj(x3=xs_, m3=ms[1], chunk_shift=0, chunk_scale=1, gain=norm_mix[1], w=wc_b,
                         n_main=n_main_c, ws=wdt, sb=tm_s // ts, rb=ts)
    act_p = _conv_prompt(pc_p, c_conv_w[0], c_conv_b[0], bp, tp, d_inner, conv_dim)
    act_s = _conv_sample(pc_s.reshape(bs, ts, n_main_c), cache_conv[0], c_conv_w[0], c_conv_b[0],
                         d_inner, conv_dim).reshape(n_s, conv_dim)
    conv_p = pc_p.reshape(bp, tp, n_main_c)[:, tp - (CONV_W - 1):, d_inner:]
    conv_s = pc_s.reshape(bs, ts, n_main_c)[:, ts - (CONV_W - 1):, d_inner:]

    def lanes_pad(v):
        return jnp.pad(v.reshape(SSM_GROUPS, 1, hpg), ((0, 0), (0, 0), (0, V7X_LANES - hpg)))

    bias_c, alog_c = lanes_pad(c_dt_bias[0]), lanes_pad(c_a_log[0])
    dfull = jnp.repeat(c_d[0], SSM_HEADDIM).reshape(SSM_GROUPS, 1, hpg * SSM_HEADDIM)

    def lanes_all(v):
        return jnp.pad(v.reshape(1, heads), ((0, 0), (0, V7X_LANES - heads)))

    bias_r = jnp.broadcast_to(c_dt_bias[0].reshape(heads, 1), (heads, MIX_CHUNK))
    alog_r = jnp.broadcast_to(c_a_log[0].reshape(heads, 1), (heads, MIX_CHUNK))
    dt_t = jnp.transpose(dt_p[:, :heads].reshape(bp, tp, heads), (0, 2, 1))
    yg_p, ssm_p = _ssd_prompt(act_p, pc_p, dt_p, dt_t, lanes_all(c_dt_bias[0]), lanes_all(c_a_log[0]),
                              bias_r, alog_r, dfull.reshape(1, d_inner), bp, tp)
    yg_s, ssm_s = _ssd_sample(act_s, pc_s, dt_s, bias_c, alog_c, dfull, state_ssm[0])
    wco_b = c_w_out[0].astype(BF16)
    xp = _outproj([yg_p], wco_b, xp, mp[1], 2, 1, 256, norm_gain=c_norm[0], tn=d)
    xs_ = _outproj([yg_s], wco_b, xs_, ms[1], 2, 256 // ts, ts, norm_gain=c_norm[0], tn=d)
    yp, ys = _moe_layer(xp, xs_, mp[1], ms[1], 1, norm_moe, moe_w_group, moe_w_expert,
                        moe_w_gate, moe_w_up, moe_w_down, final_norm)

    return (yp, ys, ret_p[None], ret_s[None], gla_p[None], gla_s[None],
            ssm_p[None], ssm_s[None], conv_p[None], conv_s[None])
```

```python
import functools
import math

import jax
import jax.numpy as jnp
from jax import lax
from jax.experimental import pallas as pl
from jax.experimental.pallas import tpu as pltpu

F32 = jnp.float32
BF16 = jnp.bfloat16
I32 = jnp.int32

EPS = 1e-6
PAST_LEN = 16384
ROPE_BASE = 10000.0
GLA_TEMP = 16.0
GLA_RANK = 16
RET_HEADS = 4
GLA_HEADS = 4
SSM_GROUPS = 8
SSM_HEADDIM = 64
SSM_STATE = 128
CONV_W = 4
MOE_GROUPS = 4
MOE_PER_GROUP = 8
MOE_EXPERTS = MOE_GROUPS * MOE_PER_GROUP
MOE_TOPK = 2

V7X_LANES = 128
V7X_SUBLANES = 8
V7X_VMEM_BYTES = 64 * 2 ** 20
V7X_VMEM_RESERVE = 6 * 2 ** 20

A_RET_DK, A_GLA_DK, A_DV = 256, 128, 256
A_QR = 0
A_KR = A_QR + RET_HEADS * A_RET_DK
A_VR = A_KR + RET_HEADS * A_RET_DK
A_GR = A_VR + RET_HEADS * A_DV
A_QG = A_GR + RET_HEADS * A_DV
A_KG = A_QG + GLA_HEADS * A_GLA_DK
A_VG = A_KG + GLA_HEADS * A_GLA_DK
A_GG = A_VG + GLA_HEADS * A_DV

MIX_CHUNK = 128
GLA_BLOCK = 16
MOE_TILE = 256
DMA_UNROLL = 8
ROW_SLOTS = 3
WEIGHT_DMA_PRIORITY = 1


def _cparams(n_grid, block_bytes, scratch_bytes=0, temp_bytes=0, **kwargs):
    need = 2 * block_bytes + scratch_bytes + temp_bytes + 8 * 2 ** 20
    limit = int(min(max(need, 24 * 2 ** 20), V7X_VMEM_BYTES - V7X_VMEM_RESERVE))
    return pltpu.CompilerParams(dimension_semantics=("arbitrary",) * n_grid,
                                vmem_limit_bytes=limit, **kwargs)


def _nbytes(shape, dtype=F32):
    return math.prod(shape) * jnp.dtype(dtype).itemsize


def _silu(x):
    return x * (1.0 / (1.0 + jnp.exp(-x)))


def _softplus(x):
    return jnp.maximum(x, 0.0) + jnp.log1p(jnp.exp(-jnp.abs(x)))


def _log_sigmoid(x):
    return -_softplus(-x)


def _dot(a, b):
    return jnp.dot(a, b, preferred_element_type=F32)


def _dot_nt(a, b):
    return lax.dot_general(a, b, (((1,), (1,)), ((), ())), preferred_element_type=F32)


def _split2(x):
    hi = x.astype(BF16)
    return hi, (x - hi.astype(F32)).astype(BF16)


def _dot3p(a, b):
    return _dot(a[0], b[0]) + _dot(a[0], b[1]) + _dot(a[1], b[0])


def _dot3(a, b):
    return _dot3p(_split2(a), _split2(b))


def _dot3_nt(a, b):
    ah, al = _split2(a)
    bh, bl = _split2(b)
    return _dot_nt(ah, bh) + _dot_nt(ah, bl) + _dot_nt(al, bh)


def _split3(x):
    a = x.astype(BF16)
    r = x - a.astype(F32)
    b = r.astype(BF16)
    r = r - b.astype(F32)
    return a, b, r.astype(BF16)


def _dot01_left(m01, x):
    a, b, c = _split3(x)
    return _dot(m01, a) + _dot(m01, b) + _dot(m01, c)


def _dot01_right(x, m01):
    a, b, c = _split3(x)
    return _dot(a, m01) + _dot(b, m01) + _dot(c, m01)


def _iota(shape, dim, dtype=I32):
    return lax.broadcasted_iota(dtype, shape, dim)


def _tril01(n, block=None, strict=False):
    i = _iota((n, n), 0)
    j = _iota((n, n), 1)
    m = (i > j) if strict else (i >= j)
    if block is not None:
        m = m & ((i // block) == (j // block))
    return jnp.where(m, 1.0, 0.0).astype(BF16)


def _rms_rows(x, g):
    return x * lax.rsqrt(jnp.mean(x * x, axis=-1, keepdims=True) + EPS) * g


def _rot(x, cos, sin):
    half = x.shape[-1] // 2
    x1, x2 = x[:, :half], x[:, half:]
    return jnp.concatenate([x1 * cos - x2 * sin, x1 * sin + x2 * cos], axis=-1)


def _head_norm(o, gain, gate, center):
    if center:
        o = o - jnp.mean(o, axis=-1, keepdims=True)
    o = o * lax.rsqrt(jnp.mean(o * o, axis=-1, keepdims=True) + EPS)
    return o * gain * _silu(gate)


def _pad_rows(x, rows):
    return jnp.concatenate([x, jnp.zeros((rows - x.shape[0], x.shape[1]), x.dtype)], axis=0)


def _ada_body(c_ref, w_ref, b_ref, o_ref):
    o_ref[0] = _dot3(_silu(c_ref[...]), w_ref[0]) + b_ref[0]


def _ada(c_all, w_ada, b_ada):
    depth, d, n6 = w_ada.shape
    m = c_all.shape[0]
    tn = 1024
    blocks = _nbytes((m, d)) + _nbytes((d, tn)) + _nbytes((m, tn))
    return pl.pallas_call(
        _ada_body, name="ada",
        grid=(depth, n6 // tn),
        in_specs=[pl.BlockSpec((m, d), lambda l, j: (0, 0)),
                  pl.BlockSpec((1, d, tn), lambda l, j: (l, 0, j)),
                  pl.BlockSpec((1, 1, tn), lambda l, j: (l, 0, j))],
        out_specs=pl.BlockSpec((1, m, tn), lambda l, j: (l, 0, j)),
        out_shape=jax.ShapeDtypeStruct((depth, m, n6), F32),
        compiler_params=_cparams(2, blocks, temp_bytes=3 * _nbytes((d, tn))),
    )(c_all, w_ada, b_ada.reshape(depth, 1, n6))


def _wparts(x):
    return x if isinstance(x, tuple) else (x,)


def _split_bf16(w):
    hi = lax.bitcast_convert_type(
        lax.bitcast_convert_type(w, jnp.uint32) & jnp.uint32(0xFFFF0000), F32)
    return hi.astype(BF16), (w - hi).astype(BF16)


def _dot_parts(a, b):
    if len(b) == 1:
        return _dot(a[0], b[0])
    return _dot3p(a, b)


INPROJ_SUB = 512


def _inproj_body(n_w, x_ref, sh_ref, sc_ref, g_ref, *refs):
    w_refs, ws_refs = refs[:n_w], refs[n_w:2 * n_w]
    o_ref, os_ref, h_scr = refs[2 * n_w:]
    tm = h_scr.shape[1]
    sb, rb, _ = x_ref.shape

    @pl.when(pl.program_id(1) == 0)
    def _():
        for r0 in range(0, tm, INPROJ_SUB):
            rows = min(INPROJ_SUB, tm)
            if sb == 1:
                x = x_ref[:, r0:r0 + rows, :]
                sc, sh = sc_ref[...], sh_ref[...]
            else:
                s0, s1 = r0 // rb, (r0 + rows) // rb
                x = x_ref[s0:s1]
                sc, sh = sc_ref[s0:s1], sh_ref[s0:s1]
            h = _rms_rows(x, g_ref[...]) * (1.0 + sc) + sh
            h = h.reshape(rows, h.shape[-1])
            hs = _split2(h) if n_w == 2 else (h.astype(BF16),)
            for k in range(n_w):
                h_scr[k, r0:r0 + rows, :] = hs[k]
            os_ref[r0:r0 + rows, :] = _dot_parts(hs, tuple(r[...] for r in ws_refs))

    o_ref[...] = _dot_parts(tuple(h_scr[k] for k in range(n_w)), tuple(r[...] for r in w_refs))


def _inproj(x3, m3, chunk_shift, chunk_scale, gain, w, n_main, ws, sb, rb, tn=512):
    w, ws = _wparts(w), _wparts(ws)
    n_w = len(w)
    sq, rq, d = x3.shape
    nb = rq // rb
    tm = sb * rb
    n_tok = sq * rq
    ns = ws[0].shape[1]
    blocks = (_nbytes((tm, d)) + 2 * _nbytes((sb, V7X_SUBLANES, d))
              + n_w * (_nbytes((d, tn), BF16) + _nbytes((d, ns), BF16))
              + _nbytes((tm, tn)) + _nbytes((tm, ns)))
    w_specs = [pl.BlockSpec((d, tn), lambda i, j: (0, j)) for _ in w]
    ws_specs = [pl.BlockSpec((d, ns), lambda i, j: (0, 0)) for _ in ws]
    return pl.pallas_call(
        functools.partial(_inproj_body, n_w), name="inproj",
        grid=(n_tok // tm, n_main // tn),
        in_specs=[pl.BlockSpec((sb, rb, d), lambda i, j: (i // nb, i % nb, 0)),
                  pl.BlockSpec((sb, 1, d), lambda i, j: (i // nb, 0, chunk_shift)),
                  pl.BlockSpec((sb, 1, d), lambda i, j: (i // nb, 0, chunk_scale)),
                  pl.BlockSpec((1, d), lambda i, j: (0, 0))] + w_specs + ws_specs,
        out_specs=[pl.BlockSpec((tm, tn), lambda i, j: (i, j)),
                   pl.BlockSpec((tm, ns), lambda i, j: (i, 0))],
        out_shape=[jax.ShapeDtypeStruct((n_tok, n_main), F32),
                   jax.ShapeDtypeStruct((n_tok, ns), F32)],
        scratch_shapes=[pltpu.VMEM((n_w, tm, d), BF16)],
        compiler_params=_cparams(2, blocks, n_w * _nbytes((tm, d), BF16),
                                 4 * _nbytes((min(tm, INPROJ_SUB), d))),
    )(x3, m3, m3, gain.reshape(1, d), *w, *ws)


def _outproj_body(n_in, n_w, norm, *refs):
    o_refs = refs[:n_in]
    w_refs = refs[n_in:n_in + n_in * n_w]
    pos = n_in + n_in * n_w
    if norm:
        gn_ref = refs[pos]
        pos += 1
    x_ref, gate_ref, out_ref, o_scr = refs[pos:pos + 4]

    @pl.when(pl.program_id(1) == 0)
    def _():
        for k in range(n_in):
            o = o_refs[k][...]
            if norm:
                o = _rms_rows(o, gn_ref[...])
            parts = _split2(o) if n_w == 2 else (o.astype(BF16),)
            for p in range(n_w):
                o_scr[k, p] = parts[p]

    acc = None
    for k in range(n_in):
        part = _dot_parts(tuple(o_scr[k, p] for p in range(n_w)),
                          tuple(w_refs[k * n_w + p][...] for p in range(n_w)))
        acc = part if acc is None else acc + part
    x = x_ref[...]
    out_ref[...] = x + gate_ref[...] * acc.reshape(x.shape)


def _outproj(o_list, w, x3, m3, chunk_gate, sb, rb, norm_gain=None, tn=512):
    w = _wparts(w)
    n_w = len(w)
    sq, rq, d = x3.shape
    nb = rq // rb
    tm = sb * rb
    n_in = len(o_list)
    nj = d // tn
    widths = [o.shape[1] for o in o_list]
    assert all(wd == widths[0] for wd in widths) and sum(widths) == w[0].shape[0]
    kw = widths[0]
    in_specs = [pl.BlockSpec((tm, kw), lambda i, j: (i, 0)) for _ in o_list]
    args = list(o_list)
    for k in range(n_in):
        for part in w:
            mode = pl.Buffered(1) if nj == 1 else None
            in_specs.append(pl.BlockSpec((kw, tn), lambda i, j, k=k: (k, j), pipeline_mode=mode))
            args.append(part)
    if norm_gain is not None:
        in_specs.append(pl.BlockSpec((1, kw), lambda i, j: (0, 0)))
        args.append(norm_gain.reshape(1, kw))
    in_specs += [pl.BlockSpec((sb, rb, tn), lambda i, j: (i // nb, i % nb, j)),
                 pl.BlockSpec((sb, 1, tn), lambda i, j: (i // nb, 0, chunk_gate * nj + j))]
    args += [x3, m3]
    blocks = (n_in * (_nbytes((tm, kw)) + n_w * _nbytes((kw, tn), BF16))
              + 2 * _nbytes((tm, tn)) + _nbytes((sb, V7X_SUBLANES, tn)))
    scr = n_in * n_w * _nbytes((tm, kw), BF16)
    return pl.pallas_call(
        functools.partial(_outproj_body, n_in, n_w, norm_gain is not None), name="outproj",
        grid=((sq * rq) // tm, nj),
        in_specs=in_specs,
        out_specs=pl.BlockSpec((sb, rb, tn), lambda i, j: (i // nb, i % nb, j)),
        out_shape=jax.ShapeDtypeStruct(x3.shape, F32),
        scratch_shapes=[pltpu.VMEM((n_in, n_w, tm, kw), BF16)],
        compiler_params=_cparams(2, blocks, scr, 3 * _nbytes((tm, kw))),
    )(*args)


def _ret_tables(lam, c, width):
    lam1 = lam[:, :1]
    rel = (_iota((c, c), 0) - _iota((c, c), 1)).astype(F32)
    din = jnp.where(rel >= 0, jnp.exp(lam1 * jnp.maximum(rel, 0.0)), 0.0)
    r = _iota((c, width), 0).astype(F32)
    dq = jnp.exp(lam * (r + 1.0))
    dk = jnp.exp(lam * (c - 1.0 - r))
    gc = jnp.exp(lam * float(c))
    return din, dq, dk, gc


def _gla_log_alpha(alow, w2, b2):
    xa = _dot3(alow, w2) + b2
    return _log_sigmoid(xa) * (1.0 / GLA_TEMP)


def _mixa_prompt_body(pa, alow, w2, b2, cos, sin, gnr, gng, lam,
                      o_r, o_g, sret_out, sgla_out,
                      s_ret, st_gla, din_s, dq_s, dk_s, a_s):
    c_id = pl.program_id(1)
    n_c = pl.num_programs(1)
    c = pa.shape[0]
    dk_r, dk_g, dv = A_RET_DK, A_GLA_DK, A_DV

    @pl.when(c_id == 0)
    def _():
        s_ret[...] = jnp.zeros_like(s_ret)
        st_gla[...] = jnp.zeros_like(st_gla)
        for hh in range(RET_HEADS):
            din, dq, dk, _ = _ret_tables(lam[hh], c, dk_r)
            din_s[hh] = din
            dq_s[hh] = dq
            dk_s[hh] = dk

    cs, sn = cos[...], sin[...]
    la = _gla_log_alpha(alow[...], w2[...], b2[...])
    bc_all = _dot01_left(_tril01(c), la)
    rows = _iota((c, dk_g), 0)
    nblk = c // GLA_BLOCK
    row16 = _iota((GLA_BLOCK, dk_g), 0)
    lane16 = _iota((GLA_BLOCK, c), 1)
    for hh in range(RET_HEADS):
        q = _rot(pa[:, A_QR + hh * dk_r:A_QR + (hh + 1) * dk_r], cs, sn)
        k = _rot(pa[:, A_KR + hh * dk_r:A_KR + (hh + 1) * dk_r], cs, sn) * (dk_r ** -0.5)
        qs, vs = _split2(q), _split2(pa[:, A_VR + hh * dv:A_VR + (hh + 1) * dv])
        sc = _dot3_nt(q, k) * din_s[hh]
        s_old = s_ret[hh]
        o = _dot3p(_split2(sc), vs) + _dot3p(qs, _split2(s_old)) * dq_s[hh]
        kd = k * dk_s[hh]
        s_ret[hh] = s_old * jnp.exp(lam[hh] * float(c)) + _dot3p(_split2(kd.T), vs)
        o_r[:, hh * dv:(hh + 1) * dv] = _head_norm(
            o, gnr[:, hh * dv:(hh + 1) * dv], pa[:, A_GR + hh * dv:A_GR + (hh + 1) * dv], True)

        bc = bc_all[:, hh * dk_g:(hh + 1) * dk_g]
        qg_v = pa[:, A_QG + hh * dk_g:A_QG + (hh + 1) * dk_g] * (dk_g ** -0.5)
        kg_v = pa[:, A_KG + hh * dk_g:A_KG + (hh + 1) * dk_g]
        vg_v = pa[:, A_VG + hh * dv:A_VG + (hh + 1) * dv]
        for blk in range(nblk):
            r0 = blk * GLA_BLOCK
            q_i = qg_v[r0:r0 + GLA_BLOCK]
            b_i = bc[r0:r0 + GLA_BLOCK]
            if blk == 0:
                a_i = jnp.zeros((GLA_BLOCK, c), F32)
            else:
                ref_b = bc[r0 - 1:r0]
                qt = q_i * jnp.exp(b_i - ref_b)
                kt = jnp.where(rows < r0, kg_v * jnp.exp(jnp.minimum(ref_b - bc, 0.0)), 0.0)
                a_i = _dot3_nt(qt, kt)
            for jj in range(GLA_BLOCK):
                j = r0 + jj
                e = jnp.exp(jnp.minimum(b_i - bc[j:j + 1], 0.0))
                col = jnp.sum(q_i * e * kg_v[j:j + 1], axis=-1, keepdims=True)
                col = jnp.where(row16[:, :1] >= jj, col, 0.0)
                a_i = a_i + jnp.where(lane16 == j, col, 0.0)
            a_s[hh, r0:r0 + GLA_BLOCK, :] = a_i
        st_old = st_gla[hh]
        og = _dot3(a_s[hh], vg_v) + _dot3_nt(qg_v * jnp.exp(bc), st_old)
        b_last = bc[c - 1:c]
        kdg = kg_v * jnp.exp(b_last - bc)
        st_gla[hh] = st_old * jnp.exp(b_last) + _dot3(vg_v.T, kdg)
        o_g[:, hh * dv:(hh + 1) * dv] = _head_norm(
            og, gng[:, hh * dv:(hh + 1) * dv], pa[:, A_GG + hh * dv:A_GG + (hh + 1) * dv], False)

    @pl.when(c_id == n_c - 1)
    def _():
        for hh in range(RET_HEADS):
            sret_out[0, hh] = s_ret[hh]
            sgla_out[0, hh] = st_gla[hh].T


def _mixa_prompt(pa, pa_low, w2p, b2, cos, sin, gnr, gng, lam, batch, seq):
    c = MIX_CHUNK
    nc = seq // c
    h = RET_HEADS
    dkr, dkg, dv = A_RET_DK, A_GLA_DK, A_DV
    width = pa.shape[1]
    const2 = lambda b, cc: (0, 0)
    in_specs = [pl.BlockSpec((c, width), lambda b, cc: (b * nc + cc, 0)),
                pl.BlockSpec((c, V7X_LANES), lambda b, cc: (b * nc + cc, 0)),
                pl.BlockSpec(w2p.shape, const2), pl.BlockSpec(b2.shape, const2),
                pl.BlockSpec((c, dkr // 2), lambda b, cc: (cc, 0)),
                pl.BlockSpec((c, dkr // 2), lambda b, cc: (cc, 0)),
                pl.BlockSpec(gnr.shape, const2), pl.BlockSpec(gng.shape, const2),
                pl.BlockSpec(lam.shape, lambda b, cc: (0, 0, 0))]
    out_specs = [pl.BlockSpec((c, h * dv), lambda b, cc: (b * nc + cc, 0)),
                 pl.BlockSpec((c, h * dv), lambda b, cc: (b * nc + cc, 0)),
                 pl.BlockSpec((1, h, dkr, dv), lambda b, cc: (b, 0, 0, 0)),
                 pl.BlockSpec((1, h, dkg, dv), lambda b, cc: (b, 0, 0, 0))]
    n_tok = batch * seq
    out_shape = [jax.ShapeDtypeStruct((n_tok, h * dv), F32),
                 jax.ShapeDtypeStruct((n_tok, h * dv), F32),
                 jax.ShapeDtypeStruct((batch, h, dkr, dv), F32),
                 jax.ShapeDtypeStruct((batch, h, dkg, dv), F32)]
    scratch = [pltpu.VMEM((h, dkr, dv), F32), pltpu.VMEM((h, dv, dkg), F32),
               pltpu.VMEM((h, c, c), F32), pltpu.VMEM((h, c, dkr), F32), pltpu.VMEM((h, c, dkr), F32),
               pltpu.VMEM((h, c, c), F32)]
    blocks = _nbytes((c, width)) + 2 * _nbytes((c, h * dv)) + h * (_nbytes((dkr, dv)) + _nbytes((dkg, dv)))
    scr = h * (_nbytes((dkr, dv)) + _nbytes((dv, dkg)) + 2 * _nbytes((c, c)) + 2 * _nbytes((c, dkr)))
    return pl.pallas_call(
        _mixa_prompt_body, name="mixa_prompt",
        grid=(batch, nc),
        in_specs=in_specs, out_specs=out_specs, out_shape=out_shape,
        scratch_shapes=scratch,
        compiler_params=_cparams(2, blocks, scr, 16 * 2 ** 20),
    )(pa, pa_low, w2p, b2, cos, sin, gnr, gng, lam)


def _direct_intra(q, k, v, decay_fn):
    t = q.shape[0]
    row = _iota((t, 1), 0)
    o = jnp.zeros((t, v.shape[1]), F32)
    for j in range(t):
        col = jnp.sum(q * decay_fn(j) * k[j:j + 1], axis=-1, keepdims=True)
        col = jnp.where(row >= j, col, 0.0)
        o = o + col * v[j:j + 1]
    return o


def _mixa_sample_body(sb, pa, alow, w2, b2, cos, sin, gnr, gng, lam, sret_in, sgla_in,
                      o_r, o_g, sret_out, sgla_out):
    t = 8
    dkr, dkg, dv = 256, 128, 256
    cs, sn = cos[...], sin[...]
    la = _gla_log_alpha(alow[...], w2[...], b2[...])
    bc_all = _dot01_left(_tril01(sb * t, block=t), la)
    rel = (_iota((t, 1), 0)).astype(F32)
    for hh in range(RET_HEADS):
        lam_row = lam[hh]
        dq = jnp.exp(lam_row * (rel + 1.0))
        dk = jnp.exp(lam_row * (t - 1.0 - rel))
        gc = jnp.exp(lam_row * float(t))
        for s in range(sb):
            r0 = s * t
            q = _rot(pa[r0:r0 + t, A_QR + hh * dkr:A_QR + (hh + 1) * dkr], cs, sn)
            k = _rot(pa[r0:r0 + t, A_KR + hh * dkr:A_KR + (hh + 1) * dkr], cs, sn) * (dkr ** -0.5)
            v = pa[r0:r0 + t, A_VR + hh * dv:A_VR + (hh + 1) * dv]
            g = pa[r0:r0 + t, A_GR + hh * dv:A_GR + (hh + 1) * dv]
            s_old = sret_in[s, hh]
            o = _direct_intra(q, k, v, lambda j: jnp.exp(lam_row * jnp.maximum(rel - float(j), 0.0)))
            o = o + _dot3(q, s_old) * dq
            kdt = _pad_rows(k * dk, V7X_LANES).T
            sret_out[s, hh] = s_old * gc + _dot3(kdt, _pad_rows(v, V7X_LANES))
            o_r[r0:r0 + t, hh * dv:(hh + 1) * dv] = _head_norm(
                o, gnr[:, hh * dv:(hh + 1) * dv], g, True)
            qg = pa[r0:r0 + t, A_QG + hh * dkg:A_QG + (hh + 1) * dkg] * (dkg ** -0.5)
            kg = pa[r0:r0 + t, A_KG + hh * dkg:A_KG + (hh + 1) * dkg]
            vg = pa[r0:r0 + t, A_VG + hh * dv:A_VG + (hh + 1) * dv]
            gg = pa[r0:r0 + t, A_GG + hh * dv:A_GG + (hh + 1) * dv]
            bc = bc_all[r0:r0 + t, hh * dkg:(hh + 1) * dkg]
            sg_old = sgla_in[s, hh]
            og = _direct_intra(qg, kg, vg, lambda j: jnp.exp(jnp.minimum(bc - bc[j:j + 1], 0.0)))
            og = og + _dot3(qg * jnp.exp(bc), sg_old)
            b_last = bc[t - 1:t]
            m = _pad_rows(jnp.concatenate([kg * jnp.exp(b_last - bc), jnp.exp(b_last)], axis=0),
                          V7X_LANES).T
            sgla_out[s, hh] = sg_old * m[:, t:t + 1] + _dot3(m, _pad_rows(vg, V7X_LANES))
            o_g[r0:r0 + t, hh * dv:(hh + 1) * dv] = _head_norm(
                og, gng[:, hh * dv:(hh + 1) * dv], gg, False)


def _mixa_sample(pa, pa_low, w2p, b2, cos, sin, gnr, gng, lam, sret, sgla, sb=4):
    nseq = sret.shape[0]
    t = 8
    h, dkr, dkg, dv = RET_HEADS, 256, 128, 256
    width = pa.shape[1]
    const2 = lambda i: (0, 0)
    in_specs = [pl.BlockSpec((sb * t, width), lambda i: (i, 0)),
                pl.BlockSpec((sb * t, V7X_LANES), lambda i: (i, 0)),
                pl.BlockSpec(w2p.shape, const2), pl.BlockSpec(b2.shape, const2),
                pl.BlockSpec(cos.shape, const2), pl.BlockSpec(sin.shape, const2),
                pl.BlockSpec(gnr.shape, const2), pl.BlockSpec(gng.shape, const2),
                pl.BlockSpec(lam.shape, lambda i: (0, 0, 0)),
                pl.BlockSpec((sb, h, dkr, dv), lambda i: (i, 0, 0, 0)),
                pl.BlockSpec((sb, h, dkg, dv), lambda i: (i, 0, 0, 0))]
    out_specs = [pl.BlockSpec((sb * t, h * dv), lambda i: (i, 0)),
                 pl.BlockSpec((sb * t, h * dv), lambda i: (i, 0)),
                 pl.BlockSpec((sb, h, dkr, dv), lambda i: (i, 0, 0, 0)),
                 pl.BlockSpec((sb, h, dkg, dv), lambda i: (i, 0, 0, 0))]
    out_shape = [jax.ShapeDtypeStruct((nseq * t, h * dv), F32),
                 jax.ShapeDtypeStruct((nseq * t, h * dv), F32),
                 jax.ShapeDtypeStruct(sret.shape, F32),
                 jax.ShapeDtypeStruct(sgla.shape, F32)]
    blocks = (_nbytes((sb * t, width)) + 2 * _nbytes((sb, h, dkr, dv))
              + 2 * _nbytes((sb, h, dkg, dv)) + 2 * 2 ** 20)
    return pl.pallas_call(
        functools.partial(_mixa_sample_body, sb), name="mixa_sample",
        grid=(nseq // sb,),
        in_specs=in_specs, out_specs=out_specs, out_shape=out_shape,
        compiler_params=_cparams(1, blocks, 0, 4 * 2 ** 20),
    )(pa, pa_low, w2p, b2, cos, sin, gnr, gng, lam, sret, sgla)


def _conv_prompt_body(x_ref, prev_ref, w_ref, b_ref, o_ref):
    first = pl.program_id(1) == 0
    ext = jnp.concatenate([jnp.where(first, 0.0, prev_ref[...]), x_ref[...]], axis=0)
    hist = CONV_W - 1
    acc = b_ref[...] + pltpu.roll(ext, hist, 0)[V7X_SUBLANES:] * w_ref[0:1, :]
    for j in range(1, CONV_W):
        delayed = pltpu.roll(ext, hist - j, 0)[V7X_SUBLANES:] if j < hist else x_ref[...]
        acc = acc + delayed * w_ref[j:j + 1, :]
    o_ref[...] = _silu(acc)


def _conv_prompt(pc, conv_w, conv_b, batch, seq, col0, width, tc=512, tn=1024):
    nc = seq // tc
    cb0 = col0 // tn
    blocks = 2 * _nbytes((tc, tn)) + _nbytes((V7X_SUBLANES, tn)) * 3
    return pl.pallas_call(
        _conv_prompt_body, name="conv_prompt",
        grid=(batch, nc, width // tn),
        in_specs=[pl.BlockSpec((tc, tn), lambda b, c, j: (b * nc + c, cb0 + j)),
                  pl.BlockSpec((V7X_SUBLANES, tn),
                               lambda b, c, j: (jnp.maximum((b * nc + c) * (tc // V7X_SUBLANES) - 1, 0),
                                                cb0 + j)),
                  pl.BlockSpec((CONV_W, tn), lambda b, c, j: (0, j)),
                  pl.BlockSpec((1, tn), lambda b, c, j: (0, j))],
        out_specs=pl.BlockSpec((tc, tn), lambda b, c, j: (b * nc + c, j)),
        out_shape=jax.ShapeDtypeStruct((batch * seq, width), F32),
        compiler_params=_cparams(3, blocks, 0, 6 * _nbytes((tc, tn))),
    )(pc, pc, conv_w, conv_b.reshape(1, width))


def _conv_sample_body(x_ref, cache_ref, w_ref, b_ref, o_ref, scr):
    t = x_ref.shape[1]
    hist = CONV_W - 1
    scr[:, t - hist:t, :] = cache_ref[...]
    scr[:, t:2 * t, :] = x_ref[...]
    acc = b_ref[...] + scr[:, t - hist:2 * t - hist, :] * w_ref[0:1, :]
    for j in range(1, CONV_W):
        off = t - hist + j
        acc = acc + scr[:, off:off + t, :] * w_ref[j:j + 1, :]
    o_ref[...] = _silu(acc)


def _conv_sample(pc3, cache, conv_w, conv_b, col0, width, sbc=32, tn=1024):
    nseq, t, _ = pc3.shape
    cb0 = col0 // tn
    blocks = 2 * _nbytes((sbc, t, tn)) + _nbytes((sbc, 8, tn))
    return pl.pallas_call(
        _conv_sample_body, name="conv_sample",
        grid=(nseq // sbc, width // tn),
        in_specs=[pl.BlockSpec((sbc, t, tn), lambda i, j: (i, 0, cb0 + j)),
                  pl.BlockSpec((sbc, CONV_W - 1, tn), lambda i, j: (i, 0, j)),
                  pl.BlockSpec((CONV_W, tn), lambda i, j: (0, j)),
                  pl.BlockSpec((1, tn), lambda i, j: (0, j))],
        out_specs=pl.BlockSpec((sbc, t, tn), lambda i, j: (i, 0, j)),
        out_shape=jax.ShapeDtypeStruct((nseq, t, width), F32),
        scratch_shapes=[pltpu.VMEM((sbc, 2 * t, tn), F32)],
        compiler_params=_cparams(2, blocks, _nbytes((sbc, 2 * t, tn)), 4 * _nbytes((sbc, t, tn))),
    )(pc3, cache, conv_w, conv_b.reshape(1, width))


def _expand01(heads, width, first=0):
    r = _iota((V7X_LANES, heads * width), 0)
    l = _iota((V7X_LANES, heads * width), 1)
    return jnp.where((l // width) == r - first, 1.0, 0.0).astype(BF16)


def _ssd_prompt_body(xs, bm, cm, z, dtc, dtr, bias_c, alog_c, bias_r, alog_r, dfull,
                     y_out, s_out, st):
    c_id = pl.program_id(1)
    n_c = pl.num_programs(1)
    l = xs.shape[0]
    groups, n, gw = st.shape
    heads = dtr.shape[1]
    hpg = heads // groups
    p = gw // hpg

    @pl.when(c_id == 0)
    def _():
        st[...] = jnp.zeros_like(st)

    dt_c = _softplus(dtc[...] + bias_c[...])
    cum_c = _dot01_left(_tril01(l), dt_c * (-jnp.exp(alog_c[...])))
    cum_parts, dt_parts = _split3(cum_c), _split3(dt_c)
    dt_r = _softplus(dtr[0] + bias_r[...])
    tri_u = jnp.where(_iota((l, l), 0) <= _iota((l, l), 1), 1.0, 0.0).astype(BF16)
    cum_r = _dot01_right(dt_r * (-jnp.exp(alog_r[...])), tri_u)

    causal = _iota((l, l), 0) >= _iota((l, l), 1)
    lane = _iota((l, 2 * p), 1)
    for g in range(groups):
        x = xs[:, g * gw:(g + 1) * gw]
        bmg = bm[:, g * n:(g + 1) * n]
        cmb = cm[:, g * n:(g + 1) * n].astype(BF16)
        cb = _dot_nt(cmb, bmg.astype(BF16))
        y_tiles = []
        for pair in range(hpg // 2):
            xt = x[:, pair * 2 * p:(pair + 1) * 2 * p]
            acc = None
            for sub in range(2):
                r = g * hpg + 2 * pair + sub
                seg = cum_c[:, r:r + 1] - cum_r[r:r + 1, :]
                w = jnp.where(causal, cb * jnp.exp(jnp.minimum(seg, 0.0)) * dt_r[r:r + 1, :], 0.0)
                xm = jnp.where((lane // p) == sub, xt, 0.0)
                part = _dot(w.astype(BF16), xm.astype(BF16))
                acc = part if acc is None else acc + part
            y_tiles.append(acc)
        y = jnp.concatenate(y_tiles, axis=-1)

        ex = _expand01(hpg, p, g * hpg)
        cum_f = _dot(cum_parts[0], ex) + _dot(cum_parts[1], ex) + _dot(cum_parts[2], ex)
        dt_f = _dot(dt_parts[0], ex) + _dot(dt_parts[1], ex) + _dot(dt_parts[2], ex)
        st_old = st[g]
        y = y + _dot(cmb, st_old.astype(BF16)) * jnp.exp(cum_f)
        cl_f = cum_f[l - 1:l]
        tail = jnp.exp(cl_f - cum_f) * dt_f
        st[g] = st_old * jnp.exp(cl_f) + _dot(bmg.T.astype(BF16), (x * tail).astype(BF16))
        y = y + dfull[:, g * gw:(g + 1) * gw] * x
        y_out[:, g * gw:(g + 1) * gw] = y * _silu(z[:, g * gw:(g + 1) * gw])

    @pl.when(c_id == n_c - 1)
    def _():
        for g in range(groups):
            s_out[0, g * hpg:(g + 1) * hpg] = st[g].T.reshape(hpg, p, n)


def _ssd_prompt(act, pc, dt_c, dt_t, bias_c, alog_c, bias_r, alog_r, dfull, batch, seq):
    l = MIX_CHUNK
    nc = seq // l
    g = SSM_GROUPS
    n = SSM_STATE
    p = SSM_HEADDIM
    heads = dt_t.shape[1]
    d_inner = heads * p
    gw = d_inner // g
    gn = g * n

    def row(b, cc):
        return b * nc + cc

    in_specs = [pl.BlockSpec((l, d_inner), lambda b, cc: (row(b, cc), 0)),
                pl.BlockSpec((l, gn), lambda b, cc: (row(b, cc), d_inner // gn)),
                pl.BlockSpec((l, gn), lambda b, cc: (row(b, cc), d_inner // gn + 1)),
                pl.BlockSpec((l, d_inner), lambda b, cc: (row(b, cc), 0)),
                pl.BlockSpec((l, V7X_LANES), lambda b, cc: (row(b, cc), 0)),
                pl.BlockSpec((1, heads, l), lambda b, cc: (b, 0, cc)),
                pl.BlockSpec((1, V7X_LANES), lambda b, cc: (0, 0)),
                pl.BlockSpec((1, V7X_LANES), lambda b, cc: (0, 0)),
                pl.BlockSpec((heads, l), lambda b, cc: (0, 0)),
                pl.BlockSpec((heads, l), lambda b, cc: (0, 0)),
                pl.BlockSpec((1, d_inner), lambda b, cc: (0, 0))]
    out_specs = [pl.BlockSpec((l, d_inner), lambda b, cc: (row(b, cc), 0)),
                 pl.BlockSpec((1, heads, p, n), lambda b, cc: (b, 0, 0, 0))]
    out_shape = [jax.ShapeDtypeStruct((batch * seq, d_inner), F32),
                 jax.ShapeDtypeStruct((batch, heads, p, n), F32)]
    blocks = 3 * _nbytes((l, d_inner)) + 2 * _nbytes((l, gn)) + _nbytes((heads, p, n))
    return pl.pallas_call(
        _ssd_prompt_body, name="ssd_prompt",
        grid=(batch, nc),
        in_specs=in_specs, out_specs=out_specs, out_shape=out_shape,
        scratch_shapes=[pltpu.VMEM((g, n, gw), F32)],
        compiler_params=_cparams(2, blocks, _nbytes((g, n, gw)), 16 * 2 ** 20),
    )(act, act, act, pc, dt_c, dt_t, bias_c, alog_c, bias_r, alog_r, dfull)


def _ssd_sample_body(sb, xs, bm, cm, z, dtc, bias_c, alog_c, dfull, s_in, y_out, s_out, y_s):
    t = 8
    rows = sb * t
    gw = xs.shape[1]
    hpg = s_in.shape[1]
    p = gw // hpg
    n = bm.shape[1]
    dt_c = _softplus(dtc[...] + bias_c[0])
    cum_c = _dot01_left(_tril01(rows, block=t), dt_c * (-jnp.exp(alog_c[0])))
    ex = _expand01(hpg, p)
    cum_f = _dot01_right(cum_c, ex)
    dt_f = _dot01_right(dt_c, ex)
    x = xs[...]
    x3 = x.reshape(sb, t, gw)
    c3 = cm[...].reshape(sb, t, n)
    b3 = bm[...].reshape(sb, t, n)
    cum3 = cum_f.reshape(sb, t, gw)
    dt3 = dt_f.reshape(sb, t, gw)
    rowi = _iota((sb, t, 1), 1)
    y3 = jnp.zeros((sb, t, gw), F32)
    for j in range(t):
        cbj = jnp.sum(c3 * b3[:, j:j + 1, :], axis=-1, keepdims=True)
        dec = jnp.exp(jnp.minimum(cum3 - cum3[:, j:j + 1, :], 0.0))
        w = jnp.where(rowi >= j, cbj * dec * dt3[:, j:j + 1, :], 0.0)
        y3 = y3 + w * x3[:, j:j + 1, :]
    y_s[...] = y3.reshape(rows, gw)
    cmv = cm[...]
    bmv = bm[...]
    for s in range(sb):
        r0 = s * t
        sg = s_in[s].reshape(hpg * p, n)
        cf = cum_f[r0:r0 + t]
        yi = _dot_nt(cmv[r0:r0 + t].astype(BF16), sg.astype(BF16)) * jnp.exp(cf)
        cl = cf[t - 1:t]
        xt = x[r0:r0 + t] * (jnp.exp(cl - cf) * dt_f[r0:r0 + t])
        m = _pad_rows(jnp.concatenate([xt, jnp.exp(cl)], axis=0), V7X_LANES).T
        s_new = sg * m[:, t:t + 1] + _dot(m.astype(BF16), _pad_rows(bmv[r0:r0 + t], V7X_LANES).astype(BF16))
        s_out[s] = s_new.reshape(hpg, p, n)
        y_s[r0:r0 + t, :] = y_s[r0:r0 + t, :] + yi
    y_out[...] = (y_s[...] + dfull[0] * x) * _silu(z[...])


def _ssd_sample(act, pc, pcs, bias_c, alog_c, dfull, s_ssm, sb=8):
    t = 8
    nseq = s_ssm.shape[0]
    g = SSM_GROUPS
    n = SSM_STATE
    p = SSM_HEADDIM
    d_inner = s_ssm.shape[1] * p
    gw = d_inner // g
    hpg = gw // p
    bb0 = d_inner // n
    cb0 = (d_inner + g * n) // n
    rows = sb * t
    in_specs = [pl.BlockSpec((rows, gw), lambda i, gg: (i, gg)),
                pl.BlockSpec((rows, n), lambda i, gg: (i, bb0 + gg)),
                pl.BlockSpec((rows, n), lambda i, gg: (i, cb0 + gg)),
                pl.BlockSpec((rows, gw), lambda i, gg: (i, gg)),
                pl.BlockSpec((rows, V7X_LANES), lambda i, gg: (i, gg)),
                pl.BlockSpec((1, 1, V7X_LANES), lambda i, gg: (gg, 0, 0)),
                pl.BlockSpec((1, 1, V7X_LANES), lambda i, gg: (gg, 0, 0)),
                pl.BlockSpec((1, 1, gw), lambda i, gg: (gg, 0, 0)),
                pl.BlockSpec((sb, hpg, p, n), lambda i, gg: (i, gg, 0, 0))]
    out_specs = [pl.BlockSpec((rows, gw), lambda i, gg: (i, gg)),
                 pl.BlockSpec((sb, hpg, p, n), lambda i, gg: (i, gg, 0, 0))]
    out_shape = [jax.ShapeDtypeStruct((nseq * t, d_inner), F32),
                 jax.ShapeDtypeStruct(s_ssm.shape, F32)]
    blocks = 3 * _nbytes((rows, gw)) + 3 * _nbytes((rows, n)) + 2 * _nbytes((sb, hpg, p, n))
    return pl.pallas_call(
        functools.partial(_ssd_sample_body, sb), name="ssd_sample",
        grid=(nseq // sb, g),
        in_specs=in_specs, out_specs=out_specs, out_shape=out_shape,
        scratch_shapes=[pltpu.VMEM((rows, gw), F32)],
        compiler_params=_cparams(2, blocks, _nbytes((rows, gw)), 8 * 2 ** 20),
    )(act, act, act, pc, pcs, bias_c, alog_c, dfull, s_ssm)


def _moe_pre_body(n_pt, xp_ref, shp_ref, scp_ref, xs_ref, shs_ref, scs_ref, g_ref, wr_ref,
                  h_out, route_out):
    i = pl.program_id(0)

    @pl.when(i < n_pt)
    def _():
        _moe_route(xp_ref, shp_ref, scp_ref, g_ref, wr_ref, h_out, route_out)

    @pl.when(i >= n_pt)
    def _():
        _moe_route(xs_ref, shs_ref, scs_ref, g_ref, wr_ref, h_out, route_out)


def _moe_route(x_ref, sh_ref, sc_ref, g_ref, wr_ref, h_out, route_out):
    x = x_ref[...]
    h = _rms_rows(x, g_ref[...]) * (1.0 + sc_ref[...]) + sh_ref[...]
    h = h.reshape(h_out.shape[0], x.shape[-1])
    h_out[...] = h.reshape(h_out.shape)
    logits = _dot3(h, wr_ref[...])
    tm = logits.shape[0]
    lane = _iota((tm, V7X_LANES), 1).astype(F32)
    neg = -jnp.inf
    lg = jnp.where(lane < MOE_GROUPS, logits, neg)
    mg = jnp.max(lg, axis=-1, keepdims=True)
    top_g = jnp.min(jnp.where(lg == mg, lane, float(V7X_LANES)), axis=-1, keepdims=True)
    p_top = 1.0 / jnp.sum(jnp.where(lane < MOE_GROUPS, jnp.exp(logits - mg), 0.0),
                          axis=-1, keepdims=True)
    lo = MOE_GROUPS + MOE_PER_GROUP * top_g
    le = jnp.where((lane >= lo) & (lane < lo + MOE_PER_GROUP), logits, neg)
    v1 = jnp.max(le, axis=-1, keepdims=True)
    i1 = jnp.min(jnp.where(le == v1, lane, float(V7X_LANES)), axis=-1, keepdims=True)
    le2 = jnp.where(lane == i1, neg, le)
    v2 = jnp.max(le2, axis=-1, keepdims=True)
    i2 = jnp.min(jnp.where(le2 == v2, lane, float(V7X_LANES)), axis=-1, keepdims=True)
    tt = jnp.exp(v2 - v1)
    w1 = p_top / (1.0 + tt)
    w2 = p_top * tt / (1.0 + tt)
    route_out[...] = jnp.where(lane == 0, i1 - MOE_GROUPS,
                     jnp.where(lane == 1, i2 - MOE_GROUPS,
                     jnp.where(lane == 2, w1, jnp.where(lane == 3, w2, 0.0))))


def _moe_pre(xp, mp, xs_, ms, chunk_shift, chunk_scale, gain, wr, tm):
    bp, tp, d = xp.shape
    bs, ts, _ = xs_.shape
    nb = tp // tm
    n_pt = bp * nb
    sbs = tm // ts
    n_st = bs // sbs
    n_all = bp * tp + bs * ts

    def pi(i):
        return jnp.minimum(i, n_pt - 1)

    def si(i):
        return jnp.maximum(i - n_pt, 0)

    in_specs = [pl.BlockSpec((1, tm, d), lambda i: (pi(i) // nb, pi(i) % nb, 0)),
                pl.BlockSpec((1, 1, d), lambda i: (pi(i) // nb, 0, chunk_shift)),
                pl.BlockSpec((1, 1, d), lambda i: (pi(i) // nb, 0, chunk_scale)),
                pl.BlockSpec((sbs, ts, d), lambda i: (si(i), 0, 0)),
                pl.BlockSpec((sbs, 1, d), lambda i: (si(i), 0, chunk_shift)),
                pl.BlockSpec((sbs, 1, d), lambda i: (si(i), 0, chunk_scale)),
                pl.BlockSpec((1, d), lambda i: (0, 0)),
                pl.BlockSpec((d, V7X_LANES), lambda i: (0, 0))]
    blocks = (3 * _nbytes((tm, d)) + 2 * _nbytes((sbs, 8, d)) + _nbytes((d, V7X_LANES))
              + _nbytes((tm, 128)))
    return pl.pallas_call(
        functools.partial(_moe_pre_body, n_pt), name="moe_pre",
        grid=(n_pt + n_st,),
        in_specs=in_specs,
        out_specs=[pl.BlockSpec((tm, d // V7X_LANES, V7X_LANES), lambda i: (i, 0, 0)),
                   pl.BlockSpec((tm, V7X_LANES), lambda i: (i, 0))],
        out_shape=[jax.ShapeDtypeStruct((n_all, d // V7X_LANES, V7X_LANES), F32),
                   jax.ShapeDtypeStruct((n_all, V7X_LANES), F32)],
        compiler_params=_cparams(1, blocks, 0, 4 * _nbytes((tm, d))),
    )(xp, mp, mp, xs_, ms, ms, gain.reshape(1, d), wr)


def _rank_body(route_ref, rank_ref, cnt_ref, carry):
    @pl.when(pl.program_id(0) == 0)
    def _():
        carry[...] = jnp.zeros_like(carry)

    r = route_ref[...]
    tm = r.shape[0]
    lane = _iota((tm, V7X_LANES), 1).astype(F32)
    o1 = lane == r[:, 0:1]
    o2 = lane == r[:, 1:2]
    oh = jnp.where(o1, 1.0, 0.0) + jnp.where(o2, 1.0, 0.0)
    pex = _dot(_tril01(tm, strict=True), oh.astype(BF16)) + carry[0:1, :]
    r1 = jnp.sum(jnp.where(o1, pex, 0.0), axis=-1, keepdims=True)
    r2 = jnp.sum(jnp.where(o2, pex, 0.0), axis=-1, keepdims=True)
    rank_ref[...] = jnp.where(lane == 0, r1, jnp.where(lane == 1, r2, 0.0))
    carry[...] = carry[...] + jnp.sum(oh, axis=0, keepdims=True)
    cnt_ref[...] = carry[...]


def _rank(route, tm=256):
    n = route.shape[0]
    return pl.pallas_call(
        _rank_body, name="moe_rank",
        grid=(n // tm,),
        in_specs=[pl.BlockSpec((tm, V7X_LANES), lambda i: (i, 0))],
        out_specs=[pl.BlockSpec((tm, V7X_LANES), lambda i: (i, 0)),
                   pl.BlockSpec((V7X_SUBLANES, V7X_LANES), lambda i: (0, 0))],
        out_shape=[jax.ShapeDtypeStruct((n, V7X_LANES), F32),
                   jax.ShapeDtypeStruct((V7X_SUBLANES, V7X_LANES), F32)],
        scratch_shapes=[pltpu.VMEM((V7X_SUBLANES, V7X_LANES), F32)],
        compiler_params=_cparams(1, 2 * _nbytes((tm, 128))),
    )(route)


def _gather_tokens(src_hbm, dst, sem, n, row_of):
    def issue(r, carry):
        pltpu.make_async_copy(src_hbm.at[row_of(r)], dst.at[r], sem).start()
        return carry

    lax.fori_loop(0, n, issue, 0, unroll=DMA_UNROLL)


def _gather_wait(src_hbm, dst, sem):
    pltpu.make_async_copy(src_hbm.at[pl.ds(0, dst.shape[0])], dst, sem).wait()


def _expert_body(layer, inv_ref, te_ref, first_ref, wslot_ref, nxt_ref, nt_ref,
                 h3_hbm, wg_hbm, wu_hbm, wd_hbm, o_ref,
                 xbuf, wg_f, wu_f, wd_f, wg_b, wu_b, wd_b, xsem, wsem):
    i = pl.program_id(0)
    nt = nt_ref[0]
    tm = xbuf.shape[1]
    active = i < nt

    def w_copies(e, slot):
        return [pltpu.make_async_copy(src.at[layer, e], dst.at[slot], wsem.at[k])
                for k, (src, dst) in enumerate(((wg_hbm, wg_f), (wu_hbm, wu_f), (wd_hbm, wd_f)))]

    def fetch_rows(tile, slot):
        _gather_tokens(h3_hbm, xbuf.at[slot], xsem.at[slot], tm, lambda r: inv_ref[tile * tm + r])

    @pl.when(i == 0)
    def _():
        for cp in w_copies(te_ref[0], wslot_ref[0]):
            cp.start(priority=WEIGHT_DMA_PRIORITY)
        fetch_rows(0, 0)

        @pl.when(nt > 1)
        def _():
            fetch_rows(1, 1)

    @pl.when(active)
    def _():
        slot = i % ROW_SLOTS
        ws = wslot_ref[i]

        @pl.when(first_ref[i] == 1)
        def _():
            for cp in w_copies(te_ref[i], ws):
                cp.wait()
            wg_b[...] = wg_f[ws].astype(BF16)
            wu_b[...] = wu_f[ws].astype(BF16)
            wd_b[...] = wd_f[ws].astype(BF16)

            @pl.when(nxt_ref[i] >= 0)
            def _():
                for cp in w_copies(nxt_ref[i], 1 - ws):
                    cp.start(priority=WEIGHT_DMA_PRIORITY)

        @pl.when(i + 2 < nt)
        def _():
            fetch_rows(i + 2, (i + 2) % ROW_SLOTS)

        _gather_wait(h3_hbm, xbuf.at[slot], xsem.at[slot])
        xb = xbuf[slot].reshape(tm, wg_b.shape[0]).astype(BF16)
        hid = _silu(_dot(xb, wg_b[...])) * _dot(xb, wu_b[...])
        o_ref[...] = _dot(hid.astype(BF16), wd_b[...]).reshape(o_ref.shape)

    @pl.when(jnp.logical_not(active))
    def _():
        o_ref[...] = jnp.zeros_like(o_ref)


def _experts(inv, tile_expert, first, wslot, nxt, ntiles, h3, wg, wu, wd, layer):
    _, dc, ln = h3.shape
    d = dc * ln
    ff = wg.shape[3]
    tm = MOE_TILE
    p_pad = inv.shape[0]
    scratch = [pltpu.VMEM((ROW_SLOTS, tm, dc, ln), F32),
               pltpu.VMEM((2, d, ff), F32), pltpu.VMEM((2, d, ff), F32), pltpu.VMEM((2, ff, d), F32),
               pltpu.VMEM((d, ff), BF16), pltpu.VMEM((d, ff), BF16), pltpu.VMEM((ff, d), BF16),
               pltpu.SemaphoreType.DMA((ROW_SLOTS,)), pltpu.SemaphoreType.DMA((3,))]
    scr_bytes = ROW_SLOTS * _nbytes((tm, d)) + 6 * _nbytes((d, ff)) + 3 * _nbytes((d, ff), BF16)
    any_spec = pl.BlockSpec(memory_space=pl.ANY)
    return pl.pallas_call(
        functools.partial(_expert_body, layer), name="moe_experts",
        grid_spec=pltpu.PrefetchScalarGridSpec(
            num_scalar_prefetch=6,
            grid=(p_pad // tm,),
            in_specs=[any_spec, any_spec, any_spec, any_spec],
            out_specs=pl.BlockSpec((tm, dc, ln), lambda i, *_: (i, 0, 0)),
            scratch_shapes=scratch),
        out_shape=jax.ShapeDtypeStruct((p_pad, dc, ln), F32),
        compiler_params=_cparams(1, _nbytes((tm, d)), scr_bytes, 3 * _nbytes((tm, d)),
                                 disable_bounds_checks=True),
    )(inv, tile_expert, first, wslot, nxt, ntiles, h3, wg, wu, wd)


def _combine_body(tok_off, final, pos_ref, eo_hbm, route_ref, x_ref, gate_ref, fn_ref, out_ref,
                  buf, sem):
    i = pl.program_id(0)
    n = pl.num_programs(0)
    tc = route_ref.shape[0]
    d = x_ref.shape[-1]

    def fetch(step, slot):
        base = 2 * (tok_off + step * tc)
        for k in range(MOE_TOPK):
            _gather_tokens(eo_hbm, buf.at[slot, k], sem.at[slot], tc,
                           lambda r, k=k: pos_ref[base + MOE_TOPK * r + k])

    @pl.when(i == 0)
    def _():
        fetch(0, 0)

    slot = i % 2

    @pl.when(i + 1 < n)
    def _():
        fetch(i + 1, 1 - slot)

    for k in range(MOE_TOPK):
        _gather_wait(eo_hbm, buf.at[slot, k], sem.at[slot])
    rt = route_ref[...]
    y = rt[:, 2:3] * buf[slot, 0].reshape(tc, d) + rt[:, 3:4] * buf[slot, 1].reshape(tc, d)
    x = x_ref[...]
    xn = x + gate_ref[...] * y.reshape(x.shape)
    if final:
        xn = _rms_rows(xn, fn_ref[...])
    out_ref[...] = xn


def _combine(pos_flat, eo, route, x3, m3, chunk_gate, sb, rb, tok_off, final_gain=None):
    sq, rq, d = x3.shape
    nb = rq // rb
    tc = sb * rb
    blk_off = tok_off // tc
    final = final_gain is not None
    fn = (final_gain if final else jnp.ones((d,), F32)).reshape(1, d)
    blocks = 2 * _nbytes((tc, d)) + _nbytes((tc, 128)) + _nbytes((sb, 8, d))
    return pl.pallas_call(
        functools.partial(_combine_body, tok_off, final), name="moe_combine",
        grid_spec=pltpu.PrefetchScalarGridSpec(
            num_scalar_prefetch=1,
            grid=((sq * rq) // tc,),
            in_specs=[pl.BlockSpec(memory_space=pl.ANY),
                      pl.BlockSpec((tc, V7X_LANES), lambda i, p: (blk_off + i, 0)),
                      pl.BlockSpec((sb, rb, d), lambda i, p: (i // nb, i % nb, 0)),
                      pl.BlockSpec((sb, 1, d), lambda i, p: (i // nb, 0, chunk_gate)),
                      pl.BlockSpec((1, d), lambda i, p: (0, 0))],
            out_specs=pl.BlockSpec((sb, rb, d), lambda i, p: (i // nb, i % nb, 0)),
            scratch_shapes=[pltpu.VMEM((2, MOE_TOPK, tc, d // V7X_LANES, V7X_LANES), F32),
                            pltpu.SemaphoreType.DMA((2,))]),
        out_shape=jax.ShapeDtypeStruct(x3.shape, F32),
        compiler_params=_cparams(1, blocks, 2 * MOE_TOPK * _nbytes((tc, d)), 4 * _nbytes((tc, d)),
                                 disable_bounds_checks=True),
    )(pos_flat, eo, route, x3, m3, fn)


def _moe_layer(xp, xs_, mp, ms, l, norm_moe, w_group, w_expert, w_gate, w_up, w_down, final_gain):
    bp, tp, d = xp.shape
    bs, ts, _ = xs_.shape
    n_p, n_s = bp * tp, bs * ts
    n_all = n_p + n_s
    wr = jnp.concatenate([w_group[l], jnp.transpose(w_expert[l], (1, 0, 2)).reshape(d, MOE_EXPERTS)],
                         axis=1)
    wr = jnp.pad(wr, ((0, 0), (0, V7X_LANES - wr.shape[1])))
    tm = MOE_TILE
    h_all, route = _moe_pre(xp, mp, xs_, ms, 3, 4, norm_moe[l], wr, tm)
    rank, cnt = _rank(route)
    counts = cnt[0, :MOE_EXPERTS].astype(I32)
    tiles_per = (counts + tm - 1) // tm
    tile_end = jnp.cumsum(tiles_per)
    row_start = (tile_end - tiles_per) * tm
    ntiles = tile_end[-1]
    e_idx = route[:, 0:MOE_TOPK].astype(I32)
    pos = row_start[e_idx] + rank[:, 0:MOE_TOPK].astype(I32)
    nt_max = (2 * n_all) // tm + MOE_EXPERTS
    p_pad = nt_max * tm
    tile_ids = jnp.arange(nt_max, dtype=I32)
    te = jnp.minimum(jnp.sum((tile_ids[:, None] >= tile_end[None, :]).astype(I32), axis=1),
                     MOE_EXPERTS - 1)
    active = tile_ids < ntiles
    te = jnp.where(active, te, te[jnp.maximum(ntiles - 1, 0)])
    first = (active & ((tile_ids == 0) | (te != jnp.roll(te, 1)))).astype(I32)
    wslot = (jnp.cumsum(first) - 1) % 2
    grp_end = tile_end[te]
    nxt = jnp.where(grp_end < ntiles, te[jnp.minimum(grp_end, nt_max - 1)], -1).astype(I32)
    pos_flat = pos.reshape(-1)
    inv = (jnp.arange(p_pad, dtype=I32) % n_all).at[pos_flat].set(jnp.arange(2 * n_all, dtype=I32) // 2)
    nt_arr = ntiles.reshape(1).astype(I32)
    eo = _experts(inv, te, first, wslot.astype(I32), nxt, nt_arr, h_all, w_gate, w_up, w_down, l)
    xp_new = _combine(pos_flat, eo, route, xp, mp, 5, 1, tm, 0, final_gain)
    xs_new = _combine(pos_flat, eo, route, xs_, ms, 5, tm // ts, ts, n_p, final_gain)
    return xp_new, xs_new


def kernel(x_prompt, x_sample, c_prompt, c_sample, state_ret, state_gla, state_ssm, cache_conv,
           w_ada, b_ada, norm_mix, norm_moe,
           a_w_in, a_ret_gn, a_gla_w2, a_gla_b2, a_gla_gn, a_w_out,
           c_w_in, c_conv_w, c_conv_b, c_dt_bias, c_a_log, c_d, c_norm, c_w_out,
           moe_w_group, moe_w_expert, moe_w_gate, moe_w_up, moe_w_down, final_norm):
    bp, tp, d = x_prompt.shape
    bs, ts, _ = x_sample.shape
    n_p, n_s = bp * tp, bs * ts

    pad = (-bp) % V7X_SUBLANES
    c_all = jnp.concatenate([c_prompt, jnp.zeros((pad, d), F32), c_sample], axis=0)
    m_all = _ada(c_all, w_ada, b_ada)
    mp = [m_all[l, :bp].reshape(bp, 1, 6 * d) for l in range(2)]
    ms = [m_all[l, bp + pad:].reshape(bs, 1, 6 * d) for l in range(2)]

    tm_p, tm_s = 512, 512

    wa = a_w_in[0]
    n_main_a = (wa.shape[1] // 512) * 512
    wa_low = jnp.pad(wa[:, n_main_a:], ((0, 0), (0, V7X_LANES - (wa.shape[1] - n_main_a))))
    w2p = jnp.pad(a_gla_w2[0], ((0, V7X_LANES - GLA_RANK), (0, 0)))
    b2 = a_gla_b2[0].reshape(1, -1)
    gnr = a_ret_gn[0].reshape(1, -1)
    gng = a_gla_gn[0].reshape(1, -1)
    half = A_RET_DK // 2
    inv_freq = ROPE_BASE ** (-jnp.arange(half, dtype=F32) / half)
    ang_p = jnp.arange(tp, dtype=F32)[:, None] * inv_freq[None, :]
    ang_s = (PAST_LEN + jnp.arange(ts, dtype=F32))[:, None] * inv_freq[None, :]
    log_gamma = jnp.log(1.0 - 2.0 ** (-5.0 - jnp.arange(RET_HEADS, dtype=F32)))
    lam = jnp.broadcast_to(log_gamma[:, None, None], (RET_HEADS, 1, A_RET_DK))

    wa_parts, wa_low_parts, wo_parts = _split_bf16(wa), _split_bf16(wa_low), _split_bf16(a_w_out[0])
    pa_p, low_p = _inproj(x_prompt, mp[0], 0, 1, norm_mix[0], wa_parts, n_main_a, wa_low_parts, 1, tm_p)
    pa_s, low_s = _inproj(x_sample, ms[0], 0, 1, norm_mix[0], wa_parts, n_main_a, wa_low_parts,
                          tm_s // ts, ts)
    or_p, og_p, ret_p, gla_p = _mixa_prompt(pa_p, low_p, w2p, b2, jnp.cos(ang_p), jnp.sin(ang_p),
                                            gnr, gng, lam, bp, tp)
    or_s, og_s, ret_s, gla_s = _mixa_sample(pa_s, low_s, w2p, b2, jnp.cos(ang_s), jnp.sin(ang_s),
                                            gnr, gng, lam, state_ret[0], state_gla[0])
    xp = _outproj([or_p, og_p], wo_parts, x_prompt, mp[0], 2, 1, 256, tn=d)
    xs_ = _outproj([or_s, og_s], wo_parts, x_sample, ms[0], 2, 256 // ts, ts, tn=d)
    xp, xs_ = _moe_layer(xp, xs_, mp[0], ms[0], 0, norm_moe, moe_w_group, moe_w_expert,
                         moe_w_gate, moe_w_up, moe_w_down, None)

    wc = c_w_in[0]
    n_main_c = (wc.shape[1] // 512) * 512
    heads = wc.shape[1] - n_main_c
    hpg = heads // SSM_GROUPS
    d_inner = heads * SSM_HEADDIM
    conv_dim = n_main_c - d_inner
    wdt = jnp.pad(wc[:, n_main_c:].reshape(d, SSM_GROUPS, hpg),
                  ((0, 0), (0, 0), (0, V7X_LANES - hpg))).reshape(d, SSM_GROUPS * V7X_LANES).astype(BF16)
    wc_b = wc.astype(BF16)
    wdt_all = jnp.pad(wc[:, n_main_c:], ((0, 0), (0, V7X_LANES - heads))).astype(BF16)
    pc_p, dt_p = _inproj(x3=xp, m3=mp[1], chunk_shift=0, chunk_scale=1, gain=norm_mix[1], w=wc_b,
                         n_main=n_main_c, ws=wdt_all, sb=1, rb=2 * tm_p)
    pc_s, dt_s = _inproj(x3=xs_, m3=ms[1], chunk_shift=0, chunk_scale=1, gain=norm_mix[1], w=wc_b,
                         n_main=n_main_c, ws=wdt, sb=tm_s // ts, rb=ts)
    act_p = _conv_prompt(pc_p, c_conv_w[0], c_conv_b[0], bp, tp, d_inner, conv_dim)
    act_s = _conv_sample(pc_s.reshape(bs, ts, n_main_c), cache_conv[0], c_conv_w[0], c_conv_b[0],
                         d_inner, conv_dim).reshape(n_s, conv_dim)
    conv_p = pc_p.reshape(bp, tp, n_main_c)[:, tp - (CONV_W - 1):, d_inner:]
    conv_s = pc_s.reshape(bs, ts, n_main_c)[:, ts - (CONV_W - 1):, d_inner:]

    def lanes_pad(v):
        return jnp.pad(v.reshape(SSM_GROUPS, 1, hpg), ((0, 0), (0, 0), (0, V7X_LANES - hpg)))

    bias_c, alog_c = lanes_pad(c_dt_bias[0]), lanes_pad(c_a_log[0])
    dfull = jnp.repeat(c_d[0], SSM_HEADDIM).reshape(SSM_GROUPS, 1, hpg * SSM_HEADDIM)

    def lanes_all(v):
        return jnp.pad(v.reshape(1, heads), ((0, 0), (0, V7X_LANES - heads)))

    bias_r = jnp.broadcast_to(c_dt_bias[0].reshape(heads, 1), (heads, MIX_CHUNK))
    alog_r = jnp.broadcast_to(c_a_log[0].reshape(heads, 1), (heads, MIX_CHUNK))
    dt_t = jnp.transpose(dt_p[:, :heads].reshape(bp, tp, heads), (0, 2, 1))
    yg_p, ssm_p = _ssd_prompt(act_p, pc_p, dt_p, dt_t, lanes_all(c_dt_bias[0]), lanes_all(c_a_log[0]),
                              bias_r, alog_r, dfull.reshape(1, d_inner), bp, tp)
    yg_s, ssm_s = _ssd_sample(act_s, pc_s, dt_s, bias_c, alog_c, dfull, state_ssm[0])
    wco_b = c_w_out[0].astype(BF16)
    xp = _outproj([yg_p], wco_b, xp, mp[1], 2, 1, 256, norm_gain=c_norm[0], tn=d)
    xs_ = _outproj([yg_s], wco_b, xs_, ms[1], 2, 256 // ts, ts, norm_gain=c_norm[0], tn=d)
    yp, ys = _moe_layer(xp, xs_, mp[1], ms[1], 1, norm_moe, moe_w_group, moe_w_expert,
                        moe_w_gate, moe_w_up, moe_w_down, final_norm)

    return (yp, ys, ret_p[None], ret_s[None], gla_p[None], gla_s[None],
            ssm_p[None], ssm_s[None], conv_p[None], conv_s[None])
```

```python
import functools
import math

import jax
import jax.numpy as jnp
from jax import lax
from jax.experimental import pallas as pl
from jax.experimental.pallas import tpu as pltpu

F32 = jnp.float32
BF16 = jnp.bfloat16
I32 = jnp.int32

EPS = 1e-6
PAST_LEN = 16384
ROPE_BASE = 10000.0
GLA_TEMP = 16.0
GLA_RANK = 16
RET_HEADS = 4
GLA_HEADS = 4
SSM_GROUPS = 8
SSM_HEADDIM = 64
SSM_STATE = 128
CONV_W = 4
MOE_GROUPS = 4
MOE_PER_GROUP = 8
MOE_EXPERTS = MOE_GROUPS * MOE_PER_GROUP
MOE_TOPK = 2

V7X_LANES = 128
V7X_SUBLANES = 8
V7X_VMEM_BYTES = 64 * 2 ** 20
V7X_VMEM_RESERVE = 6 * 2 ** 20

A_RET_DK, A_GLA_DK, A_DV = 256, 128, 256
A_QR = 0
A_KR = A_QR + RET_HEADS * A_RET_DK
A_VR = A_KR + RET_HEADS * A_RET_DK
A_GR = A_VR + RET_HEADS * A_DV
A_QG = A_GR + RET_HEADS * A_DV
A_KG = A_QG + GLA_HEADS * A_GLA_DK
A_VG = A_KG + GLA_HEADS * A_GLA_DK
A_GG = A_VG + GLA_HEADS * A_DV

MIX_CHUNK = 128
GLA_BLOCK = 16
MOE_TILE = 256
DMA_UNROLL = 8
ROW_SLOTS = 3
WEIGHT_DMA_PRIORITY = 1


def _cparams(n_grid, block_bytes, scratch_bytes=0, temp_bytes=0, **kwargs):
    need = 2 * block_bytes + scratch_bytes + temp_bytes + 8 * 2 ** 20
    limit = int(min(max(need, 24 * 2 ** 20), V7X_VMEM_BYTES - V7X_VMEM_RESERVE))
    return pltpu.CompilerParams(dimension_semantics=("arbitrary",) * n_grid,
                                vmem_limit_bytes=limit, **kwargs)


def _nbytes(shape, dtype=F32):
    return math.prod(shape) * jnp.dtype(dtype).itemsize


def _silu(x):
    return x * (1.0 / (1.0 + jnp.exp(-x)))


def _softplus(x):
    return jnp.maximum(x, 0.0) + jnp.log1p(jnp.exp(-jnp.abs(x)))


def _log_sigmoid(x):
    return -_softplus(-x)


def _dot(a, b):
    return jnp.dot(a, b, preferred_element_type=F32)


def _dot_nt(a, b):
    return lax.dot_general(a, b, (((1,), (1,)), ((), ())), preferred_element_type=F32)


def _split2(x):
    hi = x.astype(BF16)
    return hi, (x - hi.astype(F32)).astype(BF16)


def _dot3p(a, b):
    return _dot(a[0], b[0]) + _dot(a[0], b[1]) + _dot(a[1], b[0])


def _dot3(a, b):
    return _dot3p(_split2(a), _split2(b))


def _dot3_nt(a, b):
    ah, al = _split2(a)
    bh, bl = _split2(b)
    return _dot_nt(ah, bh) + _dot_nt(ah, bl) + _dot_nt(al, bh)


def _split3(x):
    a = x.astype(BF16)
    r = x - a.astype(F32)
    b = r.astype(BF16)
    r = r - b.astype(F32)
    return a, b, r.astype(BF16)


def _dot01_left(m01, x):
    a, b, c = _split3(x)
    return _dot(m01, a) + _dot(m01, b) + _dot(m01, c)


def _dot01_right(x, m01):
    a, b, c = _split3(x)
    return _dot(a, m01) + _dot(b, m01) + _dot(c, m01)


def _iota(shape, dim, dtype=I32):
    return lax.broadcasted_iota(dtype, shape, dim)


def _tril01(n, block=None, strict=False):
    i = _iota((n, n), 0)
    j = _iota((n, n), 1)
    m = (i > j) if strict else (i >= j)
    if block is not None:
        m = m & ((i // block) == (j // block))
    return jnp.where(m, 1.0, 0.0).astype(BF16)


def _rms_rows(x, g):
    return x * lax.rsqrt(jnp.mean(x * x, axis=-1, keepdims=True) + EPS) * g


def _rot(x, cos, sin):
    half = x.shape[-1] // 2
    x1, x2 = x[:, :half], x[:, half:]
    return jnp.concatenate([x1 * cos - x2 * sin, x1 * sin + x2 * cos], axis=-1)


def _head_norm(o, gain, gate, center):
    if center:
        o = o - jnp.mean(o, axis=-1, keepdims=True)
    o = o * lax.rsqrt(jnp.mean(o * o, axis=-1, keepdims=True) + EPS)
    return o * gain * _silu(gate)


def _pad_rows(x, rows):
    return jnp.concatenate([x, jnp.zeros((rows - x.shape[0], x.shape[1]), x.dtype)], axis=0)


def _ada_body(c_ref, w_ref, b_ref, o_ref):
    o_ref[0] = _dot3(_silu(c_ref[...]), w_ref[0]) + b_ref[0]


def _ada(c_all, w_ada, b_ada):
    depth, d, n6 = w_ada.shape
    m = c_all.shape[0]
    tn = 1024
    blocks = _nbytes((m, d)) + _nbytes((d, tn)) + _nbytes((m, tn))
    return pl.pallas_call(
        _ada_body, name="ada",
        grid=(depth, n6 // tn),
        in_specs=[pl.BlockSpec((m, d), lambda l, j: (0, 0)),
                  pl.BlockSpec((1, d, tn), lambda l, j: (l, 0, j)),
                  pl.BlockSpec((1, 1, tn), lambda l, j: (l, 0, j))],
        out_specs=pl.BlockSpec((1, m, tn), lambda l, j: (l, 0, j)),
        out_shape=jax.ShapeDtypeStruct((depth, m, n6), F32),
        compiler_params=_cparams(2, blocks, temp_bytes=3 * _nbytes((d, tn))),
    )(c_all, w_ada, b_ada.reshape(depth, 1, n6))


def _wparts(x):
    return x if isinstance(x, tuple) else (x,)


def _split_bf16(w):
    hi = lax.bitcast_convert_type(
        lax.bitcast_convert_type(w, jnp.uint32) & jnp.uint32(0xFFFF0000), F32)
    return hi.astype(BF16), (w - hi).astype(BF16)


def _dot_parts(a, b):
    if len(b) == 1:
        return _dot(a[0], b[0])
    return _dot3p(a, b)


INPROJ_SUB = 512


def _inproj_body(n_w, x_ref, sh_ref, sc_ref, g_ref, *refs):
    w_refs, ws_refs = refs[:n_w], refs[n_w:2 * n_w]
    o_ref, os_ref, h_scr = refs[2 * n_w:]
    tm = h_scr.shape[1]
    sb, rb, _ = x_ref.shape

    @pl.when(pl.program_id(1) == 0)
    def _():
        for r0 in range(0, tm, INPROJ_SUB):
            rows = min(INPROJ_SUB, tm)
            if sb == 1:
                x = x_ref[:, r0:r0 + rows, :]
                sc, sh = sc_ref[...], sh_ref[...]
            else:
                s0, s1 = r0 // rb, (r0 + rows) // rb
                x = x_ref[s0:s1]
                sc, sh = sc_ref[s0:s1], sh_ref[s0:s1]
            h = _rms_rows(x, g_ref[...]) * (1.0 + sc) + sh
            h = h.reshape(rows, h.shape[-1])
            hs = _split2(h) if n_w == 2 else (h.astype(BF16),)
            for k in range(n_w):
                h_scr[k, r0:r0 + rows, :] = hs[k]
            os_ref[r0:r0 + rows, :] = _dot_parts(hs, tuple(r[...] for r in ws_refs))

    o_ref[...] = _dot_parts(tuple(h_scr[k] for k in range(n_w)), tuple(r[...] for r in w_refs))


def _inproj(x3, m3, chunk_shift, chunk_scale, gain, w, n_main, ws, sb, rb, tn=512):
    w, ws = _wparts(w), _wparts(ws)
    n_w = len(w)
    sq, rq, d = x3.shape
    nb = rq // rb
    tm = sb * rb
    n_tok = sq * rq
    ns = ws[0].shape[1]
    blocks = (_nbytes((tm, d)) + 2 * _nbytes((sb, V7X_SUBLANES, d))
              + n_w * (_nbytes((d, tn), BF16) + _nbytes((d, ns), BF16))
              + _nbytes((tm, tn)) + _nbytes((tm, ns)))
    w_specs = [pl.BlockSpec((d, tn), lambda i, j: (0, j)) for _ in w]
    ws_specs = [pl.BlockSpec((d, ns), lambda i, j: (0, 0)) for _ in ws]
    return pl.pallas_call(
        functools.partial(_inproj_body, n_w), name="inproj",
        grid=(n_tok // tm, n_main // tn),
        in_specs=[pl.BlockSpec((sb, rb, d), lambda i, j: (i // nb, i % nb, 0)),
                  pl.BlockSpec((sb, 1, d), lambda i, j: (i // nb, 0, chunk_shift)),
                  pl.BlockSpec((sb, 1, d), lambda i, j: (i // nb, 0, chunk_scale)),
                  pl.BlockSpec((1, d), lambda i, j: (0, 0))] + w_specs + ws_specs,
        out_specs=[pl.BlockSpec((tm, tn), lambda i, j: (i, j)),
                   pl.BlockSpec((tm, ns), lambda i, j: (i, 0))],
        out_shape=[jax.ShapeDtypeStruct((n_tok, n_main), F32),
                   jax.ShapeDtypeStruct((n_tok, ns), F32)],
        scratch_shapes=[pltpu.VMEM((n_w, tm, d), BF16)],
        compiler_params=_cparams(2, blocks, n_w * _nbytes((tm, d), BF16),
                                 4 * _nbytes((min(tm, INPROJ_SUB), d))),
    )(x3, m3, m3, gain.reshape(1, d), *w, *ws)


def _outproj_body(n_in, n_w, norm, *refs):
    o_refs = refs[:n_in]
    w_refs = refs[n_in:n_in + n_in * n_w]
    pos = n_in + n_in * n_w
    if norm:
        gn_ref = refs[pos]
        pos += 1
    x_ref, gate_ref, out_ref, o_scr = refs[pos:pos + 4]

    @pl.when(pl.program_id(1) == 0)
    def _():
        for k in range(n_in):
            o = o_refs[k][...]
            if norm:
                o = _rms_rows(o, gn_ref[...])
            parts = _split2(o) if n_w == 2 else (o.astype(BF16),)
            for p in range(n_w):
                o_scr[k, p] = parts[p]

    acc = None
    for k in range(n_in):
        part = _dot_parts(tuple(o_scr[k, p] for p in range(n_w)),
                          tuple(w_refs[k * n_w + p][...] for p in range(n_w)))
        acc = part if acc is None else acc + part
    x = x_ref[...]
    out_ref[...] = x + gate_ref[...] * acc.reshape(x.shape)


def _outproj(o_list, w, x3, m3, chunk_gate, sb, rb, norm_gain=None, tn=512):
    w = _wparts(w)
    n_w = len(w)
    sq, rq, d = x3.shape
    nb = rq // rb
    tm = sb * rb
    n_in = len(o_list)
    nj = d // tn
    widths = [o.shape[1] for o in o_list]
    assert all(wd == widths[0] for wd in widths) and sum(widths) == w[0].shape[0]
    kw = widths[0]
    in_specs = [pl.BlockSpec((tm, kw), lambda i, j: (i, 0)) for _ in o_list]
    args = list(o_list)
    for k in range(n_in):
        for part in w:
            mode = pl.Buffered(1) if nj == 1 else None
            in_specs.append(pl.BlockSpec((kw, tn), lambda i, j, k=k: (k, j), pipeline_mode=mode))
            args.append(part)
    if norm_gain is not None:
        in_specs.append(pl.BlockSpec((1, kw), lambda i, j: (0, 0)))
        args.append(norm_gain.reshape(1, kw))
    in_specs += [pl.BlockSpec((sb, rb, tn), lambda i, j: (i // nb, i % nb, j)),
                 pl.BlockSpec((sb, 1, tn), lambda i, j: (i // nb, 0, chunk_gate * nj + j))]
    args += [x3, m3]
    blocks = (n_in * (_nbytes((tm, kw)) + n_w * _nbytes((kw, tn), BF16))
              + 2 * _nbytes((tm, tn)) + _nbytes((sb, V7X_SUBLANES, tn)))
    scr = n_in * n_w * _nbytes((tm, kw), BF16)
    return pl.pallas_call(
        functools.partial(_outproj_body, n_in, n_w, norm_gain is not None), name="outproj",
        grid=((sq * rq) // tm, nj),
        in_specs=in_specs,
        out_specs=pl.BlockSpec((sb, rb, tn), lambda i, j: (i // nb, i % nb, j)),
        out_shape=jax.ShapeDtypeStruct(x3.shape, F32),
        scratch_shapes=[pltpu.VMEM((n_in, n_w, tm, kw), BF16)],
        compiler_params=_cparams(2, blocks, scr, 3 * _nbytes((tm, kw))),
    )(*args)


def _ret_tables(lam, c, width):
    lam1 = lam[:, :1]
    rel = (_iota((c, c), 0) - _iota((c, c), 1)).astype(F32)
    din = jnp.where(rel >= 0, jnp.exp(lam1 * jnp.maximum(rel, 0.0)), 0.0)
    r = _iota((c, width), 0).astype(F32)
    dq = jnp.exp(lam * (r + 1.0))
    dk = jnp.exp(lam * (c - 1.0 - r))
    gc = jnp.exp(lam * float(c))
    return din, dq, dk, gc


def _gla_log_alpha(alow, w2, b2):
    xa = _dot3(alow, w2) + b2
    return _log_sigmoid(xa) * (1.0 / GLA_TEMP)


def _mixa_prompt_body(pa, alow, w2, b2, cos, sin, gnr, gng, lam,
                      o_r, o_g, sret_out, sgla_out,
                      s_ret, st_gla, din_s, dq_s, dk_s, a_s):
    c_id = pl.program_id(1)
    n_c = pl.num_programs(1)
    c = pa.shape[0]
    dk_r, dk_g, dv = A_RET_DK, A_GLA_DK, A_DV

    @pl.when(c_id == 0)
    def _():
        s_ret[...] = jnp.zeros_like(s_ret)
        st_gla[...] = jnp.zeros_like(st_gla)
        for hh in range(RET_HEADS):
            din, dq, dk, _ = _ret_tables(lam[hh], c, dk_r)
            din_s[hh] = din
            dq_s[hh] = dq
            dk_s[hh] = dk

    cs, sn = cos[...], sin[...]
    la = _gla_log_alpha(alow[...], w2[...], b2[...])
    bc_all = _dot01_left(_tril01(c), la)
    rows = _iota((c, dk_g), 0)
    nblk = c // GLA_BLOCK
    row16 = _iota((GLA_BLOCK, dk_g), 0)
    lane16 = _iota((GLA_BLOCK, c), 1)
    for hh in range(RET_HEADS):
        q = _rot(pa[:, A_QR + hh * dk_r:A_QR + (hh + 1) * dk_r], cs, sn)
        k = _rot(pa[:, A_KR + hh * dk_r:A_KR + (hh + 1) * dk_r], cs, sn) * (dk_r ** -0.5)
        qs, vs = _split2(q), _split2(pa[:, A_VR + hh * dv:A_VR + (hh + 1) * dv])
        sc = _dot3_nt(q, k) * din_s[hh]
        s_old = s_ret[hh]
        o = _dot3p(_split2(sc), vs) + _dot3p(qs, _split2(s_old)) * dq_s[hh]
        kd = k * dk_s[hh]
        s_ret[hh] = s_old * jnp.exp(lam[hh] * float(c)) + _dot3p(_split2(kd.T), vs)
        o_r[:, hh * dv:(hh + 1) * dv] = _head_norm(
            o, gnr[:, hh * dv:(hh + 1) * dv], pa[:, A_GR + hh * dv:A_GR + (hh + 1) * dv], True)

        bc = bc_all[:, hh * dk_g:(hh + 1) * dk_g]
        qg_v = pa[:, A_QG + hh * dk_g:A_QG + (hh + 1) * dk_g] * (dk_g ** -0.5)
        kg_v = pa[:, A_KG + hh * dk_g:A_KG + (hh + 1) * dk_g]
        vg_v = pa[:, A_VG + hh * dv:A_VG + (hh + 1) * dv]
        for blk in range(nblk):
            r0 = blk * GLA_BLOCK
            q_i = qg_v[r0:r0 + GLA_BLOCK]
            b_i = bc[r0:r0 + GLA_BLOCK]
            if blk == 0:
                a_i = jnp.zeros((GLA_BLOCK, c), F32)
            else:
                ref_b = bc[r0 - 1:r0]
                qt = q_i * jnp.exp(b_i - ref_b)
                kt = jnp.where(rows < r0, kg_v * jnp.exp(jnp.minimum(ref_b - bc, 0.0)), 0.0)
                a_i = _dot3_nt(qt, kt)
            for jj in range(GLA_BLOCK):
                j = r0 + jj
                e = jnp.exp(jnp.minimum(b_i - bc[j:j + 1], 0.0))
                col = jnp.sum(q_i * e * kg_v[j:j + 1], axis=-1, keepdims=True)
                col = jnp.where(row16[:, :1] >= jj, col, 0.0)
                a_i = a_i + jnp.where(lane16 == j, col, 0.0)
            a_s[hh, r0:r0 + GLA_BLOCK, :] = a_i
        st_old = st_gla[hh]
        og = _dot3(a_s[hh], vg_v) + _dot3_nt(qg_v * jnp.exp(bc), st_old)
        b_last = bc[c - 1:c]
        kdg = kg_v * jnp.exp(b_last - bc)
        st_gla[hh] = st_old * jnp.exp(b_last) + _dot3(vg_v.T, kdg)
        o_g[:, hh * dv:(hh + 1) * dv] = _head_norm(
            og, gng[:, hh * dv:(hh + 1) * dv], pa[:, A_GG + hh * dv:A_GG + (hh + 1) * dv], False)

    @pl.when(c_id == n_c - 1)
    def _():
        for hh in range(RET_HEADS):
            sret_out[0, hh] = s_ret[hh]
            sgla_out[0, hh] = st_gla[hh].T


def _mixa_prompt(pa, pa_low, w2p, b2, cos, sin, gnr, gng, lam, batch, seq):
    c = MIX_CHUNK
    nc = seq // c
    h = RET_HEADS
    dkr, dkg, dv = A_RET_DK, A_GLA_DK, A_DV
    width = pa.shape[1]
    const2 = lambda b, cc: (0, 0)
    in_specs = [pl.BlockSpec((c, width), lambda b, cc: (b * nc + cc, 0)),
                pl.BlockSpec((c, V7X_LANES), lambda b, cc: (b * nc + cc, 0)),
                pl.BlockSpec(w2p.shape, const2), pl.BlockSpec(b2.shape, const2),
                pl.BlockSpec((c, dkr // 2), lambda b, cc: (cc, 0)),
                pl.BlockSpec((c, dkr // 2), lambda b, cc: (cc, 0)),
                pl.BlockSpec(gnr.shape, const2), pl.BlockSpec(gng.shape, const2),
                pl.BlockSpec(lam.shape, lambda b, cc: (0, 0, 0))]
    out_specs = [pl.BlockSpec((c, h * dv), lambda b, cc: (b * nc + cc, 0)),
                 pl.BlockSpec((c, h * dv), lambda b, cc: (b * nc + cc, 0)),
                 pl.BlockSpec((1, h, dkr, dv), lambda b, cc: (b, 0, 0, 0)),
                 pl.BlockSpec((1, h, dkg, dv), lambda b, cc: (b, 0, 0, 0))]
    n_tok = batch * seq
    out_shape = [jax.ShapeDtypeStruct((n_tok, h * dv), F32),
                 jax.ShapeDtypeStruct((n_tok, h * dv), F32),
                 jax.ShapeDtypeStruct((batch, h, dkr, dv), F32),
                 jax.ShapeDtypeStruct((batch, h, dkg, dv), F32)]
    scratch = [pltpu.VMEM((h, dkr, dv), F32), pltpu.VMEM((h, dv, dkg), F32),
               pltpu.VMEM((h, c, c), F32), pltpu.VMEM((h, c, dkr), F32), pltpu.VMEM((h, c, dkr), F32),
               pltpu.VMEM((h, c, c), F32)]
    blocks = _nbytes((c, width)) + 2 * _nbytes((c, h * dv)) + h * (_nbytes((dkr, dv)) + _nbytes((dkg, dv)))
    scr = h * (_nbytes((dkr, dv)) + _nbytes((dv, dkg)) + 2 * _nbytes((c, c)) + 2 * _nbytes((c, dkr)))
    return pl.pallas_call(
        _mixa_prompt_body, name="mixa_prompt",
        grid=(batch, nc),
        in_specs=in_specs, out_specs=out_specs, out_shape=out_shape,
        scratch_shapes=scratch,
        compiler_params=_cparams(2, blocks, scr, 16 * 2 ** 20),
    )(pa, pa_low, w2p, b2, cos, sin, gnr, gng, lam)


def _direct_intra(q, k, v, decay_fn):
    t = q.shape[0]
    row = _iota((t, 1), 0)
    o = jnp.zeros((t, v.shape[1]), F32)
    for j in range(t):
        col = jnp.sum(q * decay_fn(j) * k[j:j + 1], axis=-1, keepdims=True)
        col = jnp.where(row >= j, col, 0.0)
        o = o + col * v[j:j + 1]
    return o


def _mixa_sample_body(sb, pa, alow, w2, b2, cos, sin, gnr, gng, lam, sret_in, sgla_in,
                      o_r, o_g, sret_out, sgla_out):
    t = 8
    dkr, dkg, dv = 256, 128, 256
    cs, sn = cos[...], sin[...]
    la = _gla_log_alpha(alow[...], w2[...], b2[...])
    bc_all = _dot01_left(_tril01(sb * t, block=t), la)
    rel = (_iota((t, 1), 0)).astype(F32)
    for hh in range(RET_HEADS):
        lam_row = lam[hh]
        dq = jnp.exp(lam_row * (rel + 1.0))
        dk = jnp.exp(lam_row * (t - 1.0 - rel))
        gc = jnp.exp(lam_row * float(t))
        for s in range(sb):
            r0 = s * t
            q = _rot(pa[r0:r0 + t, A_QR + hh * dkr:A_QR + (hh + 1) * dkr], cs, sn)
            k = _rot(pa[r0:r0 + t, A_KR + hh * dkr:A_KR + (hh + 1) * dkr], cs, sn) * (dkr ** -0.5)
            v = pa[r0:r0 + t, A_VR + hh * dv:A_VR + (hh + 1) * dv]
            g = pa[r0:r0 + t, A_GR + hh * dv:A_GR + (hh + 1) * dv]
            s_old = sret_in[s, hh]
            o = _direct_intra(q, k, v, lambda j: jnp.exp(lam_row * jnp.maximum(rel - float(j), 0.0)))
            o = o + _dot3(q, s_old) * dq
            kdt = _pad_rows(k * dk, V7X_LANES).T
            sret_out[s, hh] = s_old * gc + _dot3(kdt, _pad_rows(v, V7X_LANES))
            o_r[r0:r0 + t, hh * dv:(hh + 1) * dv] = _head_norm(
                o, gnr[:, hh * dv:(hh + 1) * dv], g, True)
            qg = pa[r0:r0 + t, A_QG + hh * dkg:A_QG + (hh + 1) * dkg] * (dkg ** -0.5)
            kg = pa[r0:r0 + t, A_KG + hh * dkg:A_KG + (hh + 1) * dkg]
            vg = pa[r0:r0 + t, A_VG + hh * dv:A_VG + (hh + 1) * dv]
            gg = pa[r0:r0 + t, A_GG + hh * dv:A_GG + (hh + 1) * dv]
            bc = bc_all[r0:r0 + t, hh * dkg:(hh + 1) * dkg]
            sg_old = sgla_in[s, hh]
            og = _direct_intra(qg, kg, vg, lambda j: jnp.exp(jnp.minimum(bc - bc[j:j + 1], 0.0)))
            og = og + _dot3(qg * jnp.exp(bc), sg_old)
            b_last = bc[t - 1:t]
            m = _pad_rows(jnp.concatenate([kg * jnp.exp(b_last - bc), jnp.exp(b_last)], axis=0),
                          V7X_LANES).T
            sgla_out[s, hh] = sg_old * m[:, t:t + 1] + _dot3(m, _pad_rows(vg, V7X_LANES))
            o_g[r0:r0 + t, hh * dv:(hh + 1) * dv] = _head_norm(
                og, gng[:, hh * dv:(hh + 1) * dv], gg, False)


def _mixa_sample(pa, pa_low, w2p, b2, cos, sin, gnr, gng, lam, sret, sgla, sb=4):
    nseq = sret.shape[0]
    t = 8
    h, dkr, dkg, dv = RET_HEADS, 256, 128, 256
    width = pa.shape[1]
    const2 = lambda i: (0, 0)
    in_specs = [pl.BlockSpec((sb * t, width), lambda i: (i, 0)),
                pl.BlockSpec((sb * t, V7X_LANES), lambda i: (i, 0)),
                pl.BlockSpec(w2p.shape, const2), pl.BlockSpec(b2.shape, const2),
                pl.BlockSpec(cos.shape, const2), pl.BlockSpec(sin.shape, const2),
                pl.BlockSpec(gnr.shape, const2), pl.BlockSpec(gng.shape, const2),
                pl.BlockSpec(lam.shape, lambda i: (0, 0, 0)),
                pl.BlockSpec((sb, h, dkr, dv), lambda i: (i, 0, 0, 0)),
                pl.BlockSpec((sb, h, dkg, dv), lambda i: (i, 0, 0, 0))]
    out_specs = [pl.BlockSpec((sb * t, h * dv), lambda i: (i, 0)),
                 pl.BlockSpec((sb * t, h * dv), lambda i: (i, 0)),
                 pl.BlockSpec((sb, h, dkr, dv), lambda i: (i, 0, 0, 0)),
                 pl.BlockSpec((sb, h, dkg, dv), lambda i: (i, 0, 0, 0))]
    out_shape = [jax.ShapeDtypeStruct((nseq * t, h * dv), F32),
                 jax.ShapeDtypeStruct((nseq * t, h * dv), F32),
                 jax.ShapeDtypeStruct(sret.shape, F32),
                 jax.ShapeDtypeStruct(sgla.shape, F32)]
    blocks = (_nbytes((sb * t, width)) + 2 * _nbytes((sb, h, dkr, dv))
              + 2 * _nbytes((sb, h, dkg, dv)) + 2 * 2 ** 20)
    return pl.pallas_call(
        functools.partial(_mixa_sample_body, sb), name="mixa_sample",
        grid=(nseq // sb,),
        in_specs=in_specs, out_specs=out_specs, out_shape=out_shape,
        compiler_params=_cparams(1, blocks, 0, 4 * 2 ** 20),
    )(pa, pa_low, w2p, b2, cos, sin, gnr, gng, lam, sret, sgla)


def _conv_prompt_body(x_ref, prev_ref, w_ref, b_ref, o_ref):
    first = pl.program_id(1) == 0
    ext = jnp.concatenate([jnp.where(first, 0.0, prev_ref[...]), x_ref[...]], axis=0)
    hist = CONV_W - 1
    acc = b_ref[...] + pltpu.roll(ext, hist, 0)[V7X_SUBLANES:] * w_ref[0:1, :]
    for j in range(1, CONV_W):
        delayed = pltpu.roll(ext, hist - j, 0)[V7X_SUBLANES:] if j < hist else x_ref[...]
        acc = acc + delayed * w_ref[j:j + 1, :]
    o_ref[...] = _silu(acc)


def _conv_prompt(pc, conv_w, conv_b, batch, seq, col0, width, tc=512, tn=1024):
    nc = seq // tc
    cb0 = col0 // tn
    blocks = 2 * _nbytes((tc, tn)) + _nbytes((V7X_SUBLANES, tn)) * 3
    return pl.pallas_call(
        _conv_prompt_body, name="conv_prompt",
        grid=(batch, nc, width // tn),
        in_specs=[pl.BlockSpec((tc, tn), lambda b, c, j: (b * nc + c, cb0 + j)),
                  pl.BlockSpec((V7X_SUBLANES, tn),
                               lambda b, c, j: (jnp.maximum((b * nc + c) * (tc // V7X_SUBLANES) - 1, 0),
                                                cb0 + j)),
                  pl.BlockSpec((CONV_W, tn), lambda b, c, j: (0, j)),
                  pl.BlockSpec((1, tn), lambda b, c, j: (0, j))],
        out_specs=pl.BlockSpec((tc, tn), lambda b, c, j: (b * nc + c, j)),
        out_shape=jax.ShapeDtypeStruct((batch * seq, width), F32),
        compiler_params=_cparams(3, blocks, 0, 6 * _nbytes((tc, tn))),
    )(pc, pc, conv_w, conv_b.reshape(1, width))


def _conv_sample_body(x_ref, cache_ref, w_ref, b_ref, o_ref, scr):
    t = x_ref.shape[1]
    hist = CONV_W - 1
    scr[:, t - hist:t, :] = cache_ref[...]
    scr[:, t:2 * t, :] = x_ref[...]
    acc = b_ref[...] + scr[:, t - hist:2 * t - hist, :] * w_ref[0:1, :]
    for j in range(1, CONV_W):
        off = t - hist + j
        acc = acc + scr[:, off:off + t, :] * w_ref[j:j + 1, :]
    o_ref[...] = _silu(acc)


def _conv_sample(pc3, cache, conv_w, conv_b, col0, width, sbc=32, tn=1024):
    nseq, t, _ = pc3.shape
    cb0 = col0 // tn
    blocks = 2 * _nbytes((sbc, t, tn)) + _nbytes((sbc, 8, tn))
    return pl.pallas_call(
        _conv_sample_body, name="conv_sample",
        grid=(nseq // sbc, width // tn),
        in_specs=[pl.BlockSpec((sbc, t, tn), lambda i, j: (i, 0, cb0 + j)),
                  pl.BlockSpec((sbc, CONV_W - 1, tn), lambda i, j: (i, 0, j)),
                  pl.BlockSpec((CONV_W, tn), lambda i, j: (0, j)),
                  pl.BlockSpec((1, tn), lambda i, j: (0, j))],
        out_specs=pl.BlockSpec((sbc, t, tn), lambda i, j: (i, 0, j)),
        out_shape=jax.ShapeDtypeStruct((nseq, t, width), F32),
        scratch_shapes=[pltpu.VMEM((sbc, 2 * t, tn), F32)],
        compiler_params=_cparams(2, blocks, _nbytes((sbc, 2 * t, tn)), 4 * _nbytes((sbc, t, tn))),
    )(pc3, cache, conv_w, conv_b.reshape(1, width))


def _expand01(heads, width, first=0):
    r = _iota((V7X_LANES, heads * width), 0)
    l = _iota((V7X_LANES, heads * width), 1)
    return jnp.where((l // width) == r - first, 1.0, 0.0).astype(BF16)


def _ssd_prompt_body(xs, bm, cm, z, dtc, dtr, bias_c, alog_c, bias_r, alog_r, dfull,
                     y_out, s_out, st):
    c_id = pl.program_id(1)
    n_c = pl.num_programs(1)
    l = xs.shape[0]
    groups, n, gw = st.shape
    heads = dtr.shape[1]
    hpg = heads // groups
    p = gw // hpg

    @pl.when(c_id == 0)
    def _():
        st[...] = jnp.zeros_like(st)

    dt_c = _softplus(dtc[...] + bias_c[...])
    cum_c = _dot01_left(_tril01(l), dt_c * (-jnp.exp(alog_c[...])))
    cum_parts, dt_parts = _split3(cum_c), _split3(dt_c)
    dt_r = _softplus(dtr[0] + bias_r[...])
    tri_u = jnp.where(_iota((l, l), 0) <= _iota((l, l), 1), 1.0, 0.0).astype(BF16)
    cum_r = _dot01_right(dt_r * (-jnp.exp(alog_r[...])), tri_u)

    causal = _iota((l, l), 0) >= _iota((l, l), 1)
    lane = _iota((l, 2 * p), 1)
    for g in range(groups):
        x = xs[:, g * gw:(g + 1) * gw]
        bmg = bm[:, g * n:(g + 1) * n]
        cmb = cm[:, g * n:(g + 1) * n].astype(BF16)
        cb = _dot_nt(cmb, bmg.astype(BF16))
        y_tiles = []
        for pair in range(hpg // 2):
            xt = x[:, pair * 2 * p:(pair + 1) * 2 * p]
            acc = None
            for sub in range(2):
                r = g * hpg + 2 * pair + sub
                seg = cum_c[:, r:r + 1] - cum_r[r:r + 1, :]
                w = jnp.where(causal, cb * jnp.exp(jnp.minimum(seg, 0.0)) * dt_r[r:r + 1, :], 0.0)
                xm = jnp.where((lane // p) == sub, xt, 0.0)
                part = _dot(w.astype(BF16), xm.astype(BF16))
                acc = part if acc is None else acc + part
            y_tiles.append(acc)
        y = jnp.concatenate(y_tiles, axis=-1)

        ex = _expand01(hpg, p, g * hpg)
        cum_f = _dot(cum_parts[0], ex) + _dot(cum_parts[1], ex) + _dot(cum_parts[2], ex)
        dt_f = _dot(dt_parts[0], ex) + _dot(dt_parts[1], ex) + _dot(dt_parts[2], ex)
        st_old = st[g]
        y = y + _dot(cmb, st_old.astype(BF16)) * jnp.exp(cum_f)
        cl_f = cum_f[l - 1:l]
        tail = jnp.exp(cl_f - cum_f) * dt_f
        st[g] = st_old * jnp.exp(cl_f) + _dot(bmg.T.astype(BF16), (x * tail).astype(BF16))
        y = y + dfull[:, g * gw:(g + 1) * gw] * x
        y_out[:, g * gw:(g + 1) * gw] = y * _silu(z[:, g * gw:(g + 1) * gw])

    @pl.when(c_id == n_c - 1)
    def _():
        for g in range(groups):
            s_out[0, g * hpg:(g + 1) * hpg] = st[g].T.reshape(hpg, p, n)


def _ssd_prompt(act, pc, dt_c, dt_t, bias_c, alog_c, bias_r, alog_r, dfull, batch, seq):
    l = MIX_CHUNK
    nc = seq // l
    g = SSM_GROUPS
    n = SSM_STATE
    p = SSM_HEADDIM
    heads = dt_t.shape[1]
    d_inner = heads * p
    gw = d_inner // g
    gn = g * n

    def row(b, cc):
        return b * nc + cc

    in_specs = [pl.BlockSpec((l, d_inner), lambda b, cc: (row(b, cc), 0)),
                pl.BlockSpec((l, gn), lambda b, cc: (row(b, cc), d_inner // gn)),
                pl.BlockSpec((l, gn), lambda b, cc: (row(b, cc), d_inner // gn + 1)),
                pl.BlockSpec((l, d_inner), lambda b, cc: (row(b, cc), 0)),
                pl.BlockSpec((l, V7X_LANES), lambda b, cc: (row(b, cc), 0)),
                pl.BlockSpec((1, heads, l), lambda b, cc: (b, 0, cc)),
                pl.BlockSpec((1, V7X_LANES), lambda b, cc: (0, 0)),
                pl.BlockSpec((1, V7X_LANES), lambda b, cc: (0, 0)),
                pl.BlockSpec((heads, l), lambda b, cc: (0, 0)),
                pl.BlockSpec((heads, l), lambda b, cc: (0, 0)),
                pl.BlockSpec((1, d_inner), lambda b, cc: (0, 0))]
    out_specs = [pl.BlockSpec((l, d_inner), lambda b, cc: (row(b, cc), 0)),
                 pl.BlockSpec((1, heads, p, n), lambda b, cc: (b, 0, 0, 0))]
    out_shape = [jax.ShapeDtypeStruct((batch * seq, d_inner), F32),
                 jax.ShapeDtypeStruct((batch, heads, p, n), F32)]
    blocks = 3 * _nbytes((l, d_inner)) + 2 * _nbytes((l, gn)) + _nbytes((heads, p, n))
    return pl.pallas_call(
        _ssd_prompt_body, name="ssd_prompt",
        grid=(batch, nc),
        in_specs=in_specs, out_specs=out_specs, out_shape=out_shape,
        scratch_shapes=[pltpu.VMEM((g, n, gw), F32)],
        compiler_params=_cparams(2, blocks, _nbytes((g, n, gw)), 16 * 2 ** 20),
    )(act, act, act, pc, dt_c, dt_t, bias_c, alog_c, bias_r, alog_r, dfull)


def _ssd_sample_body(sb, xs, bm, cm, z, dtc, bias_c, alog_c, dfull, s_in, y_out, s_out, y_s):
    t = 8
    rows = sb * t
    gw = xs.shape[1]
    hpg = s_in.shape[1]
    p = gw // hpg
    n = bm.shape[1]
    dt_c = _softplus(dtc[...] + bias_c[0])
    cum_c = _dot01_left(_tril01(rows, block=t), dt_c * (-jnp.exp(alog_c[0])))
    ex = _expand01(hpg, p)
    cum_f = _dot01_right(cum_c, ex)
    dt_f = _dot01_right(dt_c, ex)
    x = xs[...]
    x3 = x.reshape(sb, t, gw)
    c3 = cm[...].reshape(sb, t, n)
    b3 = bm[...].reshape(sb, t, n)
    cum3 = cum_f.reshape(sb, t, gw)
    dt3 = dt_f.reshape(sb, t, gw)
    rowi = _iota((sb, t, 1), 1)
    y3 = jnp.zeros((sb, t, gw), F32)
    for j in range(t):
        cbj = jnp.sum(c3 * b3[:, j:j + 1, :], axis=-1, keepdims=True)
        dec = jnp.exp(jnp.minimum(cum3 - cum3[:, j:j + 1, :], 0.0))
        w = jnp.where(rowi >= j, cbj * dec * dt3[:, j:j + 1, :], 0.0)
        y3 = y3 + w * x3[:, j:j + 1, :]
    y_s[...] = y3.reshape(rows, gw)
    cmv = cm[...]
    bmv = bm[...]
    for s in range(sb):
        r0 = s * t
        sg = s_in[s].reshape(hpg * p, n)
        cf = cum_f[r0:r0 + t]
        yi = _dot_nt(cmv[r0:r0 + t].astype(BF16), sg.astype(BF16)) * jnp.exp(cf)
        cl = cf[t - 1:t]
        xt = x[r0:r0 + t] * (jnp.exp(cl - cf) * dt_f[r0:r0 + t])
        m = _pad_rows(jnp.concatenate([xt, jnp.exp(cl)], axis=0), V7X_LANES).T
        s_new = sg * m[:, t:t + 1] + _dot(m.astype(BF16), _pad_rows(bmv[r0:r0 + t], V7X_LANES).astype(BF16))
        s_out[s] = s_new.reshape(hpg, p, n)
        y_s[r0:r0 + t, :] = y_s[r0:r0 + t, :] + yi
    y_out[...] = (y_s[...] + dfull[0] * x) * _silu(z[...])


def _ssd_sample(act, pc, pcs, bias_c, alog_c, dfull, s_ssm, sb=8):
    t = 8
    nseq = s_ssm.shape[0]
    g = SSM_GROUPS
    n = SSM_STATE
    p = SSM_HEADDIM
    d_inner = s_ssm.shape[1] * p
    gw = d_inner // g
    hpg = gw // p
    bb0 = d_inner // n
    cb0 = (d_inner + g * n) // n
    rows = sb * t
    in_specs = [pl.BlockSpec((rows, gw), lambda i, gg: (i, gg)),
                pl.BlockSpec((rows, n), lambda i, gg: (i, bb0 + gg)),
                pl.BlockSpec((rows, n), lambda i, gg: (i, cb0 + gg)),
                pl.BlockSpec((rows, gw), lambda i, gg: (i, gg)),
                pl.BlockSpec((rows, V7X_LANES), lambda i, gg: (i, gg)),
                pl.BlockSpec((1, 1, V7X_LANES), lambda i, gg: (gg, 0, 0)),
                pl.BlockSpec((1, 1, V7X_LANES), lambda i, gg: (gg, 0, 0)),
                pl.BlockSpec((1, 1, gw), lambda i, gg: (gg, 0, 0)),
                pl.BlockSpec((sb, hpg, p, n), lambda i, gg: (i, gg, 0, 0))]
    out_specs = [pl.BlockSpec((rows, gw), lambda i, gg: (i, gg)),
                 pl.BlockSpec((sb, hpg, p, n), lambda i, gg: (i, gg, 0, 0))]
    out_shape = [jax.ShapeDtypeStruct((nseq * t, d_inner), F32),
                 jax.ShapeDtypeStruct(s_ssm.shape, F32)]
    blocks = 3 * _nbytes((rows, gw)) + 3 * _nbytes((rows, n)) + 2 * _nbytes((sb, hpg, p, n))
    return pl.pallas_call(
        functools.partial(_ssd_sample_body, sb), name="ssd_sample",
        grid=(nseq // sb, g),
        in_specs=in_specs, out_specs=out_specs, out_shape=out_shape,
        scratch_shapes=[pltpu.VMEM((rows, gw), F32)],
        compiler_params=_cparams(2, blocks, _nbytes((rows, gw)), 8 * 2 ** 20),
    )(act, act, act, pc, pcs, bias_c, alog_c, dfull, s_ssm)


def _moe_pre_body(n_pt, xp_ref, shp_ref, scp_ref, xs_ref, shs_ref, scs_ref, g_ref, wr_ref,
                  h_out, route_out):
    i = pl.program_id(0)

    @pl.when(i < n_pt)
    def _():
        _moe_route(xp_ref, shp_ref, scp_ref, g_ref, wr_ref, h_out, route_out)

    @pl.when(i >= n_pt)
    def _():
        _moe_route(xs_ref, shs_ref, scs_ref, g_ref, wr_ref, h_out, route_out)


def _moe_route(x_ref, sh_ref, sc_ref, g_ref, wr_ref, h_out, route_out):
    x = x_ref[...]
    h = _rms_rows(x, g_ref[...]) * (1.0 + sc_ref[...]) + sh_ref[...]
    h = h.reshape(h_out.shape[0], x.shape[-1])
    h_out[...] = h.reshape(h_out.shape)
    logits = _dot3(h, wr_ref[...])
    tm = logits.shape[0]
    lane = _iota((tm, V7X_LANES), 1).astype(F32)
    neg = -jnp.inf
    lg = jnp.where(lane < MOE_GROUPS, logits, neg)
    mg = jnp.max(lg, axis=-1, keepdims=True)
    top_g = jnp.min(jnp.where(lg == mg, lane, float(V7X_LANES)), axis=-1, keepdims=True)
    p_top = 1.0 / jnp.sum(jnp.where(lane < MOE_GROUPS, jnp.exp(logits - mg), 0.0),
                          axis=-1, keepdims=True)
    lo = MOE_GROUPS + MOE_PER_GROUP * top_g
    le = jnp.where((lane >= lo) & (lane < lo + MOE_PER_GROUP), logits, neg)
    v1 = jnp.max(le, axis=-1, keepdims=True)
    i1 = jnp.min(jnp.where(le == v1, lane, float(V7X_LANES)), axis=-1, keepdims=True)
    le2 = jnp.where(lane == i1, neg, le)
    v2 = jnp.max(le2, axis=-1, keepdims=True)
    i2 = jnp.min(jnp.where(le2 == v2, lane, float(V7X_LANES)), axis=-1, keepdims=True)
    tt = jnp.exp(v2 - v1)
    w1 = p_top / (1.0 + tt)
    w2 = p_top * tt / (1.0 + tt)
    route_out[...] = jnp.where(lane == 0, i1 - MOE_GROUPS,
                     jnp.where(lane == 1, i2 - MOE_GROUPS,
                     jnp.where(lane == 2, w1, jnp.where(lane == 3, w2, 0.0))))


def _moe_pre(xp, mp, xs_, ms, chunk_shift, chunk_scale, gain, wr, tm):
    bp, tp, d = xp.shape
    bs, ts, _ = xs_.shape
    nb = tp // tm
    n_pt = bp * nb
    sbs = tm // ts
    n_st = bs // sbs
    n_all = bp * tp + bs * ts

    def pi(i):
        return jnp.minimum(i, n_pt - 1)

    def si(i):
        return jnp.maximum(i - n_pt, 0)

    in_specs = [pl.BlockSpec((1, tm, d), lambda i: (pi(i) // nb, pi(i) % nb, 0)),
                pl.BlockSpec((1, 1, d), lambda i: (pi(i) // nb, 0, chunk_shift)),
                pl.BlockSpec((1, 1, d), lambda i: (pi(i) // nb, 0, chunk_scale)),
                pl.BlockSpec((sbs, ts, d), lambda i: (si(i), 0, 0)),
                pl.BlockSpec((sbs, 1, d), lambda i: (si(i), 0, chunk_shift)),
                pl.BlockSpec((sbs, 1, d), lambda i: (si(i), 0, chunk_scale)),
                pl.BlockSpec((1, d), lambda i: (0, 0)),
                pl.BlockSpec((d, V7X_LANES), lambda i: (0, 0))]
    blocks = (3 * _nbytes((tm, d)) + 2 * _nbytes((sbs, 8, d)) + _nbytes((d, V7X_LANES))
              + _nbytes((tm, 128)))
    return pl.pallas_call(
        functools.partial(_moe_pre_body, n_pt), name="moe_pre",
        grid=(n_pt + n_st,),
        in_specs=in_specs,
        out_specs=[pl.BlockSpec((tm, d // V7X_LANES, V7X_LANES), lambda i: (i, 0, 0)),
                   pl.BlockSpec((tm, V7X_LANES), lambda i: (i, 0))],
        out_shape=[jax.ShapeDtypeStruct((n_all, d // V7X_LANES, V7X_LANES), F32),
                   jax.ShapeDtypeStruct((n_all, V7X_LANES), F32)],
        compiler_params=_cparams(1, blocks, 0, 4 * _nbytes((tm, d))),
    )(xp, mp, mp, xs_, ms, ms, gain.reshape(1, d), wr)


def _rank_body(route_ref, rank_ref, cnt_ref, carry):
    @pl.when(pl.program_id(0) == 0)
    def _():
        carry[...] = jnp.zeros_like(carry)

    r = route_ref[...]
    tm = r.shape[0]
    lane = _iota((tm, V7X_LANES), 1).astype(F32)
    o1 = lane == r[:, 0:1]
    o2 = lane == r[:, 1:2]
    oh = jnp.where(o1, 1.0, 0.0) + jnp.where(o2, 1.0, 0.0)
    pex = _dot(_tril01(tm, strict=True), oh.astype(BF16)) + carry[0:1, :]
    r1 = jnp.sum(jnp.where(o1, pex, 0.0), axis=-1, keepdims=True)
    r2 = jnp.sum(jnp.where(o2, pex, 0.0), axis=-1, keepdims=True)
    rank_ref[...] = jnp.where(lane == 0, r1, jnp.where(lane == 1, r2, 0.0))
    carry[...] = carry[...] + jnp.sum(oh, axis=0, keepdims=True)
    cnt_ref[...] = carry[...]


def _rank(route, tm=256):
    n = route.shape[0]
    return pl.pallas_call(
        _rank_body, name="moe_rank",
        grid=(n // tm,),
        in_specs=[pl.BlockSpec((tm, V7X_LANES), lambda i: (i, 0))],
        out_specs=[pl.BlockSpec((tm, V7X_LANES), lambda i: (i, 0)),
                   pl.BlockSpec((V7X_SUBLANES, V7X_LANES), lambda i: (0, 0))],
        out_shape=[jax.ShapeDtypeStruct((n, V7X_LANES), F32),
                   jax.ShapeDtypeStruct((V7X_SUBLANES, V7X_LANES), F32)],
        scratch_shapes=[pltpu.VMEM((V7X_SUBLANES, V7X_LANES), F32)],
        compiler_params=_cparams(1, 2 * _nbytes((tm, 128))),
    )(route)


def _gather_tokens(src_hbm, dst, sem, n, row_of, first=0):
    def issue(r, carry):
        pltpu.make_async_copy(src_hbm.at[row_of(r)], dst.at[r], sem).start()
        return carry

    lax.fori_loop(first, n, issue, 0, unroll=DMA_UNROLL)


def _gather_wait(src_hbm, dst, sem):
    pltpu.make_async_copy(src_hbm.at[pl.ds(0, dst.shape[0])], dst, sem).wait()


def _expert_body(layer, inv_ref, te_ref, first_ref, wslot_ref, nxt_ref, nt_ref,
                 h3_hbm, wg_hbm, wu_hbm, wd_hbm, o_ref,
                 xbuf, wg_f, wu_f, wd_f, wg_b, wu_b, wd_b, xsem, wsem):
    i = pl.program_id(0)
    nt = nt_ref[0]
    tm = xbuf.shape[1]
    active = i < nt

    def w_copies(e, slot):
        return [pltpu.make_async_copy(src.at[layer, e], dst.at[slot], wsem.at[k])
                for k, (src, dst) in enumerate(((wg_hbm, wg_f), (wu_hbm, wu_f), (wd_hbm, wd_f)))]

    def fetch_rows(tile, slot, first=0, last=tm):
        _gather_tokens(h3_hbm, xbuf.at[slot], xsem.at[slot], last, lambda r: inv_ref[tile * tm + r], first)

    @pl.when(i == 0)
    def _():
        for cp in w_copies(te_ref[0], wslot_ref[0]):
            cp.start(priority=WEIGHT_DMA_PRIORITY)
        fetch_rows(0, 0)

        @pl.when(nt > 1)
        def _():
            fetch_rows(1, 1)

    @pl.when(active)
    def _():
        slot = i % ROW_SLOTS
        ws = wslot_ref[i]

        @pl.when(first_ref[i] == 1)
        def _():
            for cp in w_copies(te_ref[i], ws):
                cp.wait()
            wg_b[...] = wg_f[ws].astype(BF16)
            wu_b[...] = wu_f[ws].astype(BF16)
            wd_b[...] = wd_f[ws].astype(BF16)

            @pl.when(nxt_ref[i] >= 0)
            def _():
                for cp in w_copies(nxt_ref[i], 1 - ws):
                    cp.start(priority=WEIGHT_DMA_PRIORITY)

        @pl.when(i + 2 < nt)
        def _():
            fetch_rows(i + 2, (i + 2) % ROW_SLOTS, 0, tm // 2)

        _gather_wait(h3_hbm, xbuf.at[slot], xsem.at[slot])
        xb = xbuf[slot].reshape(tm, wg_b.shape[0]).astype(BF16)
        hid = (_silu(_dot(xb, wg_b[...])) * _dot(xb, wu_b[...])).astype(BF16)

        @pl.when(i + 2 < nt)
        def _():
            fetch_rows(i + 2, (i + 2) % ROW_SLOTS, tm // 2, tm)

        o_ref[...] = _dot(hid, wd_b[...]).reshape(o_ref.shape)

    @pl.when(jnp.logical_not(active))
    def _():
        o_ref[...] = jnp.zeros_like(o_ref)


def _experts(inv, tile_expert, first, wslot, nxt, ntiles, h3, wg, wu, wd, layer):
    _, dc, ln = h3.shape
    d = dc * ln
    ff = wg.shape[3]
    tm = MOE_TILE
    p_pad = inv.shape[0]
    scratch = [pltpu.VMEM((ROW_SLOTS, tm, dc, ln), F32),
               pltpu.VMEM((2, d, ff), F32), pltpu.VMEM((2, d, ff), F32), pltpu.VMEM((2, ff, d), F32),
               pltpu.VMEM((d, ff), BF16), pltpu.VMEM((d, ff), BF16), pltpu.VMEM((ff, d), BF16),
               pltpu.SemaphoreType.DMA((ROW_SLOTS,)), pltpu.SemaphoreType.DMA((3,))]
    scr_bytes = ROW_SLOTS * _nbytes((tm, d)) + 6 * _nbytes((d, ff)) + 3 * _nbytes((d, ff), BF16)
    any_spec = pl.BlockSpec(memory_space=pl.ANY)
    return pl.pallas_call(
        functools.partial(_expert_body, layer), name="moe_experts",
        grid_spec=pltpu.PrefetchScalarGridSpec(
            num_scalar_prefetch=6,
            grid=(p_pad // tm,),
            in_specs=[any_spec, any_spec, any_spec, any_spec],
            out_specs=pl.BlockSpec((tm, dc, ln), lambda i, *_: (i, 0, 0)),
            scratch_shapes=scratch),
        out_shape=jax.ShapeDtypeStruct((p_pad, dc, ln), F32),
        compiler_params=_cparams(1, _nbytes((tm, d)), scr_bytes, 3 * _nbytes((tm, d)),
                                 disable_bounds_checks=True),
    )(inv, tile_expert, first, wslot, nxt, ntiles, h3, wg, wu, wd)


def _combine_body(tok_off, final, pos_ref, eo_hbm, route_ref, x_ref, gate_ref, fn_ref, out_ref,
                  buf, sem):
    i = pl.program_id(0)
    n = pl.num_programs(0)
    tc = route_ref.shape[0]
    d = x_ref.shape[-1]

    def fetch(step, slot):
        base = 2 * (tok_off + step * tc)
        for k in range(MOE_TOPK):
            _gather_tokens(eo_hbm, buf.at[slot, k], sem.at[slot], tc,
                           lambda r, k=k: pos_ref[base + MOE_TOPK * r + k])

    @pl.when(i == 0)
    def _():
        fetch(0, 0)

    slot = i % 2

    @pl.when(i + 1 < n)
    def _():
        fetch(i + 1, 1 - slot)

    for k in range(MOE_TOPK):
        _gather_wait(eo_hbm, buf.at[slot, k], sem.at[slot])
    rt = route_ref[...]
    y = rt[:, 2:3] * buf[slot, 0].reshape(tc, d) + rt[:, 3:4] * buf[slot, 1].reshape(tc, d)
    x = x_ref[...]
    xn = x + gate_ref[...] * y.reshape(x.shape)
    if final:
        xn = _rms_rows(xn, fn_ref[...])
    out_ref[...] = xn


def _combine(pos_flat, eo, route, x3, m3, chunk_gate, sb, rb, tok_off, final_gain=None):
    sq, rq, d = x3.shape
    nb = rq // rb
    tc = sb * rb
    blk_off = tok_off // tc
    final = final_gain is not None
    fn = (final_gain if final else jnp.ones((d,), F32)).reshape(1, d)
    blocks = 2 * _nbytes((tc, d)) + _nbytes((tc, 128)) + _nbytes((sb, 8, d))
    return pl.pallas_call(
        functools.partial(_combine_body, tok_off, final), name="moe_combine",
        grid_spec=pltpu.PrefetchScalarGridSpec(
            num_scalar_prefetch=1,
            grid=((sq * rq) // tc,),
            in_specs=[pl.BlockSpec(memory_space=pl.ANY),
                      pl.BlockSpec((tc, V7X_LANES), lambda i, p: (blk_off + i, 0)),
                      pl.BlockSpec((sb, rb, d), lambda i, p: (i // nb, i % nb, 0)),
                      pl.BlockSpec((sb, 1, d), lambda i, p: (i // nb, 0, chunk_gate)),
                      pl.BlockSpec((1, d), lambda i, p: (0, 0))],
            out_specs=pl.BlockSpec((sb, rb, d), lambda i, p: (i // nb, i % nb, 0)),
            scratch_shapes=[pltpu.VMEM((2, MOE_TOPK, tc, d // V7X_LANES, V7X_LANES), F32),
                            pltpu.SemaphoreType.DMA((2,))]),
        out_shape=jax.ShapeDtypeStruct(x3.shape, F32),
        compiler_params=_cparams(1, blocks, 2 * MOE_TOPK * _nbytes((tc, d)), 4 * _nbytes((tc, d)),
                                 disable_bounds_checks=True),
    )(pos_flat, eo, route, x3, m3, fn)


def _moe_layer(xp, xs_, mp, ms, l, norm_moe, w_group, w_expert, w_gate, w_up, w_down, final_gain):
    bp, tp, d = xp.shape
    bs, ts, _ = xs_.shape
    n_p, n_s = bp * tp, bs * ts
    n_all = n_p + n_s
    wr = jnp.concatenate([w_group[l], jnp.transpose(w_expert[l], (1, 0, 2)).reshape(d, MOE_EXPERTS)],
                         axis=1)
    wr = jnp.pad(wr, ((0, 0), (0, V7X_LANES - wr.shape[1])))
    tm = MOE_TILE
    h_all, route = _moe_pre(xp, mp, xs_, ms, 3, 4, norm_moe[l], wr, tm)
    rank, cnt = _rank(route)
    counts = cnt[0, :MOE_EXPERTS].astype(I32)
    tiles_per = (counts + tm - 1) // tm
    tile_end = jnp.cumsum(tiles_per)
    row_start = (tile_end - tiles_per) * tm
    ntiles = tile_end[-1]
    e_idx = route[:, 0:MOE_TOPK].astype(I32)
    pos = row_start[e_idx] + rank[:, 0:MOE_TOPK].astype(I32)
    nt_max = (2 * n_all) // tm + MOE_EXPERTS
    p_pad = nt_max * tm
    tile_ids = jnp.arange(nt_max, dtype=I32)
    te = jnp.minimum(jnp.sum((tile_ids[:, None] >= tile_end[None, :]).astype(I32), axis=1),
                     MOE_EXPERTS - 1)
    active = tile_ids < ntiles
    te = jnp.where(active, te, te[jnp.maximum(ntiles - 1, 0)])
    first = (active & ((tile_ids == 0) | (te != jnp.roll(te, 1)))).astype(I32)
    wslot = (jnp.cumsum(first) - 1) % 2
    grp_end = tile_end[te]
    nxt = jnp.where(grp_end < ntiles, te[jnp.minimum(grp_end, nt_max - 1)], -1).astype(I32)
    pos_flat = pos.reshape(-1)
    inv = (jnp.arange(p_pad, dtype=I32) % n_all).at[pos_flat].set(jnp.arange(2 * n_all, dtype=I32) // 2)
    nt_arr = ntiles.reshape(1).astype(I32)
    eo = _experts(inv, te, first, wslot.astype(I32), nxt, nt_arr, h_all, w_gate, w_up, w_down, l)
    xp_new = _combine(pos_flat, eo, route, xp, mp, 5, 1, tm, 0, final_gain)
    xs_new = _combine(pos_flat, eo, route, xs_, ms, 5, tm // ts, ts, n_p, final_gain)
    return xp_new, xs_new


def kernel(x_prompt, x_sample, c_prompt, c_sample, state_ret, state_gla, state_ssm, cache_conv,
           w_ada, b_ada, norm_mix, norm_moe,
           a_w_in, a_ret_gn, a_gla_w2, a_gla_b2, a_gla_gn, a_w_out,
           c_w_in, c_conv_w, c_conv_b, c_dt_bias, c_a_log, c_d, c_norm, c_w_out,
           moe_w_group, moe_w_expert, moe_w_gate, moe_w_up, moe_w_down, final_norm):
    bp, tp, d = x_prompt.shape
    bs, ts, _ = x_sample.shape
    n_p, n_s = bp * tp, bs * ts

    pad = (-bp) % V7X_SUBLANES
    c_all = jnp.concatenate([c_prompt, jnp.zeros((pad, d), F32), c_sample], axis=0)
    m_all = _ada(c_all, w_ada, b_ada)
    mp = [m_all[l, :bp].reshape(bp, 1, 6 * d) for l in range(2)]
    ms = [m_all[l, bp + pad:].reshape(bs, 1, 6 * d) for l in range(2)]

    tm_p, tm_s = 512, 512

    wa = a_w_in[0]
    n_main_a = (wa.shape[1] // 512) * 512
    wa_low = jnp.pad(wa[:, n_main_a:], ((0, 0), (0, V7X_LANES - (wa.shape[1] - n_main_a))))
    w2p = jnp.pad(a_gla_w2[0], ((0, V7X_LANES - GLA_RANK), (0, 0)))
    b2 = a_gla_b2[0].reshape(1, -1)
    gnr = a_ret_gn[0].reshape(1, -1)
    gng = a_gla_gn[0].reshape(1, -1)
    half = A_RET_DK // 2
    inv_freq = ROPE_BASE ** (-jnp.arange(half, dtype=F32) / half)
    ang_p = jnp.arange(tp, dtype=F32)[:, None] * inv_freq[None, :]
    ang_s = (PAST_LEN + jnp.arange(ts, dtype=F32))[:, None] * inv_freq[None, :]
    log_gamma = jnp.log(1.0 - 2.0 ** (-5.0 - jnp.arange(RET_HEADS, dtype=F32)))
    lam = jnp.broadcast_to(log_gamma[:, None, None], (RET_HEADS, 1, A_RET_DK))

    wa_parts, wa_low_parts, wo_parts = _split_bf16(wa), _split_bf16(wa_low), _split_bf16(a_w_out[0])
    pa_p, low_p = _inproj(x_prompt, mp[0], 0, 1, norm_mix[0], wa_parts, n_main_a, wa_low_parts, 1, tm_p)
    pa_s, low_s = _inproj(x_sample, ms[0], 0, 1, norm_mix[0], wa_parts, n_main_a, wa_low_parts,
                          tm_s // ts, ts)
    or_p, og_p, ret_p, gla_p = _mixa_prompt(pa_p, low_p, w2p, b2, jnp.cos(ang_p), jnp.sin(ang_p),
                                            gnr, gng, lam, bp, tp)
    or_s, og_s, ret_s, gla_s = _mixa_sample(pa_s, low_s, w2p, b2, jnp.cos(ang_s), jnp.sin(ang_s),
                                            gnr, gng, lam, state_ret[0], state_gla[0])
    xp = _outproj([or_p, og_p], wo_parts, x_prompt, mp[0], 2, 1, 256, tn=d)
    xs_ = _outproj([or_s, og_s], wo_parts, x_sample, ms[0], 2, 256 // ts, ts, tn=d)
    xp, xs_ = _moe_layer(xp, xs_, mp[0], ms[0], 0, norm_moe, moe_w_group, moe_w_expert,
                         moe_w_gate, moe_w_up, moe_w_down, None)

    wc = c_w_in[0]
    n_main_c = (wc.shape[1] // 512) * 512
    heads = wc.shape[1] - n_main_c
    hpg = heads // SSM_GROUPS
    d_inner = heads * SSM_HEADDIM
    conv_dim = n_main_c - d_inner
    wdt = jnp.pad(wc[:, n_main_c:].reshape(d, SSM_GROUPS, hpg),
                  ((0, 0), (0, 0), (0, V7X_LANES - hpg))).reshape(d, SSM_GROUPS * V7X_LANES).astype(BF16)
    wc_b = wc.astype(BF16)
    wdt_all = jnp.pad(wc[:, n_main_c:], ((0, 0), (0, V7X_LANES - heads))).astype(BF16)
    pc_p, dt_p = _inproj(x3=xp, m3=mp[1], chunk_shift=0, chunk_scale=1, gain=norm_mix[1], w=wc_b,
                         n_main=n_main_c, ws=wdt_all, sb=1, rb=2 * tm_p)
    pc_s, dt_s = _inproj(x3=xs_, m3=ms[1], chunk_shift=0, chunk_scale=1, gain=norm_mix[1], w=wc_b,
                         n_main=n_main_c, ws=wdt, sb=tm_s // ts, rb=ts)
    act_p = _conv_prompt(pc_p, c_conv_w[0], c_conv_b[0], bp, tp, d_inner, conv_dim)
    act_s = _conv_sample(pc_s.reshape(bs, ts, n_main_c), cache_conv[0], c_conv_w[0], c_conv_b[0],
                         d_inner, conv_dim).reshape(n_s, conv_dim)
    conv_p = pc_p.reshape(bp, tp, n_main_c)[:, tp - (CONV_W - 1):, d_inner:]
    conv_s = pc_s.reshape(bs, ts, n_main_c)[:, ts - (CONV_W - 1):, d_inner:]

    def lanes_pad(v):
        return jnp.pad(v.reshape(SSM_GROUPS, 1, hpg), ((0, 0), (0, 0), (0, V7X_LANES - hpg)))

    bias_c, alog_c = lanes_pad(c_dt_bias[0]), lanes_pad(c_a_log[0])
    dfull = jnp.repeat(c_d[0], SSM_HEADDIM).reshape(SSM_GROUPS, 1, hpg * SSM_HEADDIM)

    def lanes_all(v):
        return jnp.pad(v.reshape(1, heads), ((0, 0), (0, V7X_LANES - heads)))

    bias_r = jnp.broadcast_to(c_dt_bias[0].reshape(heads, 1), (heads, MIX_CHUNK))
    alog_r = jnp.broadcast_to(c_a_log[0].reshape(heads, 1), (heads, MIX_CHUNK))
    dt_t = jnp.transpose(dt_p[:, :heads].reshape(bp, tp, heads), (0, 2, 1))
    yg_p, ssm_p = _ssd_prompt(act_p, pc_p, dt_p, dt_t, lanes_all(c_dt_bias[0]), lanes_all(c_a_log[0]),
                              bias_r, alog_r, dfull.reshape(1, d_inner), bp, tp)
    yg_s, ssm_s = _ssd_sample(act_s, pc_s, dt_s, bias_c, alog_c, dfull, state_ssm[0])
    wco_b = c_w_out[0].astype(BF16)
    xp = _outproj([yg_p], wco_b, xp, mp[1], 2, 1, 256, norm_gain=c_norm[0], tn=d)
    xs_ = _outproj([yg_s], wco_b, xs_, ms[1], 2, 256 // ts, ts, norm_gain=c_norm[0], tn=d)
    yp, ys = _moe_layer(xp, xs_, mp[1], ms[1], 1, norm_moe, moe_w_group, moe_w_expert,
                        moe_w_gate, moe_w_up, moe_w_down, final_norm)

    return (yp, ys, ret_p[None], ret_s[None], gla_p[None], gla_s[None],
            ssm_p[None], ssm_s[None], conv_p[None], conv_s[None])
```
